```python
import jax, jax.numpy as jnp
from jax import lax
import numpy as np

D_MODEL = 1024
BATCH = 32
SEQ = 256
DEPTH = 4
DEC_BATCH = 4
DEC_SEQ = 4096
PAST_LEN = 512

GRID_W = 64
CHUNK = 64
N_EVEN = (DEPTH + 1) // 2
N_ODD = DEPTH // 2
NORM_EPS = 1e-6
GLA_HEADS = 4
GLA_DK = D_MODEL // 16
GLA_DV = D_MODEL // 8
GLA_RANK = 16
GLA_TAU = 16.0
GLA_QK = GLA_HEADS * GLA_DK
GLA_V = GLA_HEADS * GLA_DV
HGRN_HEADS = 4
HGRN_DK = D_MODEL // 8
HGRN_DV = D_MODEL // 8
HG_K = HGRN_HEADS * HGRN_DK
HG_V = HGRN_HEADS * HGRN_DV
RET_HEADS = D_MODEL // 256
RET_DK = 256
RET_DV = 512
RET_QK = RET_HEADS * RET_DK
RET_V = RET_HEADS * RET_DV
ROPE_BASE = 10000.0
ROPE_PAIRS = RET_DK // 4
EVEN_SIZES = (GLA_QK, GLA_QK, GLA_V, GLA_V, 2 * GLA_RANK, HG_K, HG_K, HG_K, HG_V, HG_V)
EVEN_IN = 2 * GLA_QK + 2 * GLA_V + 2 * GLA_RANK + 3 * HG_K + 2 * HG_V
EVEN_MIX = GLA_V + HG_V
ODD_SIZES = (RET_QK, RET_QK, RET_V, RET_V)
ODD_IN = 2 * RET_QK + 2 * RET_V
N_GROUPS = 4
EXP_PER_GROUP = 8
N_EXPERTS = N_GROUPS * EXP_PER_GROUP
TOP_K = 2
D_EXPERT = D_MODEL // 4

kernel_name = 'hybrid_gla_hgrn2_retention_hmoe_diffusion_step'

F32 = jnp.float32


def rms_norm(x, w):
    xf = x.astype(F32)
    y = xf * lax.rsqrt(jnp.mean(xf * xf, axis=-1, keepdims=True) + NORM_EPS)
    return (y * w.astype(F32)).astype(x.dtype)


def head_rms(o):
    of = o.astype(F32)
    return (of * lax.rsqrt(jnp.mean(of * of, axis=-1, keepdims=True) + NORM_EPS)).astype(o.dtype)


def split_sizes(z, sizes):
    idx = np.cumsum(np.array(sizes))[:-1].tolist()
    return jnp.split(z, idx, axis=-1)


def to_heads(t, nh):
    bsz, L, _ = t.shape
    return t.reshape(bsz, L, nh, -1).transpose(0, 2, 1, 3)


def from_heads(t):
    bsz, nh, L, d = t.shape
    return t.transpose(0, 2, 1, 3).reshape(bsz, L, nh * d)


def chunk_recurrence(q, k, v, log_a, s0):
    dtype = v.dtype
    bsz, nh, L, dk = q.shape
    dv = v.shape[-1]
    n = L // CHUNK

    def blk(t):
        return t.astype(F32).reshape(bsz, nh, n, CHUNK, t.shape[-1])

    q, k, v, log_a = blk(q), blk(k), blk(v), blk(log_a)
    b = jnp.cumsum(log_a, axis=3)
    b_ref = b[:, :, :, CHUNK // 2:CHUNK // 2 + 1]
    b_last = b[:, :, :, -1:]
    scores = jnp.einsum('bhncd,bhnsd->bhncs', q * jnp.exp(b - b_ref), k * jnp.exp(b_ref - b))
    causal = jnp.tril(jnp.ones((CHUNK, CHUNK), dtype=bool))
    o_intra = jnp.einsum('bhncs,bhnsv->bhncv', jnp.where(causal, scores, 0.0), v)
    q_in = q * jnp.exp(b)
    k_out = k * jnp.exp(b_last - b)
    g = jnp.exp(b_last[:, :, :, 0])

    def step(S, xs):
        qc, kc, vc, gc = xs
        o = jnp.einsum('bhcd,bhdv->bhcv', qc, S)
        S = gc[..., None] * S + jnp.einsum('bhcd,bhcv->bhdv', kc, vc)
        return S, o

    mv = lambda t: jnp.moveaxis(t, 2, 0)
    s_final, o_inter = lax.scan(step, s0.astype(F32), (mv(q_in), mv(k_out), mv(v), mv(g)))
    o = o_intra + jnp.moveaxis(o_inter, 0, 2)
    return o.reshape(bsz, nh, L, dv).astype(dtype), s_final.astype(dtype)


def bidirectional(q, k_f, k_b, v, la_f, la_b, s0):
    o_f, s_f = chunk_recurrence(q, k_f, v, la_f, s0[:, 0])
    flip = lambda t: jnp.flip(t, axis=2)
    o_b, s_b = chunk_recurrence(flip(q), flip(k_b), flip(v), flip(la_b), s0[:, 1])
    return o_f + flip(o_b), jnp.stack([s_f, s_b], axis=1)


def rope_2d(x, row, col):
    freqs = ROPE_BASE ** (-jnp.arange(ROPE_PAIRS, dtype=F32) / ROPE_PAIRS)

    def rot(xp, p):
        ang = p.astype(F32)[:, None] * freqs
        cos, sin = jnp.cos(ang), jnp.sin(ang)
        x1, x2 = xp[..., :ROPE_PAIRS].astype(F32), xp[..., ROPE_PAIRS:].astype(F32)
        return jnp.concatenate([x1 * cos - x2 * sin, x1 * sin + x2 * cos], axis=-1)

    half = RET_DK // 2
    return jnp.concatenate([rot(x[..., :half], row), rot(x[..., half:], col)], axis=-1).astype(x.dtype)


def even_mixer(h, s_gla, s_hgrn, w_in, w_a2, b_a, lb, gn_a, gn_b, w_out):
    bsz, L, _ = h.shape
    z = h @ w_in
    qa, ka, va, ra, alr, qb, fbf, fbb, ib, gb = split_sizes(z, EVEN_SIZES)
    if s_gla is None:
        s_gla = jnp.zeros((bsz, 2, GLA_HEADS, GLA_DK, GLA_DV), h.dtype)
        s_hgrn = jnp.zeros((bsz, 2, HGRN_HEADS, HGRN_DK, HGRN_DV), h.dtype)
    qa_h = to_heads(qa, GLA_HEADS) * (GLA_DK ** -0.5)
    ka_h = to_heads(ka, GLA_HEADS)
    va_h = to_heads(va, GLA_HEADS)
    la = [to_heads(jax.nn.log_sigmoid(alr[..., d * GLA_RANK:(d + 1) * GLA_RANK] @ w_a2[d] + b_a[d]) / GLA_TAU,
                   GLA_HEADS) for d in (0, 1)]
    o_a, st_a = bidirectional(qa_h, ka_h, ka_h, va_h, la[0], la[1], s_gla)
    o_a = from_heads(head_rms(o_a)) * gn_a * jax.nn.silu(ra)
    lb_h = lb.reshape(HGRN_HEADS, 1, HGRN_DK)
    fz = [to_heads(fbf, HGRN_HEADS), to_heads(fbb, HGRN_HEADS)]
    log_f = [jnp.log(lb_h + (1.0 - lb_h) * jax.nn.sigmoid(f)) for f in fz]
    keys = [(1.0 - lb_h) * jax.nn.sigmoid(-f) for f in fz]
    qb_h = jax.nn.silu(to_heads(qb, HGRN_HEADS))
    o_b, st_b = bidirectional(qb_h, keys[0], keys[1], to_heads(ib, HGRN_HEADS), log_f[0], log_f[1], s_hgrn)
    o_b = from_heads(head_rms(o_b)) * gn_b * jax.nn.silu(gb)
    return jnp.concatenate([o_a, o_b], axis=-1) @ w_out, st_a, st_b


def odd_mixer(h, s_ret, pos, w_in, w_out):
    bsz, L, _ = h.shape
    z = h @ w_in
    q, k, v, g = split_sizes(z, ODD_SIZES)
    q = to_heads(q, RET_HEADS)
    k = to_heads(k, RET_HEADS) * (RET_DK ** -0.5)
    v = to_heads(v, RET_HEADS)
    if pos is not None:
        q, k = rope_2d(q, pos[0], pos[1]), rope_2d(k, pos[0], pos[1])
    if s_ret is None:
        s_ret = jnp.zeros((bsz, 2, RET_HEADS, RET_DK, RET_DV), h.dtype)
    log_g = jnp.log1p(-jnp.exp2(-5.0 - jnp.arange(RET_HEADS, dtype=F32)))
    la_f = jnp.broadcast_to(log_g[None, :, None, None], q.shape).astype(q.dtype)
    la_b = jnp.broadcast_to(jnp.flip(log_g)[None, :, None, None], q.shape).astype(q.dtype)
    o, st = bidirectional(q, k, k, v, la_f, la_b, s_ret)
    o = from_heads(head_rms(o)) * jax.nn.silu(g)
    return o @ w_out, st


def hier_moe(h, rg_w, rg_b, re_w, re_b, w_gate, w_up, w_down):
    bsz, L, d = h.shape
    xt = h.reshape(-1, d)
    lg = (xt @ rg_w + rg_b).astype(F32)
    pg = jax.nn.softmax(lg, axis=-1)
    gsel = jnp.argmax(lg, axis=-1)
    le = (xt @ re_w + re_b).astype(F32).reshape(-1, N_GROUPS, EXP_PER_GROUP)
    le = jnp.take_along_axis(le, gsel[:, None, None], axis=1)[:, 0]
    top_p, top_i = lax.top_k(jax.nn.softmax(le, axis=-1), TOP_K)
    top_p = top_p / jnp.sum(top_p, axis=-1, keepdims=True)
    wts = jnp.take_along_axis(pg, gsel[:, None], axis=1) * top_p
    eidx = gsel[:, None] * EXP_PER_GROUP + top_i
    comb = jnp.sum(jax.nn.one_hot(eidx, N_EXPERTS, dtype=F32) * wts[..., None], axis=1).astype(h.dtype)
    a = jax.nn.silu(jnp.einsum('td,edf->tef', xt, w_gate)) * jnp.einsum('td,edf->tef', xt, w_up)
    y = jnp.einsum('tef,efd->td', a * comb[..., None], w_down)
    return y.reshape(bsz, L, d)


def trunk(x, cond, states, pos, norm1_w, norm2_w, normf_w, w_mod, b_mod, w_in_even, gla_w_alpha, gla_b_alpha,
          hgrn_lb_logits, gla_norm_w, hgrn_norm_w, w_out_even, w_in_odd, w_out_odd, router_g_w, router_g_b,
          router_e_w, router_e_b, moe_w_gate, moe_w_up, moe_w_down):
    lb_all = jnp.cumsum(jax.nn.softmax(hgrn_lb_logits.astype(F32), axis=0), axis=0)[:N_EVEN].astype(x.dtype)
    sc = jax.nn.silu(cond)
    st_gla, st_hgrn, st_ret = [], [], []
    for l in range(DEPTH):
        m = (sc @ w_mod[l] + b_mod[l])[:, None, :]
        sh1, sc1, g1, sh2, sc2, g2 = jnp.split(m, 6, axis=-1)
        h = rms_norm(x, norm1_w[l]) * (1.0 + sc1) + sh1
        if l % 2 == 0:
            e = l // 2
            s_a = None if states is None else states[0][:, e]
            s_b = None if states is None else states[1][:, e]
            o, sa, sb = even_mixer(h, s_a, s_b, w_in_even[e], gla_w_alpha[e], gla_b_alpha[e], lb_all[e],
                                   gla_norm_w[e], hgrn_norm_w[e], w_out_even[e])
            st_gla.append(sa)
            st_hgrn.append(sb)
        else:
            j = l // 2
            s_c = None if states is None else states[2][:, j]
            o, scs = odd_mixer(h, s_c, pos, w_in_odd[j], w_out_odd[j])
            st_ret.append(scs)
        x = x + g1 * o
        h = rms_norm(x, norm2_w[l]) * (1.0 + sc2) + sh2
        x = x + g2 * hier_moe(h, router_g_w[l], router_g_b[l], router_e_w[l], router_e_b[l],
                              moe_w_gate[l], moe_w_up[l], moe_w_down[l])
    y = rms_norm(x, normf_w)
    return y, jnp.stack(st_gla, axis=1), jnp.stack(st_hgrn, axis=1), jnp.stack(st_ret, axis=1)


def setup_inputs(seed: int = 0) -> dict:
    key = jax.random.key(seed)
    ks = jax.random.split(key, 32)
    nrm = lambda k, shape, s: jax.random.normal(k, shape, F32) * s
    D = D_MODEL
    return {
        'x_prompt': nrm(ks[0], (BATCH, SEQ, D), 1.0),
        'x_sample': nrm(ks[1], (DEC_BATCH, DEC_SEQ, D), 1.0),
        'state_gla': nrm(ks[2], (DEC_BATCH, N_EVEN, 2, GLA_HEADS, GLA_DK, GLA_DV), 0.5),
        'state_hgrn': nrm(ks[3], (DEC_BATCH, N_EVEN, 2, HGRN_HEADS, HGRN_DK, HGRN_DV), 0.5),
        'state_ret': nrm(ks[4], (DEC_BATCH, N_ODD, 2, RET_HEADS, RET_DK, RET_DV), 0.5),
        'c': nrm(ks[5], (DEC_BATCH, D), 1.0),
        'c_ctx': nrm(ks[6], (D,), 1.0),
        'norm1_w': 1.0 + nrm(ks[7], (DEPTH, D), 0.02),
        'norm2_w': 1.0 + nrm(ks[8], (DEPTH, D), 0.02),
        'normf_w': 1.0 + nrm(ks[9], (D,), 0.02),
        'w_mod': nrm(ks[10], (DEPTH, D, 6 * D), 0.5 * D ** -0.5),
        'b_mod': nrm(ks[11], (DEPTH, 6 * D), 0.02),
        'w_in_even': nrm(ks[12], (N_EVEN, D, EVEN_IN), D ** -0.5),
        'gla_w_alpha': nrm(ks[13], (N_EVEN, 2, GLA_RANK, GLA_QK), GLA_RANK ** -0.5),
        'gla_b_alpha': nrm(ks[14], (N_EVEN, 2, GLA_QK), 0.02),
        'hgrn_lb_logits': nrm(ks[15], (N_EVEN + 1, HG_K), 0.1),
        'gla_norm_w': 1.0 + nrm(ks[16], (N_EVEN, GLA_V), 0.02),
        'hgrn_norm_w': 1.0 + nrm(ks[17], (N_EVEN, HG_V), 0.02),
        'w_out_even': nrm(ks[18], (N_EVEN, EVEN_MIX, D), EVEN_MIX ** -0.5),
        'w_in_odd': nrm(ks[19], (N_ODD, D, ODD_IN), D ** -0.5),
        'w_out_odd': nrm(ks[20], (N_ODD, RET_V, D), RET_V ** -0.5),
        'router_g_w': nrm(ks[21], (DEPTH, D, N_GROUPS), D ** -0.5),
        'router_g_b': nrm(ks[22], (DEPTH, N_GROUPS), 0.01),
        'router_e_w': nrm(ks[23], (DEPTH, D, N_EXPERTS), D ** -0.5),
        'router_e_b': nrm(ks[24], (DEPTH, N_EXPERTS), 0.01),
        'moe_w_gate': nrm(ks[25], (DEPTH, N_EXPERTS, D, D_EXPERT), D ** -0.5),
        'moe_w_up': nrm(ks[26], (DEPTH, N_EXPERTS, D, D_EXPERT), D ** -0.5),
        'moe_w_down': nrm(ks[27], (DEPTH, N_EXPERTS, D_EXPERT, D), D_EXPERT ** -0.5),
    }


def reference(x_prompt, x_sample, state_gla, state_hgrn, state_ret, c, c_ctx, norm1_w, norm2_w, normf_w,
              w_mod, b_mod, w_in_even, gla_w_alpha, gla_b_alpha, hgrn_lb_logits, gla_norm_w, hgrn_norm_w,
              w_out_even, w_in_odd, w_out_odd, router_g_w, router_g_b, router_e_w, router_e_b,
              moe_w_gate, moe_w_up, moe_w_down):
    weights = (norm1_w, norm2_w, normf_w, w_mod, b_mod, w_in_even, gla_w_alpha, gla_b_alpha, hgrn_lb_logits,
               gla_norm_w, hgrn_norm_w, w_out_even, w_in_odd, w_out_odd, router_g_w, router_g_b,
               router_e_w, router_e_b, moe_w_gate, moe_w_up, moe_w_down)
    y_prompt, new_gla, new_hgrn, new_ret = trunk(x_prompt, c_ctx[None, :], None, None, *weights)
    rows = x_sample.shape[1] // GRID_W
    row = jnp.repeat(jnp.arange(rows), GRID_W)
    col = jnp.tile(jnp.arange(GRID_W), rows)
    y_sample, _, _, _ = trunk(x_sample, c, (state_gla, state_hgrn, state_ret), (row, col), *weights)
    return (y_prompt, y_sample, new_gla, new_hgrn, new_ret)
```

```python
import functools

import numpy as np
import jax
import jax.numpy as jnp
from jax import lax
from jax.experimental import pallas as pl
from jax.experimental.pallas import tpu as pltpu

F32 = jnp.float32
BF16 = jnp.bfloat16

D = 1024
CTX_B, CTX_L = 32, 256
SMP_B, SMP_L = 4, 4096
DEPTH = 4
N_EVEN, N_ODD = 2, 2
GRID_W = 64
NORM_EPS = 1e-6
T_CTX = CTX_B * CTX_L
T_SMP = SMP_B * SMP_L
T_ALL = T_CTX + T_SMP
N_COND = 1 + SMP_B
COND_ROWS = 8

GLA_HEADS, GLA_DK, GLA_DV, GLA_RANK, GLA_TAU = 4, 64, 128, 16, 16.0
GLA_QK, GLA_V = GLA_HEADS * GLA_DK, GLA_HEADS * GLA_DV
HG_HEADS, HG_DK, HG_DV = 4, 128, 128
HG_K, HG_V = HG_HEADS * HG_DK, HG_HEADS * HG_DV
RET_HEADS, RET_DK, RET_DV = 4, 256, 512
RET_QK, RET_V = RET_HEADS * RET_DK, RET_HEADS * RET_DV
ROPE_BASE, ROPE_PAIRS = 10000.0, 64
N_GROUPS, EXP_PER_GROUP, N_EXPERTS, D_EXPERT = 4, 8, 32, 256

LANES = 128
CHUNK = 64
TL = 256
TM = 256
TM_MOE = 1024
VMEM_LIMIT = 56 * 1024 * 1024

EV_QA, EV_KA, EV_VA, EV_RA, EV_QB, EV_FF, EV_FB, EV_IB, EV_GB, EV_AL = (
    0, 256, 512, 1024, 1536, 2048, 2560, 3072, 3584, 4096)
EV_N = 4096 + LANES
OD_Q, OD_K, OD_V, OD_G = 0, 1024, 2048, 4096
OD_N = 6144
ROUTER_N = LANES


def _dot(a, b):
    return jnp.dot(a, b, preferred_element_type=F32)


def _dot_nt(a, b):
    return lax.dot_general(a, b, (((1,), (1,)), ((), ())), preferred_element_type=F32)


def _dot_tn(a, b):
    return lax.dot_general(a, b, (((0,), (0,)), ((), ())), preferred_element_type=F32)


def _split_bf16(x):
    hi = x.astype(BF16)
    lo = (x - hi.astype(F32)).astype(BF16)
    return hi, lo


def _sigmoid(x):
    return 1.0 / (1.0 + jnp.exp(-x))


def _silu(x):
    return x * _sigmoid(x)


def _params(n_axes):
    return pltpu.CompilerParams(dimension_semantics=("arbitrary",) * n_axes,
                                vmem_limit_bytes=VMEM_LIMIT)


def _cond_of_tile(i, tm):
    n_ctx = T_CTX // tm
    per_seq = SMP_L // tm
    return jnp.where(i < n_ctx, 0, 1 + (i - n_ctx) // per_seq)


def _norm_mod(x, nw, shift, scale):
    var = jnp.mean(x * x, axis=-1, keepdims=True)
    return x * lax.rsqrt(var + NORM_EPS) * nw * (1.0 + scale) + shift


def _mods_kernel(cond_ref, w_ref, b_ref, o_ref):
    c = cond_ref[...]
    a_hi, a_lo = _split_bf16(_silu(c))
    w_hi, w_lo = _split_bf16(w_ref[0])
    o_ref[0] = _dot(a_hi, w_hi) + _dot(a_hi, w_lo) + _dot(a_lo, w_hi) + b_ref[0]


def _mods(cond8, w_mod, b_mod):
    return pl.pallas_call(
        _mods_kernel,
        grid=(DEPTH, 6),
        in_specs=[pl.BlockSpec((COND_ROWS, D), lambda l, j: (0, 0)),
                  pl.BlockSpec((1, D, D), lambda l, j: (l, 0, j)),
                  pl.BlockSpec((1, 1, D), lambda l, j: (l, 0, j))],
        out_specs=pl.BlockSpec((1, COND_ROWS, D), lambda l, j: (l, 0, j)),
        out_shape=jax.ShapeDtypeStruct((DEPTH, COND_ROWS, 6 * D), F32),
        compiler_params=_params(2), name="mods",
    )(cond8, w_mod, b_mod.reshape(DEPTH, 1, 6 * D))


def _inproj_even_kernel(x_ref, mod_ref, nw_ref, w_ref, z_ref):
    m = mod_ref[...]
    h = _norm_mod(x_ref[...], nw_ref[...], m[0:1], m[1:2]).astype(BF16)
    for j, width in ((0, 1024), (1024, 1024), (2048, 1024), (3072, 1024), (EV_AL, LANES)):
        z_ref[:, j:j + width] = _dot(h, w_ref[:, j:j + width])


def _inproj_odd_kernel(x_ref, mod_ref, nw_ref, w_ref, cos_ref, sin_ref, z_ref):
    m = mod_ref[...]
    h = _norm_mod(x_ref[...], nw_ref[...], m[0:1], m[1:2]).astype(BF16)
    for j in range(0, OD_V, 2 * LANES):
        zz = _dot(h, w_ref[:, j:j + 2 * LANES])
        for s in range(2):
            blk = zz[:, s * LANES:(s + 1) * LANES]
            cs = cos_ref[:, s * LANES:(s + 1) * LANES]
            sn = sin_ref[:, s * LANES:(s + 1) * LANES]
            z_ref[:, j + s * LANES:j + (s + 1) * LANES] = (
                blk * cs + pltpu.roll(blk, ROPE_PAIRS, 1) * sn)
    step = 1024
    for j in range(OD_V, OD_N, step):
        z_ref[:, j:j + step] = _dot(h, w_ref[:, j:j + step])


def _inproj_even(x, mods_l, nw, w_bf):
    return pl.pallas_call(
        _inproj_even_kernel,
        grid=(T_ALL // TM,),
        in_specs=[pl.BlockSpec((TM, D), lambda i: (i, 0)),
                  pl.BlockSpec((None, 6, D), lambda i: (_cond_of_tile(i, TM), 0, 0)),
                  pl.BlockSpec((1, D), lambda i: (0, 0)),
                  pl.BlockSpec((D, EV_N), lambda i: (0, 0))],
        out_specs=pl.BlockSpec((TM, EV_N), lambda i: (i, 0)),
        out_shape=jax.ShapeDtypeStruct((T_ALL, EV_N), F32),
        compiler_params=_params(1), name="inproj_even",
    )(x, mods_l, nw, w_bf)


def _rope_block(i):
    n_ctx = T_CTX // TM
    per_seq = SMP_L // TM
    return jnp.where(i < n_ctx, 0, 1 + (i - n_ctx) % per_seq)


def _inproj_odd(x, mods_l, nw, w_bf, cos_t, sin_t):
    return pl.pallas_call(
        _inproj_odd_kernel,
        grid=(T_ALL // TM,),
        in_specs=[pl.BlockSpec((TM, D), lambda i: (i, 0)),
                  pl.BlockSpec((None, 6, D), lambda i: (_cond_of_tile(i, TM), 0, 0)),
                  pl.BlockSpec((1, D), lambda i: (0, 0)),
                  pl.BlockSpec((D, OD_N), lambda i: (0, 0)),
                  pl.BlockSpec((TM, RET_DK), lambda i: (_rope_block(i), 0)),
                  pl.BlockSpec((TM, RET_DK), lambda i: (_rope_block(i), 0))],
        out_specs=pl.BlockSpec((TM, OD_N), lambda i: (i, 0)),
        out_shape=jax.ShapeDtypeStruct((T_ALL, OD_N), F32),
        compiler_params=_params(1), name="inproj_odd",
    )(x, mods_l, nw, w_bf, cos_t, sin_t)


def _scan_work():
    tf, tb, first, last, seq = [], [], [], [], []
    base = 0
    for s, length in enumerate([CTX_L] * CTX_B + [SMP_L] * SMP_B):
        n = length // TL
        for t in range(n):
            tf.append(base + t)
            tb.append(base + n - 1 - t)
            first.append(int(t == 0))
            last.append(int(t == n - 1))
            seq.append(s)
        base += n
    return tuple(np.asarray(a, np.int32) for a in (tf, tb, first, last, seq))


N_WORK = T_ALL // TL


def _gated_chunk(q, k, b, v_list, st_ref, st_idx, head_masks, causal, fwd):
    b_last = b[CHUNK - 1:CHUNK] if fwd else b[0:1]
    b_mid = b[CHUNK // 2:CHUNK // 2 + 1]
    e_q = jnp.exp(b - b_mid)
    e_k = jnp.exp(b_mid - b)
    qi = q * e_q
    ki = k * e_k
    q_in = qi * jnp.exp(b_mid)
    kk = ki * jnp.exp(b_last - b_mid)
    ki_bf = ki.astype(BF16)
    st = st_ref[st_idx]
    st_bf = st.astype(BF16)
    outs = []
    upd = None
    for v, msk in zip(v_list, head_masks):
        if msk is None:
            qi_h, qin_h, kk_h = qi, q_in, kk
        else:
            qi_h = jnp.where(msk, qi, 0.0)
            qin_h = jnp.where(msk, q_in, 0.0)
            kk_h = jnp.where(msk, kk, 0.0)
        a = jnp.where(causal, _dot_nt(qi_h.astype(BF16), ki_bf), 0.0)
        v_bf = v.astype(BF16)
        outs.append(_dot(a.astype(BF16), v_bf) + _dot_nt(qin_h.astype(BF16), st_bf))
        u = _dot_tn(v_bf, kk_h.astype(BF16))
        upd = u if upd is None else upd + u
    st_ref[st_idx] = st * jnp.exp(b_last) + upd
    return outs


def _even_dir(d, qk, va, qb, fg, ib, al, w2_ref, ba_ref, lb_ref, stg, sth, o_ref):
    fwd = d == 0
    r = lax.broadcasted_iota(jnp.int32, (TL, TL), 0)
    c = lax.broadcasted_iota(jnp.int32, (TL, TL), 1)
    same = (r // CHUNK) == (c // CHUNK)
    tri = jnp.where(same & ((c <= r) if fwd else (c >= r)), 1.0, 0.0).astype(BF16)
    rc = lax.broadcasted_iota(jnp.int32, (CHUNK, CHUNK), 0)
    cc = lax.broadcasted_iota(jnp.int32, (CHUNK, CHUNK), 1)
    causal = (cc <= rc) if fwd else (cc >= rc)
    lane = lax.broadcasted_iota(jnp.int32, (CHUNK, LANES), 1)
    pair_masks = [lane < GLA_DK, lane >= GLA_DK]

    y = _dot(al[...].astype(BF16), w2_ref[d]) + ba_ref[d]
    la_a = (jnp.minimum(y, 0.0) - jnp.log1p(jnp.exp(-jnp.abs(y)))) * (1.0 / GLA_TAU)
    f = fg[...]
    lbv = lb_ref[...]
    la_h = jnp.log(lbv + (1.0 - lbv) * _sigmoid(f))
    key_h = (1.0 - lbv) * _sigmoid(-f)
    la = jnp.concatenate([la_a, la_h], axis=1)
    la_hi, la_lo = _split_bf16(la)
    b_all = _dot(tri, la_hi) + _dot(tri, la_lo)
    qa = qk[:, 0:GLA_QK] * (GLA_DK ** -0.5)
    ka = qk[:, GLA_QK:2 * GLA_QK]
    qh = _silu(qb[...])
    vv = va[...]
    iv = ib[...]

    n_chunks = TL // CHUNK
    for ci in range(n_chunks):
        cpos = ci if fwd else n_chunks - 1 - ci
        rows = slice(cpos * CHUNK, (cpos + 1) * CHUNK)
        for p in range(GLA_HEADS // 2):
            ln = slice(p * LANES, (p + 1) * LANES)
            vs = [vv[rows, (2 * p + hh) * GLA_DV:(2 * p + hh + 1) * GLA_DV] for hh in range(2)]
            outs = _gated_chunk(qa[rows, ln], ka[rows, ln], b_all[rows, ln], vs,
                                stg, (d, p), pair_masks, causal, fwd)
            for hh in range(2):
                col = (2 * p + hh) * GLA_DV
                o_ref[rows, col:col + GLA_DV] = outs[hh]
        for h in range(HG_HEADS):
            ln = slice(h * LANES, (h + 1) * LANES)
            bl = slice(GLA_QK + h * LANES, GLA_QK + (h + 1) * LANES)
            outs = _gated_chunk(qh[rows, ln], key_h[rows, ln], b_all[rows, bl], [iv[rows, ln]],
                                sth, (d, h), [None], causal, fwd)
            o_ref[rows, GLA_V + h * HG_DV:GLA_V + (h + 1) * HG_DV] = outs[0]


def _even_scan_kernel(tf_ref, tb_ref, first_ref, last_ref, seq_ref,
                      qk_f, va_f, qb_f, fg_f, ib_f, al_f,
                      qk_b, va_b, qb_b, fg_b, ib_b, al_b,
                      w2_ref, ba_ref, lb_ref, sg_ref, sh_ref,
                      of_ref, ob_ref, stg_ref, sth_ref,
                      stg, sth):
    i = pl.program_id(0)
    is_first = first_ref[i] == 1
    is_last = last_ref[i] == 1
    is_ctx = seq_ref[i] < CTX_B

    @pl.when(jnp.logical_and(is_first, is_ctx))
    def _():
        stg[...] = jnp.zeros_like(stg)
        sth[...] = jnp.zeros_like(sth)

    @pl.when(jnp.logical_and(is_first, jnp.logical_not(is_ctx)))
    def _():
        for d in range(2):
            for p in range(GLA_HEADS // 2):
                stg[d, p] = sg_ref[d, p].T
            for h in range(HG_HEADS):
                sth[d, h] = sh_ref[d, h].T

    _even_dir(0, qk_f, va_f, qb_f, fg_f, ib_f, al_f, w2_ref, ba_ref, lb_ref, stg, sth, of_ref)
    _even_dir(1, qk_b, va_b, qb_b, fg_b, ib_b, al_b, w2_ref, ba_ref, lb_ref, stg, sth, ob_ref)

    @pl.when(jnp.logical_and(is_last, is_ctx))
    def _():
        for d in range(2):
            for p in range(GLA_HEADS // 2):
                stg_ref[d, p] = stg[d, p].T
            for h in range(HG_HEADS):
                sth_ref[d, h] = sth[d, h].T


def _even_scan(z, w2_pad, b_a, lb, s_gla, s_hgrn):
    work = _scan_work()

    def zspec(width, col_block, which):
        return pl.BlockSpec((TL, width), lambda i, tf, tb, fi, la, sq: ((tf, tb)[which][i], col_block))

    def dir_specs(which):
        return [zspec(512, EV_QA // 512, which), zspec(512, EV_VA // 512, which),
                zspec(512, EV_QB // 512, which), zspec(512, (EV_FF, EV_FB)[which] // 512, which),
                zspec(512, EV_IB // 512, which), zspec(LANES, EV_AL // LANES, which)]

    def const(shape):
        return pl.BlockSpec(shape, lambda i, *_: (0,) * len(shape))

    def s_in(shape):
        return pl.BlockSpec((None,) + shape,
                            lambda i, tf, tb, fi, la, sq: (jnp.clip(sq[i] - CTX_B, 0, SMP_B - 1),) + (0,) * len(shape))

    def s_out(shape):
        return pl.BlockSpec((None,) + shape,
                            lambda i, tf, tb, fi, la, sq: (jnp.minimum(sq[i], CTX_B - 1),) + (0,) * len(shape))

    gshape = (2, GLA_HEADS // 2, LANES, LANES)
    hshape = (2, HG_HEADS, HG_DK, HG_DV)
    grid_spec = pltpu.PrefetchScalarGridSpec(
        num_scalar_prefetch=5, grid=(N_WORK,),
        in_specs=dir_specs(0) + dir_specs(1) + [
            const((2, LANES, GLA_QK)), const((2, 1, GLA_QK)), const((1, HG_K)),
            s_in(gshape), s_in(hshape)],
        out_specs=[pl.BlockSpec((TL, D), lambda i, tf, tb, fi, la, sq: (tf[i], 0)),
                   pl.BlockSpec((TL, D), lambda i, tf, tb, fi, la, sq: (tb[i], 0)),
                   s_out(gshape), s_out(hshape)],
        scratch_shapes=[pltpu.VMEM(gshape, F32), pltpu.VMEM(hshape, F32)])
    return pl.pallas_call(
        _even_scan_kernel, grid_spec=grid_spec,
        out_shape=[jax.ShapeDtypeStruct((T_ALL, D), F32), jax.ShapeDtypeStruct((T_ALL, D), F32),
                   jax.ShapeDtypeStruct((CTX_B,) + gshape, F32),
                   jax.ShapeDtypeStruct((CTX_B,) + hshape, F32)],
        compiler_params=_params(1), name="even_scan",
    )(*work, z, z, z, z, z, z, z, z, z, z, z, z, w2_pad, b_a, lb, s_gla, s_hgrn)


RET_HP = 2


def _ret_tables():
    lg = np.log1p(-np.exp2(-5.0 - np.arange(RET_HEADS, dtype=np.float64)))
    pos = np.arange(TL, dtype=np.float64)
    dm = np.zeros((2, RET_HEADS, TL, TL))
    rq = np.zeros((2, RET_HEADS, TL, 1))
    rk = np.zeros((2, RET_HEADS, TL, 1))
    gc = np.zeros((2, RET_HEADS, 1, RET_DV))
    diff = pos[:, None] - pos[None, :]
    kscale = RET_DK ** -0.5
    for h in range(RET_HEADS):
        dm[0, h] = np.where(diff >= 0, np.exp(lg[h] * np.maximum(diff, 0)), 0.0) * kscale
        rq[0, h, :, 0] = np.exp(lg[h] * (pos + 1))
        rk[0, h, :, 0] = np.exp(lg[h] * (TL - 1 - pos)) * kscale
        gc[0, h] = np.exp(lg[h] * TL)
        lb = lg[RET_HEADS - 1 - h]
        dm[1, h] = np.where(diff <= 0, np.exp(lb * np.maximum(-diff, 0)), 0.0) * kscale
        rq[1, h, :, 0] = np.exp(lb * (TL - pos))
        rk[1, h, :, 0] = np.exp(lb * pos) * kscale
        gc[1, h] = np.exp(lb * TL)
    return tuple(jnp.asarray(a, F32) for a in (dm, rq, rk, gc))


def _odd_dir(d, q_ref, k_ref, v_ref, dm_ref, rq_ref, rk_ref, gc_ref, st, o_ref):
    for hh in range(RET_HP):
        kf = k_ref[:, hh * RET_DK:(hh + 1) * RET_DK]
        q = q_ref[:, hh * RET_DK:(hh + 1) * RET_DK].astype(BF16)
        v = v_ref[:, hh * RET_DV:(hh + 1) * RET_DV].astype(BF16)
        a = _dot_nt(q, kf.astype(BF16)) * dm_ref[d, hh]
        s = st[d, hh]
        o = _dot(a.astype(BF16), v) + rq_ref[d, hh] * _dot(q, s.astype(BF16))
        kk = (kf * rk_ref[d, hh]).astype(BF16)
        st[d, hh] = gc_ref[d, hh] * s + _dot_tn(kk, v)
        o_ref[:, hh * RET_DV:(hh + 1) * RET_DV] = o


def _odd_scan_kernel(tf_ref, tb_ref, first_ref, last_ref, seq_ref,
                     q_f, k_f, v_f, q_b, k_b, v_b,
                     dm_ref, rq_ref, rk_ref, gc_ref, s0_ref,
                     of_ref, ob_ref, sout_ref, st):
    i = pl.program_id(1)
    is_first = first_ref[i] == 1
    is_last = last_ref[i] == 1
    is_ctx = seq_ref[i] < CTX_B

    @pl.when(jnp.logical_and(is_first, is_ctx))
    def _():
        st[...] = jnp.zeros_like(st)

    @pl.when(jnp.logical_and(is_first, jnp.logical_not(is_ctx)))
    def _():
        st[...] = s0_ref[...]

    _odd_dir(0, q_f, k_f, v_f, dm_ref, rq_ref, rk_ref, gc_ref, st, of_ref)
    _odd_dir(1, q_b, k_b, v_b, dm_ref, rq_ref, rk_ref, gc_ref, st, ob_ref)

    @pl.when(jnp.logical_and(is_last, is_ctx))
    def _():
        sout_ref[...] = st[...]


def _odd_scan(z, tables, s_ret):
    work = _scan_work()
    dm, rq, rk, gc = tables
    qw, vw = RET_HP * RET_DK, RET_HP * RET_DV

    def zspec(width, col0, which):
        return pl.BlockSpec((TL, width),
                            lambda hp, i, tf, tb, fi, la, sq: ((tf, tb)[which][i], col0 // width + hp))

    def dir_specs(which):
        return [zspec(qw, OD_Q, which), zspec(qw, OD_K, which), zspec(vw, OD_V, which)]

    def table(shape):
        return pl.BlockSpec((2, RET_HP) + shape, lambda hp, i, *_: (0, hp) + (0,) * len(shape))

    sshape = (2, RET_HP, RET_DK, RET_DV)
    grid_spec = pltpu.PrefetchScalarGridSpec(
        num_scalar_prefetch=5, grid=(RET_HEADS // RET_HP, N_WORK),
        in_specs=dir_specs(0) + dir_specs(1) + [
            table((TL, TL)), table((TL, 1)), table((TL, 1)), table((1, RET_DV)),
            pl.BlockSpec((None,) + sshape,
                         lambda hp, i, tf, tb, fi, la, sq:
                         (jnp.clip(sq[i] - CTX_B, 0, SMP_B - 1), 0, hp, 0, 0))],
        out_specs=[pl.BlockSpec((TL, vw), lambda hp, i, tf, tb, fi, la, sq: (tf[i], hp)),
                   pl.BlockSpec((TL, vw), lambda hp, i, tf, tb, fi, la, sq: (tb[i], hp)),
                   pl.BlockSpec((None,) + sshape,
                                lambda hp, i, tf, tb, fi, la, sq:
                                (jnp.minimum(sq[i], CTX_B - 1), 0, hp, 0, 0))],
        scratch_shapes=[pltpu.VMEM(sshape, F32)])
    return pl.pallas_call(
        _odd_scan_kernel, grid_spec=grid_spec,
        out_shape=[jax.ShapeDtypeStruct((T_ALL, RET_V), F32), jax.ShapeDtypeStruct((T_ALL, RET_V), F32),
                   jax.ShapeDtypeStruct((CTX_B, 2, RET_HEADS, RET_DK, RET_DV), F32)],
        compiler_params=_params(2), name="odd_scan",
    )(*work, z, z, z, z, z, z, dm, rq, rk, gc, s_ret)


def _head_rms(o, width):
    parts = []
    for j in range(0, o.shape[1], width):
        blk = o[:, j:j + width]
        parts.append(blk * lax.rsqrt(jnp.mean(blk * blk, axis=-1, keepdims=True) + NORM_EPS))
    return parts


def _route(h2, wr_hi_ref, wr_lo_ref, br_ref):
    h_hi, h_lo = _split_bf16(h2)
    logits = (_dot(h_hi, wr_hi_ref[...]) + _dot(h_hi, wr_lo_ref[...]) + _dot(h_lo, wr_hi_ref[...])
              + br_ref[...])
    lane_i = lax.broadcasted_iota(jnp.int32, logits.shape, 1)
    lane = lane_i.astype(F32)
    neg = -jnp.inf
    big = 1e9
    is_grp = (lane_i >= N_EXPERTS) & (lane_i < N_EXPERTS + N_GROUPS)
    lg = jnp.where(is_grp, logits, neg)
    mg = jnp.max(lg, axis=-1, keepdims=True)
    gsel = jnp.min(jnp.where(lg == mg, lane - N_EXPERTS, big), axis=-1, keepdims=True)
    pg = 1.0 / jnp.sum(jnp.where(is_grp, jnp.exp(lg - mg), 0.0), axis=-1, keepdims=True)
    in_grp = (lane_i < N_EXPERTS) & ((lane_i // EXP_PER_GROUP).astype(F32) == gsel)
    le = jnp.where(in_grp, logits, neg)
    m1 = jnp.max(le, axis=-1, keepdims=True)
    i1 = jnp.min(jnp.where(le == m1, lane, big), axis=-1, keepdims=True)
    le2 = jnp.where(lane == i1, neg, le)
    m2 = jnp.max(le2, axis=-1, keepdims=True)
    i2 = jnp.min(jnp.where(le2 == m2, lane, big), axis=-1, keepdims=True)
    e2 = jnp.exp(m2 - m1)
    w1 = pg / (1.0 + e2)
    w2 = pg * e2 / (1.0 + e2)
    return jnp.where(lane == i1, w1, 0.0) + jnp.where(lane == i2, w2, 0.0)


def _post_tail(x, out, m, nw2_ref, wr_hi_ref, wr_lo_ref, br_ref, x1_ref, h2_ref, comb_ref):
    x1 = x + m[2:3] * out
    x1_ref[...] = x1
    h2 = _norm_mod(x1, nw2_ref[...], m[3:4], m[4:5])
    h2_ref[...] = h2.astype(BF16)
    comb_ref[...] = _route(h2, wr_hi_ref, wr_lo_ref, br_ref)


def _post_even_kernel(of_ref, ob_ref, ra_ref, gb_ref, x_ref, mod_ref, gn_ref, wo_ref,
                      nw2_ref, wr_hi_ref, wr_lo_ref, br_ref, x1_ref, h2_ref, comb_ref):
    o = of_ref[...] + ob_ref[...]
    normed = jnp.concatenate(_head_rms(o, GLA_DV), axis=1)
    gate = jnp.concatenate([_silu(ra_ref[...]), _silu(gb_ref[...])], axis=1)
    mixed = (normed * gn_ref[...] * gate).astype(BF16)
    out = _dot(mixed, wo_ref[...])
    _post_tail(x_ref[...], out, mod_ref[...], nw2_ref, wr_hi_ref, wr_lo_ref, br_ref,
               x1_ref, h2_ref, comb_ref)


def _post_odd_kernel(of_ref, ob_ref, g_ref, x_ref, mod_ref, wo_ref,
                     nw2_ref, wr_hi_ref, wr_lo_ref, br_ref, x1_ref, h2_ref, comb_ref):
    o = of_ref[...] + ob_ref[...]
    normed = jnp.concatenate(_head_rms(o, RET_DV), axis=1)
    mixed = (normed * _silu(g_ref[...])).astype(BF16)
    out = _dot(mixed, wo_ref[...])
    _post_tail(x_ref[...], out, mod_ref[...], nw2_ref, wr_hi_ref, wr_lo_ref, br_ref,
               x1_ref, h2_ref, comb_ref)


def _post_common_specs():
    tile = lambda w: pl.BlockSpec((TM, w), lambda i: (i, 0))
    const = lambda s: pl.BlockSpec(s, lambda i: (0,) * len(s))
    mod = pl.BlockSpec((None, 6, D), lambda i: (_cond_of_tile(i, TM), 0, 0))
    out_specs = [tile(D), tile(D), tile(ROUTER_N)]
    out_shape = [jax.ShapeDtypeStruct((T_ALL, D), F32), jax.ShapeDtypeStruct((T_ALL, D), BF16),
                 jax.ShapeDtypeStruct((T_ALL, ROUTER_N), F32)]
    return tile, const, mod, out_specs, out_shape


def _post_even(o_f, o_b, z, x, mods_l, gn, wo_bf, nw2, wr_hi, wr_lo, br):
    tile, const, mod, out_specs, out_shape = _post_common_specs()
    zcol = lambda c0: pl.BlockSpec((TM, 512), lambda i: (i, c0 // 512))
    return pl.pallas_call(
        _post_even_kernel, grid=(T_ALL // TM,),
        in_specs=[tile(D), tile(D), zcol(EV_RA), zcol(EV_GB), tile(D), mod, const((1, D)),
                  const((D, D)), const((1, D)), const((D, ROUTER_N)), const((D, ROUTER_N)),
                  const((1, ROUTER_N))],
        out_specs=out_specs, out_shape=out_shape,
        compiler_params=_params(1), name="post_even",
    )(o_f, o_b, z, z, x, mods_l, gn, wo_bf, nw2, wr_hi, wr_lo, br)


def _post_odd(o_f, o_b, z, x, mods_l, wo_bf, nw2, wr_hi, wr_lo, br):
    tile, const, mod, out_specs, out_shape = _post_common_specs()
    return pl.pallas_call(
        _post_odd_kernel, grid=(T_ALL // TM,),
        in_specs=[tile(RET_V), tile(RET_V),
                  pl.BlockSpec((TM, RET_V), lambda i: (i, OD_G // RET_V)), tile(D), mod,
                  const((RET_V, D)), const((1, D)), const((D, ROUTER_N)), const((D, ROUTER_N)),
                  const((1, ROUTER_N))],
        out_specs=out_specs, out_shape=out_shape,
        compiler_params=_params(1), name="post_odd",
    )(o_f, o_b, z, x, mods_l, wo_bf, nw2, wr_hi, wr_lo, br)


def _moe_kernel(h_ref, comb_ref, x_ref, mod_ref, wg_ref, wu_ref, wd_ref, o_ref, acc_ref):
    e = pl.program_id(1)

    @pl.when(e == 0)
    def _():
        acc_ref[...] = jnp.zeros_like(acc_ref)

    h = h_ref[...]
    comb = comb_ref[...]
    lane = lax.broadcasted_iota(jnp.int32, comb.shape, 1)
    w_e = jnp.sum(jnp.where(lane == e, comb, 0.0), axis=-1, keepdims=True)
    a = _silu(_dot(h, wg_ref[0])) * _dot(h, wu_ref[0]) * w_e
    acc_ref[...] += _dot(a.astype(BF16), wd_ref[0])

    @pl.when(e == N_EXPERTS - 1)
    def _():
        o_ref[...] = x_ref[...] + mod_ref[5:6] * acc_ref[...]


def _moe(h2, comb, x1, mods_l, wg_bf, wu_bf, wd_bf):
    tile = lambda w: pl.BlockSpec((TM_MOE, w), lambda i, e: (i, 0))
    return pl.pallas_call(
        _moe_kernel, grid=(T_ALL // TM_MOE, N_EXPERTS),
        in_specs=[tile(D), tile(ROUTER_N), tile(D),
                  pl.BlockSpec((None, 6, D), lambda i, e: (_cond_of_tile(i, TM_MOE), 0, 0)),
                  pl.BlockSpec((1, D, D_EXPERT), lambda i, e: (e, 0, 0)),
                  pl.BlockSpec((1, D, D_EXPERT), lambda i, e: (e, 0, 0)),
                  pl.BlockSpec((1, D_EXPERT, D), lambda i, e: (e, 0, 0))],
        out_specs=tile(D),
        out_shape=jax.ShapeDtypeStruct((T_ALL, D), F32),
        scratch_shapes=[pltpu.VMEM((TM_MOE, D), F32)],
        compiler_params=_params(2), name="moe",
    )(h2, comb, x1, mods_l, wg_bf, wu_bf, wd_bf)


def _final_norm_kernel(x_ref, nw_ref, y_ref):
    x = x_ref[...]
    var = jnp.mean(x * x, axis=-1, keepdims=True)
    y_ref[...] = x * lax.rsqrt(var + NORM_EPS) * nw_ref[...]


def _final_norm(x, nw, tile0, n_tok):
    return pl.pallas_call(
        _final_norm_kernel, grid=(n_tok // TM,),
        in_specs=[pl.BlockSpec((TM, D), lambda i: (tile0 + i, 0)),
                  pl.BlockSpec((1, D), lambda i: (0, 0))],
        out_specs=pl.BlockSpec((TM, D), lambda i: (i, 0)),
        out_shape=jax.ShapeDtypeStruct((n_tok, D), F32),
        compiler_params=_params(1), name="final_norm",
    )(x, nw)


def _rope_tables():
    freqs = ROPE_BASE ** (-jnp.arange(ROPE_PAIRS, dtype=F32) / ROPE_PAIRS)
    t = jnp.arange(SMP_L)
    halves_c, halves_s = [], []
    for p in (t // GRID_W, t % GRID_W):
        ang = p.astype(F32)[:, None] * freqs
        cs, sn = jnp.cos(ang), jnp.sin(ang)
        halves_c += [cs, cs]
        halves_s += [-sn, sn]
    cos_t = jnp.concatenate(halves_c, axis=1)
    sin_t = jnp.concatenate(halves_s, axis=1)
    cos_t = jnp.concatenate([jnp.ones((TM, RET_DK), F32), cos_t], axis=0)
    sin_t = jnp.concatenate([jnp.zeros((TM, RET_DK), F32), sin_t], axis=0)
    return cos_t, sin_t


def kernel(x_prompt, x_sample, state_gla, state_hgrn, state_ret, c, c_ctx, norm1_w, norm2_w, normf_w,
           w_mod, b_mod, w_in_even, gla_w_alpha, gla_b_alpha, hgrn_lb_logits, gla_norm_w, hgrn_norm_w,
           w_out_even, w_in_odd, w_out_odd, router_g_w, router_g_b, router_e_w, router_e_b,
           moe_w_gate, moe_w_up, moe_w_down):
    x = jnp.concatenate([x_prompt.reshape(T_CTX, D), x_sample.reshape(T_SMP, D)], axis=0)
    cond8 = jnp.concatenate([c_ctx[None, :], c, jnp.zeros((COND_ROWS - N_COND, D), F32)], axis=0)
    mods = _mods(cond8, w_mod, b_mod).reshape(DEPTH, COND_ROWS, 6, D)
    lb_all = jnp.cumsum(jax.nn.softmax(hgrn_lb_logits.astype(F32), axis=0), axis=0)[:N_EVEN]
    cos_t, sin_t = _rope_tables()
    ret_tables = _ret_tables()

    st_gla, st_hgrn, st_ret = [], [], []
    for l in range(DEPTH):
        mods_l = mods[l]
        nw1 = norm1_w[l].reshape(1, D)
        nw2 = norm2_w[l].reshape(1, D)
        wr = jnp.concatenate([router_e_w[l], router_g_w[l],
                              jnp.zeros((D, ROUTER_N - N_EXPERTS - N_GROUPS), F32)], axis=1)
        wr_hi = wr.astype(BF16)
        wr_lo = (wr - wr_hi.astype(F32)).astype(BF16)
        br = jnp.concatenate([router_e_b[l], router_g_b[l],
                              jnp.zeros((ROUTER_N - N_EXPERTS - N_GROUPS,), F32)]).reshape(1, ROUTER_N)
        if l % 2 == 0:
            e = l // 2
            w = w_in_even[e]
            a0 = 2 * GLA_QK + 2 * GLA_V
            a1 = a0 + 2 * GLA_RANK
            w_bf = jnp.concatenate([w[:, :a0], w[:, a1:], w[:, a0:a1],
                                    jnp.zeros((D, LANES - 2 * GLA_RANK), F32)], axis=1).astype(BF16)
            z = _inproj_even(x, mods_l, nw1, w_bf)
            w2_pad = jnp.zeros((2, LANES, GLA_QK), F32)
            for d in range(2):
                w2_pad = w2_pad.at[d, d * GLA_RANK:(d + 1) * GLA_RANK].set(gla_w_alpha[e, d])
            s_g = state_gla[:, e].reshape(SMP_B, 2, GLA_HEADS // 2, LANES, LANES)
            o_f, o_b, ng, nh = _even_scan(z, w2_pad.astype(BF16), gla_b_alpha[e].reshape(2, 1, GLA_QK),
                                          lb_all[e].reshape(1, HG_K), s_g, state_hgrn[:, e])
            st_gla.append(ng.reshape(CTX_B, 2, GLA_HEADS, GLA_DK, GLA_DV))
            st_hgrn.append(nh)
            gn = jnp.concatenate([gla_norm_w[e], hgrn_norm_w[e]]).reshape(1, D)
            x1, h2, comb = _post_even(o_f, o_b, z, x, mods_l, gn, w_out_even[e].astype(BF16),
                                      nw2, wr_hi, wr_lo, br)
        else:
            j = l // 2
            z = _inproj_odd(x, mods_l, nw1, w_in_odd[j].astype(BF16), cos_t, sin_t)
            o_f, o_b, nr = _odd_scan(z, ret_tables, state_ret[:, j])
            st_ret.append(nr)
            x1, h2, comb = _post_odd(o_f, o_b, z, x, mods_l, w_out_odd[j].astype(BF16),
                                     nw2, wr_hi, wr_lo, br)
        x = _moe(h2, comb, x1, mods_l, moe_w_gate[l].astype(BF16), moe_w_up[l].astype(BF16),
                 moe_w_down[l].astype(BF16))

    nwf = normf_w.reshape(1, D)
    y_prompt = _final_norm(x, nwf, 0, T_CTX).reshape(CTX_B, CTX_L, D)
    y_sample = _final_norm(x, nwf, T_CTX // TM, T_SMP).reshape(SMP_B, SMP_L, D)
    return (y_prompt, y_sample, jnp.stack(st_gla, axis=1), jnp.stack(st_hgrn, axis=1),
            jnp.stack(st_ret, axis=1))
```

```python
import functools

import numpy as np
import jax
import jax.numpy as jnp
from jax import lax
from jax.experimental import pallas as pl
from jax.experimental.pallas import tpu as pltpu

F32 = jnp.float32
BF16 = jnp.bfloat16

D = 1024
CTX_B, CTX_L = 32, 256
SMP_B, SMP_L = 4, 4096
DEPTH = 4
N_EVEN, N_ODD = 2, 2
GRID_W = 64
NORM_EPS = 1e-6
T_CTX = CTX_B * CTX_L
T_SMP = SMP_B * SMP_L
T_ALL = T_CTX + T_SMP
N_COND = 1 + SMP_B
COND_ROWS = 8

GLA_HEADS, GLA_DK, GLA_DV, GLA_RANK, GLA_TAU = 4, 64, 128, 16, 16.0
GLA_QK, GLA_V = GLA_HEADS * GLA_DK, GLA_HEADS * GLA_DV
HG_HEADS, HG_DK, HG_DV = 4, 128, 128
HG_K, HG_V = HG_HEADS * HG_DK, HG_HEADS * HG_DV
RET_HEADS, RET_DK, RET_DV = 4, 256, 512
RET_QK, RET_V = RET_HEADS * RET_DK, RET_HEADS * RET_DV
ROPE_BASE, ROPE_PAIRS = 10000.0, 64
N_GROUPS, EXP_PER_GROUP, N_EXPERTS, D_EXPERT = 4, 8, 32, 256

LANES = 128
CHUNK = 64
TL = 256
TM = 256
PAIRS_PER_GROUP = EXP_PER_GROUP * (EXP_PER_GROUP - 1) // 2
N_PAIRS = N_GROUPS * PAIRS_PER_GROUP
TMG = 128
N_TILES_MOE = T_ALL // TMG + N_PAIRS
ROW_SUB = 16
DISPATCH_ROWS = 512
VMEM_LIMIT = 56 * 1024 * 1024

EV_QA, EV_KA, EV_VA, EV_RA, EV_QB, EV_FF, EV_FB, EV_IB, EV_GB, EV_AL = (
    0, 256, 512, 1024, 1536, 2048, 2560, 3072, 3584, 4096)
EV_N = 4096 + LANES
OD_Q, OD_K, OD_V, OD_G = 0, 1024, 2048, 4096
OD_N = 6144
ROUTER_N = LANES


def _dot(a, b):
    return jnp.dot(a, b, preferred_element_type=F32)


def _dot_nt(a, b):
    return lax.dot_general(a, b, (((1,), (1,)), ((), ())), preferred_element_type=F32)


def _dot_tn(a, b):
    return lax.dot_general(a, b, (((0,), (0,)), ((), ())), preferred_element_type=F32)


def _split_bf16(x):
    hi = x.astype(BF16)
    lo = (x - hi.astype(F32)).astype(BF16)
    return hi, lo


def _sigmoid(x):
    return 1.0 / (1.0 + jnp.exp(-x))


def _silu(x):
    return x * _sigmoid(x)


def _params(n_axes):
    return pltpu.CompilerParams(dimension_semantics=("arbitrary",) * n_axes,
                                vmem_limit_bytes=VMEM_LIMIT)


def _cond_of_tile(i, tm):
    n_ctx = T_CTX // tm
    per_seq = SMP_L // tm
    return jnp.where(i < n_ctx, 0, 1 + (i - n_ctx) // per_seq)


def _norm_mod(x, nw, shift, scale):
    var = jnp.mean(x * x, axis=-1, keepdims=True)
    return x * lax.rsqrt(var + NORM_EPS) * nw * (1.0 + scale) + shift


def _mods_kernel(cond_ref, w_ref, b_ref, o_ref):
    c = cond_ref[...]
    a_hi, a_lo = _split_bf16(_silu(c))
    w_hi, w_lo = _split_bf16(w_ref[0])
    o_ref[0] = _dot(a_hi, w_hi) + _dot(a_hi, w_lo) + _dot(a_lo, w_hi) + b_ref[0]


def _mods(cond8, w_mod, b_mod):
    return pl.pallas_call(
        _mods_kernel,
        grid=(DEPTH, 6),
        in_specs=[pl.BlockSpec((COND_ROWS, D), lambda l, j: (0, 0)),
                  pl.BlockSpec((1, D, D), lambda l, j: (l, 0, j)),
                  pl.BlockSpec((1, 1, D), lambda l, j: (l, 0, j))],
        out_specs=pl.BlockSpec((1, COND_ROWS, D), lambda l, j: (l, 0, j)),
        out_shape=jax.ShapeDtypeStruct((DEPTH, COND_ROWS, 6 * D), F32),
        compiler_params=_params(2), name="mods",
    )(cond8, w_mod, b_mod.reshape(DEPTH, 1, 6 * D))


def _inproj_even_kernel(x_ref, mod_ref, nw_ref, w_ref, z_ref):
    m = mod_ref[...]
    h = _norm_mod(x_ref[...], nw_ref[...], m[0:1], m[1:2]).astype(BF16)
    for j, width in ((0, 1024), (1024, 1024), (2048, 1024), (3072, 1024), (EV_AL, LANES)):
        z_ref[:, j:j + width] = _dot(h, w_ref[:, j:j + width])


def _inproj_odd_kernel(x_ref, mod_ref, nw_ref, w_ref, cos_ref, sin_ref, z_ref):
    m = mod_ref[...]
    h = _norm_mod(x_ref[...], nw_ref[...], m[0:1], m[1:2]).astype(BF16)
    for j in range(0, OD_V, 2 * LANES):
        zz = _dot(h, w_ref[:, j:j + 2 * LANES])
        for s in range(2):
            blk = zz[:, s * LANES:(s + 1) * LANES]
            cs = cos_ref[:, s * LANES:(s + 1) * LANES]
            sn = sin_ref[:, s * LANES:(s + 1) * LANES]
            z_ref[:, j + s * LANES:j + (s + 1) * LANES] = (
                blk * cs + pltpu.roll(blk, ROPE_PAIRS, 1) * sn)
    step = 1024
    for j in range(OD_V, OD_N, step):
        z_ref[:, j:j + step] = _dot(h, w_ref[:, j:j + step])


def _inproj_even(x, mods_l, nw, w_bf):
    return pl.pallas_call(
        _inproj_even_kernel,
        grid=(T_ALL // TM,),
        in_specs=[pl.BlockSpec((TM, D), lambda i: (i, 0)),
                  pl.BlockSpec((None, 6, D), lambda i: (_cond_of_tile(i, TM), 0, 0)),
                  pl.BlockSpec((1, D), lambda i: (0, 0)),
                  pl.BlockSpec((D, EV_N), lambda i: (0, 0))],
        out_specs=pl.BlockSpec((TM, EV_N), lambda i: (i, 0)),
        out_shape=jax.ShapeDtypeStruct((T_ALL, EV_N), F32),
        compiler_params=_params(1), name="inproj_even",
    )(x, mods_l, nw, w_bf)


def _rope_block(i):
    n_ctx = T_CTX // TM
    per_seq = SMP_L // TM
    return jnp.where(i < n_ctx, 0, 1 + (i - n_ctx) % per_seq)


def _inproj_odd(x, mods_l, nw, w_bf, cos_t, sin_t):
    return pl.pallas_call(
        _inproj_odd_kernel,
        grid=(T_ALL // TM,),
        in_specs=[pl.BlockSpec((TM, D), lambda i: (i, 0)),
                  pl.BlockSpec((None, 6, D), lambda i: (_cond_of_tile(i, TM), 0, 0)),
                  pl.BlockSpec((1, D), lambda i: (0, 0)),
                  pl.BlockSpec((D, OD_N), lambda i: (0, 0)),
                  pl.BlockSpec((TM, RET_DK), lambda i: (_rope_block(i), 0)),
                  pl.BlockSpec((TM, RET_DK), lambda i: (_rope_block(i), 0))],
        out_specs=pl.BlockSpec((TM, OD_N), lambda i: (i, 0)),
        out_shape=jax.ShapeDtypeStruct((T_ALL, OD_N), F32),
        compiler_params=_params(1), name="inproj_odd",
    )(x, mods_l, nw, w_bf, cos_t, sin_t)


def _scan_work():
    tf, tb, first, last, seq = [], [], [], [], []
    base = 0
    for s, length in enumerate([CTX_L] * CTX_B + [SMP_L] * SMP_B):
        n = length // TL
        for t in range(n):
            tf.append(base + t)
            tb.append(base + n - 1 - t)
            first.append(int(t == 0))
            last.append(int(t == n - 1))
            seq.append(s)
        base += n
    return tuple(np.asarray(a, np.int32) for a in (tf, tb, first, last, seq))


N_WORK = T_ALL // TL


def _gated_chunk(q, k, b, v_list, st_ref, st_idx, head_masks, causal, fwd):
    b_last = b[CHUNK - 1:CHUNK] if fwd else b[0:1]
    b_mid = b[CHUNK // 2:CHUNK // 2 + 1]
    e_q = jnp.exp(b - b_mid)
    e_k = jnp.exp(b_mid - b)
    qi = q * e_q
    ki = k * e_k
    q_in = qi * jnp.exp(b_mid)
    kk = ki * jnp.exp(b_last - b_mid)
    ki_bf = ki.astype(BF16)
    st = st_ref[st_idx]
    st_bf = st.astype(BF16)
    outs = []
    upd = None
    for v, msk in zip(v_list, head_masks):
        if msk is None:
            qi_h, qin_h, kk_h = qi, q_in, kk
        else:
            qi_h = jnp.where(msk, qi, 0.0)
            qin_h = jnp.where(msk, q_in, 0.0)
            kk_h = jnp.where(msk, kk, 0.0)
        a = jnp.where(causal, _dot_nt(qi_h.astype(BF16), ki_bf), 0.0)
        v_bf = v.astype(BF16)
        outs.append(_dot(a.astype(BF16), v_bf) + _dot_nt(qin_h.astype(BF16), st_bf))
        u = _dot_tn(v_bf, kk_h.astype(BF16))
        upd = u if upd is None else upd + u
    st_ref[st_idx] = st * jnp.exp(b_last) + upd
    return outs


def _even_dir(d, qk, va, qb, fg, ib, al, w2_ref, ba_ref, lb_ref, stg, sth, o_ref):
    fwd = d == 0
    r = lax.broadcasted_iota(jnp.int32, (TL, TL), 0)
    c = lax.broadcasted_iota(jnp.int32, (TL, TL), 1)
    same = (r // CHUNK) == (c // CHUNK)
    tri = jnp.where(same & ((c <= r) if fwd else (c >= r)), 1.0, 0.0).astype(BF16)
    rc = lax.broadcasted_iota(jnp.int32, (CHUNK, CHUNK), 0)
    cc = lax.broadcasted_iota(jnp.int32, (CHUNK, CHUNK), 1)
    causal = (cc <= rc) if fwd else (cc >= rc)
    lane = lax.broadcasted_iota(jnp.int32, (CHUNK, LANES), 1)
    pair_masks = [lane < GLA_DK, lane >= GLA_DK]

    y = _dot(al[...].astype(BF16), w2_ref[d]) + ba_ref[d]
    la_a = (jnp.minimum(y, 0.0) - jnp.log1p(jnp.exp(-jnp.abs(y)))) * (1.0 / GLA_TAU)
    f = fg[...]
    lbv = lb_ref[...]
    la_h = jnp.log(lbv + (1.0 - lbv) * _sigmoid(f))
    key_h = (1.0 - lbv) * _sigmoid(-f)
    la = jnp.concatenate([la_a, la_h], axis=1)
    la_hi, la_lo = _split_bf16(la)
    b_all = _dot(tri, la_hi) + _dot(tri, la_lo)
    qa = qk[:, 0:GLA_QK] * (GLA_DK ** -0.5)
    ka = qk[:, GLA_QK:2 * GLA_QK]
    qh = _silu(qb[...])
    vv = va[...]
    iv = ib[...]

    n_chunks = TL // CHUNK
    for ci in range(n_chunks):
        cpos = ci if fwd else n_chunks - 1 - ci
        rows = slice(cpos * CHUNK, (cpos + 1) * CHUNK)
        for p in range(GLA_HEADS // 2):
            ln = slice(p * LANES, (p + 1) * LANES)
            vs = [vv[rows, (2 * p + hh) * GLA_DV:(2 * p + hh + 1) * GLA_DV] for hh in range(2)]
            outs = _gated_chunk(qa[rows, ln], ka[rows, ln], b_all[rows, ln], vs,
                                stg, (d, p), pair_masks, causal, fwd)
            for hh in range(2):
                col = (2 * p + hh) * GLA_DV
                o_ref[rows, col:col + GLA_DV] = outs[hh]
        for h in range(HG_HEADS):
            ln = slice(h * LANES, (h + 1) * LANES)
            bl = slice(GLA_QK + h * LANES, GLA_QK + (h + 1) * LANES)
            outs = _gated_chunk(qh[rows, ln], key_h[rows, ln], b_all[rows, bl], [iv[rows, ln]],
                                sth, (d, h), [None], causal, fwd)
            o_ref[rows, GLA_V + h * HG_DV:GLA_V + (h + 1) * HG_DV] = outs[0]


def _even_scan_kernel(tf_ref, tb_ref, first_ref, last_ref, seq_ref,
                      qk_f, va_f, qb_f, fg_f, ib_f, al_f,
                      qk_b, va_b, qb_b, fg_b, ib_b, al_b,
                      w2_ref, ba_ref, lb_ref, sg_ref, sh_ref,
                      of_ref, ob_ref, stg_ref, sth_ref,
                      stg, sth):
    i = pl.program_id(0)
    is_first = first_ref[i] == 1
    is_last = last_ref[i] == 1
    is_ctx = seq_ref[i] < CTX_B

    @pl.when(jnp.logical_and(is_first, is_ctx))
    def _():
        stg[...] = jnp.zeros_like(stg)
        sth[...] = jnp.zeros_like(sth)

    @pl.when(jnp.logical_and(is_first, jnp.logical_not(is_ctx)))
    def _():
        for d in range(2):
            for p in range(GLA_HEADS // 2):
                stg[d, p] = sg_ref[d, p].T
            for h in range(HG_HEADS):
                sth[d, h] = sh_ref[d, h].T

    _even_dir(0, qk_f, va_f, qb_f, fg_f, ib_f, al_f, w2_ref, ba_ref, lb_ref, stg, sth, of_ref)
    _even_dir(1, qk_b, va_b, qb_b, fg_b, ib_b, al_b, w2_ref, ba_ref, lb_ref, stg, sth, ob_ref)

    @pl.when(jnp.logical_and(is_last, is_ctx))
    def _():
        for d in range(2):
            for p in range(GLA_HEADS // 2):
                stg_ref[d, p] = stg[d, p].T
            for h in range(HG_HEADS):
                sth_ref[d, h] = sth[d, h].T


def _even_scan(z, w2_pad, b_a, lb, s_gla, s_hgrn):
    work = _scan_work()

    def zspec(width, col_block, which):
        return pl.BlockSpec((TL, width), lambda i, tf, tb, fi, la, sq: ((tf, tb)[which][i], col_block))

    def dir_specs(which):
        return [zspec(512, EV_QA // 512, which), zspec(512, EV_VA // 512, which),
                zspec(512, EV_QB // 512, which), zspec(512, (EV_FF, EV_FB)[which] // 512, which),
                zspec(512, EV_IB // 512, which), zspec(LANES, EV_AL // LANES, which)]

    def const(shape):
        return pl.BlockSpec(shape, lambda i, *_: (0,) * len(shape))

    def s_in(shape):
        return pl.BlockSpec((None,) + shape,
                            lambda i, tf, tb, fi, la, sq: (jnp.clip(sq[i] - CTX_B, 0, SMP_B - 1),) + (0,) * len(shape))

    def s_out(shape):
        return pl.BlockSpec((None,) + shape,
                            lambda i, tf, tb, fi, la, sq: (jnp.minimum(sq[i], CTX_B - 1),) + (0,) * len(shape))

    gshape = (2, GLA_HEADS // 2, LANES, LANES)
    hshape = (2, HG_HEADS, HG_DK, HG_DV)
    grid_spec = pltpu.PrefetchScalarGridSpec(
        num_scalar_prefetch=5, grid=(N_WORK,),
        in_specs=dir_specs(0) + dir_specs(1) + [
            const((2, LANES, GLA_QK)), const((2, 1, GLA_QK)), const((1, HG_K)),
            s_in(gshape), s_in(hshape)],
        out_specs=[pl.BlockSpec((TL, D), lambda i, tf, tb, fi, la, sq: (tf[i], 0)),
                   pl.BlockSpec((TL, D), lambda i, tf, tb, fi, la, sq: (tb[i], 0)),
                   s_out(gshape), s_out(hshape)],
        scratch_shapes=[pltpu.VMEM(gshape, F32), pltpu.VMEM(hshape, F32)])
    return pl.pallas_call(
        _even_scan_kernel, grid_spec=grid_spec,
        out_shape=[jax.ShapeDtypeStruct((T_ALL, D), F32), jax.ShapeDtypeStruct((T_ALL, D), F32),
                   jax.ShapeDtypeStruct((CTX_B,) + gshape, F32),
                   jax.ShapeDtypeStruct((CTX_B,) + hshape, F32)],
        compiler_params=_params(1), name="even_scan",
    )(*work, z, z, z, z, z, z, z, z, z, z, z, z, w2_pad, b_a, lb, s_gla, s_hgrn)


RET_HP = 2


def _ret_tables():
    lg = np.log1p(-np.exp2(-5.0 - np.arange(RET_HEADS, dtype=np.float64)))
    pos = np.arange(TL, dtype=np.float64)
    dm = np.zeros((2, RET_HEADS, TL, TL))
    rq = np.zeros((2, RET_HEADS, TL, 1))
    rk = np.zeros((2, RET_HEADS, TL, 1))
    gc = np.zeros((2, RET_HEADS, 1, RET_DV))
    diff = pos[:, None] - pos[None, :]
    kscale = RET_DK ** -0.5
    for h in range(RET_HEADS):
        dm[0, h] = np.where(diff >= 0, np.exp(lg[h] * np.maximum(diff, 0)), 0.0) * kscale
        rq[0, h, :, 0] = np.exp(lg[h] * (pos + 1))
        rk[0, h, :, 0] = np.exp(lg[h] * (TL - 1 - pos)) * kscale
        gc[0, h] = np.exp(lg[h] * TL)
        lb = lg[RET_HEADS - 1 - h]
        dm[1, h] = np.where(diff <= 0, np.exp(lb * np.maximum(-diff, 0)), 0.0) * kscale
        rq[1, h, :, 0] = np.exp(lb * (TL - pos))
        rk[1, h, :, 0] = np.exp(lb * pos) * kscale
        gc[1, h] = np.exp(lb * TL)
    return tuple(jnp.asarray(a, F32) for a in (dm, rq, rk, gc))


def _odd_dir(d, q_ref, k_ref, v_ref, dm_ref, rq_ref, rk_ref, gc_ref, st, o_ref):
    for hh in range(RET_HP):
        kf = k_ref[:, hh * RET_DK:(hh + 1) * RET_DK]
        q = q_ref[:, hh * RET_DK:(hh + 1) * RET_DK].astype(BF16)
        v = v_ref[:, hh * RET_DV:(hh + 1) * RET_DV].astype(BF16)
        a = _dot_nt(q, kf.astype(BF16)) * dm_ref[d, hh]
        s = st[d, hh]
        o = _dot(a.astype(BF16), v) + rq_ref[d, hh] * _dot(q, s.astype(BF16))
        kk = (kf * rk_ref[d, hh]).astype(BF16)
        st[d, hh] = gc_ref[d, hh] * s + _dot_tn(kk, v)
        o_ref[:, hh * RET_DV:(hh + 1) * RET_DV] = o


def _odd_scan_kernel(tf_ref, tb_ref, first_ref, last_ref, seq_ref,
                     q_f, k_f, v_f, q_b, k_b, v_b,
                     dm_ref, rq_ref, rk_ref, gc_ref, s0_ref,
                     of_ref, ob_ref, sout_ref, st):
    i = pl.program_id(1)
    is_first = first_ref[i] == 1
    is_last = last_ref[i] == 1
    is_ctx = seq_ref[i] < CTX_B

    @pl.when(jnp.logical_and(is_first, is_ctx))
    def _():
        st[...] = jnp.zeros_like(st)

    @pl.when(jnp.logical_and(is_first, jnp.logical_not(is_ctx)))
    def _():
        st[...] = s0_ref[...]

    _odd_dir(0, q_f, k_f, v_f, dm_ref, rq_ref, rk_ref, gc_ref, st, of_ref)
    _odd_dir(1, q_b, k_b, v_b, dm_ref, rq_ref, rk_ref, gc_ref, st, ob_ref)

    @pl.when(jnp.logical_and(is_last, is_ctx))
    def _():
        sout_ref[...] = st[...]


def _odd_scan(z, tables, s_ret):
    work = _scan_work()
    dm, rq, rk, gc = tables
    qw, vw = RET_HP * RET_DK, RET_HP * RET_DV

    def zspec(width, col0, which):
        return pl.BlockSpec((TL, width),
                            lambda hp, i, tf, tb, fi, la, sq: ((tf, tb)[which][i], col0 // width + hp))

    def dir_specs(which):
        return [zspec(qw, OD_Q, which), zspec(qw, OD_K, which), zspec(vw, OD_V, which)]

    def table(shape):
        return pl.BlockSpec((2, RET_HP) + shape, lambda hp, i, *_: (0, hp) + (0,) * len(shape))

    sshape = (2, RET_HP, RET_DK, RET_DV)
    grid_spec = pltpu.PrefetchScalarGridSpec(
        num_scalar_prefetch=5, grid=(RET_HEADS // RET_HP, N_WORK),
        in_specs=dir_specs(0) + dir_specs(1) + [
            table((TL, TL)), table((TL, 1)), table((TL, 1)), table((1, RET_DV)),
            pl.BlockSpec((None,) + sshape,
                         lambda hp, i, tf, tb, fi, la, sq:
                         (jnp.clip(sq[i] - CTX_B, 0, SMP_B - 1), 0, hp, 0, 0))],
        out_specs=[pl.BlockSpec((TL, vw), lambda hp, i, tf, tb, fi, la, sq: (tf[i], hp)),
                   pl.BlockSpec((TL, vw), lambda hp, i, tf, tb, fi, la, sq: (tb[i], hp)),
                   pl.BlockSpec((None,) + sshape,
                                lambda hp, i, tf, tb, fi, la, sq:
                                (jnp.minimum(sq[i], CTX_B - 1), 0, hp, 0, 0))],
        scratch_shapes=[pltpu.VMEM(sshape, F32)])
    return pl.pallas_call(
        _odd_scan_kernel, grid_spec=grid_spec,
        out_shape=[jax.ShapeDtypeStruct((T_ALL, RET_V), F32), jax.ShapeDtypeStruct((T_ALL, RET_V), F32),
                   jax.ShapeDtypeStruct((CTX_B, 2, RET_HEADS, RET_DK, RET_DV), F32)],
        compiler_params=_params(2), name="odd_scan",
    )(*work, z, z, z, z, z, z, dm, rq, rk, gc, s_ret)


def _head_rms(o, width):
    parts = []
    for j in range(0, o.shape[1], width):
        blk = o[:, j:j + width]
        parts.append(blk * lax.rsqrt(jnp.mean(blk * blk, axis=-1, keepdims=True) + NORM_EPS))
    return parts


def _route(h2, wr_hi_ref, wr_lo_ref, br_ref):
    h_hi, h_lo = _split_bf16(h2)
    logits = (_dot(h_hi, wr_hi_ref[...]) + _dot(h_hi, wr_lo_ref[...]) + _dot(h_lo, wr_hi_ref[...])
              + br_ref[...])
    lane_i = lax.broadcasted_iota(jnp.int32, logits.shape, 1)
    lane = lane_i.astype(F32)
    neg = -jnp.inf
    big = 1e9
    is_grp = (lane_i >= N_EXPERTS) & (lane_i < N_EXPERTS + N_GROUPS)
    lg = jnp.where(is_grp, logits, neg)
    mg = jnp.max(lg, axis=-1, keepdims=True)
    gsel = jnp.min(jnp.where(lg == mg, lane - N_EXPERTS, big), axis=-1, keepdims=True)
    pg = 1.0 / jnp.sum(jnp.where(is_grp, jnp.exp(lg - mg), 0.0), axis=-1, keepdims=True)
    in_grp = (lane_i < N_EXPERTS) & ((lane_i // EXP_PER_GROUP).astype(F32) == gsel)
    le = jnp.where(in_grp, logits, neg)
    m1 = jnp.max(le, axis=-1, keepdims=True)
    i1 = jnp.min(jnp.where(le == m1, lane, big), axis=-1, keepdims=True)
    le2 = jnp.where(lane == i1, neg, le)
    m2 = jnp.max(le2, axis=-1, keepdims=True)
    i2 = jnp.min(jnp.where(le2 == m2, lane, big), axis=-1, keepdims=True)
    e2 = jnp.exp(m2 - m1)
    w1 = pg / (1.0 + e2)
    w2 = pg * e2 / (1.0 + e2)
    first_lo = i1 < i2
    ia = jnp.minimum(i1, i2) - gsel * EXP_PER_GROUP
    ib = jnp.maximum(i1, i2) - gsel * EXP_PER_GROUP
    pair = gsel * PAIRS_PER_GROUP + ia * (2 * EXP_PER_GROUP - 1 - ia) * 0.5 + (ib - ia - 1.0)
    return pair, jnp.where(first_lo, w1, w2), jnp.where(first_lo, w2, w1)


def _post_tail(i, x, out, m, nw2_ref, wr_hi_ref, wr_lo_ref, br_ref,
               x1_ref, hrow_ref, meta_ref, cnt_ref, carry):
    x1 = x + m[2:3] * out
    x1_ref[...] = x1
    h2 = _norm_mod(x1, nw2_ref[...], m[3:4], m[4:5])
    pair, w_a, w_b = _route(h2, wr_hi_ref, wr_lo_ref, br_ref)

    @pl.when(i == 0)
    def _():
        carry[...] = jnp.zeros_like(carry)

    lane = lax.broadcasted_iota(jnp.int32, (TM, LANES), 1)
    onehot = jnp.where(lane.astype(F32) == pair, 1.0, 0.0)
    r = lax.broadcasted_iota(jnp.int32, (TM, TM), 0)
    c = lax.broadcasted_iota(jnp.int32, (TM, TM), 1)
    earlier = jnp.where(c < r, 1.0, 0.0).astype(BF16)
    before = _dot(earlier, onehot.astype(BF16)) + carry[...]
    rank = jnp.sum(onehot * before, axis=-1, keepdims=True)
    carry[...] += jnp.sum(onehot, axis=0, keepdims=True)
    cnt_ref[...] = carry[...]
    meta_ref[...] = jnp.where(lane == 0, pair, jnp.where(lane == 1, rank, 0.0))
    for s in range(D // LANES):
        hrow_ref[:, s, :] = h2[:, s * LANES:(s + 1) * LANES]
    hrow_ref[:, D // LANES, :] = jnp.where(lane == 0, w_a, jnp.where(lane == 1, w_b, 0.0))
    for s in range(D // LANES + 1, ROW_SUB):
        hrow_ref[:, s, :] = jnp.zeros((TM, LANES), F32)


def _post_even_kernel(of_ref, ob_ref, ra_ref, gb_ref, x_ref, mod_ref, gn_ref, wo_ref,
                      nw2_ref, wr_hi_ref, wr_lo_ref, br_ref,
                      x1_ref, hrow_ref, meta_ref, cnt_ref, carry):
    o = of_ref[...] + ob_ref[...]
    normed = jnp.concatenate(_head_rms(o, GLA_DV), axis=1)
    gate = jnp.concatenate([_silu(ra_ref[...]), _silu(gb_ref[...])], axis=1)
    mixed = (normed * gn_ref[...] * gate).astype(BF16)
    out = _dot(mixed, wo_ref[...])
    _post_tail(pl.program_id(0), x_ref[...], out, mod_ref[...], nw2_ref, wr_hi_ref, wr_lo_ref, br_ref,
               x1_ref, hrow_ref, meta_ref, cnt_ref, carry)


def _post_odd_kernel(of_ref, ob_ref, g_ref, x_ref, mod_ref, wo_ref,
                     nw2_ref, wr_hi_ref, wr_lo_ref, br_ref,
                     x1_ref, hrow_ref, meta_ref, cnt_ref, carry):
    o = of_ref[...] + ob_ref[...]
    normed = jnp.concatenate(_head_rms(o, RET_DV), axis=1)
    mixed = (normed * _silu(g_ref[...])).astype(BF16)
    out = _dot(mixed, wo_ref[...])
    _post_tail(pl.program_id(0), x_ref[...], out, mod_ref[...], nw2_ref, wr_hi_ref, wr_lo_ref, br_ref,
               x1_ref, hrow_ref, meta_ref, cnt_ref, carry)


def _post_common_specs():
    tile = lambda w: pl.BlockSpec((TM, w), lambda i: (i, 0))
    const = lambda s: pl.BlockSpec(s, lambda i: (0,) * len(s))
    mod = pl.BlockSpec((None, 6, D), lambda i: (_cond_of_tile(i, TM), 0, 0))
    out_specs = [tile(D), pl.BlockSpec((TM, ROW_SUB, LANES), lambda i: (i, 0, 0)), tile(LANES),
                 const((1, LANES))]
    out_shape = [jax.ShapeDtypeStruct((T_ALL, D), F32),
                 jax.ShapeDtypeStruct((T_ALL, ROW_SUB, LANES), F32),
                 jax.ShapeDtypeStruct((T_ALL, LANES), F32),
                 jax.ShapeDtypeStruct((1, LANES), F32)]
    scratch = [pltpu.VMEM((1, LANES), F32)]
    return tile, const, mod, out_specs, out_shape, scratch


def _post_even(o_f, o_b, z, x, mods_l, gn, wo_bf, nw2, wr_hi, wr_lo, br):
    tile, const, mod, out_specs, out_shape, scratch = _post_common_specs()
    zcol = lambda c0: pl.BlockSpec((TM, 512), lambda i: (i, c0 // 512))
    return pl.pallas_call(
        _post_even_kernel, grid=(T_ALL // TM,),
        in_specs=[tile(D), tile(D), zcol(EV_RA), zcol(EV_GB), tile(D), mod, const((1, D)),
                  const((D, D)), const((1, D)), const((D, ROUTER_N)), const((D, ROUTER_N)),
                  const((1, ROUTER_N))],
        out_specs=out_specs, out_shape=out_shape, scratch_shapes=scratch,
        compiler_params=_params(1), name="post_even",
    )(o_f, o_b, z, z, x, mods_l, gn, wo_bf, nw2, wr_hi, wr_lo, br)


def _post_odd(o_f, o_b, z, x, mods_l, wo_bf, nw2, wr_hi, wr_lo, br):
    tile, const, mod, out_specs, out_shape, scratch = _post_common_specs()
    return pl.pallas_call(
        _post_odd_kernel, grid=(T_ALL // TM,),
        in_specs=[tile(RET_V), tile(RET_V),
                  pl.BlockSpec((TM, RET_V), lambda i: (i, OD_G // RET_V)), tile(D), mod,
                  const((RET_V, D)), const((1, D)), const((D, ROUTER_N)), const((D, ROUTER_N)),
                  const((1, ROUTER_N))],
        out_specs=out_specs, out_shape=out_shape, scratch_shapes=scratch,
        compiler_params=_params(1), name="post_odd",
    )(o_f, o_b, z, x, mods_l, wo_bf, nw2, wr_hi, wr_lo, br)


def _pair_tables():
    ea, eb = [], []
    for g in range(N_GROUPS):
        for a in range(EXP_PER_GROUP):
            for b in range(a + 1, EXP_PER_GROUP):
                ea.append(g * EXP_PER_GROUP + a)
                eb.append(g * EXP_PER_GROUP + b)
    pad = LANES - len(ea)
    return (jnp.asarray(ea + [0] * pad, jnp.int32), jnp.asarray(eb + [0] * pad, jnp.int32))


def _dispatch_plan(meta, counts):
    pair = meta[:, 0].astype(jnp.int32)
    rank = meta[:, 1].astype(jnp.int32)
    cnt = counts[0].astype(jnp.int32)
    tiles_g = (cnt + TMG - 1) // TMG
    tile_end = jnp.cumsum(tiles_g)
    tile_start = tile_end - tiles_g
    slot0 = tile_start * TMG
    ids = jnp.arange(LANES, dtype=jnp.int32)
    dest = rank + jnp.sum(jnp.where(pair[:, None] == ids[None, :], slot0[None, :], 0), axis=1)
    n_real = tile_end[-1]
    j = jnp.arange(N_TILES_MOE, dtype=jnp.int32)
    jj = jnp.minimum(j, n_real - 1)
    grp = jnp.sum((jj[:, None] >= tile_end[None, :]).astype(jnp.int32), axis=1)
    ea_t, eb_t = _pair_tables()
    onehot_g = grp[:, None] == ids[None, :]
    pick = lambda v: jnp.sum(jnp.where(onehot_g, v[None, :], 0), axis=1)
    nv = jnp.clip(pick(cnt) - (jj - pick(tile_start)) * TMG, 0, TMG)
    nv = jnp.where(j < n_real, nv, 0)
    return dest, pick(ea_t), pick(eb_t), nv, n_real.reshape(1)


def _dispatch_kernel(dest_ref, h_hbm, hs_hbm, sem):
    j = pl.program_id(0)
    n = pl.num_programs(0)
    base = j * DISPATCH_ROWS
    slot = j % 2

    def issue(r, carry):
        t = base + r
        pltpu.make_async_copy(h_hbm.at[pl.ds(t, 1)], hs_hbm.at[pl.ds(dest_ref[t], 1)],
                              sem.at[slot]).start()
        return carry

    lax.fori_loop(0, DISPATCH_ROWS, issue, 0, unroll=8)

    def wait_batch(s):
        pltpu.make_async_copy(h_hbm.at[pl.ds(0, DISPATCH_ROWS)], hs_hbm.at[pl.ds(0, DISPATCH_ROWS)],
                              sem.at[s]).wait()

    @pl.when(j > 0)
    def _():
        wait_batch(1 - slot)

    @pl.when(j == n - 1)
    def _():
        wait_batch(slot)


def _dispatch(dest, hrow):
    grid_spec = pltpu.PrefetchScalarGridSpec(
        num_scalar_prefetch=1, grid=(T_ALL // DISPATCH_ROWS,),
        in_specs=[pl.BlockSpec(memory_space=pl.ANY)],
        out_specs=pl.BlockSpec(memory_space=pl.ANY),
        scratch_shapes=[pltpu.SemaphoreType.DMA((2,))])
    return pl.pallas_call(
        _dispatch_kernel, grid_spec=grid_spec,
        out_shape=jax.ShapeDtypeStruct((N_TILES_MOE * TMG, ROW_SUB, LANES), F32),
        compiler_params=_params(1), name="moe_dispatch",
    )(dest, hrow)


def _moe_kernel(ea_ref, eb_ref, nv_ref, nreal_ref, hs_ref,
                wga_ref, wua_ref, wda_ref, wgb_ref, wub_ref, wdb_ref, ys_ref):
    j = pl.program_id(0)

    @pl.when(j < nreal_ref[0])
    def _():
        valid = lax.broadcasted_iota(jnp.int32, (TMG, 1), 0) < nv_ref[j]
        h = jnp.concatenate([hs_ref[:, s, :] for s in range(D // LANES)], axis=1)
        h = jnp.where(valid, h, 0.0).astype(BF16)
        wrow = jnp.where(valid, hs_ref[:, D // LANES, :], 0.0)
        w_a, w_b = wrow[:, 0:1], wrow[:, 1:2]
        act_a = _silu(_dot(h, wga_ref[0])) * _dot(h, wua_ref[0]) * w_a
        act_b = _silu(_dot(h, wgb_ref[0])) * _dot(h, wub_ref[0]) * w_b
        y = _dot(act_a.astype(BF16), wda_ref[0]) + _dot(act_b.astype(BF16), wdb_ref[0])
        for s in range(D // LANES):
            ys_ref[:, s, :] = y[:, s * LANES:(s + 1) * LANES]

    @pl.when(j >= nreal_ref[0])
    def _():
        ys_ref[...] = jnp.zeros_like(ys_ref)


def _moe(ea, eb, nv, n_real, hs, wg_bf, wu_bf, wd_bf):
    def wspec(shape, which):
        return pl.BlockSpec((1,) + shape, lambda j, ea, eb, nv, nr: ((ea, eb)[which][j], 0, 0))

    up, down = (D, D_EXPERT), (D_EXPERT, D)
    grid_spec = pltpu.PrefetchScalarGridSpec(
        num_scalar_prefetch=4, grid=(N_TILES_MOE,),
        in_specs=[pl.BlockSpec((TMG, ROW_SUB, LANES),
                               lambda j, ea, eb, nv, nr: (jnp.minimum(j, nr[0] - 1), 0, 0)),
                  wspec(up, 0), wspec(up, 0), wspec(down, 0),
                  wspec(up, 1), wspec(up, 1), wspec(down, 1)],
        out_specs=pl.BlockSpec((TMG, D // LANES, LANES), lambda j, ea, eb, nv, nr: (j, 0, 0)))
    return pl.pallas_call(
        _moe_kernel, grid_spec=grid_spec,
        out_shape=jax.ShapeDtypeStruct((N_TILES_MOE * TMG, D // LANES, LANES), F32),
        compiler_params=_params(1), name="moe",
    )(ea, eb, nv, n_real, hs, wg_bf, wu_bf, wd_bf, wg_bf, wu_bf, wd_bf)


def _combine_kernel(dest_ref, ys_hbm, x1_ref, mod_ref, o_ref, buf, sem):
    j = pl.program_id(0)
    n = pl.num_programs(0)
    slot = j % 2

    def gather(tile, s):
        def issue(r, carry):
            pltpu.make_async_copy(ys_hbm.at[pl.ds(dest_ref[tile * TM + r], 1)],
                                  buf.at[s, pl.ds(r, 1)], sem.at[s]).start()
            return carry
        lax.fori_loop(0, TM, issue, 0, unroll=8)

    @pl.when(j == 0)
    def _():
        gather(0, 0)

    @pl.when(j + 1 < n)
    def _():
        gather(j + 1, 1 - slot)

    pltpu.make_async_copy(ys_hbm.at[pl.ds(0, TM)], buf.at[slot], sem.at[slot]).wait()
    for s in range(D // LANES):
        cols = slice(s * LANES, (s + 1) * LANES)
        o_ref[:, cols] = x1_ref[:, cols] + mod_ref[5:6, cols] * buf[slot, :, s, :]


def _combine(dest, ys, x1, mods_l):
    grid_spec = pltpu.PrefetchScalarGridSpec(
        num_scalar_prefetch=1, grid=(T_ALL // TM,),
        in_specs=[pl.BlockSpec(memory_space=pl.ANY),
                  pl.BlockSpec((TM, D), lambda i, dest: (i, 0)),
                  pl.BlockSpec((None, 6, D), lambda i, dest: (_cond_of_tile(i, TM), 0, 0))],
        out_specs=pl.BlockSpec((TM, D), lambda i, dest: (i, 0)),
        scratch_shapes=[pltpu.VMEM((2, TM, D // LANES, LANES), F32), pltpu.SemaphoreType.DMA((2,))])
    return pl.pallas_call(
        _combine_kernel, grid_spec=grid_spec,
        out_shape=jax.ShapeDtypeStruct((T_ALL, D), F32),
        compiler_params=_params(1), name="moe_combine",
    )(dest, ys, x1, mods_l)


def _final_norm_kernel(x_ref, nw_ref, y_ref):
    x = x_ref[...]
    var = jnp.mean(x * x, axis=-1, keepdims=True)
    y_ref[...] = x * lax.rsqrt(var + NORM_EPS) * nw_ref[...]


def _final_norm(x, nw, tile0, n_tok):
    return pl.pallas_call(
        _final_norm_kernel, grid=(n_tok // TM,),
        in_specs=[pl.BlockSpec((TM, D), lambda i: (tile0 + i, 0)),
                  pl.BlockSpec((1, D), lambda i: (0, 0))],
        out_specs=pl.BlockSpec((TM, D), lambda i: (i, 0)),
        out_shape=jax.ShapeDtypeStruct((n_tok, D), F32),
        compiler_params=_params(1), name="final_norm",
    )(x, nw)


def _rope_tables():
    freqs = ROPE_BASE ** (-jnp.arange(ROPE_PAIRS, dtype=F32) / ROPE_PAIRS)
    t = jnp.arange(SMP_L)
    halves_c, halves_s = [], []
    for p in (t // GRID_W, t % GRID_W):
        ang = p.astype(F32)[:, None] * freqs
        cs, sn = jnp.cos(ang), jnp.sin(ang)
        halves_c += [cs, cs]
        halves_s += [-sn, sn]
    cos_t = jnp.concatenate(halves_c, axis=1)
    sin_t = jnp.concatenate(halves_s, axis=1)
    cos_t = jnp.concatenate([jnp.ones((TM, RET_DK), F32), cos_t], axis=0)
    sin_t = jnp.concatenate([jnp.zeros((TM, RET_DK), F32), sin_t], axis=0)
    return cos_t, sin_t


def kernel(x_prompt, x_sample, state_gla, state_hgrn, state_ret, c, c_ctx, norm1_w, norm2_w, normf_w,
           w_mod, b_mod, w_in_even, gla_w_alpha, gla_b_alpha, hgrn_lb_logits, gla_norm_w, hgrn_norm_w,
           w_out_even, w_in_odd, w_out_odd, router_g_w, router_g_b, router_e_w, router_e_b,
           moe_w_gate, moe_w_up, moe_w_down):
    x = jnp.concatenate([x_prompt.reshape(T_CTX, D), x_sample.reshape(T_SMP, D)], axis=0)
    cond8 = jnp.concatenate([c_ctx[None, :], c, jnp.zeros((COND_ROWS - N_COND, D), F32)], axis=0)
    mods = _mods(cond8, w_mod, b_mod).reshape(DEPTH, COND_ROWS, 6, D)
    lb_all = jnp.cumsum(jax.nn.softmax(hgrn_lb_logits.astype(F32), axis=0), axis=0)[:N_EVEN]
    cos_t, sin_t = _rope_tables()
    ret_tables = _ret_tables()

    st_gla, st_hgrn, st_ret = [], [], []
    for l in range(DEPTH):
        mods_l = mods[l]
        nw1 = norm1_w[l].reshape(1, D)
        nw2 = norm2_w[l].reshape(1, D)
        wr = jnp.concatenate([router_e_w[l], router_g_w[l],
                              jnp.zeros((D, ROUTER_N - N_EXPERTS - N_GROUPS), F32)], axis=1)
        wr_hi = wr.astype(BF16)
        wr_lo = (wr - wr_hi.astype(F32)).astype(BF16)
        br = jnp.concatenate([router_e_b[l], router_g_b[l],
                              jnp.zeros((ROUTER_N - N_EXPERTS - N_GROUPS,), F32)]).reshape(1, ROUTER_N)
        if l % 2 == 0:
            e = l // 2
            w = w_in_even[e]
            a0 = 2 * GLA_QK + 2 * GLA_V
            a1 = a0 + 2 * GLA_RANK
            w_bf = jnp.concatenate([w[:, :a0], w[:, a1:], w[:, a0:a1],
                                    jnp.zeros((D, LANES - 2 * GLA_RANK), F32)], axis=1).astype(BF16)
            z = _inproj_even(x, mods_l, nw1, w_bf)
            w2_pad = jnp.zeros((2, LANES, GLA_QK), F32)
            for d in range(2):
                w2_pad = w2_pad.at[d, d * GLA_RANK:(d + 1) * GLA_RANK].set(gla_w_alpha[e, d])
            s_g = state_gla[:, e].reshape(SMP_B, 2, GLA_HEADS // 2, LANES, LANES)
            o_f, o_b, ng, nh = _even_scan(z, w2_pad.astype(BF16), gla_b_alpha[e].reshape(2, 1, GLA_QK),
                                          lb_all[e].reshape(1, HG_K), s_g, state_hgrn[:, e])
            st_gla.append(ng.reshape(CTX_B, 2, GLA_HEADS, GLA_DK, GLA_DV))
            st_hgrn.append(nh)
            gn = jnp.concatenate([gla_norm_w[e], hgrn_norm_w[e]]).reshape(1, D)
            x1, hrow, meta, counts = _post_even(o_f, o_b, z, x, mods_l, gn, w_out_even[e].astype(BF16),
                                                nw2, wr_hi, wr_lo, br)
        else:
            j = l // 2
            z = _inproj_odd(x, mods_l, nw1, w_in_odd[j].astype(BF16), cos_t, sin_t)
            o_f, o_b, nr = _odd_scan(z, ret_tables, state_ret[:, j])
            st_ret.append(nr)
            x1, hrow, meta, counts = _post_odd(o_f, o_b, z, x, mods_l, w_out_odd[j].astype(BF16),
                                               nw2, wr_hi, wr_lo, br)
        dest, ea, eb, nv, n_real = _dispatch_plan(meta, counts)
        hs = _dispatch(dest, hrow)
        ys = _moe(ea, eb, nv, n_real, hs, moe_w_gate[l].astype(BF16), moe_w_up[l].astype(BF16),
                  moe_w_down[l].astype(BF16))
        x = _combine(dest, ys, x1, mods_l)

    nwf = normf_w.reshape(1, D)
    y_prompt = _final_norm(x, nwf, 0, T_CTX).reshape(CTX_B, CTX_L, D)
    y_sample = _final_norm(x, nwf, T_CTX // TM, T_SMP).reshape(SMP_B, SMP_L, D)
    return (y_prompt, y_sample, jnp.stack(st_gla, axis=1), jnp.stack(st_hgrn, axis=1),
            jnp.stack(st_ret, axis=1))
```

```python
import functools

import numpy as np
import jax
import jax.numpy as jnp
from jax import lax
from jax.experimental import pallas as pl
from jax.experimental.pallas import tpu as pltpu

F32 = jnp.float32
BF16 = jnp.bfloat16

D = 1024
CTX_B, CTX_L = 32, 256
SMP_B, SMP_L = 4, 4096
DEPTH = 4
N_EVEN, N_ODD = 2, 2
GRID_W = 64
NORM_EPS = 1e-6
T_CTX = CTX_B * CTX_L
T_SMP = SMP_B * SMP_L
T_ALL = T_CTX + T_SMP
N_COND = 1 + SMP_B
COND_ROWS = 8

GLA_HEADS, GLA_DK, GLA_DV, GLA_RANK, GLA_TAU = 4, 64, 128, 16, 16.0
GLA_QK, GLA_V = GLA_HEADS * GLA_DK, GLA_HEADS * GLA_DV
HG_HEADS, HG_DK, HG_DV = 4, 128, 128
HG_K, HG_V = HG_HEADS * HG_DK, HG_HEADS * HG_DV
RET_HEADS, RET_DK, RET_DV = 4, 256, 512
RET_QK, RET_V = RET_HEADS * RET_DK, RET_HEADS * RET_DV
ROPE_BASE, ROPE_PAIRS = 10000.0, 64
N_GROUPS, EXP_PER_GROUP, N_EXPERTS, D_EXPERT = 4, 8, 32, 256

LANES = 128
CHUNK = 64
TL = 256
TM = 256
PAIRS_PER_GROUP = EXP_PER_GROUP * (EXP_PER_GROUP - 1) // 2
N_PAIRS = N_GROUPS * PAIRS_PER_GROUP
TMG = 128
N_TILES_MOE = T_ALL // TMG + N_PAIRS
ROW_SUB = 16
DISPATCH_ROWS = 512
VMEM_LIMIT = 56 * 1024 * 1024

EV_QA, EV_KA, EV_VA, EV_RA, EV_QB, EV_FF, EV_FB, EV_IB, EV_GB, EV_AL = (
    0, 256, 512, 1024, 1536, 2048, 2560, 3072, 3584, 4096)
EV_N = 4096 + LANES
OD_Q, OD_K, OD_V, OD_G = 0, 1024, 2048, 4096
OD_N = 6144
ROUTER_N = LANES


def _dot(a, b):
    return jnp.dot(a, b, preferred_element_type=F32)


def _dot_nt(a, b):
    return lax.dot_general(a, b, (((1,), (1,)), ((), ())), preferred_element_type=F32)


def _dot_tn(a, b):
    return lax.dot_general(a, b, (((0,), (0,)), ((), ())), preferred_element_type=F32)


def _split_bf16(x):
    hi = x.astype(BF16)
    lo = (x - hi.astype(F32)).astype(BF16)
    return hi, lo


def _sigmoid(x):
    return 1.0 / (1.0 + jnp.exp(-x))


def _silu(x):
    return x * _sigmoid(x)


def _params(n_axes):
    return pltpu.CompilerParams(dimension_semantics=("arbitrary",) * n_axes,
                                vmem_limit_bytes=VMEM_LIMIT)


def _cond_of_tile(i, tm):
    n_ctx = T_CTX // tm
    per_seq = SMP_L // tm
    return jnp.where(i < n_ctx, 0, 1 + (i - n_ctx) // per_seq)


def _norm_mod(x, nw, shift, scale):
    var = jnp.mean(x * x, axis=-1, keepdims=True)
    return x * lax.rsqrt(var + NORM_EPS) * nw * (1.0 + scale) + shift


def _mods_kernel(cond_ref, w_ref, b_ref, o_ref):
    c = cond_ref[...]
    a_hi, a_lo = _split_bf16(_silu(c))
    w_hi, w_lo = _split_bf16(w_ref[0])
    o_ref[0] = _dot(a_hi, w_hi) + _dot(a_hi, w_lo) + _dot(a_lo, w_hi) + b_ref[0]


def _mods(cond8, w_mod, b_mod):
    return pl.pallas_call(
        _mods_kernel,
        grid=(DEPTH, 6),
        in_specs=[pl.BlockSpec((COND_ROWS, D), lambda l, j: (0, 0)),
                  pl.BlockSpec((1, D, D), lambda l, j: (l, 0, j)),
                  pl.BlockSpec((1, 1, D), lambda l, j: (l, 0, j))],
        out_specs=pl.BlockSpec((1, COND_ROWS, D), lambda l, j: (l, 0, j)),
        out_shape=jax.ShapeDtypeStruct((DEPTH, COND_ROWS, 6 * D), F32),
        compiler_params=_params(2), name="mods",
    )(cond8, w_mod, b_mod.reshape(DEPTH, 1, 6 * D))


def _inproj_even_kernel(x_ref, mod_ref, nw_ref, w_ref, z_ref):
    m = mod_ref[...]
    h = _norm_mod(x_ref[...], nw_ref[...], m[0:1], m[1:2]).astype(BF16)
    for j, width in ((0, 1024), (1024, 1024), (2048, 1024), (3072, 1024), (EV_AL, LANES)):
        z_ref[:, j:j + width] = _dot(h, w_ref[:, j:j + width])


def _inproj_odd_kernel(x_ref, mod_ref, nw_ref, w_ref, cos_ref, sin_ref, z_ref):
    m = mod_ref[...]
    h = _norm_mod(x_ref[...], nw_ref[...], m[0:1], m[1:2]).astype(BF16)
    for j in range(0, OD_V, 2 * LANES):
        zz = _dot(h, w_ref[:, j:j + 2 * LANES])
        for s in range(2):
            blk = zz[:, s * LANES:(s + 1) * LANES]
            cs = cos_ref[:, s * LANES:(s + 1) * LANES]
            sn = sin_ref[:, s * LANES:(s + 1) * LANES]
            z_ref[:, j + s * LANES:j + (s + 1) * LANES] = (
                blk * cs + pltpu.roll(blk, ROPE_PAIRS, 1) * sn)
    step = 1024
    for j in range(OD_V, OD_N, step):
        z_ref[:, j:j + step] = _dot(h, w_ref[:, j:j + step])


def _inproj_even(x, mods_l, nw, w_bf):
    return pl.pallas_call(
        _inproj_even_kernel,
        grid=(T_ALL // TM,),
        in_specs=[pl.BlockSpec((TM, D), lambda i: (i, 0)),
                  pl.BlockSpec((None, 6, D), lambda i: (_cond_of_tile(i, TM), 0, 0)),
                  pl.BlockSpec((1, D), lambda i: (0, 0)),
                  pl.BlockSpec((D, EV_N), lambda i: (0, 0))],
        out_specs=pl.BlockSpec((TM, EV_N), lambda i: (i, 0)),
        out_shape=jax.ShapeDtypeStruct((T_ALL, EV_N), F32),
        compiler_params=_params(1), name="inproj_even",
    )(x, mods_l, nw, w_bf)


def _rope_block(i):
    n_ctx = T_CTX // TM
    per_seq = SMP_L // TM
    return jnp.where(i < n_ctx, 0, 1 + (i - n_ctx) % per_seq)


def _inproj_odd(x, mods_l, nw, w_bf, cos_t, sin_t):
    return pl.pallas_call(
        _inproj_odd_kernel,
        grid=(T_ALL // TM,),
        in_specs=[pl.BlockSpec((TM, D), lambda i: (i, 0)),
                  pl.BlockSpec((None, 6, D), lambda i: (_cond_of_tile(i, TM), 0, 0)),
                  pl.BlockSpec((1, D), lambda i: (0, 0)),
                  pl.BlockSpec((D, OD_N), lambda i: (0, 0)),
                  pl.BlockSpec((TM, RET_DK), lambda i: (_rope_block(i), 0)),
                  pl.BlockSpec((TM, RET_DK), lambda i: (_rope_block(i), 0))],
        out_specs=pl.BlockSpec((TM, OD_N), lambda i: (i, 0)),
        out_shape=jax.ShapeDtypeStruct((T_ALL, OD_N), F32),
        compiler_params=_params(1), name="inproj_odd",
    )(x, mods_l, nw, w_bf, cos_t, sin_t)


def _scan_work():
    tf, tb, first, last, seq = [], [], [], [], []
    base = 0
    for s, length in enumerate([CTX_L] * CTX_B + [SMP_L] * SMP_B):
        n = length // TL
        for t in range(n):
            tf.append(base + t)
            tb.append(base + n - 1 - t)
            first.append(int(t == 0))
            last.append(int(t == n - 1))
            seq.append(s)
        base += n
    return tuple(np.asarray(a, np.int32) for a in (tf, tb, first, last, seq))


N_WORK = T_ALL // TL


def _gated_chunk(q, k, b, v_list, st_ref, st_idx, head_masks, causal, fwd):
    b_last = b[CHUNK - 1:CHUNK] if fwd else b[0:1]
    b_mid = b[CHUNK // 2:CHUNK // 2 + 1]
    e_q = jnp.exp(b - b_mid)
    e_k = jnp.exp(b_mid - b)
    qi = q * e_q
    ki = k * e_k
    q_in = qi * jnp.exp(b_mid)
    kk = ki * jnp.exp(b_last - b_mid)
    ki_bf = ki.astype(BF16)
    st = st_ref[st_idx]
    st_bf = st.astype(BF16)
    outs = []
    upd = None
    for v, msk in zip(v_list, head_masks):
        if msk is None:
            qi_h, qin_h, kk_h = qi, q_in, kk
        else:
            qi_h = jnp.where(msk, qi, 0.0)
            qin_h = jnp.where(msk, q_in, 0.0)
            kk_h = jnp.where(msk, kk, 0.0)
        a = jnp.where(causal, _dot_nt(qi_h.astype(BF16), ki_bf), 0.0)
        v_bf = v.astype(BF16)
        outs.append(_dot(a.astype(BF16), v_bf) + _dot_nt(qin_h.astype(BF16), st_bf))
        u = _dot_tn(v_bf, kk_h.astype(BF16))
        upd = u if upd is None else upd + u
    st_ref[st_idx] = st * jnp.exp(b_last) + upd
    return outs


def _even_dir(d, qk, va, qb, fg, ib, al, w2_ref, ba_ref, lb_ref, stg, sth, o_ref):
    fwd = d == 0
    r = lax.broadcasted_iota(jnp.int32, (TL, TL), 0)
    c = lax.broadcasted_iota(jnp.int32, (TL, TL), 1)
    same = (r // CHUNK) == (c // CHUNK)
    tri = jnp.where(same & ((c <= r) if fwd else (c >= r)), 1.0, 0.0).astype(BF16)
    rc = lax.broadcasted_iota(jnp.int32, (CHUNK, CHUNK), 0)
    cc = lax.broadcasted_iota(jnp.int32, (CHUNK, CHUNK), 1)
    causal = (cc <= rc) if fwd else (cc >= rc)
    lane = lax.broadcasted_iota(jnp.int32, (CHUNK, LANES), 1)
    pair_masks = [lane < GLA_DK, lane >= GLA_DK]

    y = _dot(al[...].astype(BF16), w2_ref[d]) + ba_ref[d]
    la_a = (jnp.minimum(y, 0.0) - jnp.log1p(jnp.exp(-jnp.abs(y)))) * (1.0 / GLA_TAU)
    f = fg[...]
    lbv = lb_ref[...]
    la_h = jnp.log(lbv + (1.0 - lbv) * _sigmoid(f))
    key_h = (1.0 - lbv) * _sigmoid(-f)
    la = jnp.concatenate([la_a, la_h], axis=1)
    la_hi, la_lo = _split_bf16(la)
    b_all = _dot(tri, la_hi) + _dot(tri, la_lo)
    qa = qk[:, 0:GLA_QK] * (GLA_DK ** -0.5)
    ka = qk[:, GLA_QK:2 * GLA_QK]
    qh = _silu(qb[...])
    vv = va[...]
    iv = ib[...]

    n_chunks = TL // CHUNK
    for ci in range(n_chunks):
        cpos = ci if fwd else n_chunks - 1 - ci
        rows = slice(cpos * CHUNK, (cpos + 1) * CHUNK)
        for p in range(GLA_HEADS // 2):
            ln = slice(p * LANES, (p + 1) * LANES)
            vs = [vv[rows, (2 * p + hh) * GLA_DV:(2 * p + hh + 1) * GLA_DV] for hh in range(2)]
            outs = _gated_chunk(qa[rows, ln], ka[rows, ln], b_all[rows, ln], vs,
                                stg, (d, p), pair_masks, causal, fwd)
            for hh in range(2):
                col = (2 * p + hh) * GLA_DV
                o_ref[rows, col:col + GLA_DV] = outs[hh]
        for h in range(HG_HEADS):
            ln = slice(h * LANES, (h + 1) * LANES)
            bl = slice(GLA_QK + h * LANES, GLA_QK + (h + 1) * LANES)
            outs = _gated_chunk(qh[rows, ln], key_h[rows, ln], b_all[rows, bl], [iv[rows, ln]],
                                sth, (d, h), [None], causal, fwd)
            o_ref[rows, GLA_V + h * HG_DV:GLA_V + (h + 1) * HG_DV] = outs[0]


def _even_scan_kernel(tf_ref, tb_ref, first_ref, last_ref, seq_ref,
                      qk_f, va_f, qb_f, fg_f, ib_f, al_f,
                      qk_b, va_b, qb_b, fg_b, ib_b, al_b,
                      w2_ref, ba_ref, lb_ref, sg_ref, sh_ref,
                      of_ref, ob_ref, stg_ref, sth_ref,
                      stg, sth):
    i = pl.program_id(0)
    is_first = first_ref[i] == 1
    is_last = last_ref[i] == 1
    is_ctx = seq_ref[i] < CTX_B

    @pl.when(jnp.logical_and(is_first, is_ctx))
    def _():
        stg[...] = jnp.zeros_like(stg)
        sth[...] = jnp.zeros_like(sth)

    @pl.when(jnp.logical_and(is_first, jnp.logical_not(is_ctx)))
    def _():
        for d in range(2):
            for p in range(GLA_HEADS // 2):
                stg[d, p] = sg_ref[d, p].T
            for h in range(HG_HEADS):
                sth[d, h] = sh_ref[d, h].T

    _even_dir(0, qk_f, va_f, qb_f, fg_f, ib_f, al_f, w2_ref, ba_ref, lb_ref, stg, sth, of_ref)
    _even_dir(1, qk_b, va_b, qb_b, fg_b, ib_b, al_b, w2_ref, ba_ref, lb_ref, stg, sth, ob_ref)

    @pl.when(jnp.logical_and(is_last, is_ctx))
    def _():
        for d in range(2):
            for p in range(GLA_HEADS // 2):
                stg_ref[d, p] = stg[d, p].T
            for h in range(HG_HEADS):
                sth_ref[d, h] = sth[d, h].T


def _even_scan(z, w2_pad, b_a, lb, s_gla, s_hgrn):
    work = _scan_work()

    def zspec(width, col_block, which):
        return pl.BlockSpec((TL, width), lambda i, tf, tb, fi, la, sq: ((tf, tb)[which][i], col_block))

    def dir_specs(which):
        return [zspec(512, EV_QA // 512, which), zspec(512, EV_VA // 512, which),
                zspec(512, EV_QB // 512, which), zspec(512, (EV_FF, EV_FB)[which] // 512, which),
                zspec(512, EV_IB // 512, which), zspec(LANES, EV_AL // LANES, which)]

    def const(shape):
        return pl.BlockSpec(shape, lambda i, *_: (0,) * len(shape))

    def s_in(shape):
        return pl.BlockSpec((None,) + shape,
                            lambda i, tf, tb, fi, la, sq: (jnp.clip(sq[i] - CTX_B, 0, SMP_B - 1),) + (0,) * len(shape))

    def s_out(shape):
        return pl.BlockSpec((None,) + shape,
                            lambda i, tf, tb, fi, la, sq: (jnp.minimum(sq[i], CTX_B - 1),) + (0,) * len(shape))

    gshape = (2, GLA_HEADS // 2, LANES, LANES)
    hshape = (2, HG_HEADS, HG_DK, HG_DV)
    grid_spec = pltpu.PrefetchScalarGridSpec(
        num_scalar_prefetch=5, grid=(N_WORK,),
        in_specs=dir_specs(0) + dir_specs(1) + [
            const((2, LANES, GLA_QK)), const((2, 1, GLA_QK)), const((1, HG_K)),
            s_in(gshape), s_in(hshape)],
        out_specs=[pl.BlockSpec((TL, D), lambda i, tf, tb, fi, la, sq: (tf[i], 0)),
                   pl.BlockSpec((TL, D), lambda i, tf, tb, fi, la, sq: (tb[i], 0)),
                   s_out(gshape), s_out(hshape)],
        scratch_shapes=[pltpu.VMEM(gshape, F32), pltpu.VMEM(hshape, F32)])
    return pl.pallas_call(
        _even_scan_kernel, grid_spec=grid_spec,
        out_shape=[jax.ShapeDtypeStruct((T_ALL, D), F32), jax.ShapeDtypeStruct((T_ALL, D), F32),
                   jax.ShapeDtypeStruct((CTX_B,) + gshape, F32),
                   jax.ShapeDtypeStruct((CTX_B,) + hshape, F32)],
        compiler_params=_params(1), name="even_scan",
    )(*work, z, z, z, z, z, z, z, z, z, z, z, z, w2_pad, b_a, lb, s_gla, s_hgrn)


RET_HP = 2


def _ret_tables():
    lg = np.log1p(-np.exp2(-5.0 - np.arange(RET_HEADS, dtype=np.float64)))
    pos = np.arange(TL, dtype=np.float64)
    dm = np.zeros((2, RET_HEADS, TL, TL))
    rq = np.zeros((2, RET_HEADS, TL, 1))
    rk = np.zeros((2, RET_HEADS, TL, 1))
    gc = np.zeros((2, RET_HEADS, 1, RET_DV))
    diff = pos[:, None] - pos[None, :]
    kscale = RET_DK ** -0.5
    for h in range(RET_HEADS):
        dm[0, h] = np.where(diff >= 0, np.exp(lg[h] * np.maximum(diff, 0)), 0.0) * kscale
        rq[0, h, :, 0] = np.exp(lg[h] * (pos + 1))
        rk[0, h, :, 0] = np.exp(lg[h] * (TL - 1 - pos)) * kscale
        gc[0, h] = np.exp(lg[h] * TL)
        lb = lg[RET_HEADS - 1 - h]
        dm[1, h] = np.where(diff <= 0, np.exp(lb * np.maximum(-diff, 0)), 0.0) * kscale
        rq[1, h, :, 0] = np.exp(lb * (TL - pos))
        rk[1, h, :, 0] = np.exp(lb * pos) * kscale
        gc[1, h] = np.exp(lb * TL)
    return tuple(jnp.asarray(a, F32) for a in (dm, rq, rk, gc))


def _odd_dir(d, q_ref, k_ref, v_ref, dm_ref, rq_ref, rk_ref, gc_ref, st, o_ref):
    for hh in range(RET_HP):
        kf = k_ref[:, hh * RET_DK:(hh + 1) * RET_DK]
        q = q_ref[:, hh * RET_DK:(hh + 1) * RET_DK].astype(BF16)
        v = v_ref[:, hh * RET_DV:(hh + 1) * RET_DV].astype(BF16)
        a = _dot_nt(q, kf.astype(BF16)) * dm_ref[d, hh]
        s = st[d, hh]
        o = _dot(a.astype(BF16), v) + rq_ref[d, hh] * _dot(q, s.astype(BF16))
        kk = (kf * rk_ref[d, hh]).astype(BF16)
        st[d, hh] = gc_ref[d, hh] * s + _dot_tn(kk, v)
        o_ref[:, hh * RET_DV:(hh + 1) * RET_DV] = o


def _odd_scan_kernel(tf_ref, tb_ref, first_ref, last_ref, seq_ref,
                     q_f, k_f, v_f, q_b, k_b, v_b,
                     dm_ref, rq_ref, rk_ref, gc_ref, s0_ref,
                     of_ref, ob_ref, sout_ref, st):
    i = pl.program_id(1)
    is_first = first_ref[i] == 1
    is_last = last_ref[i] == 1
    is_ctx = seq_ref[i] < CTX_B

    @pl.when(jnp.logical_and(is_first, is_ctx))
    def _():
        st[...] = jnp.zeros_like(st)

    @pl.when(jnp.logical_and(is_first, jnp.logical_not(is_ctx)))
    def _():
        st[...] = s0_ref[...]

    _odd_dir(0, q_f, k_f, v_f, dm_ref, rq_ref, rk_ref, gc_ref, st, of_ref)
    _odd_dir(1, q_b, k_b, v_b, dm_ref, rq_ref, rk_ref, gc_ref, st, ob_ref)

    @pl.when(jnp.logical_and(is_last, is_ctx))
    def _():
        sout_ref[...] = st[...]


def _odd_scan(z, tables, s_ret):
    work = _scan_work()
    dm, rq, rk, gc = tables
    qw, vw = RET_HP * RET_DK, RET_HP * RET_DV

    def zspec(width, col0, which):
        return pl.BlockSpec((TL, width),
                            lambda hp, i, tf, tb, fi, la, sq: ((tf, tb)[which][i], col0 // width + hp))

    def dir_specs(which):
        return [zspec(qw, OD_Q, which), zspec(qw, OD_K, which), zspec(vw, OD_V, which)]

    def table(shape):
        return pl.BlockSpec((2, RET_HP) + shape, lambda hp, i, *_: (0, hp) + (0,) * len(shape))

    sshape = (2, RET_HP, RET_DK, RET_DV)
    grid_spec = pltpu.PrefetchScalarGridSpec(
        num_scalar_prefetch=5, grid=(RET_HEADS // RET_HP, N_WORK),
        in_specs=dir_specs(0) + dir_specs(1) + [
            table((TL, TL)), table((TL, 1)), table((TL, 1)), table((1, RET_DV)),
            pl.BlockSpec((None,) + sshape,
                         lambda hp, i, tf, tb, fi, la, sq:
                         (jnp.clip(sq[i] - CTX_B, 0, SMP_B - 1), 0, hp, 0, 0))],
        out_specs=[pl.BlockSpec((TL, vw), lambda hp, i, tf, tb, fi, la, sq: (tf[i], hp)),
                   pl.BlockSpec((TL, vw), lambda hp, i, tf, tb, fi, la, sq: (tb[i], hp)),
                   pl.BlockSpec((None,) + sshape,
                                lambda hp, i, tf, tb, fi, la, sq:
                                (jnp.minimum(sq[i], CTX_B - 1), 0, hp, 0, 0))],
        scratch_shapes=[pltpu.VMEM(sshape, F32)])
    return pl.pallas_call(
        _odd_scan_kernel, grid_spec=grid_spec,
        out_shape=[jax.ShapeDtypeStruct((T_ALL, RET_V), F32), jax.ShapeDtypeStruct((T_ALL, RET_V), F32),
                   jax.ShapeDtypeStruct((CTX_B, 2, RET_HEADS, RET_DK, RET_DV), F32)],
        compiler_params=_params(2), name="odd_scan",
    )(*work, z, z, z, z, z, z, dm, rq, rk, gc, s_ret)


def _head_rms(o, width):
    parts = []
    for j in range(0, o.shape[1], width):
        blk = o[:, j:j + width]
        parts.append(blk * lax.rsqrt(jnp.mean(blk * blk, axis=-1, keepdims=True) + NORM_EPS))
    return parts


def _route(h2, wr_hi_ref, wr_lo_ref, br_ref):
    h_hi, h_lo = _split_bf16(h2)
    logits = (_dot(h_hi, wr_hi_ref[...]) + _dot(h_hi, wr_lo_ref[...]) + _dot(h_lo, wr_hi_ref[...])
              + br_ref[...])
    lane_i = lax.broadcasted_iota(jnp.int32, logits.shape, 1)
    lane = lane_i.astype(F32)
    neg = -jnp.inf
    big = 1e9
    is_grp = (lane_i >= N_EXPERTS) & (lane_i < N_EXPERTS + N_GROUPS)
    lg = jnp.where(is_grp, logits, neg)
    mg = jnp.max(lg, axis=-1, keepdims=True)
    gsel = jnp.min(jnp.where(lg == mg, lane - N_EXPERTS, big), axis=-1, keepdims=True)
    pg = 1.0 / jnp.sum(jnp.where(is_grp, jnp.exp(lg - mg), 0.0), axis=-1, keepdims=True)
    in_grp = (lane_i < N_EXPERTS) & ((lane_i // EXP_PER_GROUP).astype(F32) == gsel)
    le = jnp.where(in_grp, logits, neg)
    m1 = jnp.max(le, axis=-1, keepdims=True)
    i1 = jnp.min(jnp.where(le == m1, lane, big), axis=-1, keepdims=True)
    le2 = jnp.where(lane == i1, neg, le)
    m2 = jnp.max(le2, axis=-1, keepdims=True)
    i2 = jnp.min(jnp.where(le2 == m2, lane, big), axis=-1, keepdims=True)
    e2 = jnp.exp(m2 - m1)
    w1 = pg / (1.0 + e2)
    w2 = pg * e2 / (1.0 + e2)
    first_lo = i1 < i2
    ia = jnp.minimum(i1, i2) - gsel * EXP_PER_GROUP
    ib = jnp.maximum(i1, i2) - gsel * EXP_PER_GROUP
    pair = gsel * PAIRS_PER_GROUP + ia * (2 * EXP_PER_GROUP - 1 - ia) * 0.5 + (ib - ia - 1.0)
    return pair, jnp.where(first_lo, w1, w2), jnp.where(first_lo, w2, w1)


def _post_tail(i, x, out, m, nw2_ref, wr_hi_ref, wr_lo_ref, br_ref,
               x1_ref, hrow_ref, meta_ref, cnt_ref, carry):
    x1 = x + m[2:3] * out
    x1_ref[...] = x1
    h2 = _norm_mod(x1, nw2_ref[...], m[3:4], m[4:5])
    pair, w_a, w_b = _route(h2, wr_hi_ref, wr_lo_ref, br_ref)

    @pl.when(i == 0)
    def _():
        carry[...] = jnp.zeros_like(carry)

    lane = lax.broadcasted_iota(jnp.int32, (TM, LANES), 1)
    onehot = jnp.where(lane.astype(F32) == pair, 1.0, 0.0)
    r = lax.broadcasted_iota(jnp.int32, (TM, TM), 0)
    c = lax.broadcasted_iota(jnp.int32, (TM, TM), 1)
    earlier = jnp.where(c < r, 1.0, 0.0).astype(BF16)
    before = _dot(earlier, onehot.astype(BF16)) + carry[...]
    rank = jnp.sum(onehot * before, axis=-1, keepdims=True)
    carry[...] += jnp.sum(onehot, axis=0, keepdims=True)
    cnt_ref[...] = carry[...]
    meta_ref[...] = jnp.where(lane == 0, pair, jnp.where(lane == 1, rank, 0.0))
    for s in range(D // LANES):
        hrow_ref[:, s, :] = h2[:, s * LANES:(s + 1) * LANES]
    hrow_ref[:, D // LANES, :] = jnp.where(lane == 0, w_a, jnp.where(lane == 1, w_b, 0.0))
    for s in range(D // LANES + 1, ROW_SUB):
        hrow_ref[:, s, :] = jnp.zeros((TM, LANES), F32)


def _post_even_kernel(of_ref, ob_ref, ra_ref, gb_ref, x_ref, mod_ref, gn_ref, wo_ref,
                      nw2_ref, wr_hi_ref, wr_lo_ref, br_ref,
                      x1_ref, hrow_ref, meta_ref, cnt_ref, carry):
    o = of_ref[...] + ob_ref[...]
    normed = jnp.concatenate(_head_rms(o, GLA_DV), axis=1)
    gate = jnp.concatenate([_silu(ra_ref[...]), _silu(gb_ref[...])], axis=1)
    mixed = (normed * gn_ref[...] * gate).astype(BF16)
    out = _dot(mixed, wo_ref[...])
    _post_tail(pl.program_id(0), x_ref[...], out, mod_ref[...], nw2_ref, wr_hi_ref, wr_lo_ref, br_ref,
               x1_ref, hrow_ref, meta_ref, cnt_ref, carry)


def _post_odd_kernel(of_ref, ob_ref, g_ref, x_ref, mod_ref, wo_ref,
                     nw2_ref, wr_hi_ref, wr_lo_ref, br_ref,
                     x1_ref, hrow_ref, meta_ref, cnt_ref, carry):
    o = of_ref[...] + ob_ref[...]
    normed = jnp.concatenate(_head_rms(o, RET_DV), axis=1)
    mixed = (normed * _silu(g_ref[...])).astype(BF16)
    out = _dot(mixed, wo_ref[...])
    _post_tail(pl.program_id(0), x_ref[...], out, mod_ref[...], nw2_ref, wr_hi_ref, wr_lo_ref, br_ref,
               x1_ref, hrow_ref, meta_ref, cnt_ref, carry)


def _post_common_specs():
    tile = lambda w: pl.BlockSpec((TM, w), lambda i: (i, 0))
    const = lambda s: pl.BlockSpec(s, lambda i: (0,) * len(s))
    mod = pl.BlockSpec((None, 6, D), lambda i: (_cond_of_tile(i, TM), 0, 0))
    out_specs = [tile(D), pl.BlockSpec((TM, ROW_SUB, LANES), lambda i: (i, 0, 0)), tile(LANES),
                 const((1, LANES))]
    out_shape = [jax.ShapeDtypeStruct((T_ALL, D), F32),
                 jax.ShapeDtypeStruct((T_ALL, ROW_SUB, LANES), F32),
                 jax.ShapeDtypeStruct((T_ALL, LANES), F32),
                 jax.ShapeDtypeStruct((1, LANES), F32)]
    scratch = [pltpu.VMEM((1, LANES), F32)]
    return tile, const, mod, out_specs, out_shape, scratch


def _post_even(o_f, o_b, z, x, mods_l, gn, wo_bf, nw2, wr_hi, wr_lo, br):
    tile, const, mod, out_specs, out_shape, scratch = _post_common_specs()
    zcol = lambda c0: pl.BlockSpec((TM, 512), lambda i: (i, c0 // 512))
    return pl.pallas_call(
        _post_even_kernel, grid=(T_ALL // TM,),
        in_specs=[tile(D), tile(D), zcol(EV_RA), zcol(EV_GB), tile(D), mod, const((1, D)),
                  const((D, D)), const((1, D)), const((D, ROUTER_N)), const((D, ROUTER_N)),
                  const((1, ROUTER_N))],
        out_specs=out_specs, out_shape=out_shape, scratch_shapes=scratch,
        compiler_params=_params(1), name="post_even",
    )(o_f, o_b, z, z, x, mods_l, gn, wo_bf, nw2, wr_hi, wr_lo, br)


def _post_odd(o_f, o_b, z, x, mods_l, wo_bf, nw2, wr_hi, wr_lo, br):
    tile, const, mod, out_specs, out_shape, scratch = _post_common_specs()
    return pl.pallas_call(
        _post_odd_kernel, grid=(T_ALL // TM,),
        in_specs=[tile(RET_V), tile(RET_V),
                  pl.BlockSpec((TM, RET_V), lambda i: (i, OD_G // RET_V)), tile(D), mod,
                  const((RET_V, D)), const((1, D)), const((D, ROUTER_N)), const((D, ROUTER_N)),
                  const((1, ROUTER_N))],
        out_specs=out_specs, out_shape=out_shape, scratch_shapes=scratch,
        compiler_params=_params(1), name="post_odd",
    )(o_f, o_b, z, x, mods_l, wo_bf, nw2, wr_hi, wr_lo, br)


def _pair_tables():
    ea, eb = [], []
    for g in range(N_GROUPS):
        for a in range(EXP_PER_GROUP):
            for b in range(a + 1, EXP_PER_GROUP):
                ea.append(g * EXP_PER_GROUP + a)
                eb.append(g * EXP_PER_GROUP + b)
    pad = LANES - len(ea)
    return (jnp.asarray(ea + [0] * pad, jnp.int32), jnp.asarray(eb + [0] * pad, jnp.int32))


def _dispatch_plan(meta, counts):
    pair = meta[:, 0].astype(jnp.int32)
    rank = meta[:, 1].astype(jnp.int32)
    cnt = counts[0].astype(jnp.int32)
    tiles_g = (cnt + TMG - 1) // TMG
    tile_end = jnp.cumsum(tiles_g)
    tile_start = tile_end - tiles_g
    slot0 = tile_start * TMG
    ids = jnp.arange(LANES, dtype=jnp.int32)
    dest = rank + jnp.sum(jnp.where(pair[:, None] == ids[None, :], slot0[None, :], 0), axis=1)
    n_real = tile_end[-1]
    j = jnp.arange(N_TILES_MOE, dtype=jnp.int32)
    jj = jnp.minimum(j, n_real - 1)
    grp = jnp.sum((jj[:, None] >= tile_end[None, :]).astype(jnp.int32), axis=1)
    ea_t, eb_t = _pair_tables()
    onehot_g = grp[:, None] == ids[None, :]
    pick = lambda v: jnp.sum(jnp.where(onehot_g, v[None, :], 0), axis=1)
    nv = jnp.clip(pick(cnt) - (jj - pick(tile_start)) * TMG, 0, TMG)
    nv = jnp.where(j < n_real, nv, 0)
    return dest, pick(ea_t), pick(eb_t), nv, n_real.reshape(1)


DISPATCH_STEPS = T_ALL // DISPATCH_ROWS
PAD_TILES_PER_STEP = -(-N_TILES_MOE // DISPATCH_STEPS)


def _dispatch_kernel(dest_ref, nv_ref, nreal_ref, h_ref, hs_hbm, zeros, sem, pad_sem):
    j = pl.program_id(0)
    base = j * DISPATCH_ROWS

    @pl.when(j == 0)
    def _():
        zeros[...] = jnp.zeros_like(zeros)

    def issue(r, carry):
        pltpu.make_async_copy(h_ref.at[pl.ds(r, 1)], hs_hbm.at[pl.ds(dest_ref[base + r], 1)],
                              sem.at[0]).start()
        return carry

    lax.fori_loop(0, DISPATCH_ROWS, issue, 0, unroll=8)

    def pad_row(tile, r):
        return pltpu.make_async_copy(zeros.at[pl.ds(r, 1)], hs_hbm.at[pl.ds(tile * TMG + r, 1)],
                                     pad_sem.at[0])

    def pad_tile(tile):
        return pltpu.make_async_copy(zeros, hs_hbm.at[pl.ds(tile * TMG, TMG)], pad_sem.at[0])

    for k in range(PAD_TILES_PER_STEP):
        tile = j * PAD_TILES_PER_STEP + k
        in_range = tile < N_TILES_MOE
        is_real = tile < nreal_ref[0]

        @pl.when(jnp.logical_and(in_range, is_real))
        def _():
            first = nv_ref[tile]
            lax.fori_loop(first, TMG, lambda r, c: (pad_row(tile, r).start(), c)[1], 0)
            lax.fori_loop(first, TMG, lambda r, c: (pad_row(tile, r).wait(), c)[1], 0)

        @pl.when(jnp.logical_and(in_range, jnp.logical_not(is_real)))
        def _():
            pad_tile(tile).start()
            pad_tile(tile).wait()

    pltpu.make_async_copy(h_ref, hs_hbm.at[pl.ds(0, DISPATCH_ROWS)], sem.at[0]).wait()


def _dispatch(dest, nv, n_real, hrow):
    grid_spec = pltpu.PrefetchScalarGridSpec(
        num_scalar_prefetch=3, grid=(DISPATCH_STEPS,),
        in_specs=[pl.BlockSpec((DISPATCH_ROWS, ROW_SUB, LANES), lambda j, *_: (j, 0, 0))],
        out_specs=pl.BlockSpec(memory_space=pl.ANY),
        scratch_shapes=[pltpu.VMEM((TMG, ROW_SUB, LANES), F32),
                        pltpu.SemaphoreType.DMA((1,)), pltpu.SemaphoreType.DMA((1,))])
    return pl.pallas_call(
        _dispatch_kernel, grid_spec=grid_spec,
        out_shape=jax.ShapeDtypeStruct((N_TILES_MOE * TMG, ROW_SUB, LANES), F32),
        compiler_params=_params(1), name="moe_dispatch",
    )(dest, nv, n_real, hrow)


def _moe_kernel(ea_ref, eb_ref, nreal_ref, hs_ref,
                wga_ref, wua_ref, wda_ref, wgb_ref, wub_ref, wdb_ref, ys_ref):
    j = pl.program_id(0)

    @pl.when(j < nreal_ref[0])
    def _():
        h = jnp.concatenate([hs_ref[:, s, :] for s in range(D // LANES)], axis=1).astype(BF16)
        wrow = hs_ref[:, D // LANES, :]
        w_a, w_b = wrow[:, 0:1], wrow[:, 1:2]
        act_a = _silu(_dot(h, wga_ref[0])) * _dot(h, wua_ref[0]) * w_a
        act_b = _silu(_dot(h, wgb_ref[0])) * _dot(h, wub_ref[0]) * w_b
        y = _dot(act_a.astype(BF16), wda_ref[0]) + _dot(act_b.astype(BF16), wdb_ref[0])
        for s in range(D // LANES):
            ys_ref[:, s, :] = y[:, s * LANES:(s + 1) * LANES]

    @pl.when(j >= nreal_ref[0])
    def _():
        ys_ref[...] = jnp.zeros_like(ys_ref)


def _moe(ea, eb, n_real, hs, wg_bf, wu_bf, wd_bf):
    def wspec(shape, which):
        return pl.BlockSpec((1,) + shape, lambda j, ea, eb, nr: ((ea, eb)[which][j], 0, 0))

    up, down = (D, D_EXPERT), (D_EXPERT, D)
    grid_spec = pltpu.PrefetchScalarGridSpec(
        num_scalar_prefetch=3, grid=(N_TILES_MOE,),
        in_specs=[pl.BlockSpec((TMG, ROW_SUB, LANES),
                               lambda j, ea, eb, nr: (jnp.minimum(j, nr[0] - 1), 0, 0)),
                  wspec(up, 0), wspec(up, 0), wspec(down, 0),
                  wspec(up, 1), wspec(up, 1), wspec(down, 1)],
        out_specs=pl.BlockSpec((TMG, D // LANES, LANES), lambda j, ea, eb, nr: (j, 0, 0)))
    return pl.pallas_call(
        _moe_kernel, grid_spec=grid_spec,
        out_shape=jax.ShapeDtypeStruct((N_TILES_MOE * TMG, D // LANES, LANES), F32),
        compiler_params=_params(1), name="moe",
    )(ea, eb, n_real, hs, wg_bf, wu_bf, wd_bf, wg_bf, wu_bf, wd_bf)


def _combine_kernel(dest_ref, ys_hbm, x1_ref, mod_ref, o_ref, buf, sem):
    j = pl.program_id(0)
    n = pl.num_programs(0)
    slot = j % 2

    def gather(tile, s):
        def issue(r, carry):
            pltpu.make_async_copy(ys_hbm.at[pl.ds(dest_ref[tile * TM + r], 1)],
                                  buf.at[s, pl.ds(r, 1)], sem.at[s]).start()
            return carry
        lax.fori_loop(0, TM, issue, 0, unroll=8)

    @pl.when(j == 0)
    def _():
        gather(0, 0)

    @pl.when(j + 1 < n)
    def _():
        gather(j + 1, 1 - slot)

    pltpu.make_async_copy(ys_hbm.at[pl.ds(0, TM)], buf.at[slot], sem.at[slot]).wait()
    for s in range(D // LANES):
        cols = slice(s * LANES, (s + 1) * LANES)
        o_ref[:, cols] = x1_ref[:, cols] + mod_ref[5:6, cols] * buf[slot, :, s, :]


def _combine(dest, ys, x1, mods_l):
    grid_spec = pltpu.PrefetchScalarGridSpec(
        num_scalar_prefetch=1, grid=(T_ALL // TM,),
        in_specs=[pl.BlockSpec(memory_space=pl.ANY),
                  pl.BlockSpec((TM, D), lambda i, dest: (i, 0)),
                  pl.BlockSpec((None, 6, D), lambda i, dest: (_cond_of_tile(i, TM), 0, 0))],
        out_specs=pl.BlockSpec((TM, D), lambda i, dest: (i, 0)),
        scratch_shapes=[pltpu.VMEM((2, TM, D // LANES, LANES), F32), pltpu.SemaphoreType.DMA((2,))])
    return pl.pallas_call(
        _combine_kernel, grid_spec=grid_spec,
        out_shape=jax.ShapeDtypeStruct((T_ALL, D), F32),
        compiler_params=_params(1), name="moe_combine",
    )(dest, ys, x1, mods_l)


def _final_norm_kernel(x_ref, nw_ref, y_ref):
    x = x_ref[...]
    var = jnp.mean(x * x, axis=-1, keepdims=True)
    y_ref[...] = x * lax.rsqrt(var + NORM_EPS) * nw_ref[...]


def _final_norm(x, nw, tile0, n_tok):
    return pl.pallas_call(
        _final_norm_kernel, grid=(n_tok // TM,),
        in_specs=[pl.BlockSpec((TM, D), lambda i: (tile0 + i, 0)),
                  pl.BlockSpec((1, D), lambda i: (0, 0))],
        out_specs=pl.BlockSpec((TM, D), lambda i: (i, 0)),
        out_shape=jax.ShapeDtypeStruct((n_tok, D), F32),
        compiler_params=_params(1), name="final_norm",
    )(x, nw)


def _rope_tables():
    freqs = ROPE_BASE ** (-jnp.arange(ROPE_PAIRS, dtype=F32) / ROPE_PAIRS)
    t = jnp.arange(SMP_L)
    halves_c, halves_s = [], []
    for p in (t // GRID_W, t % GRID_W):
        ang = p.astype(F32)[:, None] * freqs
        cs, sn = jnp.cos(ang), jnp.sin(ang)
        halves_c += [cs, cs]
        halves_s += [-sn, sn]
    cos_t = jnp.concatenate(halves_c, axis=1)
    sin_t = jnp.concatenate(halves_s, axis=1)
    cos_t = jnp.concatenate([jnp.ones((TM, RET_DK), F32), cos_t], axis=0)
    sin_t = jnp.concatenate([jnp.zeros((TM, RET_DK), F32), sin_t], axis=0)
    return cos_t, sin_t


def kernel(x_prompt, x_sample, state_gla, state_hgrn, state_ret, c, c_ctx, norm1_w, norm2_w, normf_w,
           w_mod, b_mod, w_in_even, gla_w_alpha, gla_b_alpha, hgrn_lb_logits, gla_norm_w, hgrn_norm_w,
           w_out_even, w_in_odd, w_out_odd, router_g_w, router_g_b, router_e_w, router_e_b,
           moe_w_gate, moe_w_up, moe_w_down):
    x = jnp.concatenate([x_prompt.reshape(T_CTX, D), x_sample.reshape(T_SMP, D)], axis=0)
    cond8 = jnp.concatenate([c_ctx[None, :], c, jnp.zeros((COND_ROWS - N_COND, D), F32)], axis=0)
    mods = _mods(cond8, w_mod, b_mod).reshape(DEPTH, COND_ROWS, 6, D)
    lb_all = jnp.cumsum(jax.nn.softmax(hgrn_lb_logits.astype(F32), axis=0), axis=0)[:N_EVEN]
    cos_t, sin_t = _rope_tables()
    ret_tables = _ret_tables()

    st_gla, st_hgrn, st_ret = [], [], []
    for l in range(DEPTH):
        mods_l = mods[l]
        nw1 = norm1_w[l].reshape(1, D)
        nw2 = norm2_w[l].reshape(1, D)
        wr = jnp.concatenate([router_e_w[l], router_g_w[l],
                              jnp.zeros((D, ROUTER_N - N_EXPERTS - N_GROUPS), F32)], axis=1)
        wr_hi = wr.astype(BF16)
        wr_lo = (wr - wr_hi.astype(F32)).astype(BF16)
        br = jnp.concatenate([router_e_b[l], router_g_b[l],
                              jnp.zeros((ROUTER_N - N_EXPERTS - N_GROUPS,), F32)]).reshape(1, ROUTER_N)
        if l % 2 == 0:
            e = l // 2
            w = w_in_even[e]
            a0 = 2 * GLA_QK + 2 * GLA_V
            a1 = a0 + 2 * GLA_RANK
            w_bf = jnp.concatenate([w[:, :a0], w[:, a1:], w[:, a0:a1],
                                    jnp.zeros((D, LANES - 2 * GLA_RANK), F32)], axis=1).astype(BF16)
            z = _inproj_even(x, mods_l, nw1, w_bf)
            w2_pad = jnp.zeros((2, LANES, GLA_QK), F32)
            for d in range(2):
                w2_pad = w2_pad.at[d, d * GLA_RANK:(d + 1) * GLA_RANK].set(gla_w_alpha[e, d])
            s_g = state_gla[:, e].reshape(SMP_B, 2, GLA_HEADS // 2, LANES, LANES)
            o_f, o_b, ng, nh = _even_scan(z, w2_pad.astype(BF16), gla_b_alpha[e].reshape(2, 1, GLA_QK),
                                          lb_all[e].reshape(1, HG_K), s_g, state_hgrn[:, e])
            st_gla.append(ng.reshape(CTX_B, 2, GLA_HEADS, GLA_DK, GLA_DV))
            st_hgrn.append(nh)
            gn = jnp.concatenate([gla_norm_w[e], hgrn_norm_w[e]]).reshape(1, D)
            x1, hrow, meta, counts = _post_even(o_f, o_b, z, x, mods_l, gn, w_out_even[e].astype(BF16),
                                                nw2, wr_hi, wr_lo, br)
        else:
            j = l // 2
            z = _inproj_odd(x, mods_l, nw1, w_in_odd[j].astype(BF16), cos_t, sin_t)
            o_f, o_b, nr = _odd_scan(z, ret_tables, state_ret[:, j])
            st_ret.append(nr)
            x1, hrow, meta, counts = _post_odd(o_f, o_b, z, x, mods_l, w_out_odd[j].astype(BF16),
                                               nw2, wr_hi, wr_lo, br)
        dest, ea, eb, nv, n_real = _dispatch_plan(meta, counts)
        hs = _dispatch(dest, nv, n_real, hrow)
        ys = _moe(ea, eb, n_real, hs, moe_w_gate[l].astype(BF16), moe_w_up[l].astype(BF16),
                  moe_w_down[l].astype(BF16))
        x = _combine(dest, ys, x1, mods_l)

    nwf = normf_w.reshape(1, D)
    y_prompt = _final_norm(x, nwf, 0, T_CTX).reshape(CTX_B, CTX_L, D)
    y_sample = _final_norm(x, nwf, T_CTX // TM, T_SMP).reshape(SMP_B, SMP_L, D)
    return (y_prompt, y_sample, jnp.stack(st_gla, axis=1), jnp.stack(st_hgrn, axis=1),
            jnp.stack(st_ret, axis=1))
```

```python
import functools

import numpy as np
import jax
import jax.numpy as jnp
from jax import lax
from jax.experimental import pallas as pl
from jax.experimental.pallas import tpu as pltpu

F32 = jnp.float32
BF16 = jnp.bfloat16

D = 1024
CTX_B, CTX_L = 32, 256
SMP_B, SMP_L = 4, 4096
DEPTH = 4
N_EVEN, N_ODD = 2, 2
GRID_W = 64
NORM_EPS = 1e-6
T_CTX = CTX_B * CTX_L
T_SMP = SMP_B * SMP_L
T_ALL = T_CTX + T_SMP
N_COND = 1 + SMP_B
COND_ROWS = 8

GLA_HEADS, GLA_DK, GLA_DV, GLA_RANK, GLA_TAU = 4, 64, 128, 16, 16.0
GLA_QK, GLA_V = GLA_HEADS * GLA_DK, GLA_HEADS * GLA_DV
HG_HEADS, HG_DK, HG_DV = 4, 128, 128
HG_K, HG_V = HG_HEADS * HG_DK, HG_HEADS * HG_DV
RET_HEADS, RET_DK, RET_DV = 4, 256, 512
RET_QK, RET_V = RET_HEADS * RET_DK, RET_HEADS * RET_DV
ROPE_BASE, ROPE_PAIRS = 10000.0, 64
N_GROUPS, EXP_PER_GROUP, N_EXPERTS, D_EXPERT = 4, 8, 32, 256

LANES = 128
CHUNK = 64
TL = 256
TM = 256
PAIRS_PER_GROUP = EXP_PER_GROUP * (EXP_PER_GROUP - 1) // 2
N_PAIRS = N_GROUPS * PAIRS_PER_GROUP
TMG = 128
N_TILES_MOE = T_ALL // TMG + N_PAIRS
ROW_W = D + LANES
DISPATCH_ROWS = 1024
VMEM_LIMIT = 56 * 1024 * 1024

EV_QA, EV_KA, EV_VA, EV_RA, EV_QB, EV_FF, EV_FB, EV_IB, EV_GB, EV_AL = (
    0, 256, 512, 1024, 1536, 2048, 2560, 3072, 3584, 4096)
EV_N = 4096 + LANES
OD_Q, OD_K, OD_V, OD_G = 0, 1024, 2048, 4096
OD_N = 6144
ROUTER_N = LANES


def _dot(a, b):
    return jnp.dot(a, b, preferred_element_type=F32)


def _dot_nt(a, b):
    return lax.dot_general(a, b, (((1,), (1,)), ((), ())), preferred_element_type=F32)


def _dot_tn(a, b):
    return lax.dot_general(a, b, (((0,), (0,)), ((), ())), preferred_element_type=F32)


def _split_bf16(x):
    hi = x.astype(BF16)
    lo = (x - hi.astype(F32)).astype(BF16)
    return hi, lo


def _sigmoid(x):
    return 1.0 / (1.0 + jnp.exp(-x))


def _silu(x):
    return x * _sigmoid(x)


def _params(n_axes):
    return pltpu.CompilerParams(dimension_semantics=("arbitrary",) * n_axes,
                                vmem_limit_bytes=VMEM_LIMIT)


def _cond_of_tile(i, tm):
    n_ctx = T_CTX // tm
    per_seq = SMP_L // tm
    return jnp.where(i < n_ctx, 0, 1 + (i - n_ctx) // per_seq)


def _norm_mod(x, nw, shift, scale):
    var = jnp.mean(x * x, axis=-1, keepdims=True)
    return x * lax.rsqrt(var + NORM_EPS) * nw * (1.0 + scale) + shift


def _mods_kernel(cond_ref, w_ref, b_ref, o_ref):
    c = cond_ref[...]
    a_hi, a_lo = _split_bf16(_silu(c))
    w_hi, w_lo = _split_bf16(w_ref[0])
    o_ref[0] = _dot(a_hi, w_hi) + _dot(a_hi, w_lo) + _dot(a_lo, w_hi) + b_ref[0]


def _mods(cond8, w_mod, b_mod):
    return pl.pallas_call(
        _mods_kernel,
        grid=(DEPTH, 6),
        in_specs=[pl.BlockSpec((COND_ROWS, D), lambda l, j: (0, 0)),
                  pl.BlockSpec((1, D, D), lambda l, j: (l, 0, j)),
                  pl.BlockSpec((1, 1, D), lambda l, j: (l, 0, j))],
        out_specs=pl.BlockSpec((1, COND_ROWS, D), lambda l, j: (l, 0, j)),
        out_shape=jax.ShapeDtypeStruct((DEPTH, COND_ROWS, 6 * D), F32),
        compiler_params=_params(2), name="mods",
    )(cond8, w_mod, b_mod.reshape(DEPTH, 1, 6 * D))


def _inproj_even_kernel(x_ref, mod_ref, nw_ref, w_ref, z_ref):
    m = mod_ref[...]
    h = _norm_mod(x_ref[...], nw_ref[...], m[0:1], m[1:2]).astype(BF16)
    for j, width in ((0, 1024), (1024, 1024), (2048, 1024), (3072, 1024), (EV_AL, LANES)):
        z_ref[:, j:j + width] = _dot(h, w_ref[:, j:j + width])


def _inproj_odd_kernel(x_ref, mod_ref, nw_ref, w_ref, cos_ref, sin_ref, z_ref):
    m = mod_ref[...]
    h = _norm_mod(x_ref[...], nw_ref[...], m[0:1], m[1:2]).astype(BF16)
    for j in range(0, OD_V, 2 * LANES):
        zz = _dot(h, w_ref[:, j:j + 2 * LANES])
        for s in range(2):
            blk = zz[:, s * LANES:(s + 1) * LANES]
            cs = cos_ref[:, s * LANES:(s + 1) * LANES]
            sn = sin_ref[:, s * LANES:(s + 1) * LANES]
            z_ref[:, j + s * LANES:j + (s + 1) * LANES] = (
                blk * cs + pltpu.roll(blk, ROPE_PAIRS, 1) * sn).astype(BF16)
    step = 1024
    for j in range(OD_V, OD_N, step):
        z_ref[:, j:j + step] = _dot(h, w_ref[:, j:j + step]).astype(BF16)


def _inproj_even(x, mods_l, nw, w_bf):
    return pl.pallas_call(
        _inproj_even_kernel,
        grid=(T_ALL // TM,),
        in_specs=[pl.BlockSpec((TM, D), lambda i: (i, 0)),
                  pl.BlockSpec((None, 6, D), lambda i: (_cond_of_tile(i, TM), 0, 0)),
                  pl.BlockSpec((1, D), lambda i: (0, 0)),
                  pl.BlockSpec((D, EV_N), lambda i: (0, 0))],
        out_specs=pl.BlockSpec((TM, EV_N), lambda i: (i, 0)),
        out_shape=jax.ShapeDtypeStruct((T_ALL, EV_N), F32),
        compiler_params=_params(1), name="inproj_even",
    )(x, mods_l, nw, w_bf)


def _rope_block(i):
    n_ctx = T_CTX // TM
    per_seq = SMP_L // TM
    return jnp.where(i < n_ctx, 0, 1 + (i - n_ctx) % per_seq)


def _inproj_odd(x, mods_l, nw, w_bf, cos_t, sin_t):
    return pl.pallas_call(
        _inproj_odd_kernel,
        grid=(T_ALL // TM,),
        in_specs=[pl.BlockSpec((TM, D), lambda i: (i, 0)),
                  pl.BlockSpec((None, 6, D), lambda i: (_cond_of_tile(i, TM), 0, 0)),
                  pl.BlockSpec((1, D), lambda i: (0, 0)),
                  pl.BlockSpec((D, OD_N), lambda i: (0, 0)),
                  pl.BlockSpec((TM, RET_DK), lambda i: (_rope_block(i), 0)),
                  pl.BlockSpec((TM, RET_DK), lambda i: (_rope_block(i), 0))],
        out_specs=pl.BlockSpec((TM, OD_N), lambda i: (i, 0)),
        out_shape=jax.ShapeDtypeStruct((T_ALL, OD_N), BF16),
        compiler_params=_params(1), name="inproj_odd",
    )(x, mods_l, nw, w_bf, cos_t, sin_t)


def _scan_work():
    tf, tb, first, last, seq = [], [], [], [], []
    base = 0
    for s, length in enumerate([CTX_L] * CTX_B + [SMP_L] * SMP_B):
        n = length // TL
        for t in range(n):
            tf.append(base + t)
            tb.append(base + n - 1 - t)
            first.append(int(t == 0))
            last.append(int(t == n - 1))
            seq.append(s)
        base += n
    return tuple(np.asarray(a, np.int32) for a in (tf, tb, first, last, seq))


N_WORK = T_ALL // TL


def _gated_chunk(q, k, b, v_list, st_ref, st_idx, head_masks, causal, fwd):
    b_last = b[CHUNK - 1:CHUNK] if fwd else b[0:1]
    b_mid = b[CHUNK // 2:CHUNK // 2 + 1]
    e_q = jnp.exp(b - b_mid)
    e_k = jnp.exp(b_mid - b)
    qi = q * e_q
    ki = k * e_k
    q_in = qi * jnp.exp(b_mid)
    kk = ki * jnp.exp(b_last - b_mid)
    ki_bf = ki.astype(BF16)
    st = st_ref[st_idx]
    st_bf = st.astype(BF16)
    outs = []
    upd = None
    for v, msk in zip(v_list, head_masks):
        if msk is None:
            qi_h, qin_h, kk_h = qi, q_in, kk
        else:
            qi_h = jnp.where(msk, qi, 0.0)
            qin_h = jnp.where(msk, q_in, 0.0)
            kk_h = jnp.where(msk, kk, 0.0)
        a = jnp.where(causal, _dot_nt(qi_h.astype(BF16), ki_bf), 0.0)
        v_bf = v.astype(BF16)
        outs.append(_dot(a.astype(BF16), v_bf) + _dot_nt(qin_h.astype(BF16), st_bf))
        u = _dot_tn(v_bf, kk_h.astype(BF16))
        upd = u if upd is None else upd + u
    st_ref[st_idx] = st * jnp.exp(b_last) + upd
    return outs


def _even_dir(d, qk, va, qb, fg, ib, al, w2_ref, ba_ref, lb_ref, stg, sth, o_ref):
    fwd = d == 0
    r = lax.broadcasted_iota(jnp.int32, (TL, TL), 0)
    c = lax.broadcasted_iota(jnp.int32, (TL, TL), 1)
    same = (r // CHUNK) == (c // CHUNK)
    tri = jnp.where(same & ((c <= r) if fwd else (c >= r)), 1.0, 0.0).astype(BF16)
    rc = lax.broadcasted_iota(jnp.int32, (CHUNK, CHUNK), 0)
    cc = lax.broadcasted_iota(jnp.int32, (CHUNK, CHUNK), 1)
    causal = (cc <= rc) if fwd else (cc >= rc)
    lane = lax.broadcasted_iota(jnp.int32, (CHUNK, LANES), 1)
    pair_masks = [lane < GLA_DK, lane >= GLA_DK]

    y = _dot(al[...].astype(BF16), w2_ref[d]) + ba_ref[d]
    la_a = (jnp.minimum(y, 0.0) - jnp.log1p(jnp.exp(-jnp.abs(y)))) * (1.0 / GLA_TAU)
    f = fg[...]
    lbv = lb_ref[...]
    la_h = jnp.log(lbv + (1.0 - lbv) * _sigmoid(f))
    key_h = (1.0 - lbv) * _sigmoid(-f)
    la = jnp.concatenate([la_a, la_h], axis=1)
    la_hi, la_lo = _split_bf16(la)
    b_all = _dot(tri, la_hi) + _dot(tri, la_lo)
    qa = qk[:, 0:GLA_QK] * (GLA_DK ** -0.5)
    ka = qk[:, GLA_QK:2 * GLA_QK]
    qh = _silu(qb[...])
    vv = va[...]
    iv = ib[...]

    n_chunks = TL // CHUNK
    for ci in range(n_chunks):
        cpos = ci if fwd else n_chunks - 1 - ci
        rows = slice(cpos * CHUNK, (cpos + 1) * CHUNK)
        for p in range(GLA_HEADS // 2):
            ln = slice(p * LANES, (p + 1) * LANES)
            vs = [vv[rows, (2 * p + hh) * GLA_DV:(2 * p + hh + 1) * GLA_DV] for hh in range(2)]
            outs = _gated_chunk(qa[rows, ln], ka[rows, ln], b_all[rows, ln], vs,
                                stg, (d, p), pair_masks, causal, fwd)
            for hh in range(2):
                col = (2 * p + hh) * GLA_DV
                o_ref[rows, col:col + GLA_DV] = outs[hh]
        for h in range(HG_HEADS):
            ln = slice(h * LANES, (h + 1) * LANES)
            bl = slice(GLA_QK + h * LANES, GLA_QK + (h + 1) * LANES)
            outs = _gated_chunk(qh[rows, ln], key_h[rows, ln], b_all[rows, bl], [iv[rows, ln]],
                                sth, (d, h), [None], causal, fwd)
            o_ref[rows, GLA_V + h * HG_DV:GLA_V + (h + 1) * HG_DV] = outs[0]


def _even_scan_kernel(tf_ref, tb_ref, first_ref, last_ref, seq_ref,
                      qk_f, va_f, qb_f, fg_f, ib_f, al_f,
                      qk_b, va_b, qb_b, fg_b, ib_b, al_b,
                      w2_ref, ba_ref, lb_ref, sg_ref, sh_ref,
                      of_ref, ob_ref, stg_ref, sth_ref,
                      stg, sth):
    i = pl.program_id(0)
    is_first = first_ref[i] == 1
    is_last = last_ref[i] == 1
    is_ctx = seq_ref[i] < CTX_B

    @pl.when(jnp.logical_and(is_first, is_ctx))
    def _():
        stg[...] = jnp.zeros_like(stg)
        sth[...] = jnp.zeros_like(sth)

    @pl.when(jnp.logical_and(is_first, jnp.logical_not(is_ctx)))
    def _():
        for d in range(2):
            for p in range(GLA_HEADS // 2):
                stg[d, p] = sg_ref[d, p].T
            for h in range(HG_HEADS):
                sth[d, h] = sh_ref[d, h].T

    _even_dir(0, qk_f, va_f, qb_f, fg_f, ib_f, al_f, w2_ref, ba_ref, lb_ref, stg, sth, of_ref)
    _even_dir(1, qk_b, va_b, qb_b, fg_b, ib_b, al_b, w2_ref, ba_ref, lb_ref, stg, sth, ob_ref)

    @pl.when(jnp.logical_and(is_last, is_ctx))
    def _():
        for d in range(2):
            for p in range(GLA_HEADS // 2):
                stg_ref[d, p] = stg[d, p].T
            for h in range(HG_HEADS):
                sth_ref[d, h] = sth[d, h].T


def _even_scan(z, w2_pad, b_a, lb, s_gla, s_hgrn):
    work = _scan_work()

    def zspec(width, col_block, which):
        return pl.BlockSpec((TL, width), lambda i, tf, tb, fi, la, sq: ((tf, tb)[which][i], col_block))

    def dir_specs(which):
        return [zspec(512, EV_QA // 512, which), zspec(512, EV_VA // 512, which),
                zspec(512, EV_QB // 512, which), zspec(512, (EV_FF, EV_FB)[which] // 512, which),
                zspec(512, EV_IB // 512, which), zspec(LANES, EV_AL // LANES, which)]

    def const(shape):
        return pl.BlockSpec(shape, lambda i, *_: (0,) * len(shape))

    def s_in(shape):
        return pl.BlockSpec((None,) + shape,
                            lambda i, tf, tb, fi, la, sq: (jnp.clip(sq[i] - CTX_B, 0, SMP_B - 1),) + (0,) * len(shape))

    def s_out(shape):
        return pl.BlockSpec((None,) + shape,
                            lambda i, tf, tb, fi, la, sq: (jnp.minimum(sq[i], CTX_B - 1),) + (0,) * len(shape))

    gshape = (2, GLA_HEADS // 2, LANES, LANES)
    hshape = (2, HG_HEADS, HG_DK, HG_DV)
    grid_spec = pltpu.PrefetchScalarGridSpec(
        num_scalar_prefetch=5, grid=(N_WORK,),
        in_specs=dir_specs(0) + dir_specs(1) + [
            const((2, LANES, GLA_QK)), const((2, 1, GLA_QK)), const((1, HG_K)),
            s_in(gshape), s_in(hshape)],
        out_specs=[pl.BlockSpec((TL, D), lambda i, tf, tb, fi, la, sq: (tf[i], 0)),
                   pl.BlockSpec((TL, D), lambda i, tf, tb, fi, la, sq: (tb[i], 0)),
                   s_out(gshape), s_out(hshape)],
        scratch_shapes=[pltpu.VMEM(gshape, F32), pltpu.VMEM(hshape, F32)])
    return pl.pallas_call(
        _even_scan_kernel, grid_spec=grid_spec,
        out_shape=[jax.ShapeDtypeStruct((T_ALL, D), F32), jax.ShapeDtypeStruct((T_ALL, D), F32),
                   jax.ShapeDtypeStruct((CTX_B,) + gshape, F32),
                   jax.ShapeDtypeStruct((CTX_B,) + hshape, F32)],
        compiler_params=_params(1), name="even_scan",
    )(*work, z, z, z, z, z, z, z, z, z, z, z, z, w2_pad, b_a, lb, s_gla, s_hgrn)


RET_HP = 2


def _ret_tables():
    lg = np.log1p(-np.exp2(-5.0 - np.arange(RET_HEADS, dtype=np.float64)))
    pos = np.arange(TL, dtype=np.float64)
    dm = np.zeros((2, RET_HEADS, TL, TL))
    rq = np.zeros((2, RET_HEADS, TL, 1))
    rk = np.zeros((2, RET_HEADS, TL, 1))
    gc = np.zeros((2, RET_HEADS, 1, RET_DV))
    diff = pos[:, None] - pos[None, :]
    kscale = RET_DK ** -0.5
    for h in range(RET_HEADS):
        dm[0, h] = np.where(diff >= 0, np.exp(lg[h] * np.maximum(diff, 0)), 0.0) * kscale
        rq[0, h, :, 0] = np.exp(lg[h] * (pos + 1))
        rk[0, h, :, 0] = np.exp(lg[h] * (TL - 1 - pos)) * kscale
        gc[0, h] = np.exp(lg[h] * TL)
        lb = lg[RET_HEADS - 1 - h]
        dm[1, h] = np.where(diff <= 0, np.exp(lb * np.maximum(-diff, 0)), 0.0) * kscale
        rq[1, h, :, 0] = np.exp(lb * (TL - pos))
        rk[1, h, :, 0] = np.exp(lb * pos) * kscale
        gc[1, h] = np.exp(lb * TL)
    return tuple(jnp.asarray(a, F32) for a in (dm, rq, rk, gc))


def _odd_dir(d, q_ref, k_ref, v_ref, dm_ref, rq_ref, rk_ref, gc_ref, st, o_ref):
    for hh in range(RET_HP):
        k = k_ref[:, hh * RET_DK:(hh + 1) * RET_DK]
        q = q_ref[:, hh * RET_DK:(hh + 1) * RET_DK]
        v = v_ref[:, hh * RET_DV:(hh + 1) * RET_DV]
        a = _dot_nt(q, k) * dm_ref[d, hh]
        s = st[d, hh]
        o = _dot(a.astype(BF16), v) + rq_ref[d, hh] * _dot(q, s.astype(BF16))
        kk = (k.astype(F32) * rk_ref[d, hh]).astype(BF16)
        st[d, hh] = gc_ref[d, hh] * s + _dot_tn(kk, v)
        o_ref[:, hh * RET_DV:(hh + 1) * RET_DV] = o


def _odd_scan_kernel(tf_ref, tb_ref, first_ref, last_ref, seq_ref,
                     q_f, k_f, v_f, q_b, k_b, v_b,
                     dm_ref, rq_ref, rk_ref, gc_ref, s0_ref,
                     of_ref, ob_ref, sout_ref, st):
    i = pl.program_id(1)
    is_first = first_ref[i] == 1
    is_last = last_ref[i] == 1
    is_ctx = seq_ref[i] < CTX_B

    @pl.when(jnp.logical_and(is_first, is_ctx))
    def _():
        st[...] = jnp.zeros_like(st)

    @pl.when(jnp.logical_and(is_first, jnp.logical_not(is_ctx)))
    def _():
        st[...] = s0_ref[...]

    _odd_dir(0, q_f, k_f, v_f, dm_ref, rq_ref, rk_ref, gc_ref, st, of_ref)
    _odd_dir(1, q_b, k_b, v_b, dm_ref, rq_ref, rk_ref, gc_ref, st, ob_ref)

    @pl.when(jnp.logical_and(is_last, is_ctx))
    def _():
        sout_ref[...] = st[...]


def _odd_scan(z, tables, s_ret):
    work = _scan_work()
    dm, rq, rk, gc = tables
    qw, vw = RET_HP * RET_DK, RET_HP * RET_DV

    def zspec(width, col0, which):
        return pl.BlockSpec((TL, width),
                            lambda hp, i, tf, tb, fi, la, sq: ((tf, tb)[which][i], col0 // width + hp))

    def dir_specs(which):
        return [zspec(qw, OD_Q, which), zspec(qw, OD_K, which), zspec(vw, OD_V, which)]

    def table(shape):
        return pl.BlockSpec((2, RET_HP) + shape, lambda hp, i, *_: (0, hp) + (0,) * len(shape))

    sshape = (2, RET_HP, RET_DK, RET_DV)
    grid_spec = pltpu.PrefetchScalarGridSpec(
        num_scalar_prefetch=5, grid=(RET_HEADS // RET_HP, N_WORK),
        in_specs=dir_specs(0) + dir_specs(1) + [
            table((TL, TL)), table((TL, 1)), table((TL, 1)), table((1, RET_DV)),
            pl.BlockSpec((None,) + sshape,
                         lambda hp, i, tf, tb, fi, la, sq:
                         (jnp.clip(sq[i] - CTX_B, 0, SMP_B - 1), 0, hp, 0, 0))],
        out_specs=[pl.BlockSpec((TL, vw), lambda hp, i, tf, tb, fi, la, sq: (tf[i], hp)),
                   pl.BlockSpec((TL, vw), lambda hp, i, tf, tb, fi, la, sq: (tb[i], hp)),
                   pl.BlockSpec((None,) + sshape,
                                lambda hp, i, tf, tb, fi, la, sq:
                                (jnp.minimum(sq[i], CTX_B - 1), 0, hp, 0, 0))],
        scratch_shapes=[pltpu.VMEM(sshape, F32)])
    return pl.pallas_call(
        _odd_scan_kernel, grid_spec=grid_spec,
        out_shape=[jax.ShapeDtypeStruct((T_ALL, RET_V), F32), jax.ShapeDtypeStruct((T_ALL, RET_V), F32),
                   jax.ShapeDtypeStruct((CTX_B, 2, RET_HEADS, RET_DK, RET_DV), F32)],
        compiler_params=_params(2), name="odd_scan",
    )(*work, z, z, z, z, z, z, dm, rq, rk, gc, s_ret)


def _head_rms(o, width):
    parts = []
    for j in range(0, o.shape[1], width):
        blk = o[:, j:j + width]
        parts.append(blk * lax.rsqrt(jnp.mean(blk * blk, axis=-1, keepdims=True) + NORM_EPS))
    return parts


def _route(h2, wr_hi_ref, wr_lo_ref, br_ref):
    h_hi, h_lo = _split_bf16(h2)
    logits = (_dot(h_hi, wr_hi_ref[...]) + _dot(h_hi, wr_lo_ref[...]) + _dot(h_lo, wr_hi_ref[...])
              + br_ref[...])
    lane_i = lax.broadcasted_iota(jnp.int32, logits.shape, 1)
    lane = lane_i.astype(F32)
    neg = -jnp.inf
    big = 1e9
    is_grp = (lane_i >= N_EXPERTS) & (lane_i < N_EXPERTS + N_GROUPS)
    lg = jnp.where(is_grp, logits, neg)
    mg = jnp.max(lg, axis=-1, keepdims=True)
    gsel = jnp.min(jnp.where(lg == mg, lane - N_EXPERTS, big), axis=-1, keepdims=True)
    pg = 1.0 / jnp.sum(jnp.where(is_grp, jnp.exp(lg - mg), 0.0), axis=-1, keepdims=True)
    in_grp = (lane_i < N_EXPERTS) & ((lane_i // EXP_PER_GROUP).astype(F32) == gsel)
    le = jnp.where(in_grp, logits, neg)
    m1 = jnp.max(le, axis=-1, keepdims=True)
    i1 = jnp.min(jnp.where(le == m1, lane, big), axis=-1, keepdims=True)
    le2 = jnp.where(lane == i1, neg, le)
    m2 = jnp.max(le2, axis=-1, keepdims=True)
    i2 = jnp.min(jnp.where(le2 == m2, lane, big), axis=-1, keepdims=True)
    e2 = jnp.exp(m2 - m1)
    w1 = pg / (1.0 + e2)
    w2 = pg * e2 / (1.0 + e2)
    first_lo = i1 < i2
    ia = jnp.minimum(i1, i2) - gsel * EXP_PER_GROUP
    ib = jnp.maximum(i1, i2) - gsel * EXP_PER_GROUP
    pair = gsel * PAIRS_PER_GROUP + ia * (2 * EXP_PER_GROUP - 1 - ia) * 0.5 + (ib - ia - 1.0)
    return pair, jnp.where(first_lo, w1, w2), jnp.where(first_lo, w2, w1)


def _post_tail(i, x, out, m, nw2_ref, wr_hi_ref, wr_lo_ref, br_ref,
               x1_ref, hrow_ref, meta_ref, cnt_ref, carry):
    x1 = x + m[2:3] * out
    x1_ref[...] = x1
    h2 = _norm_mod(x1, nw2_ref[...], m[3:4], m[4:5])
    pair, w_a, w_b = _route(h2, wr_hi_ref, wr_lo_ref, br_ref)

    @pl.when(i == 0)
    def _():
        carry[...] = jnp.zeros_like(carry)

    lane = lax.broadcasted_iota(jnp.int32, (TM, LANES), 1)
    onehot = jnp.where(lane.astype(F32) == pair, 1.0, 0.0)
    r = lax.broadcasted_iota(jnp.int32, (TM, TM), 0)
    c = lax.broadcasted_iota(jnp.int32, (TM, TM), 1)
    earlier = jnp.where(c < r, 1.0, 0.0).astype(BF16)
    before = _dot(earlier, onehot.astype(BF16)) + carry[...]
    rank = jnp.sum(onehot * before, axis=-1, keepdims=True)
    carry[...] += jnp.sum(onehot, axis=0, keepdims=True)
    cnt_ref[...] = carry[...]
    meta_ref[...] = jnp.where(lane == 0, pair, jnp.where(lane == 1, rank, 0.0))
    hrow_ref[:, 0:D] = h2
    hrow_ref[:, D:ROW_W] = jnp.where(lane == 0, w_a, jnp.where(lane == 1, w_b, 0.0))


def _post_even_kernel(of_ref, ob_ref, ra_ref, gb_ref, x_ref, mod_ref, gn_ref, wo_ref,
                      nw2_ref, wr_hi_ref, wr_lo_ref, br_ref,
                      x1_ref, hrow_ref, meta_ref, cnt_ref, carry):
    o = of_ref[...] + ob_ref[...]
    normed = jnp.concatenate(_head_rms(o, GLA_DV), axis=1)
    gate = jnp.concatenate([_silu(ra_ref[...]), _silu(gb_ref[...])], axis=1)
    mixed = (normed * gn_ref[...] * gate).astype(BF16)
    out = _dot(mixed, wo_ref[...])
    _post_tail(pl.program_id(0), x_ref[...], out, mod_ref[...], nw2_ref, wr_hi_ref, wr_lo_ref, br_ref,
               x1_ref, hrow_ref, meta_ref, cnt_ref, carry)


def _post_odd_kernel(of_ref, ob_ref, g_ref, x_ref, mod_ref, wo_ref,
                     nw2_ref, wr_hi_ref, wr_lo_ref, br_ref,
                     x1_ref, hrow_ref, meta_ref, cnt_ref, carry):
    o = of_ref[...] + ob_ref[...]
    normed = jnp.concatenate(_head_rms(o, RET_DV), axis=1)
    mixed = (normed * _silu(g_ref[...].astype(F32))).astype(BF16)
    out = _dot(mixed, wo_ref[...])
    _post_tail(pl.program_id(0), x_ref[...], out, mod_ref[...], nw2_ref, wr_hi_ref, wr_lo_ref, br_ref,
               x1_ref, hrow_ref, meta_ref, cnt_ref, carry)


def _post_common_specs():
    tile = lambda w: pl.BlockSpec((TM, w), lambda i: (i, 0))
    const = lambda s: pl.BlockSpec(s, lambda i: (0,) * len(s))
    mod = pl.BlockSpec((None, 6, D), lambda i: (_cond_of_tile(i, TM), 0, 0))
    out_specs = [tile(D), tile(ROW_W), tile(LANES), const((1, LANES))]
    out_shape = [jax.ShapeDtypeStruct((T_ALL, D), F32),
                 jax.ShapeDtypeStruct((T_ALL, ROW_W), F32),
                 jax.ShapeDtypeStruct((T_ALL, LANES), F32),
                 jax.ShapeDtypeStruct((1, LANES), F32)]
    scratch = [pltpu.VMEM((1, LANES), F32)]
    return tile, const, mod, out_specs, out_shape, scratch


def _post_even(o_f, o_b, z, x, mods_l, gn, wo_bf, nw2, wr_hi, wr_lo, br):
    tile, const, mod, out_specs, out_shape, scratch = _post_common_specs()
    zcol = lambda c0: pl.BlockSpec((TM, 512), lambda i: (i, c0 // 512))
    return pl.pallas_call(
        _post_even_kernel, grid=(T_ALL // TM,),
        in_specs=[tile(D), tile(D), zcol(EV_RA), zcol(EV_GB), tile(D), mod, const((1, D)),
                  const((D, D)), const((1, D)), const((D, ROUTER_N)), const((D, ROUTER_N)),
                  const((1, ROUTER_N))],
        out_specs=out_specs, out_shape=out_shape, scratch_shapes=scratch,
        compiler_params=_params(1), name="post_even",
    )(o_f, o_b, z, z, x, mods_l, gn, wo_bf, nw2, wr_hi, wr_lo, br)


def _post_odd(o_f, o_b, z, x, mods_l, wo_bf, nw2, wr_hi, wr_lo, br):
    tile, const, mod, out_specs, out_shape, scratch = _post_common_specs()
    return pl.pallas_call(
        _post_odd_kernel, grid=(T_ALL // TM,),
        in_specs=[tile(RET_V), tile(RET_V),
                  pl.BlockSpec((TM, RET_V), lambda i: (i, OD_G // RET_V)), tile(D), mod,
                  const((RET_V, D)), const((1, D)), const((D, ROUTER_N)), const((D, ROUTER_N)),
                  const((1, ROUTER_N))],
        out_specs=out_specs, out_shape=out_shape, scratch_shapes=scratch,
        compiler_params=_params(1), name="post_odd",
    )(o_f, o_b, z, x, mods_l, wo_bf, nw2, wr_hi, wr_lo, br)


def _pair_tables():
    ea, eb = [], []
    for g in range(N_GROUPS):
        for a in range(EXP_PER_GROUP):
            for b in range(a + 1, EXP_PER_GROUP):
                ea.append(g * EXP_PER_GROUP + a)
                eb.append(g * EXP_PER_GROUP + b)
    pad = LANES - len(ea)
    return (jnp.asarray(ea + [0] * pad, jnp.int32), jnp.asarray(eb + [0] * pad, jnp.int32))


def _dispatch_plan(meta, counts):
    pair = meta[:, 0].astype(jnp.int32)
    rank = meta[:, 1].astype(jnp.int32)
    cnt = counts[0].astype(jnp.int32)
    tiles_g = (cnt + TMG - 1) // TMG
    tile_end = jnp.cumsum(tiles_g)
    tile_start = tile_end - tiles_g
    slot0 = tile_start * TMG
    ids = jnp.arange(LANES, dtype=jnp.int32)
    dest = rank + jnp.sum(jnp.where(pair[:, None] == ids[None, :], slot0[None, :], 0), axis=1)
    n_real = tile_end[-1]
    j = jnp.arange(N_TILES_MOE, dtype=jnp.int32)
    jj = jnp.minimum(j, n_real - 1)
    grp = jnp.sum((jj[:, None] >= tile_end[None, :]).astype(jnp.int32), axis=1)
    ea_t, eb_t = _pair_tables()
    onehot_g = grp[:, None] == ids[None, :]
    pick = lambda v: jnp.sum(jnp.where(onehot_g, v[None, :], 0), axis=1)
    nv = jnp.clip(pick(cnt) - (jj - pick(tile_start)) * TMG, 0, TMG)
    nv = jnp.where(j < n_real, nv, 0)
    return dest, pick(ea_t), pick(eb_t), nv, n_real.reshape(1)


DISPATCH_STEPS = T_ALL // DISPATCH_ROWS


def _dispatch_kernel(dest_ref, nv_ref, h_ref, hs_hbm, zeros, sem, pad_sem):
    j = pl.program_id(0)
    base = j * DISPATCH_ROWS

    def pad_tile(tile):
        return pltpu.make_async_copy(zeros, hs_hbm.at[pl.ds(tile * TMG, TMG)], pad_sem.at[0])

    @pl.when(j == 0)
    def _():
        zeros[...] = jnp.zeros_like(zeros)

        def fill(tile, carry):
            @pl.when(nv_ref[tile] < TMG)
            def _():
                pad_tile(tile).start()
            return carry

        def drain(tile, carry):
            @pl.when(nv_ref[tile] < TMG)
            def _():
                pad_tile(tile).wait()
            return carry

        lax.fori_loop(0, N_TILES_MOE, fill, 0)
        lax.fori_loop(0, N_TILES_MOE, drain, 0)

    def issue(r, carry):
        pltpu.make_async_copy(h_ref.at[pl.ds(r, 1)], hs_hbm.at[pl.ds(dest_ref[base + r], 1)],
                              sem.at[0]).start()
        return carry

    lax.fori_loop(0, DISPATCH_ROWS, issue, 0, unroll=8)
    pltpu.make_async_copy(h_ref, hs_hbm.at[pl.ds(0, DISPATCH_ROWS)], sem.at[0]).wait()


def _dispatch(dest, nv, hrow):
    grid_spec = pltpu.PrefetchScalarGridSpec(
        num_scalar_prefetch=2, grid=(DISPATCH_STEPS,),
        in_specs=[pl.BlockSpec((DISPATCH_ROWS, ROW_W), lambda j, *_: (j, 0))],
        out_specs=pl.BlockSpec(memory_space=pl.ANY),
        scratch_shapes=[pltpu.VMEM((TMG, ROW_W), F32),
                        pltpu.SemaphoreType.DMA((1,)), pltpu.SemaphoreType.DMA((1,))])
    return pl.pallas_call(
        _dispatch_kernel, grid_spec=grid_spec,
        out_shape=jax.ShapeDtypeStruct((N_TILES_MOE * TMG, ROW_W), F32),
        compiler_params=_params(1), name="moe_dispatch",
    )(dest, nv, hrow)


def _moe_kernel(ea_ref, eb_ref, nreal_ref, hs_ref,
                wga_ref, wua_ref, wda_ref, wgb_ref, wub_ref, wdb_ref, ys_ref):
    j = pl.program_id(0)

    @pl.when(j < nreal_ref[0])
    def _():
        h = hs_ref[:, 0:D].astype(BF16)
        wrow = hs_ref[:, D:ROW_W]
        w_a, w_b = wrow[:, 0:1], wrow[:, 1:2]
        act_a = _silu(_dot(h, wga_ref[0])) * _dot(h, wua_ref[0]) * w_a
        act_b = _silu(_dot(h, wgb_ref[0])) * _dot(h, wub_ref[0]) * w_b
        ys_ref[...] = _dot(act_a.astype(BF16), wda_ref[0]) + _dot(act_b.astype(BF16), wdb_ref[0])

    @pl.when(j >= nreal_ref[0])
    def _():
        ys_ref[...] = jnp.zeros_like(ys_ref)


def _moe(ea, eb, n_real, hs, wg_bf, wu_bf, wd_bf):
    def wspec(shape, which):
        return pl.BlockSpec((1,) + shape, lambda j, ea, eb, nr: ((ea, eb)[which][j], 0, 0))

    up, down = (D, D_EXPERT), (D_EXPERT, D)
    grid_spec = pltpu.PrefetchScalarGridSpec(
        num_scalar_prefetch=3, grid=(N_TILES_MOE,),
        in_specs=[pl.BlockSpec((TMG, ROW_W), lambda j, ea, eb, nr: (jnp.minimum(j, nr[0] - 1), 0)),
                  wspec(up, 0), wspec(up, 0), wspec(down, 0),
                  wspec(up, 1), wspec(up, 1), wspec(down, 1)],
        out_specs=pl.BlockSpec((TMG, D), lambda j, ea, eb, nr: (j, 0)))
    return pl.pallas_call(
        _moe_kernel, grid_spec=grid_spec,
        out_shape=jax.ShapeDtypeStruct((N_TILES_MOE * TMG, D), F32),
        compiler_params=_params(1), name="moe",
    )(ea, eb, n_real, hs, wg_bf, wu_bf, wd_bf, wg_bf, wu_bf, wd_bf)


def _combine_kernel(dest_ref, ys_hbm, x1_ref, mod_ref, o_ref, buf, sem):
    j = pl.program_id(0)
    n = pl.num_programs(0)
    slot = j % 2

    def gather(tile, s):
        def issue(r, carry):
            pltpu.make_async_copy(ys_hbm.at[pl.ds(dest_ref[tile * TM + r], 1)],
                                  buf.at[s, pl.ds(r, 1)], sem.at[s]).start()
            return carry
        lax.fori_loop(0, TM, issue, 0, unroll=8)

    @pl.when(j == 0)
    def _():
        gather(0, 0)

    @pl.when(j + 1 < n)
    def _():
        gather(j + 1, 1 - slot)

    pltpu.make_async_copy(ys_hbm.at[pl.ds(0, TM)], buf.at[slot], sem.at[slot]).wait()
    o_ref[...] = x1_ref[...] + mod_ref[5:6] * buf[slot]


def _combine(dest, ys, x1, mods_l):
    grid_spec = pltpu.PrefetchScalarGridSpec(
        num_scalar_prefetch=1, grid=(T_ALL // TM,),
        in_specs=[pl.BlockSpec(memory_space=pl.ANY),
                  pl.BlockSpec((TM, D), lambda i, dest: (i, 0)),
                  pl.BlockSpec((None, 6, D), lambda i, dest: (_cond_of_tile(i, TM), 0, 0))],
        out_specs=pl.BlockSpec((TM, D), lambda i, dest: (i, 0)),
        scratch_shapes=[pltpu.VMEM((2, TM, D), F32), pltpu.SemaphoreType.DMA((2,))])
    return pl.pallas_call(
        _combine_kernel, grid_spec=grid_spec,
        out_shape=jax.ShapeDtypeStruct((T_ALL, D), F32),
        compiler_params=_params(1), name="moe_combine",
    )(dest, ys, x1, mods_l)


def _final_norm_kernel(x_ref, nw_ref, y_ref):
    x = x_ref[...]
    var = jnp.mean(x * x, axis=-1, keepdims=True)
    y_ref[...] = x * lax.rsqrt(var + NORM_EPS) * nw_ref[...]


def _final_norm(x, nw, tile0, n_tok):
    return pl.pallas_call(
        _final_norm_kernel, grid=(n_tok // TM,),
        in_specs=[pl.BlockSpec((TM, D), lambda i: (tile0 + i, 0)),
                  pl.BlockSpec((1, D), lambda i: (0, 0))],
        out_specs=pl.BlockSpec((TM, D), lambda i: (i, 0)),
        out_shape=jax.ShapeDtypeStruct((n_tok, D), F32),
        compiler_params=_params(1), name="final_norm",
    )(x, nw)


def _rope_tables():
    freqs = ROPE_BASE ** (-jnp.arange(ROPE_PAIRS, dtype=F32) / ROPE_PAIRS)
    t = jnp.arange(SMP_L)
    halves_c, halves_s = [], []
    for p in (t // GRID_W, t % GRID_W):
        ang = p.astype(F32)[:, None] * freqs
        cs, sn = jnp.cos(ang), jnp.sin(ang)
        halves_c += [cs, cs]
        halves_s += [-sn, sn]
    cos_t = jnp.concatenate(halves_c, axis=1)
    sin_t = jnp.concatenate(halves_s, axis=1)
    cos_t = jnp.concatenate([jnp.ones((TM, RET_DK), F32), cos_t], axis=0)
    sin_t = jnp.concatenate([jnp.zeros((TM, RET_DK), F32), sin_t], axis=0)
    return cos_t, sin_t


def kernel(x_prompt, x_sample, state_gla, state_hgrn, state_ret, c, c_ctx, norm1_w, norm2_w, normf_w,
           w_mod, b_mod, w_in_even, gla_w_alpha, gla_b_alpha, hgrn_lb_logits, gla_norm_w, hgrn_norm_w,
           w_out_even, w_in_odd, w_out_odd, router_g_w, router_g_b, router_e_w, router_e_b,
           moe_w_gate, moe_w_up, moe_w_down):
    x = jnp.concatenate([x_prompt.reshape(T_CTX, D), x_sample.reshape(T_SMP, D)], axis=0)
    cond8 = jnp.concatenate([c_ctx[None, :], c, jnp.zeros((COND_ROWS - N_COND, D), F32)], axis=0)
    mods = _mods(cond8, w_mod, b_mod).reshape(DEPTH, COND_ROWS, 6, D)
    lb_all = jnp.cumsum(jax.nn.softmax(hgrn_lb_logits.astype(F32), axis=0), axis=0)[:N_EVEN]
    cos_t, sin_t = _rope_tables()
    ret_tables = _ret_tables()

    st_gla, st_hgrn, st_ret = [], [], []
    for l in range(DEPTH):
        mods_l = mods[l]
        nw1 = norm1_w[l].reshape(1, D)
        nw2 = norm2_w[l].reshape(1, D)
        wr = jnp.concatenate([router_e_w[l], router_g_w[l],
                              jnp.zeros((D, ROUTER_N - N_EXPERTS - N_GROUPS), F32)], axis=1)
        wr_hi = wr.astype(BF16)
        wr_lo = (wr - wr_hi.astype(F32)).astype(BF16)
        br = jnp.concatenate([router_e_b[l], router_g_b[l],
                              jnp.zeros((ROUTER_N - N_EXPERTS - N_GROUPS,), F32)]).reshape(1, ROUTER_N)
        if l % 2 == 0:
            e = l // 2
            w = w_in_even[e]
            a0 = 2 * GLA_QK + 2 * GLA_V
            a1 = a0 + 2 * GLA_RANK
            w_bf = jnp.concatenate([w[:, :a0], w[:, a1:], w[:, a0:a1],
                                    jnp.zeros((D, LANES - 2 * GLA_RANK), F32)], axis=1).astype(BF16)
            z = _inproj_even(x, mods_l, nw1, w_bf)
            w2_pad = jnp.zeros((2, LANES, GLA_QK), F32)
            for d in range(2):
                w2_pad = w2_pad.at[d, d * GLA_RANK:(d + 1) * GLA_RANK].set(gla_w_alpha[e, d])
            s_g = state_gla[:, e].reshape(SMP_B, 2, GLA_HEADS // 2, LANES, LANES)
            o_f, o_b, ng, nh = _even_scan(z, w2_pad.astype(BF16), gla_b_alpha[e].reshape(2, 1, GLA_QK),
                                          lb_all[e].reshape(1, HG_K), s_g, state_hgrn[:, e])
            st_gla.append(ng.reshape(CTX_B, 2, GLA_HEADS, GLA_DK, GLA_DV))
            st_hgrn.append(nh)
            gn = jnp.concatenate([gla_norm_w[e], hgrn_norm_w[e]]).reshape(1, D)
            x1, hrow, meta, counts = _post_even(o_f, o_b, z, x, mods_l, gn, w_out_even[e].astype(BF16),
                                                nw2, wr_hi, wr_lo, br)
        else:
            j = l // 2
            z = _inproj_odd(x, mods_l, nw1, w_in_odd[j].astype(BF16), cos_t, sin_t)
            o_f, o_b, nr = _odd_scan(z, ret_tables, state_ret[:, j])
            st_ret.append(nr)
            x1, hrow, meta, counts = _post_odd(o_f, o_b, z, x, mods_l, w_out_odd[j].astype(BF16),
                                               nw2, wr_hi, wr_lo, br)
        dest, ea, eb, nv, n_real = _dispatch_plan(meta, counts)
        hs = _dispatch(dest, nv, hrow)
        ys = _moe(ea, eb, n_real, hs, moe_w_gate[l].astype(BF16), moe_w_up[l].astype(BF16),
                  moe_w_down[l].astype(BF16))
        x = _combine(dest, ys, x1, mods_l)

    nwf = normf_w.reshape(1, D)
    y_prompt = _final_norm(x, nwf, 0, T_CTX).reshape(CTX_B, CTX_L, D)
    y_sample = _final_norm(x, nwf, T_CTX // TM, T_SMP).reshape(SMP_B, SMP_L, D)
    return (y_prompt, y_sample, jnp.stack(st_gla, axis=1), jnp.stack(st_hgrn, axis=1),
            jnp.stack(st_ret, axis=1))
```

```python
import functools

import numpy as np
import jax
import jax.numpy as jnp
from jax import lax
from jax.experimental import pallas as pl
from jax.experimental.pallas import tpu as pltpu

F32 = jnp.float32
BF16 = jnp.bfloat16

D = 1024
CTX_B, CTX_L = 32, 256
SMP_B, SMP_L = 4, 4096
DEPTH = 4
N_EVEN, N_ODD = 2, 2
GRID_W = 64
NORM_EPS = 1e-6
T_CTX = CTX_B * CTX_L
T_SMP = SMP_B * SMP_L
T_ALL = T_CTX + T_SMP
N_COND = 1 + SMP_B
COND_ROWS = 8

GLA_HEADS, GLA_DK, GLA_DV, GLA_RANK, GLA_TAU = 4, 64, 128, 16, 16.0
GLA_QK, GLA_V = GLA_HEADS * GLA_DK, GLA_HEADS * GLA_DV
HG_HEADS, HG_DK, HG_DV = 4, 128, 128
HG_K, HG_V = HG_HEADS * HG_DK, HG_HEADS * HG_DV
RET_HEADS, RET_DK, RET_DV = 4, 256, 512
RET_QK, RET_V = RET_HEADS * RET_DK, RET_HEADS * RET_DV
ROPE_BASE, ROPE_PAIRS = 10000.0, 64
N_GROUPS, EXP_PER_GROUP, N_EXPERTS, D_EXPERT = 4, 8, 32, 256

LANES = 128
CHUNK = 64
TL = 256
TM = 256
PAIRS_PER_GROUP = EXP_PER_GROUP * (EXP_PER_GROUP - 1) // 2
N_PAIRS = N_GROUPS * PAIRS_PER_GROUP
TMG = 256
N_TILES_MOE = T_ALL // TMG + N_PAIRS
ROW_W = D + LANES
DISPATCH_ROWS = 1024
VMEM_LIMIT = 56 * 1024 * 1024

EV_QA, EV_KA, EV_VA, EV_RA, EV_QB, EV_IB, EV_GB = 0, 256, 512, 1024, 1536, 2048, 2560
EV_MAIN = 3072
EV_FF, EV_FB, EV_AL = 0, 512, 1024
EV_GATE = 1024 + LANES
OD_Q, OD_K, OD_V, OD_G = 0, 1024, 2048, 4096
OD_N = 6144
ROUTER_N = LANES


def _dot(a, b):
    return jnp.dot(a, b, preferred_element_type=F32)


def _dot_nt(a, b):
    return lax.dot_general(a, b, (((1,), (1,)), ((), ())), preferred_element_type=F32)


def _dot_tn(a, b):
    return lax.dot_general(a, b, (((0,), (0,)), ((), ())), preferred_element_type=F32)


def _split_bf16(x):
    hi = x.astype(BF16)
    lo = (x - hi.astype(F32)).astype(BF16)
    return hi, lo


def _sigmoid(x):
    return 1.0 / (1.0 + jnp.exp(-x))


def _silu(x):
    return x * _sigmoid(x)


def _params(n_axes):
    return pltpu.CompilerParams(dimension_semantics=("arbitrary",) * n_axes,
                                vmem_limit_bytes=VMEM_LIMIT)


def _cond_of_tile(i, tm):
    n_ctx = T_CTX // tm
    per_seq = SMP_L // tm
    return jnp.where(i < n_ctx, 0, 1 + (i - n_ctx) // per_seq)


def _norm_mod(x, nw, shift, scale):
    var = jnp.mean(x * x, axis=-1, keepdims=True)
    return x * lax.rsqrt(var + NORM_EPS) * nw * (1.0 + scale) + shift


def _mods_kernel(cond_ref, w_ref, b_ref, o_ref):
    c = cond_ref[...]
    a_hi, a_lo = _split_bf16(_silu(c))
    w_hi, w_lo = _split_bf16(w_ref[0])
    o_ref[0] = _dot(a_hi, w_hi) + _dot(a_hi, w_lo) + _dot(a_lo, w_hi) + b_ref[0]


def _mods(cond8, w_mod, b_mod):
    return pl.pallas_call(
        _mods_kernel,
        grid=(DEPTH, 6),
        in_specs=[pl.BlockSpec((COND_ROWS, D), lambda l, j: (0, 0)),
                  pl.BlockSpec((1, D, D), lambda l, j: (l, 0, j)),
                  pl.BlockSpec((1, 1, D), lambda l, j: (l, 0, j))],
        out_specs=pl.BlockSpec((1, COND_ROWS, D), lambda l, j: (l, 0, j)),
        out_shape=jax.ShapeDtypeStruct((DEPTH, COND_ROWS, 6 * D), F32),
        compiler_params=_params(2), name="mods",
    )(cond8, w_mod, b_mod.reshape(DEPTH, 1, 6 * D))


def _inproj_even_kernel(x_ref, mod_ref, nw_ref, wm_ref, wg_ref, zm_ref, zg_ref):
    m = mod_ref[...]
    h = _norm_mod(x_ref[...], nw_ref[...], m[0:1], m[1:2]).astype(BF16)
    for j in range(0, EV_MAIN, 1024):
        zm_ref[:, j:j + 1024] = _dot(h, wm_ref[:, j:j + 1024]).astype(BF16)
    for j, width in ((0, 1024), (EV_AL, LANES)):
        zg_ref[:, j:j + width] = _dot(h, wg_ref[:, j:j + width])


def _inproj_odd_kernel(x_ref, mod_ref, nw_ref, w_ref, cos_ref, sin_ref, z_ref):
    m = mod_ref[...]
    h = _norm_mod(x_ref[...], nw_ref[...], m[0:1], m[1:2]).astype(BF16)
    for j in range(0, OD_V, 2 * LANES):
        zz = _dot(h, w_ref[:, j:j + 2 * LANES])
        for s in range(2):
            blk = zz[:, s * LANES:(s + 1) * LANES]
            cs = cos_ref[:, s * LANES:(s + 1) * LANES]
            sn = sin_ref[:, s * LANES:(s + 1) * LANES]
            z_ref[:, j + s * LANES:j + (s + 1) * LANES] = (
                blk * cs + pltpu.roll(blk, ROPE_PAIRS, 1) * sn).astype(BF16)
    step = 1024
    for j in range(OD_V, OD_N, step):
        z_ref[:, j:j + step] = _dot(h, w_ref[:, j:j + step]).astype(BF16)


def _inproj_even(x, mods_l, nw, wm_bf, wg_bf):
    return pl.pallas_call(
        _inproj_even_kernel,
        grid=(T_ALL // TM,),
        in_specs=[pl.BlockSpec((TM, D), lambda i: (i, 0)),
                  pl.BlockSpec((None, 6, D), lambda i: (_cond_of_tile(i, TM), 0, 0)),
                  pl.BlockSpec((1, D), lambda i: (0, 0)),
                  pl.BlockSpec((D, EV_MAIN), lambda i: (0, 0)),
                  pl.BlockSpec((D, EV_GATE), lambda i: (0, 0))],
        out_specs=[pl.BlockSpec((TM, EV_MAIN), lambda i: (i, 0)),
                   pl.BlockSpec((TM, EV_GATE), lambda i: (i, 0))],
        out_shape=[jax.ShapeDtypeStruct((T_ALL, EV_MAIN), BF16),
                   jax.ShapeDtypeStruct((T_ALL, EV_GATE), F32)],
        compiler_params=_params(1), name="inproj_even",
    )(x, mods_l, nw, wm_bf, wg_bf)


def _rope_block(i):
    n_ctx = T_CTX // TM
    per_seq = SMP_L // TM
    return jnp.where(i < n_ctx, 0, 1 + (i - n_ctx) % per_seq)


def _inproj_odd(x, mods_l, nw, w_bf, cos_t, sin_t):
    return pl.pallas_call(
        _inproj_odd_kernel,
        grid=(T_ALL // TM,),
        in_specs=[pl.BlockSpec((TM, D), lambda i: (i, 0)),
                  pl.BlockSpec((None, 6, D), lambda i: (_cond_of_tile(i, TM), 0, 0)),
                  pl.BlockSpec((1, D), lambda i: (0, 0)),
                  pl.BlockSpec((D, OD_N), lambda i: (0, 0)),
                  pl.BlockSpec((TM, RET_DK), lambda i: (_rope_block(i), 0)),
                  pl.BlockSpec((TM, RET_DK), lambda i: (_rope_block(i), 0))],
        out_specs=pl.BlockSpec((TM, OD_N), lambda i: (i, 0)),
        out_shape=jax.ShapeDtypeStruct((T_ALL, OD_N), BF16),
        compiler_params=_params(1), name="inproj_odd",
    )(x, mods_l, nw, w_bf, cos_t, sin_t)


def _scan_work():
    tf, tb, first, last, seq = [], [], [], [], []
    base = 0
    for s, length in enumerate([CTX_L] * CTX_B + [SMP_L] * SMP_B):
        n = length // TL
        for t in range(n):
            tf.append(base + t)
            tb.append(base + n - 1 - t)
            first.append(int(t == 0))
            last.append(int(t == n - 1))
            seq.append(s)
        base += n
    return tuple(np.asarray(a, np.int32) for a in (tf, tb, first, last, seq))


N_WORK = T_ALL // TL


def _decay_factors(q, k, b, fwd):
    b_last = b[CHUNK - 1:CHUNK] if fwd else b[0:1]
    b_mid = b[CHUNK // 2:CHUNK // 2 + 1]
    qi = q * jnp.exp(b - b_mid)
    ki = k * jnp.exp(b_mid - b)
    q_in = qi * jnp.exp(b_mid)
    kk = ki * jnp.exp(b_last - b_mid)
    return qi, ki, q_in, kk, jnp.exp(b_last)


def _stack(a, b):
    return jnp.concatenate([a, b], axis=0)


def _gla_pair_chunk(q, k, b, v0, v1, st, masks, bd_causal, fwd):
    qi, ki, q_in, kk, g = _decay_factors(q, k, b, fwd)
    m0, m1 = masks
    split = lambda t: _stack(jnp.where(m0, t, 0.0), jnp.where(m1, t, 0.0)).astype(BF16)
    ki_bf = ki.astype(BF16)
    a = jnp.where(bd_causal, _dot_nt(split(qi), _stack(ki_bf, ki_bf)), 0.0)
    vs = _stack(v0, v1)
    o = _dot(a.astype(BF16), vs) + _dot_nt(split(q_in), st.astype(BF16))
    return o, st * g + _dot_tn(vs, split(kk))


def _hgrn_pair_chunk(q0, k0, b0, v0, q1, k1, b1, v1, st, bd_causal, fwd):
    qi0, ki0, qin0, kk0, g0 = _decay_factors(q0, k0, b0, fwd)
    qi1, ki1, qin1, kk1, g1 = _decay_factors(q1, k1, b1, fwd)
    zero = jnp.zeros((CHUNK, LANES), F32)
    wide = lambda t0, t1: _stack(jnp.concatenate([t0, zero], axis=1),
                                 jnp.concatenate([zero, t1], axis=1)).astype(BF16)
    a = jnp.where(bd_causal, _dot_nt(_stack(qi0, qi1).astype(BF16), _stack(ki0, ki1).astype(BF16)), 0.0)
    vs = _stack(v0, v1)
    o = _dot(a.astype(BF16), vs) + _dot_nt(wide(qin0, qin1), st.astype(BF16))
    return o, st * jnp.concatenate([g0, g1], axis=1) + _dot_tn(vs, wide(kk0, kk1))


def _even_gates(d, fg, al, qb, tri, w2_ref, ba_ref, lbv):
    y = _dot(al[...].astype(BF16), w2_ref[d]) + ba_ref[d]
    la_a = (jnp.minimum(y, 0.0) - jnp.log(1.0 + jnp.exp(-jnp.abs(y)))) * (1.0 / GLA_TAU)
    f = fg[...]
    t = jnp.exp(-jnp.abs(f))
    rcp = 1.0 / (1.0 + t)
    sig_f = jnp.where(f >= 0.0, rcp, t * rcp)
    sig_nf = jnp.where(f >= 0.0, t * rcp, rcp)
    la_h = jnp.log(lbv + (1.0 - lbv) * sig_f)
    key_h = (1.0 - lbv) * sig_nf
    la_hi, la_lo = _split_bf16(jnp.concatenate([la_a, la_h], axis=1))
    b_all = _dot(tri, la_hi) + _dot(tri, la_lo)
    return b_all, key_h, _silu(qb[...].astype(F32))


def _even_tile(dir_refs, w2_ref, ba_ref, lb_ref, stg, sth):
    rc = lax.broadcasted_iota(jnp.int32, (2 * CHUNK, 2 * CHUNK), 0)
    cc = lax.broadcasted_iota(jnp.int32, (2 * CHUNK, 2 * CHUNK), 1)
    same_head = (rc // CHUNK) == (cc // CHUNK)
    bd_causal = [same_head & (cc <= rc), same_head & (cc >= rc)]
    rt = lax.broadcasted_iota(jnp.int32, (TL, TL), 0)
    ct = lax.broadcasted_iota(jnp.int32, (TL, TL), 1)
    same = (rt // CHUNK) == (ct // CHUNK)
    tri = [jnp.where(same & (ct <= rt), 1.0, 0.0).astype(BF16),
           jnp.where(same & (ct >= rt), 1.0, 0.0).astype(BF16)]
    lane = lax.broadcasted_iota(jnp.int32, (CHUNK, LANES), 1)
    pair_masks = [lane < GLA_DK, lane >= GLA_DK]
    lbv = lb_ref[...]
    n_pairs = GLA_HEADS // 2
    st_g = [[stg[d, p] for p in range(n_pairs)] for d in range(2)]
    st_h = [[sth[d, p] for p in range(HG_HEADS // 2)] for d in range(2)]
    gates = [_even_gates(d, dir_refs[d][3], dir_refs[d][5], dir_refs[d][2], tri[d], w2_ref, ba_ref, lbv)
             for d in range(2)]

    n_chunks = TL // CHUNK
    for ci in range(n_chunks):
        for d in range(2):
            qk, va, _, _, ib, _, o_ref = dir_refs[d]
            b_all, key_h, qh = gates[d]
            fwd = d == 0
            cpos = ci if fwd else n_chunks - 1 - ci
            rows = slice(cpos * CHUNK, (cpos + 1) * CHUNK)
            for p in range(n_pairs):
                ln = slice(p * LANES, (p + 1) * LANES)
                kl = slice(GLA_QK + p * LANES, GLA_QK + (p + 1) * LANES)
                c0, c1 = 2 * p * GLA_DV, (2 * p + 1) * GLA_DV
                o, st_g[d][p] = _gla_pair_chunk(
                    qk[rows, ln].astype(F32) * (GLA_DK ** -0.5), qk[rows, kl].astype(F32), b_all[rows, ln],
                    va[rows, c0:c0 + GLA_DV], va[rows, c1:c1 + GLA_DV], st_g[d][p],
                    pair_masks, bd_causal[d], fwd)
                o_ref[rows, c0:c0 + GLA_DV] = o[0:CHUNK].astype(BF16)
                o_ref[rows, c1:c1 + GLA_DV] = o[CHUNK:2 * CHUNK].astype(BF16)
            for p in range(HG_HEADS // 2):
                l0 = slice(2 * p * LANES, (2 * p + 1) * LANES)
                l1 = slice((2 * p + 1) * LANES, (2 * p + 2) * LANES)
                bl0 = slice(GLA_QK + l0.start, GLA_QK + l0.stop)
                bl1 = slice(GLA_QK + l1.start, GLA_QK + l1.stop)
                o, st_h[d][p] = _hgrn_pair_chunk(
                    qh[rows, l0], key_h[rows, l0], b_all[rows, bl0], ib[rows, l0],
                    qh[rows, l1], key_h[rows, l1], b_all[rows, bl1], ib[rows, l1],
                    st_h[d][p], bd_causal[d], fwd)
                o_ref[rows, GLA_V + l0.start:GLA_V + l0.stop] = o[0:CHUNK].astype(BF16)
                o_ref[rows, GLA_V + l1.start:GLA_V + l1.stop] = o[CHUNK:2 * CHUNK].astype(BF16)
    for d in range(2):
        for p in range(n_pairs):
            stg[d, p] = st_g[d][p]
        for p in range(HG_HEADS // 2):
            sth[d, p] = st_h[d][p]


def _even_scan_kernel(tf_ref, tb_ref, first_ref, last_ref, seq_ref,
                      qk_f, va_f, qb_f, fg_f, ib_f, al_f,
                      qk_b, va_b, qb_b, fg_b, ib_b, al_b,
                      w2_ref, ba_ref, lb_ref, sg_ref, sh_ref,
                      of_ref, ob_ref, stg_ref, sth_ref,
                      stg, sth):
    i = pl.program_id(0)
    is_first = first_ref[i] == 1
    is_last = last_ref[i] == 1
    is_ctx = seq_ref[i] < CTX_B

    @pl.when(jnp.logical_and(is_first, is_ctx))
    def _():
        stg[...] = jnp.zeros_like(stg)
        sth[...] = jnp.zeros_like(sth)

    @pl.when(jnp.logical_and(is_first, jnp.logical_not(is_ctx)))
    def _():
        for d in range(2):
            for p in range(GLA_HEADS // 2):
                stg[d, p] = sg_ref[d, p].T
            for h in range(HG_HEADS):
                sth[d, h // 2, :, (h % 2) * HG_DK:(h % 2 + 1) * HG_DK] = sh_ref[d, h].T

    _even_tile([(qk_f, va_f, qb_f, fg_f, ib_f, al_f, of_ref), (qk_b, va_b, qb_b, fg_b, ib_b, al_b, ob_ref)],
               w2_ref, ba_ref, lb_ref, stg, sth)

    @pl.when(jnp.logical_and(is_last, is_ctx))
    def _():
        for d in range(2):
            for p in range(GLA_HEADS // 2):
                stg_ref[d, p] = stg[d, p].T
            for h in range(HG_HEADS):
                sth_ref[d, h] = sth[d, h // 2, :, (h % 2) * HG_DK:(h % 2 + 1) * HG_DK].T


def _even_scan(zm, zg, w2_pad, b_a, lb, s_gla, s_hgrn):
    work = _scan_work()

    def zspec(width, col_block, which):
        return pl.BlockSpec((TL, width), lambda i, tf, tb, fi, la, sq: ((tf, tb)[which][i], col_block))

    def dir_specs(which):
        return [zspec(512, EV_QA // 512, which), zspec(512, EV_VA // 512, which),
                zspec(512, EV_QB // 512, which), zspec(512, (EV_FF, EV_FB)[which] // 512, which),
                zspec(512, EV_IB // 512, which), zspec(LANES, EV_AL // LANES, which)]

    def dir_args():
        return [zm, zm, zm, zg, zm, zg]

    def const(shape):
        return pl.BlockSpec(shape, lambda i, *_: (0,) * len(shape))

    def s_in(shape):
        return pl.BlockSpec((None,) + shape,
                            lambda i, tf, tb, fi, la, sq: (jnp.clip(sq[i] - CTX_B, 0, SMP_B - 1),) + (0,) * len(shape))

    def s_out(shape):
        return pl.BlockSpec((None,) + shape,
                            lambda i, tf, tb, fi, la, sq: (jnp.minimum(sq[i], CTX_B - 1),) + (0,) * len(shape))

    gshape = (2, GLA_HEADS // 2, LANES, LANES)
    hshape = (2, HG_HEADS, HG_DK, HG_DV)
    grid_spec = pltpu.PrefetchScalarGridSpec(
        num_scalar_prefetch=5, grid=(N_WORK,),
        in_specs=dir_specs(0) + dir_specs(1) + [
            const((2, LANES, GLA_QK)), const((2, 1, GLA_QK)), const((1, HG_K)),
            s_in(gshape), s_in(hshape)],
        out_specs=[pl.BlockSpec((TL, D), lambda i, tf, tb, fi, la, sq: (tf[i], 0)),
                   pl.BlockSpec((TL, D), lambda i, tf, tb, fi, la, sq: (tb[i], 0)),
                   s_out(gshape), s_out(hshape)],
        scratch_shapes=[pltpu.VMEM(gshape, F32),
                        pltpu.VMEM((2, HG_HEADS // 2, HG_DV, 2 * HG_DK), F32)])
    return pl.pallas_call(
        _even_scan_kernel, grid_spec=grid_spec,
        out_shape=[jax.ShapeDtypeStruct((T_ALL, D), BF16), jax.ShapeDtypeStruct((T_ALL, D), BF16),
                   jax.ShapeDtypeStruct((CTX_B,) + gshape, F32),
                   jax.ShapeDtypeStruct((CTX_B,) + hshape, F32)],
        compiler_params=_params(1), name="even_scan",
    )(*work, *dir_args(), *dir_args(), w2_pad, b_a, lb, s_gla, s_hgrn)


RET_HP = 2


def _ret_tables():
    lg = np.log1p(-np.exp2(-5.0 - np.arange(RET_HEADS, dtype=np.float64)))
    pos = np.arange(TL, dtype=np.float64)
    dm = np.zeros((2, RET_HEADS, TL, TL))
    rq = np.zeros((2, RET_HEADS, TL, 1))
    rk = np.zeros((2, RET_HEADS, TL, 1))
    gc = np.zeros((2, RET_HEADS, 1, RET_DV))
    diff = pos[:, None] - pos[None, :]
    kscale = RET_DK ** -0.5
    for h in range(RET_HEADS):
        dm[0, h] = np.where(diff >= 0, np.exp(lg[h] * np.maximum(diff, 0)), 0.0) * kscale
        rq[0, h, :, 0] = np.exp(lg[h] * (pos + 1))
        rk[0, h, :, 0] = np.exp(lg[h] * (TL - 1 - pos)) * kscale
        gc[0, h] = np.exp(lg[h] * TL)
        lb = lg[RET_HEADS - 1 - h]
        dm[1, h] = np.where(diff <= 0, np.exp(lb * np.maximum(-diff, 0)), 0.0) * kscale
        rq[1, h, :, 0] = np.exp(lb * (TL - pos))
        rk[1, h, :, 0] = np.exp(lb * pos) * kscale
        gc[1, h] = np.exp(lb * TL)
    return tuple(jnp.asarray(a, F32) for a in (dm, rq, rk, gc))


def _odd_dir(d, q_ref, k_ref, v_ref, dm_ref, rq_ref, rk_ref, gc_ref, st, o_ref):
    for hh in range(RET_HP):
        k = k_ref[:, hh * RET_DK:(hh + 1) * RET_DK]
        q = q_ref[:, hh * RET_DK:(hh + 1) * RET_DK]
        v = v_ref[:, hh * RET_DV:(hh + 1) * RET_DV]
        a = _dot_nt(q, k) * dm_ref[d, hh]
        s = st[d, hh]
        o = _dot(a.astype(BF16), v) + rq_ref[d, hh] * _dot(q, s.astype(BF16))
        kk = (k.astype(F32) * rk_ref[d, hh]).astype(BF16)
        st[d, hh] = gc_ref[d, hh] * s + _dot_tn(kk, v)
        o_ref[:, hh * RET_DV:(hh + 1) * RET_DV] = o.astype(BF16)


def _odd_scan_kernel(tf_ref, tb_ref, first_ref, last_ref, seq_ref,
                     q_f, k_f, v_f, q_b, k_b, v_b,
                     dm_ref, rq_ref, rk_ref, gc_ref, s0_ref,
                     of_ref, ob_ref, sout_ref, st):
    i = pl.program_id(1)
    is_first = first_ref[i] == 1
    is_last = last_ref[i] == 1
    is_ctx = seq_ref[i] < CTX_B

    @pl.when(jnp.logical_and(is_first, is_ctx))
    def _():
        st[...] = jnp.zeros_like(st)

    @pl.when(jnp.logical_and(is_first, jnp.logical_not(is_ctx)))
    def _():
        st[...] = s0_ref[...]

    _odd_dir(0, q_f, k_f, v_f, dm_ref, rq_ref, rk_ref, gc_ref, st, of_ref)
    _odd_dir(1, q_b, k_b, v_b, dm_ref, rq_ref, rk_ref, gc_ref, st, ob_ref)

    @pl.when(jnp.logical_and(is_last, is_ctx))
    def _():
        sout_ref[...] = st[...]


def _odd_scan(z, tables, s_ret):
    work = _scan_work()
    dm, rq, rk, gc = tables
    qw, vw = RET_HP * RET_DK, RET_HP * RET_DV

    def zspec(width, col0, which):
        return pl.BlockSpec((TL, width),
                            lambda hp, i, tf, tb, fi, la, sq: ((tf, tb)[which][i], col0 // width + hp))

    def dir_specs(which):
        return [zspec(qw, OD_Q, which), zspec(qw, OD_K, which), zspec(vw, OD_V, which)]

    def table(shape):
        return pl.BlockSpec((2, RET_HP) + shape, lambda hp, i, *_: (0, hp) + (0,) * len(shape))

    sshape = (2, RET_HP, RET_DK, RET_DV)
    grid_spec = pltpu.PrefetchScalarGridSpec(
        num_scalar_prefetch=5, grid=(RET_HEADS // RET_HP, N_WORK),
        in_specs=dir_specs(0) + dir_specs(1) + [
            table((TL, TL)), table((TL, 1)), table((TL, 1)), table((1, RET_DV)),
            pl.BlockSpec((None,) + sshape,
                         lambda hp, i, tf, tb, fi, la, sq:
                         (jnp.clip(sq[i] - CTX_B, 0, SMP_B - 1), 0, hp, 0, 0))],
        out_specs=[pl.BlockSpec((TL, vw), lambda hp, i, tf, tb, fi, la, sq: (tf[i], hp)),
                   pl.BlockSpec((TL, vw), lambda hp, i, tf, tb, fi, la, sq: (tb[i], hp)),
                   pl.BlockSpec((None,) + sshape,
                                lambda hp, i, tf, tb, fi, la, sq:
                                (jnp.minimum(sq[i], CTX_B - 1), 0, hp, 0, 0))],
        scratch_shapes=[pltpu.VMEM(sshape, F32)])
    return pl.pallas_call(
        _odd_scan_kernel, grid_spec=grid_spec,
        out_shape=[jax.ShapeDtypeStruct((T_ALL, RET_V), BF16), jax.ShapeDtypeStruct((T_ALL, RET_V), BF16),
                   jax.ShapeDtypeStruct((CTX_B, 2, RET_HEADS, RET_DK, RET_DV), F32)],
        compiler_params=_params(2), name="odd_scan",
    )(*work, z, z, z, z, z, z, dm, rq, rk, gc, s_ret)


def _head_rms(o, width):
    parts = []
    for j in range(0, o.shape[1], width):
        blk = o[:, j:j + width]
        parts.append(blk * lax.rsqrt(jnp.mean(blk * blk, axis=-1, keepdims=True) + NORM_EPS))
    return parts


def _route(h2, wr_hi_ref, wr_lo_ref, br_ref):
    h_hi, h_lo = _split_bf16(h2)
    logits = (_dot(h_hi, wr_hi_ref[...]) + _dot(h_hi, wr_lo_ref[...]) + _dot(h_lo, wr_hi_ref[...])
              + br_ref[...])
    lane_i = lax.broadcasted_iota(jnp.int32, logits.shape, 1)
    lane = lane_i.astype(F32)
    neg = -jnp.inf
    big = 1e9
    is_grp = (lane_i >= N_EXPERTS) & (lane_i < N_EXPERTS + N_GROUPS)
    lg = jnp.where(is_grp, logits, neg)
    mg = jnp.max(lg, axis=-1, keepdims=True)
    gsel = jnp.min(jnp.where(lg == mg, lane - N_EXPERTS, big), axis=-1, keepdims=True)
    pg = 1.0 / jnp.sum(jnp.where(is_grp, jnp.exp(lg - mg), 0.0), axis=-1, keepdims=True)
    in_grp = (lane_i < N_EXPERTS) & ((lane_i // EXP_PER_GROUP).astype(F32) == gsel)
    le = jnp.where(in_grp, logits, neg)
    m1 = jnp.max(le, axis=-1, keepdims=True)
    i1 = jnp.min(jnp.where(le == m1, lane, big), axis=-1, keepdims=True)
    le2 = jnp.where(lane == i1, neg, le)
    m2 = jnp.max(le2, axis=-1, keepdims=True)
    i2 = jnp.min(jnp.where(le2 == m2, lane, big), axis=-1, keepdims=True)
    e2 = jnp.exp(m2 - m1)
    w1 = pg / (1.0 + e2)
    w2 = pg * e2 / (1.0 + e2)
    first_lo = i1 < i2
    ia = jnp.minimum(i1, i2) - gsel * EXP_PER_GROUP
    ib = jnp.maximum(i1, i2) - gsel * EXP_PER_GROUP
    pair = gsel * PAIRS_PER_GROUP + ia * (2 * EXP_PER_GROUP - 1 - ia) * 0.5 + (ib - ia - 1.0)
    return pair, jnp.where(first_lo, w1, w2), jnp.where(first_lo, w2, w1)


def _post_tail(i, x, out, m, nw2_ref, wr_hi_ref, wr_lo_ref, br_ref,
               x1_ref, hrow_ref, meta_ref, cnt_ref, carry):
    x1 = x + m[2:3] * out
    x1_ref[...] = x1
    h2 = _norm_mod(x1, nw2_ref[...], m[3:4], m[4:5])
    pair, w_a, w_b = _route(h2, wr_hi_ref, wr_lo_ref, br_ref)

    @pl.when(i == 0)
    def _():
        carry[...] = jnp.zeros_like(carry)

    lane = lax.broadcasted_iota(jnp.int32, (TM, LANES), 1)
    onehot = jnp.where(lane.astype(F32) == pair, 1.0, 0.0)
    r = lax.broadcasted_iota(jnp.int32, (TM, TM), 0)
    c = lax.broadcasted_iota(jnp.int32, (TM, TM), 1)
    earlier = jnp.where(c < r, 1.0, 0.0).astype(BF16)
    before = _dot(earlier, onehot.astype(BF16)) + carry[...]
    rank = jnp.sum(onehot * before, axis=-1, keepdims=True)
    carry[...] += jnp.sum(onehot, axis=0, keepdims=True)
    cnt_ref[...] = carry[...]
    meta_ref[...] = jnp.where(lane == 0, pair, jnp.where(lane == 1, rank, 0.0))
    hrow_ref[:, 0:D] = h2
    hrow_ref[:, D:ROW_W] = jnp.where(lane == 0, w_a, jnp.where(lane == 1, w_b, 0.0))


def _post_even_kernel(of_ref, ob_ref, ra_ref, gb_ref, x_ref, mod_ref, gn_ref, wo_ref,
                      nw2_ref, wr_hi_ref, wr_lo_ref, br_ref,
                      x1_ref, hrow_ref, meta_ref, cnt_ref, carry):
    o = of_ref[...].astype(F32) + ob_ref[...].astype(F32)
    normed = jnp.concatenate(_head_rms(o, GLA_DV), axis=1)
    gate = jnp.concatenate([_silu(ra_ref[...].astype(F32)), _silu(gb_ref[...].astype(F32))], axis=1)
    mixed = (normed * gn_ref[...] * gate).astype(BF16)
    out = _dot(mixed, wo_ref[...])
    _post_tail(pl.program_id(0), x_ref[...], out, mod_ref[...], nw2_ref, wr_hi_ref, wr_lo_ref, br_ref,
               x1_ref, hrow_ref, meta_ref, cnt_ref, carry)


def _post_odd_kernel(of_ref, ob_ref, g_ref, x_ref, mod_ref, wo_ref,
                     nw2_ref, wr_hi_ref, wr_lo_ref, br_ref,
                     x1_ref, hrow_ref, meta_ref, cnt_ref, carry):
    o = of_ref[...].astype(F32) + ob_ref[...].astype(F32)
    normed = jnp.concatenate(_head_rms(o, RET_DV), axis=1)
    mixed = (normed * _silu(g_ref[...].astype(F32))).astype(BF16)
    out = _dot(mixed, wo_ref[...])
    _post_tail(pl.program_id(0), x_ref[...], out, mod_ref[...], nw2_ref, wr_hi_ref, wr_lo_ref, br_ref,
               x1_ref, hrow_ref, meta_ref, cnt_ref, carry)


def _post_common_specs():
    tile = lambda w: pl.BlockSpec((TM, w), lambda i: (i, 0))
    const = lambda s: pl.BlockSpec(s, lambda i: (0,) * len(s))
    mod = pl.BlockSpec((None, 6, D), lambda i: (_cond_of_tile(i, TM), 0, 0))
    out_specs = [tile(D), tile(ROW_W), tile(LANES), const((1, LANES))]
    out_shape = [jax.ShapeDtypeStruct((T_ALL, D), F32),
                 jax.ShapeDtypeStruct((T_ALL, ROW_W), F32),
                 jax.ShapeDtypeStruct((T_ALL, LANES), F32),
                 jax.ShapeDtypeStruct((1, LANES), F32)]
    scratch = [pltpu.VMEM((1, LANES), F32)]
    return tile, const, mod, out_specs, out_shape, scratch


def _post_even(o_f, o_b, z, x, mods_l, gn, wo_bf, nw2, wr_hi, wr_lo, br):
    tile, const, mod, out_specs, out_shape, scratch = _post_common_specs()
    zcol = lambda c0: pl.BlockSpec((TM, 512), lambda i: (i, c0 // 512))
    return pl.pallas_call(
        _post_even_kernel, grid=(T_ALL // TM,),
        in_specs=[tile(D), tile(D), zcol(EV_RA), zcol(EV_GB), tile(D), mod, const((1, D)),
                  const((D, D)), const((1, D)), const((D, ROUTER_N)), const((D, ROUTER_N)),
                  const((1, ROUTER_N))],
        out_specs=out_specs, out_shape=out_shape, scratch_shapes=scratch,
        compiler_params=_params(1), name="post_even",
    )(o_f, o_b, z, z, x, mods_l, gn, wo_bf, nw2, wr_hi, wr_lo, br)


def _post_odd(o_f, o_b, z, x, mods_l, wo_bf, nw2, wr_hi, wr_lo, br):
    tile, const, mod, out_specs, out_shape, scratch = _post_common_specs()
    return pl.pallas_call(
        _post_odd_kernel, grid=(T_ALL // TM,),
        in_specs=[tile(RET_V), tile(RET_V),
                  pl.BlockSpec((TM, RET_V), lambda i: (i, OD_G // RET_V)), tile(D), mod,
                  const((RET_V, D)), const((1, D)), const((D, ROUTER_N)), const((D, ROUTER_N)),
                  const((1, ROUTER_N))],
        out_specs=out_specs, out_shape=out_shape, scratch_shapes=scratch,
        compiler_params=_params(1), name="post_odd",
    )(o_f, o_b, z, x, mods_l, wo_bf, nw2, wr_hi, wr_lo, br)


def _pair_tables():
    ea, eb = [], []
    for g in range(N_GROUPS):
        for a in range(EXP_PER_GROUP):
            for b in range(a + 1, EXP_PER_GROUP):
                ea.append(g * EXP_PER_GROUP + a)
                eb.append(g * EXP_PER_GROUP + b)
    pad = LANES - len(ea)
    return (jnp.asarray(ea + [0] * pad, jnp.int32), jnp.asarray(eb + [0] * pad, jnp.int32))


def _dispatch_plan(meta, counts):
    pair = meta[:, 0].astype(jnp.int32)
    rank = meta[:, 1].astype(jnp.int32)
    cnt = counts[0].astype(jnp.int32)
    tiles_g = (cnt + TMG - 1) // TMG
    tile_end = jnp.cumsum(tiles_g)
    tile_start = tile_end - tiles_g
    slot0 = tile_start * TMG
    ids = jnp.arange(LANES, dtype=jnp.int32)
    dest = rank + jnp.sum(jnp.where(pair[:, None] == ids[None, :], slot0[None, :], 0), axis=1)
    n_real = tile_end[-1]
    j = jnp.arange(N_TILES_MOE, dtype=jnp.int32)
    jj = jnp.minimum(j, n_real - 1)
    grp = jnp.sum((jj[:, None] >= tile_end[None, :]).astype(jnp.int32), axis=1)
    ea_t, eb_t = _pair_tables()
    onehot_g = grp[:, None] == ids[None, :]
    pick = lambda v: jnp.sum(jnp.where(onehot_g, v[None, :], 0), axis=1)
    nv = jnp.clip(pick(cnt) - (jj - pick(tile_start)) * TMG, 0, TMG)
    nv = jnp.where(j < n_real, nv, 0)
    return dest, pick(ea_t), pick(eb_t), nv, n_real.reshape(1)


DISPATCH_STEPS = T_ALL // DISPATCH_ROWS


def _dispatch_kernel(dest_ref, nv_ref, h_ref, hs_hbm, zeros, sem, pad_sem):
    j = pl.program_id(0)
    base = j * DISPATCH_ROWS

    def pad_tile(tile):
        return pltpu.make_async_copy(zeros, hs_hbm.at[pl.ds(tile * TMG, TMG)], pad_sem.at[0])

    @pl.when(j == 0)
    def _():
        zeros[...] = jnp.zeros_like(zeros)

        def fill(tile, carry):
            @pl.when(nv_ref[tile] < TMG)
            def _():
                pad_tile(tile).start()
            return carry

        def drain(tile, carry):
            @pl.when(nv_ref[tile] < TMG)
            def _():
                pad_tile(tile).wait()
            return carry

        lax.fori_loop(0, N_TILES_MOE, fill, 0)
        lax.fori_loop(0, N_TILES_MOE, drain, 0)

    def issue(r, carry):
        pltpu.make_async_copy(h_ref.at[pl.ds(r, 1)], hs_hbm.at[pl.ds(dest_ref[base + r], 1)],
                              sem.at[0]).start()
        return carry

    lax.fori_loop(0, DISPATCH_ROWS, issue, 0, unroll=8)
    pltpu.make_async_copy(h_ref, hs_hbm.at[pl.ds(0, DISPATCH_ROWS)], sem.at[0]).wait()


def _dispatch(dest, nv, hrow):
    grid_spec = pltpu.PrefetchScalarGridSpec(
        num_scalar_prefetch=2, grid=(DISPATCH_STEPS,),
        in_specs=[pl.BlockSpec((DISPATCH_ROWS, ROW_W), lambda j, *_: (j, 0))],
        out_specs=pl.BlockSpec(memory_space=pl.ANY),
        scratch_shapes=[pltpu.VMEM((TMG, ROW_W), F32),
                        pltpu.SemaphoreType.DMA((1,)), pltpu.SemaphoreType.DMA((1,))])
    return pl.pallas_call(
        _dispatch_kernel, grid_spec=grid_spec,
        out_shape=jax.ShapeDtypeStruct((N_TILES_MOE * TMG, ROW_W), F32),
        compiler_params=_params(1), name="moe_dispatch",
    )(dest, nv, hrow)


def _moe_kernel(ea_ref, eb_ref, nreal_ref, hs_ref,
                wga_ref, wua_ref, wda_ref, wgb_ref, wub_ref, wdb_ref, ys_ref):
    j = pl.program_id(0)

    @pl.when(j < nreal_ref[0])
    def _():
        h = hs_ref[:, 0:D].astype(BF16)
        wrow = hs_ref[:, D:ROW_W]
        w_a, w_b = wrow[:, 0:1], wrow[:, 1:2]
        act_a = _silu(_dot(h, wga_ref[0])) * _dot(h, wua_ref[0]) * w_a
        act_b = _silu(_dot(h, wgb_ref[0])) * _dot(h, wub_ref[0]) * w_b
        ys_ref[...] = _dot(act_a.astype(BF16), wda_ref[0]) + _dot(act_b.astype(BF16), wdb_ref[0])

    @pl.when(j >= nreal_ref[0])
    def _():
        ys_ref[...] = jnp.zeros_like(ys_ref)


def _moe(ea, eb, n_real, hs, wg_bf, wu_bf, wd_bf):
    def wspec(shape, which):
        return pl.BlockSpec((1,) + shape, lambda j, ea, eb, nr: ((ea, eb)[which][j], 0, 0))

    up, down = (D, D_EXPERT), (D_EXPERT, D)
    grid_spec = pltpu.PrefetchScalarGridSpec(
        num_scalar_prefetch=3, grid=(N_TILES_MOE,),
        in_specs=[pl.BlockSpec((TMG, ROW_W), lambda j, ea, eb, nr: (jnp.minimum(j, nr[0] - 1), 0)),
                  wspec(up, 0), wspec(up, 0), wspec(down, 0),
                  wspec(up, 1), wspec(up, 1), wspec(down, 1)],
        out_specs=pl.BlockSpec((TMG, D), lambda j, ea, eb, nr: (j, 0)))
    return pl.pallas_call(
        _moe_kernel, grid_spec=grid_spec,
        out_shape=jax.ShapeDtypeStruct((N_TILES_MOE * TMG, D), F32),
        compiler_params=_params(1), name="moe",
    )(ea, eb, n_real, hs, wg_bf, wu_bf, wd_bf, wg_bf, wu_bf, wd_bf)


def _combine_kernel(dest_ref, ys_hbm, x1_ref, mod_ref, o_ref, buf, sem):
    j = pl.program_id(0)
    n = pl.num_programs(0)
    slot = j % 2

    def gather(tile, s):
        def issue(r, carry):
            pltpu.make_async_copy(ys_hbm.at[pl.ds(dest_ref[tile * TM + r], 1)],
                                  buf.at[s, pl.ds(r, 1)], sem.at[s]).start()
            return carry
        lax.fori_loop(0, TM, issue, 0, unroll=8)

    @pl.when(j == 0)
    def _():
        gather(0, 0)

    @pl.when(j + 1 < n)
    def _():
        gather(j + 1, 1 - slot)

    pltpu.make_async_copy(ys_hbm.at[pl.ds(0, TM)], buf.at[slot], sem.at[slot]).wait()
    o_ref[...] = x1_ref[...] + mod_ref[5:6] * buf[slot]


def _combine(dest, ys, x1, mods_l):
    grid_spec = pltpu.PrefetchScalarGridSpec(
        num_scalar_prefetch=1, grid=(T_ALL // TM,),
        in_specs=[pl.BlockSpec(memory_space=pl.ANY),
                  pl.BlockSpec((TM, D), lambda i, dest: (i, 0)),
                  pl.BlockSpec((None, 6, D), lambda i, dest: (_cond_of_tile(i, TM), 0, 0))],
        out_specs=pl.BlockSpec((TM, D), lambda i, dest: (i, 0)),
        scratch_shapes=[pltpu.VMEM((2, TM, D), F32), pltpu.SemaphoreType.DMA((2,))])
    return pl.pallas_call(
        _combine_kernel, grid_spec=grid_spec,
        out_shape=jax.ShapeDtypeStruct((T_ALL, D), F32),
        compiler_params=_params(1), name="moe_combine",
    )(dest, ys, x1, mods_l)


def _final_norm_kernel(x_ref, nw_ref, y_ref):
    x = x_ref[...]
    var = jnp.mean(x * x, axis=-1, keepdims=True)
    y_ref[...] = x * lax.rsqrt(var + NORM_EPS) * nw_ref[...]


def _final_norm(x, nw, tile0, n_tok):
    return pl.pallas_call(
        _final_norm_kernel, grid=(n_tok // TM,),
        in_specs=[pl.BlockSpec((TM, D), lambda i: (tile0 + i, 0)),
                  pl.BlockSpec((1, D), lambda i: (0, 0))],
        out_specs=pl.BlockSpec((TM, D), lambda i: (i, 0)),
        out_shape=jax.ShapeDtypeStruct((n_tok, D), F32),
        compiler_params=_params(1), name="final_norm",
    )(x, nw)


def _rope_tables():
    freqs = ROPE_BASE ** (-jnp.arange(ROPE_PAIRS, dtype=F32) / ROPE_PAIRS)
    t = jnp.arange(SMP_L)
    halves_c, halves_s = [], []
    for p in (t // GRID_W, t % GRID_W):
        ang = p.astype(F32)[:, None] * freqs
        cs, sn = jnp.cos(ang), jnp.sin(ang)
        halves_c += [cs, cs]
        halves_s += [-sn, sn]
    cos_t = jnp.concatenate(halves_c, axis=1)
    sin_t = jnp.concatenate(halves_s, axis=1)
    cos_t = jnp.concatenate([jnp.ones((TM, RET_DK), F32), cos_t], axis=0)
    sin_t = jnp.concatenate([jnp.zeros((TM, RET_DK), F32), sin_t], axis=0)
    return cos_t, sin_t


def kernel(x_prompt, x_sample, state_gla, state_hgrn, state_ret, c, c_ctx, norm1_w, norm2_w, normf_w,
           w_mod, b_mod, w_in_even, gla_w_alpha, gla_b_alpha, hgrn_lb_logits, gla_norm_w, hgrn_norm_w,
           w_out_even, w_in_odd, w_out_odd, router_g_w, router_g_b, router_e_w, router_e_b,
           moe_w_gate, moe_w_up, moe_w_down):
    x = jnp.concatenate([x_prompt.reshape(T_CTX, D), x_sample.reshape(T_SMP, D)], axis=0)
    cond8 = jnp.concatenate([c_ctx[None, :], c, jnp.zeros((COND_ROWS - N_COND, D), F32)], axis=0)
    mods = _mods(cond8, w_mod, b_mod).reshape(DEPTH, COND_ROWS, 6, D)
    lb_all = jnp.cumsum(jax.nn.softmax(hgrn_lb_logits.astype(F32), axis=0), axis=0)[:N_EVEN]
    cos_t, sin_t = _rope_tables()
    ret_tables = _ret_tables()

    st_gla, st_hgrn, st_ret = [], [], []
    for l in range(DEPTH):
        mods_l = mods[l]
        nw1 = norm1_w[l].reshape(1, D)
        nw2 = norm2_w[l].reshape(1, D)
        wr = jnp.concatenate([router_e_w[l], router_g_w[l],
                              jnp.zeros((D, ROUTER_N - N_EXPERTS - N_GROUPS), F32)], axis=1)
        wr_hi = wr.astype(BF16)
        wr_lo = (wr - wr_hi.astype(F32)).astype(BF16)
        br = jnp.concatenate([router_e_b[l], router_g_b[l],
                              jnp.zeros((ROUTER_N - N_EXPERTS - N_GROUPS,), F32)]).reshape(1, ROUTER_N)
        if l % 2 == 0:
            e = l // 2
            w = w_in_even[e]
            a0 = 2 * GLA_QK + 2 * GLA_V
            a1 = a0 + 2 * GLA_RANK
            f0, f1 = a1 + HG_K, a1 + 3 * HG_K
            wm_bf = jnp.concatenate([w[:, :a0], w[:, a1:f0], w[:, f1:]], axis=1).astype(BF16)
            wg_bf = jnp.concatenate([w[:, f0:f1], w[:, a0:a1],
                                     jnp.zeros((D, LANES - 2 * GLA_RANK), F32)], axis=1).astype(BF16)
            zm, zg = _inproj_even(x, mods_l, nw1, wm_bf, wg_bf)
            w2_pad = jnp.zeros((2, LANES, GLA_QK), F32)
            for d in range(2):
                w2_pad = w2_pad.at[d, d * GLA_RANK:(d + 1) * GLA_RANK].set(gla_w_alpha[e, d])
            s_g = state_gla[:, e].reshape(SMP_B, 2, GLA_HEADS // 2, LANES, LANES)
            o_f, o_b, ng, nh = _even_scan(zm, zg, w2_pad.astype(BF16), gla_b_alpha[e].reshape(2, 1, GLA_QK),
                                          lb_all[e].reshape(1, HG_K), s_g, state_hgrn[:, e])
            st_gla.append(ng.reshape(CTX_B, 2, GLA_HEADS, GLA_DK, GLA_DV))
            st_hgrn.append(nh)
            gn = jnp.concatenate([gla_norm_w[e], hgrn_norm_w[e]]).reshape(1, D)
            x1, hrow, meta, counts = _post_even(o_f, o_b, zm, x, mods_l, gn, w_out_even[e].astype(BF16),
                                                nw2, wr_hi, wr_lo, br)
        else:
            j = l // 2
            z = _inproj_odd(x, mods_l, nw1, w_in_odd[j].astype(BF16), cos_t, sin_t)
            o_f, o_b, nr = _odd_scan(z, ret_tables, state_ret[:, j])
            st_ret.append(nr)
            x1, hrow, meta, counts = _post_odd(o_f, o_b, z, x, mods_l, w_out_odd[j].astype(BF16),
                                               nw2, wr_hi, wr_lo, br)
        dest, ea, eb, nv, n_real = _dispatch_plan(meta, counts)
        hs = _dispatch(dest, nv, hrow)
        ys = _moe(ea, eb, n_real, hs, moe_w_gate[l].astype(BF16), moe_w_up[l].astype(BF16),
                  moe_w_down[l].astype(BF16))
        x = _combine(dest, ys, x1, mods_l)

    nwf = normf_w.reshape(1, D)
    y_prompt = _final_norm(x, nwf, 0, T_CTX).reshape(CTX_B, CTX_L, D)
    y_sample = _final_norm(x, nwf, T_CTX // TM, T_SMP).reshape(SMP_B, SMP_L, D)
    return (y_prompt, y_sample, jnp.stack(st_gla, axis=1), jnp.stack(st_hgrn, axis=1),
            jnp.stack(st_ret, axis=1))
```

```python
import functools

import numpy as np
import jax
import jax.numpy as jnp
from jax import lax
from jax.experimental import pallas as pl
from jax.experimental.pallas import tpu as pltpu

F32 = jnp.float32
BF16 = jnp.bfloat16

D = 1024
CTX_B, CTX_L = 32, 256
SMP_B, SMP_L = 4, 4096
DEPTH = 4
N_EVEN, N_ODD = 2, 2
GRID_W = 64
NORM_EPS = 1e-6
T_CTX = CTX_B * CTX_L
T_SMP = SMP_B * SMP_L
T_ALL = T_CTX + T_SMP
N_COND = 1 + SMP_B
COND_ROWS = 8

GLA_HEADS, GLA_DK, GLA_DV, GLA_RANK, GLA_TAU = 4, 64, 128, 16, 16.0
GLA_QK, GLA_V = GLA_HEADS * GLA_DK, GLA_HEADS * GLA_DV
HG_HEADS, HG_DK, HG_DV = 4, 128, 128
HG_K, HG_V = HG_HEADS * HG_DK, HG_HEADS * HG_DV
RET_HEADS, RET_DK, RET_DV = 4, 256, 512
RET_QK, RET_V = RET_HEADS * RET_DK, RET_HEADS * RET_DV
ROPE_BASE, ROPE_PAIRS = 10000.0, 64
N_GROUPS, EXP_PER_GROUP, N_EXPERTS, D_EXPERT = 4, 8, 32, 256

LANES = 128
CHUNK = 64
TL = 256
TM = 512
PAIRS_PER_GROUP = EXP_PER_GROUP * (EXP_PER_GROUP - 1) // 2
N_PAIRS = N_GROUPS * PAIRS_PER_GROUP
TMG = 256
N_TILES_MOE = T_ALL // TMG + N_PAIRS
ROW_W = D + LANES
DISPATCH_ROWS = 1024
VMEM_LIMIT = 56 * 1024 * 1024

EV_QA, EV_KA, EV_VA, EV_RA, EV_QB, EV_IB, EV_GB = 0, 256, 512, 1024, 1536, 2048, 2560
EV_MAIN = 3072
EV_FF, EV_FB, EV_AL = 0, 512, 1024
EV_GATE = 1024 + LANES
OD_Q, OD_K, OD_V, OD_G = 0, 1024, 2048, 4096
OD_N = 6144
ROUTER_N = LANES


def _dot(a, b):
    return jnp.dot(a, b, preferred_element_type=F32)


def _dot_nt(a, b):
    return lax.dot_general(a, b, (((1,), (1,)), ((), ())), preferred_element_type=F32)


def _dot_tn(a, b):
    return lax.dot_general(a, b, (((0,), (0,)), ((), ())), preferred_element_type=F32)


def _split_bf16(x):
    hi = x.astype(BF16)
    lo = (x - hi.astype(F32)).astype(BF16)
    return hi, lo


def _sigmoid(x):
    return 1.0 / (1.0 + jnp.exp(-x))


def _silu(x):
    return x * _sigmoid(x)


def _params(n_axes):
    return pltpu.CompilerParams(dimension_semantics=("arbitrary",) * n_axes,
                                vmem_limit_bytes=VMEM_LIMIT)


def _cond_of_tile(i, tm):
    n_ctx = T_CTX // tm
    per_seq = SMP_L // tm
    return jnp.where(i < n_ctx, 0, 1 + (i - n_ctx) // per_seq)


def _norm_mod(x, nw, shift, scale):
    var = jnp.mean(x * x, axis=-1, keepdims=True)
    return x * lax.rsqrt(var + NORM_EPS) * nw * (1.0 + scale) + shift


def _mods_kernel(cond_ref, w_ref, b_ref, o_ref):
    c = cond_ref[...]
    a_hi, a_lo = _split_bf16(_silu(c))
    w_hi, w_lo = _split_bf16(w_ref[0])
    o_ref[0] = _dot(a_hi, w_hi) + _dot(a_hi, w_lo) + _dot(a_lo, w_hi) + b_ref[0]


def _mods(cond8, w_mod, b_mod):
    return pl.pallas_call(
        _mods_kernel,
        grid=(DEPTH, 6),
        in_specs=[pl.BlockSpec((COND_ROWS, D), lambda l, j: (0, 0)),
                  pl.BlockSpec((1, D, D), lambda l, j: (l, 0, j)),
                  pl.BlockSpec((1, 1, D), lambda l, j: (l, 0, j))],
        out_specs=pl.BlockSpec((1, COND_ROWS, D), lambda l, j: (l, 0, j)),
        out_shape=jax.ShapeDtypeStruct((DEPTH, COND_ROWS, 6 * D), F32),
        compiler_params=_params(2), name="mods",
    )(cond8, w_mod, b_mod.reshape(DEPTH, 1, 6 * D))


def _inproj_even_kernel(x_ref, mod_ref, nw_ref, wm_ref, wg_ref, zm_ref, zg_ref):
    m = mod_ref[...]
    h = _norm_mod(x_ref[...], nw_ref[...], m[0:1], m[1:2]).astype(BF16)
    for j in range(0, EV_MAIN, 1024):
        zm_ref[:, j:j + 1024] = _dot(h, wm_ref[:, j:j + 1024]).astype(BF16)
    for j, width in ((0, 1024), (EV_AL, LANES)):
        zg_ref[:, j:j + width] = _dot(h, wg_ref[:, j:j + width])


def _inproj_odd_kernel(x_ref, mod_ref, nw_ref, w_ref, cos_ref, sin_ref, z_ref):
    m = mod_ref[...]
    h = _norm_mod(x_ref[...], nw_ref[...], m[0:1], m[1:2]).astype(BF16)
    for j in range(0, OD_V, 2 * LANES):
        zz = _dot(h, w_ref[:, j:j + 2 * LANES])
        for s in range(2):
            blk = zz[:, s * LANES:(s + 1) * LANES]
            cs = cos_ref[:, s * LANES:(s + 1) * LANES]
            sn = sin_ref[:, s * LANES:(s + 1) * LANES]
            z_ref[:, j + s * LANES:j + (s + 1) * LANES] = (
                blk * cs + pltpu.roll(blk, ROPE_PAIRS, 1) * sn).astype(BF16)
    step = 1024
    for j in range(OD_V, OD_N, step):
        z_ref[:, j:j + step] = _dot(h, w_ref[:, j:j + step]).astype(BF16)


def _inproj_even(x, mods_l, nw, wm_bf, wg_bf):
    return pl.pallas_call(
        _inproj_even_kernel,
        grid=(T_ALL // TM,),
        in_specs=[pl.BlockSpec((TM, D), lambda i: (i, 0)),
                  pl.BlockSpec((None, 6, D), lambda i: (_cond_of_tile(i, TM), 0, 0)),
                  pl.BlockSpec((1, D), lambda i: (0, 0)),
                  pl.BlockSpec((D, EV_MAIN), lambda i: (0, 0)),
                  pl.BlockSpec((D, EV_GATE), lambda i: (0, 0))],
        out_specs=[pl.BlockSpec((TM, EV_MAIN), lambda i: (i, 0)),
                   pl.BlockSpec((TM, EV_GATE), lambda i: (i, 0))],
        out_shape=[jax.ShapeDtypeStruct((T_ALL, EV_MAIN), BF16),
                   jax.ShapeDtypeStruct((T_ALL, EV_GATE), F32)],
        compiler_params=_params(1), name="inproj_even",
    )(x, mods_l, nw, wm_bf, wg_bf)


def _rope_block(i):
    n_ctx = T_CTX // TM
    per_seq = SMP_L // TM
    return jnp.where(i < n_ctx, 0, 1 + (i - n_ctx) % per_seq)


def _inproj_odd(x, mods_l, nw, w_bf, cos_t, sin_t):
    return pl.pallas_call(
        _inproj_odd_kernel,
        grid=(T_ALL // TM,),
        in_specs=[pl.BlockSpec((TM, D), lambda i: (i, 0)),
                  pl.BlockSpec((None, 6, D), lambda i: (_cond_of_tile(i, TM), 0, 0)),
                  pl.BlockSpec((1, D), lambda i: (0, 0)),
                  pl.BlockSpec((D, OD_N), lambda i: (0, 0)),
                  pl.BlockSpec((TM, RET_DK), lambda i: (_rope_block(i), 0)),
                  pl.BlockSpec((TM, RET_DK), lambda i: (_rope_block(i), 0))],
        out_specs=pl.BlockSpec((TM, OD_N), lambda i: (i, 0)),
        out_shape=jax.ShapeDtypeStruct((T_ALL, OD_N), BF16),
        compiler_params=_params(1), name="inproj_odd",
    )(x, mods_l, nw, w_bf, cos_t, sin_t)


def _scan_work():
    tf, tb, first, last, seq = [], [], [], [], []
    base = 0
    for s, length in enumerate([CTX_L] * CTX_B + [SMP_L] * SMP_B):
        n = length // TL
        for t in range(n):
            tf.append(base + t)
            tb.append(base + n - 1 - t)
            first.append(int(t == 0))
            last.append(int(t == n - 1))
            seq.append(s)
        base += n
    return tuple(np.asarray(a, np.int32) for a in (tf, tb, first, last, seq))


N_WORK = T_ALL // TL


def _decay_factors(q, k, b, fwd):
    b_last = b[CHUNK - 1:CHUNK] if fwd else b[0:1]
    b_mid = b[CHUNK // 2:CHUNK // 2 + 1]
    qi = q * jnp.exp(b - b_mid)
    ki = k * jnp.exp(b_mid - b)
    q_in = qi * jnp.exp(b_mid)
    kk = ki * jnp.exp(b_last - b_mid)
    return qi, ki, q_in, kk, jnp.exp(b_last)


def _stack(a, b):
    return jnp.concatenate([a, b], axis=0)


def _gla_pair_chunk(q, k, b, v0, v1, st, masks, bd_causal, fwd):
    qi, ki, q_in, kk, g = _decay_factors(q, k, b, fwd)
    m0, m1 = masks
    split = lambda t: _stack(jnp.where(m0, t, 0.0), jnp.where(m1, t, 0.0)).astype(BF16)
    ki_bf = ki.astype(BF16)
    a = jnp.where(bd_causal, _dot_nt(split(qi), _stack(ki_bf, ki_bf)), 0.0)
    vs = _stack(v0, v1)
    o = _dot(a.astype(BF16), vs) + _dot_nt(split(q_in), st.astype(BF16))
    return o, st * g + _dot_tn(vs, split(kk))


def _hgrn_pair_chunk(q0, k0, b0, v0, q1, k1, b1, v1, st, bd_causal, fwd):
    qi0, ki0, qin0, kk0, g0 = _decay_factors(q0, k0, b0, fwd)
    qi1, ki1, qin1, kk1, g1 = _decay_factors(q1, k1, b1, fwd)
    zero = jnp.zeros((CHUNK, LANES), F32)
    wide = lambda t0, t1: _stack(jnp.concatenate([t0, zero], axis=1),
                                 jnp.concatenate([zero, t1], axis=1)).astype(BF16)
    a = jnp.where(bd_causal, _dot_nt(_stack(qi0, qi1).astype(BF16), _stack(ki0, ki1).astype(BF16)), 0.0)
    vs = _stack(v0, v1)
    o = _dot(a.astype(BF16), vs) + _dot_nt(wide(qin0, qin1), st.astype(BF16))
    return o, st * jnp.concatenate([g0, g1], axis=1) + _dot_tn(vs, wide(kk0, kk1))


def _even_gates(d, fg, al, qb, tri, w2_ref, ba_ref, lbv):
    y = _dot(al[...].astype(BF16), w2_ref[d]) + ba_ref[d]
    la_a = (jnp.minimum(y, 0.0) - jnp.log(1.0 + jnp.exp(-jnp.abs(y)))) * (1.0 / GLA_TAU)
    f = fg[...]
    t = jnp.exp(-jnp.abs(f))
    rcp = 1.0 / (1.0 + t)
    sig_f = jnp.where(f >= 0.0, rcp, t * rcp)
    sig_nf = jnp.where(f >= 0.0, t * rcp, rcp)
    la_h = jnp.log(lbv + (1.0 - lbv) * sig_f)
    key_h = (1.0 - lbv) * sig_nf
    la_hi, la_lo = _split_bf16(jnp.concatenate([la_a, la_h], axis=1))
    b_all = _dot(tri, la_hi) + _dot(tri, la_lo)
    return b_all, key_h, _silu(qb[...].astype(F32))


def _even_tile(dir_refs, w2_ref, ba_ref, lb_ref, stg, sth):
    rc = lax.broadcasted_iota(jnp.int32, (2 * CHUNK, 2 * CHUNK), 0)
    cc = lax.broadcasted_iota(jnp.int32, (2 * CHUNK, 2 * CHUNK), 1)
    same_head = (rc // CHUNK) == (cc // CHUNK)
    bd_causal = [same_head & (cc <= rc), same_head & (cc >= rc)]
    rt = lax.broadcasted_iota(jnp.int32, (TL, TL), 0)
    ct = lax.broadcasted_iota(jnp.int32, (TL, TL), 1)
    same = (rt // CHUNK) == (ct // CHUNK)
    tri = [jnp.where(same & (ct <= rt), 1.0, 0.0).astype(BF16),
           jnp.where(same & (ct >= rt), 1.0, 0.0).astype(BF16)]
    lane = lax.broadcasted_iota(jnp.int32, (CHUNK, LANES), 1)
    pair_masks = [lane < GLA_DK, lane >= GLA_DK]
    lbv = lb_ref[...]
    n_pairs = GLA_HEADS // 2
    st_g = [[stg[d, p] for p in range(n_pairs)] for d in range(2)]
    st_h = [[sth[d, p] for p in range(HG_HEADS // 2)] for d in range(2)]
    gates = [_even_gates(d, dir_refs[d][3], dir_refs[d][5], dir_refs[d][2], tri[d], w2_ref, ba_ref, lbv)
             for d in range(2)]

    n_chunks = TL // CHUNK
    for ci in range(n_chunks):
        for d in range(2):
            qk, va, _, _, ib, _, o_ref = dir_refs[d]
            b_all, key_h, qh = gates[d]
            fwd = d == 0
            cpos = ci if fwd else n_chunks - 1 - ci
            rows = slice(cpos * CHUNK, (cpos + 1) * CHUNK)
            for p in range(n_pairs):
                ln = slice(p * LANES, (p + 1) * LANES)
                kl = slice(GLA_QK + p * LANES, GLA_QK + (p + 1) * LANES)
                c0, c1 = 2 * p * GLA_DV, (2 * p + 1) * GLA_DV
                o, st_g[d][p] = _gla_pair_chunk(
                    qk[rows, ln].astype(F32) * (GLA_DK ** -0.5), qk[rows, kl].astype(F32), b_all[rows, ln],
                    va[rows, c0:c0 + GLA_DV], va[rows, c1:c1 + GLA_DV], st_g[d][p],
                    pair_masks, bd_causal[d], fwd)
                o_ref[rows, c0:c0 + GLA_DV] = o[0:CHUNK].astype(BF16)
                o_ref[rows, c1:c1 + GLA_DV] = o[CHUNK:2 * CHUNK].astype(BF16)
            for p in range(HG_HEADS // 2):
                l0 = slice(2 * p * LANES, (2 * p + 1) * LANES)
                l1 = slice((2 * p + 1) * LANES, (2 * p + 2) * LANES)
                bl0 = slice(GLA_QK + l0.start, GLA_QK + l0.stop)
                bl1 = slice(GLA_QK + l1.start, GLA_QK + l1.stop)
                o, st_h[d][p] = _hgrn_pair_chunk(
                    qh[rows, l0], key_h[rows, l0], b_all[rows, bl0], ib[rows, l0],
                    qh[rows, l1], key_h[rows, l1], b_all[rows, bl1], ib[rows, l1],
                    st_h[d][p], bd_causal[d], fwd)
                o_ref[rows, GLA_V + l0.start:GLA_V + l0.stop] = o[0:CHUNK].astype(BF16)
                o_ref[rows, GLA_V + l1.start:GLA_V + l1.stop] = o[CHUNK:2 * CHUNK].astype(BF16)
    for d in range(2):
        for p in range(n_pairs):
            stg[d, p] = st_g[d][p]
        for p in range(HG_HEADS // 2):
            sth[d, p] = st_h[d][p]


def _even_scan_kernel(tf_ref, tb_ref, first_ref, last_ref, seq_ref,
                      qk_f, va_f, qb_f, fg_f, ib_f, al_f,
                      qk_b, va_b, qb_b, fg_b, ib_b, al_b,
                      w2_ref, ba_ref, lb_ref, sg_ref, sh_ref,
                      of_ref, ob_ref, stg_ref, sth_ref,
                      stg, sth):
    i = pl.program_id(0)
    is_first = first_ref[i] == 1
    is_last = last_ref[i] == 1
    is_ctx = seq_ref[i] < CTX_B

    @pl.when(jnp.logical_and(is_first, is_ctx))
    def _():
        stg[...] = jnp.zeros_like(stg)
        sth[...] = jnp.zeros_like(sth)

    @pl.when(jnp.logical_and(is_first, jnp.logical_not(is_ctx)))
    def _():
        for d in range(2):
            for p in range(GLA_HEADS // 2):
                stg[d, p] = sg_ref[d, p].T
            for h in range(HG_HEADS):
                sth[d, h // 2, :, (h % 2) * HG_DK:(h % 2 + 1) * HG_DK] = sh_ref[d, h].T

    _even_tile([(qk_f, va_f, qb_f, fg_f, ib_f, al_f, of_ref), (qk_b, va_b, qb_b, fg_b, ib_b, al_b, ob_ref)],
               w2_ref, ba_ref, lb_ref, stg, sth)

    @pl.when(jnp.logical_and(is_last, is_ctx))
    def _():
        for d in range(2):
            for p in range(GLA_HEADS // 2):
                stg_ref[d, p] = stg[d, p].T
            for h in range(HG_HEADS):
                sth_ref[d, h] = sth[d, h // 2, :, (h % 2) * HG_DK:(h % 2 + 1) * HG_DK].T


def _even_scan(zm, zg, w2_pad, b_a, lb, s_gla, s_hgrn):
    work = _scan_work()

    def zspec(width, col_block, which):
        return pl.BlockSpec((TL, width), lambda i, tf, tb, fi, la, sq: ((tf, tb)[which][i], col_block))

    def dir_specs(which):
        return [zspec(512, EV_QA // 512, which), zspec(512, EV_VA // 512, which),
                zspec(512, EV_QB // 512, which), zspec(512, (EV_FF, EV_FB)[which] // 512, which),
                zspec(512, EV_IB // 512, which), zspec(LANES, EV_AL // LANES, which)]

    def dir_args():
        return [zm, zm, zm, zg, zm, zg]

    def const(shape):
        return pl.BlockSpec(shape, lambda i, *_: (0,) * len(shape))

    def s_in(shape):
        return pl.BlockSpec((None,) + shape,
                            lambda i, tf, tb, fi, la, sq: (jnp.clip(sq[i] - CTX_B, 0, SMP_B - 1),) + (0,) * len(shape))

    def s_out(shape):
        return pl.BlockSpec((None,) + shape,
                            lambda i, tf, tb, fi, la, sq: (jnp.minimum(sq[i], CTX_B - 1),) + (0,) * len(shape))

    gshape = (2, GLA_HEADS // 2, LANES, LANES)
    hshape = (2, HG_HEADS, HG_DK, HG_DV)
    grid_spec = pltpu.PrefetchScalarGridSpec(
        num_scalar_prefetch=5, grid=(N_WORK,),
        in_specs=dir_specs(0) + dir_specs(1) + [
            const((2, LANES, GLA_QK)), const((2, 1, GLA_QK)), const((1, HG_K)),
            s_in(gshape), s_in(hshape)],
        out_specs=[pl.BlockSpec((TL, D), lambda i, tf, tb, fi, la, sq: (tf[i], 0)),
                   pl.BlockSpec((TL, D), lambda i, tf, tb, fi, la, sq: (tb[i], 0)),
                   s_out(gshape), s_out(hshape)],
        scratch_shapes=[pltpu.VMEM(gshape, F32),
                        pltpu.VMEM((2, HG_HEADS // 2, HG_DV, 2 * HG_DK), F32)])
    return pl.pallas_call(
        _even_scan_kernel, grid_spec=grid_spec,
        out_shape=[jax.ShapeDtypeStruct((T_ALL, D), BF16), jax.ShapeDtypeStruct((T_ALL, D), BF16),
                   jax.ShapeDtypeStruct((CTX_B,) + gshape, F32),
                   jax.ShapeDtypeStruct((CTX_B,) + hshape, F32)],
        compiler_params=_params(1), name="even_scan",
    )(*work, *dir_args(), *dir_args(), w2_pad, b_a, lb, s_gla, s_hgrn)


RET_HP = 2


def _ret_tables():
    lg = np.log1p(-np.exp2(-5.0 - np.arange(RET_HEADS, dtype=np.float64)))
    pos = np.arange(TL, dtype=np.float64)
    dm = np.zeros((2, RET_HEADS, TL, TL))
    rq = np.zeros((2, RET_HEADS, TL, 1))
    rk = np.zeros((2, RET_HEADS, TL, 1))
    gc = np.zeros((2, RET_HEADS, 1, RET_DV))
    diff = pos[:, None] - pos[None, :]
    kscale = RET_DK ** -0.5
    for h in range(RET_HEADS):
        dm[0, h] = np.where(diff >= 0, np.exp(lg[h] * np.maximum(diff, 0)), 0.0) * kscale
        rq[0, h, :, 0] = np.exp(lg[h] * (pos + 1))
        rk[0, h, :, 0] = np.exp(lg[h] * (TL - 1 - pos)) * kscale
        gc[0, h] = np.exp(lg[h] * TL)
        lb = lg[RET_HEADS - 1 - h]
        dm[1, h] = np.where(diff <= 0, np.exp(lb * np.maximum(-diff, 0)), 0.0) * kscale
        rq[1, h, :, 0] = np.exp(lb * (TL - pos))
        rk[1, h, :, 0] = np.exp(lb * pos) * kscale
        gc[1, h] = np.exp(lb * TL)
    return tuple(jnp.asarray(a, F32) for a in (dm, rq, rk, gc))


def _odd_dir(d, q_ref, k_ref, v_ref, dm_ref, rq_ref, rk_ref, gc_ref, st, o_ref):
    for hh in range(RET_HP):
        k = k_ref[:, hh * RET_DK:(hh + 1) * RET_DK]
        q = q_ref[:, hh * RET_DK:(hh + 1) * RET_DK]
        v = v_ref[:, hh * RET_DV:(hh + 1) * RET_DV]
        a = _dot_nt(q, k) * dm_ref[d, hh]
        s = st[d, hh]
        o = _dot(a.astype(BF16), v) + rq_ref[d, hh] * _dot(q, s.astype(BF16))
        kk = (k.astype(F32) * rk_ref[d, hh]).astype(BF16)
        st[d, hh] = gc_ref[d, hh] * s + _dot_tn(kk, v)
        o_ref[:, hh * RET_DV:(hh + 1) * RET_DV] = o.astype(BF16)


def _odd_scan_kernel(tf_ref, tb_ref, first_ref, last_ref, seq_ref,
                     q_f, k_f, v_f, q_b, k_b, v_b,
                     dm_ref, rq_ref, rk_ref, gc_ref, s0_ref, *rest, n_prev):
    prev_ref = rest[0] if n_prev else None
    of_ref, ob_ref, sout_ref, st = rest[-4:]
    i = pl.program_id(1)
    is_first = first_ref[i] == 1
    is_last = last_ref[i] == 1
    is_ctx = seq_ref[i] < CTX_B

    @pl.when(jnp.logical_and(is_first, is_ctx))
    def _():
        st[...] = jnp.zeros_like(st)

    @pl.when(jnp.logical_and(is_first, jnp.logical_not(is_ctx)))
    def _():
        st[...] = s0_ref[...]

    _odd_dir(0, q_f, k_f, v_f, dm_ref, rq_ref, rk_ref, gc_ref, st, of_ref)
    _odd_dir(1, q_b, k_b, v_b, dm_ref, rq_ref, rk_ref, gc_ref, st, ob_ref)

    @pl.when(jnp.logical_and(is_last, is_ctx))
    def _():
        for k in range(n_prev):
            sout_ref[k] = prev_ref[k]
        sout_ref[n_prev] = st[...]


def _odd_scan(z, tables, s_ret, prev):
    n_prev = 0 if prev is None else prev.shape[1]
    work = _scan_work()
    dm, rq, rk, gc = tables
    qw, vw = RET_HP * RET_DK, RET_HP * RET_DV

    def zspec(width, col0, which):
        return pl.BlockSpec((TL, width),
                            lambda hp, i, tf, tb, fi, la, sq: ((tf, tb)[which][i], col0 // width + hp))

    def dir_specs(which):
        return [zspec(qw, OD_Q, which), zspec(qw, OD_K, which), zspec(vw, OD_V, which)]

    def table(shape):
        return pl.BlockSpec((2, RET_HP) + shape, lambda hp, i, *_: (0, hp) + (0,) * len(shape))

    sshape = (2, RET_HP, RET_DK, RET_DV)

    def ctx_states(n_layers):
        return pl.BlockSpec((None, n_layers) + sshape,
                            lambda hp, i, tf, tb, fi, la, sq: (jnp.minimum(sq[i], CTX_B - 1), 0, 0, hp, 0, 0))

    grid_spec = pltpu.PrefetchScalarGridSpec(
        num_scalar_prefetch=5, grid=(RET_HEADS // RET_HP, N_WORK),
        in_specs=dir_specs(0) + dir_specs(1) + [
            table((TL, TL)), table((TL, 1)), table((TL, 1)), table((1, RET_DV)),
            pl.BlockSpec((None,) + sshape,
                         lambda hp, i, tf, tb, fi, la, sq:
                         (jnp.clip(sq[i] - CTX_B, 0, SMP_B - 1), 0, hp, 0, 0))]
        + ([ctx_states(n_prev)] if n_prev else []),
        out_specs=[pl.BlockSpec((TL, vw), lambda hp, i, tf, tb, fi, la, sq: (tf[i], hp)),
                   pl.BlockSpec((TL, vw), lambda hp, i, tf, tb, fi, la, sq: (tb[i], hp)),
                   ctx_states(n_prev + 1)],
        scratch_shapes=[pltpu.VMEM(sshape, F32)])
    return pl.pallas_call(
        functools.partial(_odd_scan_kernel, n_prev=n_prev), grid_spec=grid_spec,
        out_shape=[jax.ShapeDtypeStruct((T_ALL, RET_V), BF16), jax.ShapeDtypeStruct((T_ALL, RET_V), BF16),
                   jax.ShapeDtypeStruct((CTX_B, n_prev + 1, 2, RET_HEADS, RET_DK, RET_DV), F32)],
        compiler_params=_params(2), name="odd_scan",
    )(*work, z, z, z, z, z, z, dm, rq, rk, gc, s_ret, *([prev] if n_prev else []))


def _head_rms(o, width):
    parts = []
    for j in range(0, o.shape[1], width):
        blk = o[:, j:j + width]
        parts.append(blk * lax.rsqrt(jnp.mean(blk * blk, axis=-1, keepdims=True) + NORM_EPS))
    return parts


def _route(h2, wr_hi_ref, wr_lo_ref, br_ref):
    h_hi, h_lo = _split_bf16(h2)
    logits = (_dot(h_hi, wr_hi_ref[...]) + _dot(h_hi, wr_lo_ref[...]) + _dot(h_lo, wr_hi_ref[...])
              + br_ref[...])
    lane_i = lax.broadcasted_iota(jnp.int32, logits.shape, 1)
    lane = lane_i.astype(F32)
    neg = -jnp.inf
    big = 1e9
    is_grp = (lane_i >= N_EXPERTS) & (lane_i < N_EXPERTS + N_GROUPS)
    lg = jnp.where(is_grp, logits, neg)
    mg = jnp.max(lg, axis=-1, keepdims=True)
    gsel = jnp.min(jnp.where(lg == mg, lane - N_EXPERTS, big), axis=-1, keepdims=True)
    pg = 1.0 / jnp.sum(jnp.where(is_grp, jnp.exp(lg - mg), 0.0), axis=-1, keepdims=True)
    in_grp = (lane_i < N_EXPERTS) & ((lane_i // EXP_PER_GROUP).astype(F32) == gsel)
    le = jnp.where(in_grp, logits, neg)
    m1 = jnp.max(le, axis=-1, keepdims=True)
    i1 = jnp.min(jnp.where(le == m1, lane, big), axis=-1, keepdims=True)
    le2 = jnp.where(lane == i1, neg, le)
    m2 = jnp.max(le2, axis=-1, keepdims=True)
    i2 = jnp.min(jnp.where(le2 == m2, lane, big), axis=-1, keepdims=True)
    e2 = jnp.exp(m2 - m1)
    w1 = pg / (1.0 + e2)
    w2 = pg * e2 / (1.0 + e2)
    first_lo = i1 < i2
    ia = jnp.minimum(i1, i2) - gsel * EXP_PER_GROUP
    ib = jnp.maximum(i1, i2) - gsel * EXP_PER_GROUP
    pair = gsel * PAIRS_PER_GROUP + ia * (2 * EXP_PER_GROUP - 1 - ia) * 0.5 + (ib - ia - 1.0)
    return pair, jnp.where(first_lo, w1, w2), jnp.where(first_lo, w2, w1)


def _post_tail(i, x, out, m, nw2_ref, wr_hi_ref, wr_lo_ref, br_ref,
               x1_ref, hrow_ref, meta_ref, cnt_ref, carry):
    x1 = x + m[2:3] * out
    x1_ref[...] = x1
    h2 = _norm_mod(x1, nw2_ref[...], m[3:4], m[4:5])
    pair, w_a, w_b = _route(h2, wr_hi_ref, wr_lo_ref, br_ref)

    @pl.when(i == 0)
    def _():
        carry[...] = jnp.zeros_like(carry)

    lane = lax.broadcasted_iota(jnp.int32, (TM, LANES), 1)
    onehot = jnp.where(lane.astype(F32) == pair, 1.0, 0.0)
    r = lax.broadcasted_iota(jnp.int32, (TM, TM), 0)
    c = lax.broadcasted_iota(jnp.int32, (TM, TM), 1)
    earlier = jnp.where(c < r, 1.0, 0.0).astype(BF16)
    before = _dot(earlier, onehot.astype(BF16)) + carry[...]
    rank = jnp.sum(onehot * before, axis=-1, keepdims=True)
    carry[...] += jnp.sum(onehot, axis=0, keepdims=True)
    cnt_ref[...] = carry[...]
    meta_ref[...] = jnp.where(lane == 0, pair, jnp.where(lane == 1, rank, 0.0))
    hrow_ref[:, 0:D] = h2
    hrow_ref[:, D:ROW_W] = jnp.where(lane == 0, w_a, jnp.where(lane == 1, w_b, 0.0))


def _post_even_kernel(of_ref, ob_ref, ra_ref, gb_ref, x_ref, mod_ref, gn_ref, wo_ref,
                      nw2_ref, wr_hi_ref, wr_lo_ref, br_ref,
                      x1_ref, hrow_ref, meta_ref, cnt_ref, carry):
    o = of_ref[...].astype(F32) + ob_ref[...].astype(F32)
    normed = jnp.concatenate(_head_rms(o, GLA_DV), axis=1)
    gate = jnp.concatenate([_silu(ra_ref[...].astype(F32)), _silu(gb_ref[...].astype(F32))], axis=1)
    mixed = (normed * gn_ref[...] * gate).astype(BF16)
    out = _dot(mixed, wo_ref[...])
    _post_tail(pl.program_id(0), x_ref[...], out, mod_ref[...], nw2_ref, wr_hi_ref, wr_lo_ref, br_ref,
               x1_ref, hrow_ref, meta_ref, cnt_ref, carry)


def _post_odd_kernel(of_ref, ob_ref, g_ref, x_ref, mod_ref, wo_ref,
                     nw2_ref, wr_hi_ref, wr_lo_ref, br_ref,
                     x1_ref, hrow_ref, meta_ref, cnt_ref, carry):
    o = of_ref[...].astype(F32) + ob_ref[...].astype(F32)
    normed = jnp.concatenate(_head_rms(o, RET_DV), axis=1)
    mixed = (normed * _silu(g_ref[...].astype(F32))).astype(BF16)
    out = _dot(mixed, wo_ref[...])
    _post_tail(pl.program_id(0), x_ref[...], out, mod_ref[...], nw2_ref, wr_hi_ref, wr_lo_ref, br_ref,
               x1_ref, hrow_ref, meta_ref, cnt_ref, carry)


def _post_common_specs():
    tile = lambda w: pl.BlockSpec((TM, w), lambda i: (i, 0))
    const = lambda s: pl.BlockSpec(s, lambda i: (0,) * len(s))
    mod = pl.BlockSpec((None, 6, D), lambda i: (_cond_of_tile(i, TM), 0, 0))
    out_specs = [tile(D), tile(ROW_W), tile(LANES), const((1, LANES))]
    out_shape = [jax.ShapeDtypeStruct((T_ALL, D), F32),
                 jax.ShapeDtypeStruct((T_ALL, ROW_W), F32),
                 jax.ShapeDtypeStruct((T_ALL, LANES), F32),
                 jax.ShapeDtypeStruct((1, LANES), F32)]
    scratch = [pltpu.VMEM((1, LANES), F32)]
    return tile, const, mod, out_specs, out_shape, scratch


def _post_even(o_f, o_b, z, x, mods_l, gn, wo_bf, nw2, wr_hi, wr_lo, br):
    tile, const, mod, out_specs, out_shape, scratch = _post_common_specs()
    zcol = lambda c0: pl.BlockSpec((TM, 512), lambda i: (i, c0 // 512))
    return pl.pallas_call(
        _post_even_kernel, grid=(T_ALL // TM,),
        in_specs=[tile(D), tile(D), zcol(EV_RA), zcol(EV_GB), tile(D), mod, const((1, D)),
                  const((D, D)), const((1, D)), const((D, ROUTER_N)), const((D, ROUTER_N)),
                  const((1, ROUTER_N))],
        out_specs=out_specs, out_shape=out_shape, scratch_shapes=scratch,
        compiler_params=_params(1), name="post_even",
    )(o_f, o_b, z, z, x, mods_l, gn, wo_bf, nw2, wr_hi, wr_lo, br)


def _post_odd(o_f, o_b, z, x, mods_l, wo_bf, nw2, wr_hi, wr_lo, br):
    tile, const, mod, out_specs, out_shape, scratch = _post_common_specs()
    return pl.pallas_call(
        _post_odd_kernel, grid=(T_ALL // TM,),
        in_specs=[tile(RET_V), tile(RET_V),
                  pl.BlockSpec((TM, RET_V), lambda i: (i, OD_G // RET_V)), tile(D), mod,
                  const((RET_V, D)), const((1, D)), const((D, ROUTER_N)), const((D, ROUTER_N)),
                  const((1, ROUTER_N))],
        out_specs=out_specs, out_shape=out_shape, scratch_shapes=scratch,
        compiler_params=_params(1), name="post_odd",
    )(o_f, o_b, z, x, mods_l, wo_bf, nw2, wr_hi, wr_lo, br)


def _pair_tables():
    ea, eb = [], []
    for g in range(N_GROUPS):
        for a in range(EXP_PER_GROUP):
            for b in range(a + 1, EXP_PER_GROUP):
                ea.append(g * EXP_PER_GROUP + a)
                eb.append(g * EXP_PER_GROUP + b)
    pad = LANES - len(ea)
    return (jnp.asarray(ea + [0] * pad, jnp.int32), jnp.asarray(eb + [0] * pad, jnp.int32))


def _dispatch_plan(meta, counts):
    pair = meta[:, 0].astype(jnp.int32)
    rank = meta[:, 1].astype(jnp.int32)
    cnt = counts[0].astype(jnp.int32)
    tiles_g = (cnt + TMG - 1) // TMG
    tile_end = jnp.cumsum(tiles_g)
    tile_start = tile_end - tiles_g
    slot0 = tile_start * TMG
    ids = jnp.arange(LANES, dtype=jnp.int32)
    dest = rank + jnp.sum(jnp.where(pair[:, None] == ids[None, :], slot0[None, :], 0), axis=1)
    n_real = tile_end[-1]
    j = jnp.arange(N_TILES_MOE, dtype=jnp.int32)
    jj = jnp.minimum(j, n_real - 1)
    grp = jnp.sum((jj[:, None] >= tile_end[None, :]).astype(jnp.int32), axis=1)
    ea_t, eb_t = _pair_tables()
    onehot_g = grp[:, None] == ids[None, :]
    pick = lambda v: jnp.sum(jnp.where(onehot_g, v[None, :], 0), axis=1)
    nv = jnp.clip(pick(cnt) - (jj - pick(tile_start)) * TMG, 0, TMG)
    nv = jnp.where(j < n_real, nv, 0)
    return dest, pick(ea_t), pick(eb_t), nv, n_real.reshape(1)


DISPATCH_STEPS = T_ALL // DISPATCH_ROWS


def _dispatch_kernel(dest_ref, nv_ref, h_ref, hs_hbm, zeros, sem, pad_sem):
    j = pl.program_id(0)
    base = j * DISPATCH_ROWS

    def pad_tile(tile):
        return pltpu.make_async_copy(zeros, hs_hbm.at[pl.ds(tile * TMG, TMG)], pad_sem.at[0])

    @pl.when(j == 0)
    def _():
        zeros[...] = jnp.zeros_like(zeros)

        def fill(tile, carry):
            @pl.when(nv_ref[tile] < TMG)
            def _():
                pad_tile(tile).start()
            return carry

        def drain(tile, carry):
            @pl.when(nv_ref[tile] < TMG)
            def _():
                pad_tile(tile).wait()
            return carry

        lax.fori_loop(0, N_TILES_MOE, fill, 0)
        lax.fori_loop(0, N_TILES_MOE, drain, 0)

    def issue(r, carry):
        pltpu.make_async_copy(h_ref.at[pl.ds(r, 1)], hs_hbm.at[pl.ds(dest_ref[base + r], 1)],
                              sem.at[0]).start()
        return carry

    lax.fori_loop(0, DISPATCH_ROWS, issue, 0, unroll=8)
    pltpu.make_async_copy(h_ref, hs_hbm.at[pl.ds(0, DISPATCH_ROWS)], sem.at[0]).wait()


def _dispatch(dest, nv, hrow):
    grid_spec = pltpu.PrefetchScalarGridSpec(
        num_scalar_prefetch=2, grid=(DISPATCH_STEPS,),
        in_specs=[pl.BlockSpec((DISPATCH_ROWS, ROW_W), lambda j, *_: (j, 0))],
        out_specs=pl.BlockSpec(memory_space=pl.ANY),
        scratch_shapes=[pltpu.VMEM((TMG, ROW_W), F32),
                        pltpu.SemaphoreType.DMA((1,)), pltpu.SemaphoreType.DMA((1,))])
    return pl.pallas_call(
        _dispatch_kernel, grid_spec=grid_spec,
        out_shape=jax.ShapeDtypeStruct((N_TILES_MOE * TMG, ROW_W), F32),
        compiler_params=_params(1), name="moe_dispatch",
    )(dest, nv, hrow)


def _moe_kernel(ea_ref, eb_ref, nreal_ref, hs_ref,
                wga_ref, wua_ref, wda_ref, wgb_ref, wub_ref, wdb_ref, ys_ref):
    j = pl.program_id(0)

    @pl.when(j < nreal_ref[0])
    def _():
        h = hs_ref[:, 0:D].astype(BF16)
        wrow = hs_ref[:, D:ROW_W]
        w_a, w_b = wrow[:, 0:1], wrow[:, 1:2]
        act_a = _silu(_dot(h, wga_ref[0])) * _dot(h, wua_ref[0]) * w_a
        act_b = _silu(_dot(h, wgb_ref[0])) * _dot(h, wub_ref[0]) * w_b
        ys_ref[...] = _dot(act_a.astype(BF16), wda_ref[0]) + _dot(act_b.astype(BF16), wdb_ref[0])

    @pl.when(j >= nreal_ref[0])
    def _():
        ys_ref[...] = jnp.zeros_like(ys_ref)


def _moe(ea, eb, n_real, hs, wg_bf, wu_bf, wd_bf):
    def wspec(shape, which):
        return pl.BlockSpec((1,) + shape, lambda j, ea, eb, nr: ((ea, eb)[which][j], 0, 0))

    up, down = (D, D_EXPERT), (D_EXPERT, D)
    grid_spec = pltpu.PrefetchScalarGridSpec(
        num_scalar_prefetch=3, grid=(N_TILES_MOE,),
        in_specs=[pl.BlockSpec((TMG, ROW_W), lambda j, ea, eb, nr: (jnp.minimum(j, nr[0] - 1), 0)),
                  wspec(up, 0), wspec(up, 0), wspec(down, 0),
                  wspec(up, 1), wspec(up, 1), wspec(down, 1)],
        out_specs=pl.BlockSpec((TMG, D), lambda j, ea, eb, nr: (j, 0)))
    return pl.pallas_call(
        _moe_kernel, grid_spec=grid_spec,
        out_shape=jax.ShapeDtypeStruct((N_TILES_MOE * TMG, D), F32),
        compiler_params=_params(1), name="moe",
    )(ea, eb, n_real, hs, wg_bf, wu_bf, wd_bf, wg_bf, wu_bf, wd_bf)


def _combine_kernel(dest_ref, ys_hbm, x1_ref, mod_ref, nwf_ref, o_ref, buf, sem, *, tile0, final):
    j = pl.program_id(0)
    n = pl.num_programs(0)
    slot = j % 2

    def gather(tile, s):
        def issue(r, carry):
            pltpu.make_async_copy(ys_hbm.at[pl.ds(dest_ref[(tile0 + tile) * TM + r], 1)],
                                  buf.at[s, pl.ds(r, 1)], sem.at[s]).start()
            return carry
        lax.fori_loop(0, TM, issue, 0, unroll=8)

    @pl.when(j == 0)
    def _():
        gather(0, 0)

    @pl.when(j + 1 < n)
    def _():
        gather(j + 1, 1 - slot)

    pltpu.make_async_copy(ys_hbm.at[pl.ds(0, TM)], buf.at[slot], sem.at[slot]).wait()
    x = x1_ref[...] + mod_ref[5:6] * buf[slot]
    if final:
        var = jnp.mean(x * x, axis=-1, keepdims=True)
        x = x * lax.rsqrt(var + NORM_EPS) * nwf_ref[...]
    o_ref[...] = x


def _combine(dest, ys, x1, mods_l, nwf, tile0=0, n_tok=T_ALL, final=False):
    grid_spec = pltpu.PrefetchScalarGridSpec(
        num_scalar_prefetch=1, grid=(n_tok // TM,),
        in_specs=[pl.BlockSpec(memory_space=pl.ANY),
                  pl.BlockSpec((TM, D), lambda i, dest: (tile0 + i, 0)),
                  pl.BlockSpec((None, 6, D), lambda i, dest: (_cond_of_tile(tile0 + i, TM), 0, 0)),
                  pl.BlockSpec((1, D), lambda i, dest: (0, 0))],
        out_specs=pl.BlockSpec((TM, D), lambda i, dest: (i, 0)),
        scratch_shapes=[pltpu.VMEM((2, TM, D), F32), pltpu.SemaphoreType.DMA((2,))])
    return pl.pallas_call(
        functools.partial(_combine_kernel, tile0=tile0, final=final), grid_spec=grid_spec,
        out_shape=jax.ShapeDtypeStruct((n_tok, D), F32),
        compiler_params=_params(1), name="moe_combine",
    )(dest, ys, x1, mods_l, nwf)


def _rope_tables():
    freqs = ROPE_BASE ** (-jnp.arange(ROPE_PAIRS, dtype=F32) / ROPE_PAIRS)
    t = jnp.arange(SMP_L)
    halves_c, halves_s = [], []
    for p in (t // GRID_W, t % GRID_W):
        ang = p.astype(F32)[:, None] * freqs
        cs, sn = jnp.cos(ang), jnp.sin(ang)
        halves_c += [cs, cs]
        halves_s += [-sn, sn]
    cos_t = jnp.concatenate(halves_c, axis=1)
    sin_t = jnp.concatenate(halves_s, axis=1)
    cos_t = jnp.concatenate([jnp.ones((TM, RET_DK), F32), cos_t], axis=0)
    sin_t = jnp.concatenate([jnp.zeros((TM, RET_DK), F32), sin_t], axis=0)
    return cos_t, sin_t


def kernel(x_prompt, x_sample, state_gla, state_hgrn, state_ret, c, c_ctx, norm1_w, norm2_w, normf_w,
           w_mod, b_mod, w_in_even, gla_w_alpha, gla_b_alpha, hgrn_lb_logits, gla_norm_w, hgrn_norm_w,
           w_out_even, w_in_odd, w_out_odd, router_g_w, router_g_b, router_e_w, router_e_b,
           moe_w_gate, moe_w_up, moe_w_down):
    x = jnp.concatenate([x_prompt.reshape(T_CTX, D), x_sample.reshape(T_SMP, D)], axis=0)
    cond8 = jnp.concatenate([c_ctx[None, :], c, jnp.zeros((COND_ROWS - N_COND, D), F32)], axis=0)
    mods = _mods(cond8, w_mod, b_mod).reshape(DEPTH, COND_ROWS, 6, D)
    lb_all = jnp.cumsum(jax.nn.softmax(hgrn_lb_logits.astype(F32), axis=0), axis=0)[:N_EVEN]
    cos_t, sin_t = _rope_tables()
    ret_tables = _ret_tables()

    nwf = normf_w.reshape(1, D)
    st_gla, st_hgrn, st_ret = [], [], None
    for l in range(DEPTH):
        mods_l = mods[l]
        nw1 = norm1_w[l].reshape(1, D)
        nw2 = norm2_w[l].reshape(1, D)
        wr = jnp.concatenate([router_e_w[l], router_g_w[l],
                              jnp.zeros((D, ROUTER_N - N_EXPERTS - N_GROUPS), F32)], axis=1)
        wr_hi = wr.astype(BF16)
        wr_lo = (wr - wr_hi.astype(F32)).astype(BF16)
        br = jnp.concatenate([router_e_b[l], router_g_b[l],
                              jnp.zeros((ROUTER_N - N_EXPERTS - N_GROUPS,), F32)]).reshape(1, ROUTER_N)
        if l % 2 == 0:
            e = l // 2
            w = w_in_even[e]
            a0 = 2 * GLA_QK + 2 * GLA_V
            a1 = a0 + 2 * GLA_RANK
            f0, f1 = a1 + HG_K, a1 + 3 * HG_K
            wm_bf = jnp.concatenate([w[:, :a0], w[:, a1:f0], w[:, f1:]], axis=1).astype(BF16)
            wg_bf = jnp.concatenate([w[:, f0:f1], w[:, a0:a1],
                                     jnp.zeros((D, LANES - 2 * GLA_RANK), F32)], axis=1).astype(BF16)
            zm, zg = _inproj_even(x, mods_l, nw1, wm_bf, wg_bf)
            w2_pad = jnp.zeros((2, LANES, GLA_QK), F32)
            for d in range(2):
                w2_pad = w2_pad.at[d, d * GLA_RANK:(d + 1) * GLA_RANK].set(gla_w_alpha[e, d])
            s_g = state_gla[:, e].reshape(SMP_B, 2, GLA_HEADS // 2, LANES, LANES)
            o_f, o_b, ng, nh = _even_scan(zm, zg, w2_pad.astype(BF16), gla_b_alpha[e].reshape(2, 1, GLA_QK),
                                          lb_all[e].reshape(1, HG_K), s_g, state_hgrn[:, e])
            st_gla.append(ng.reshape(CTX_B, 2, GLA_HEADS, GLA_DK, GLA_DV))
            st_hgrn.append(nh)
            gn = jnp.concatenate([gla_norm_w[e], hgrn_norm_w[e]]).reshape(1, D)
            x1, hrow, meta, counts = _post_even(o_f, o_b, zm, x, mods_l, gn, w_out_even[e].astype(BF16),
                                                nw2, wr_hi, wr_lo, br)
        else:
            j = l // 2
            z = _inproj_odd(x, mods_l, nw1, w_in_odd[j].astype(BF16), cos_t, sin_t)
            o_f, o_b, st_ret = _odd_scan(z, ret_tables, state_ret[:, j], st_ret)
            x1, hrow, meta, counts = _post_odd(o_f, o_b, z, x, mods_l, w_out_odd[j].astype(BF16),
                                               nw2, wr_hi, wr_lo, br)
        dest, ea, eb, nv, n_real = _dispatch_plan(meta, counts)
        hs = _dispatch(dest, nv, hrow)
        ys = _moe(ea, eb, n_real, hs, moe_w_gate[l].astype(BF16), moe_w_up[l].astype(BF16),
                  moe_w_down[l].astype(BF16))
        if l < DEPTH - 1:
            x = _combine(dest, ys, x1, mods_l, nwf)
    y_prompt = _combine(dest, ys, x1, mods_l, nwf, 0, T_CTX, final=True).reshape(CTX_B, CTX_L, D)
    y_sample = _combine(dest, ys, x1, mods_l, nwf, T_CTX // TM, T_SMP, final=True).reshape(SMP_B, SMP_L, D)
    return (y_prompt, y_sample, jnp.stack(st_gla, axis=1), jnp.stack(st_hgrn, axis=1), st_ret)
```

```python
import functools

import numpy as np
import jax
import jax.numpy as jnp
from jax import lax
from jax.experimental import pallas as pl
from jax.experimental.pallas import tpu as pltpu

F32 = jnp.float32
BF16 = jnp.bfloat16

D = 1024
CTX_B, CTX_L = 32, 256
SMP_B, SMP_L = 4, 4096
DEPTH = 4
N_EVEN, N_ODD = 2, 2
GRID_W = 64
NORM_EPS = 1e-6
T_CTX = CTX_B * CTX_L
T_SMP = SMP_B * SMP_L
T_ALL = T_CTX + T_SMP
N_COND = 1 + SMP_B
COND_ROWS = 8

GLA_HEADS, GLA_DK, GLA_DV, GLA_RANK, GLA_TAU = 4, 64, 128, 16, 16.0
GLA_QK, GLA_V = GLA_HEADS * GLA_DK, GLA_HEADS * GLA_DV
HG_HEADS, HG_DK, HG_DV = 4, 128, 128
HG_K, HG_V = HG_HEADS * HG_DK, HG_HEADS * HG_DV
RET_HEADS, RET_DK, RET_DV = 4, 256, 512
RET_QK, RET_V = RET_HEADS * RET_DK, RET_HEADS * RET_DV
ROPE_BASE, ROPE_PAIRS = 10000.0, 64
N_GROUPS, EXP_PER_GROUP, N_EXPERTS, D_EXPERT = 4, 8, 32, 256

LANES = 128
CHUNK = 64
TL = 256
TM = 512
PAIRS_PER_GROUP = EXP_PER_GROUP * (EXP_PER_GROUP - 1) // 2
N_PAIRS = N_GROUPS * PAIRS_PER_GROUP
TMG = 256
N_TILES_MOE = T_ALL // TMG + N_PAIRS
ROW_W = D + LANES
DISPATCH_ROWS = 1024
ROW_DMA_UNROLL = 8
VMEM_LIMIT = 56 * 1024 * 1024

EV_QA, EV_KA, EV_VA, EV_RA, EV_QB, EV_IB, EV_GB = 0, 256, 512, 1024, 1536, 2048, 2560
EV_MAIN = 3072
EV_FF, EV_FB, EV_AL = 0, 512, 1024
EV_GATE = 1024 + LANES
OD_Q, OD_K, OD_V, OD_G = 0, 1024, 2048, 4096
OD_N = 6144
ROUTER_N = LANES


def _dot(a, b):
    return jnp.dot(a, b, preferred_element_type=F32)


def _dot_nt(a, b):
    return lax.dot_general(a, b, (((1,), (1,)), ((), ())), preferred_element_type=F32)


def _dot_tn(a, b):
    return lax.dot_general(a, b, (((0,), (0,)), ((), ())), preferred_element_type=F32)


def _split_bf16(x):
    hi = x.astype(BF16)
    lo = (x - hi.astype(F32)).astype(BF16)
    return hi, lo


def _sigmoid(x):
    return 1.0 / (1.0 + jnp.exp(-x))


def _silu(x):
    return x * _sigmoid(x)


def _params(n_axes):
    return pltpu.CompilerParams(dimension_semantics=("arbitrary",) * n_axes,
                                vmem_limit_bytes=VMEM_LIMIT)


def _cond_of_tile(i, tm):
    n_ctx = T_CTX // tm
    per_seq = SMP_L // tm
    return jnp.where(i < n_ctx, 0, 1 + (i - n_ctx) // per_seq)


def _norm_mod(x, nw, shift, scale):
    var = jnp.mean(x * x, axis=-1, keepdims=True)
    return x * lax.rsqrt(var + NORM_EPS) * nw * (1.0 + scale) + shift


def _mods_kernel(cond_ref, w_ref, b_ref, o_ref):
    c = cond_ref[...]
    a_hi, a_lo = _split_bf16(_silu(c))
    w_hi, w_lo = _split_bf16(w_ref[0])
    o_ref[0] = _dot(a_hi, w_hi) + _dot(a_hi, w_lo) + _dot(a_lo, w_hi) + b_ref[0]


def _mods(cond8, w_mod, b_mod):
    return pl.pallas_call(
        _mods_kernel,
        grid=(DEPTH, 6),
        in_specs=[pl.BlockSpec((COND_ROWS, D), lambda l, j: (0, 0)),
                  pl.BlockSpec((1, D, D), lambda l, j: (l, 0, j)),
                  pl.BlockSpec((1, 1, D), lambda l, j: (l, 0, j))],
        out_specs=pl.BlockSpec((1, COND_ROWS, D), lambda l, j: (l, 0, j)),
        out_shape=jax.ShapeDtypeStruct((DEPTH, COND_ROWS, 6 * D), F32),
        compiler_params=_params(2), name="mods",
    )(cond8, w_mod, b_mod.reshape(DEPTH, 1, 6 * D))


def _x_specs(x_parts):
    if len(x_parts) == 1:
        return [pl.BlockSpec((TM, D), lambda i: (i, 0))]
    n_ctx = T_CTX // TM
    return [pl.BlockSpec((TM, D), lambda i: (jnp.minimum(i, n_ctx - 1), 0)),
            pl.BlockSpec((TM, D), lambda i: (jnp.maximum(i - n_ctx, 0), 0))]


def _load_x(x_refs):
    if len(x_refs) == 1:
        return x_refs[0][...]
    return jnp.where(pl.program_id(0) < T_CTX // TM, x_refs[0][...], x_refs[1][...])


def _inproj_even_kernel(*refs, n_x):
    x_refs = refs[:n_x]
    mod_ref, nw_ref, wm_ref, wg_ref, zm_ref, zg_ref = refs[n_x:]
    m = mod_ref[...]
    h = _norm_mod(_load_x(x_refs), nw_ref[...], m[0:1], m[1:2]).astype(BF16)
    for j in range(0, EV_MAIN, 1024):
        zm_ref[:, j:j + 1024] = _dot(h, wm_ref[:, j:j + 1024]).astype(BF16)
    for j, width in ((0, 1024), (EV_AL, LANES)):
        zg_ref[:, j:j + width] = _dot(h, wg_ref[:, j:j + width])


def _inproj_odd_kernel(x_ref, mod_ref, nw_ref, w_ref, cos_ref, sin_ref, z_ref):
    m = mod_ref[...]
    h = _norm_mod(x_ref[...], nw_ref[...], m[0:1], m[1:2]).astype(BF16)
    for j in range(0, OD_V, 2 * LANES):
        zz = _dot(h, w_ref[:, j:j + 2 * LANES])
        for s in range(2):
            blk = zz[:, s * LANES:(s + 1) * LANES]
            cs = cos_ref[:, s * LANES:(s + 1) * LANES]
            sn = sin_ref[:, s * LANES:(s + 1) * LANES]
            z_ref[:, j + s * LANES:j + (s + 1) * LANES] = (
                blk * cs + pltpu.roll(blk, ROPE_PAIRS, 1) * sn).astype(BF16)
    step = 1024
    for j in range(OD_V, OD_N, step):
        z_ref[:, j:j + step] = _dot(h, w_ref[:, j:j + step]).astype(BF16)


def _inproj_even(x_parts, mods_l, nw, wm_bf, wg_bf):
    return pl.pallas_call(
        functools.partial(_inproj_even_kernel, n_x=len(x_parts)),
        grid=(T_ALL // TM,),
        in_specs=_x_specs(x_parts) + [
                  pl.BlockSpec((None, 6, D), lambda i: (_cond_of_tile(i, TM), 0, 0)),
                  pl.BlockSpec((1, D), lambda i: (0, 0)),
                  pl.BlockSpec((D, EV_MAIN), lambda i: (0, 0)),
                  pl.BlockSpec((D, EV_GATE), lambda i: (0, 0))],
        out_specs=[pl.BlockSpec((TM, EV_MAIN), lambda i: (i, 0)),
                   pl.BlockSpec((TM, EV_GATE), lambda i: (i, 0))],
        out_shape=[jax.ShapeDtypeStruct((T_ALL, EV_MAIN), BF16),
                   jax.ShapeDtypeStruct((T_ALL, EV_GATE), F32)],
        compiler_params=_params(1), name="inproj_even",
    )(*x_parts, mods_l, nw, wm_bf, wg_bf)


def _rope_block(i):
    n_ctx = T_CTX // TM
    per_seq = SMP_L // TM
    return jnp.where(i < n_ctx, 0, 1 + (i - n_ctx) % per_seq)


def _inproj_odd(x, mods_l, nw, w_bf, cos_t, sin_t):
    return pl.pallas_call(
        _inproj_odd_kernel,
        grid=(T_ALL // TM,),
        in_specs=[pl.BlockSpec((TM, D), lambda i: (i, 0)),
                  pl.BlockSpec((None, 6, D), lambda i: (_cond_of_tile(i, TM), 0, 0)),
                  pl.BlockSpec((1, D), lambda i: (0, 0)),
                  pl.BlockSpec((D, OD_N), lambda i: (0, 0)),
                  pl.BlockSpec((TM, RET_DK), lambda i: (_rope_block(i), 0)),
                  pl.BlockSpec((TM, RET_DK), lambda i: (_rope_block(i), 0))],
        out_specs=pl.BlockSpec((TM, OD_N), lambda i: (i, 0)),
        out_shape=jax.ShapeDtypeStruct((T_ALL, OD_N), BF16),
        compiler_params=_params(1), name="inproj_odd",
    )(x, mods_l, nw, w_bf, cos_t, sin_t)


def _scan_work():
    tf, tb, first, last, seq = [], [], [], [], []
    base = 0
    for s, length in enumerate([CTX_L] * CTX_B + [SMP_L] * SMP_B):
        n = length // TL
        for t in range(n):
            tf.append(base + t)
            tb.append(base + n - 1 - t)
            first.append(int(t == 0))
            last.append(int(t == n - 1))
            seq.append(s)
        base += n
    return tuple(np.asarray(a, np.int32) for a in (tf, tb, first, last, seq))


N_WORK = T_ALL // TL


def _decay_factors(q, k, b, fwd):
    b_last = b[CHUNK - 1:CHUNK] if fwd else b[0:1]
    b_mid = b[CHUNK // 2:CHUNK // 2 + 1]
    qi = q * jnp.exp(b - b_mid)
    ki = k * jnp.exp(b_mid - b)
    q_in = qi * jnp.exp(b_mid)
    kk = ki * jnp.exp(b_last - b_mid)
    return qi, ki, q_in, kk, jnp.exp(b_last)


def _stack(a, b):
    return jnp.concatenate([a, b], axis=0)


def _gla_pair_chunk(q, k, b, v0, v1, st, masks, bd_causal, fwd):
    qi, ki, q_in, kk, g = _decay_factors(q, k, b, fwd)
    m0, m1 = masks
    split = lambda t: _stack(jnp.where(m0, t, 0.0), jnp.where(m1, t, 0.0)).astype(BF16)
    ki_bf = ki.astype(BF16)
    a = jnp.where(bd_causal, _dot_nt(split(qi), _stack(ki_bf, ki_bf)), 0.0)
    vs = _stack(v0, v1)
    o = _dot(a.astype(BF16), vs) + _dot_nt(split(q_in), st.astype(BF16))
    return o, st * g + _dot_tn(vs, split(kk))


def _hgrn_pair_chunk(q0, k0, b0, v0, q1, k1, b1, v1, st, bd_causal, fwd):
    qi0, ki0, qin0, kk0, g0 = _decay_factors(q0, k0, b0, fwd)
    qi1, ki1, qin1, kk1, g1 = _decay_factors(q1, k1, b1, fwd)
    zero = jnp.zeros((CHUNK, LANES), F32)
    wide = lambda t0, t1: _stack(jnp.concatenate([t0, zero], axis=1),
                                 jnp.concatenate([zero, t1], axis=1)).astype(BF16)
    a = jnp.where(bd_causal, _dot_nt(_stack(qi0, qi1).astype(BF16), _stack(ki0, ki1).astype(BF16)), 0.0)
    vs = _stack(v0, v1)
    o = _dot(a.astype(BF16), vs) + _dot_nt(wide(qin0, qin1), st.astype(BF16))
    return o, st * jnp.concatenate([g0, g1], axis=1) + _dot_tn(vs, wide(kk0, kk1))


def _even_gates(d, fg, al, qb, tri, w2_ref, ba_ref, lbv):
    y = _dot(al[...].astype(BF16), w2_ref[d]) + ba_ref[d]
    la_a = (jnp.minimum(y, 0.0) - jnp.log(1.0 + jnp.exp(-jnp.abs(y)))) * (1.0 / GLA_TAU)
    f = fg[...]
    t = jnp.exp(-jnp.abs(f))
    rcp = 1.0 / (1.0 + t)
    sig_f = jnp.where(f >= 0.0, rcp, t * rcp)
    sig_nf = jnp.where(f >= 0.0, t * rcp, rcp)
    la_h = jnp.log(lbv + (1.0 - lbv) * sig_f)
    key_h = (1.0 - lbv) * sig_nf
    la_hi, la_lo = _split_bf16(jnp.concatenate([la_a, la_h], axis=1))
    b_all = _dot(tri, la_hi) + _dot(tri, la_lo)
    return b_all, key_h, _silu(qb[...].astype(F32))


def _even_tile(dir_refs, w2_ref, ba_ref, lb_ref, stg, sth):
    rc = lax.broadcasted_iota(jnp.int32, (2 * CHUNK, 2 * CHUNK), 0)
    cc = lax.broadcasted_iota(jnp.int32, (2 * CHUNK, 2 * CHUNK), 1)
    same_head = (rc // CHUNK) == (cc // CHUNK)
    bd_causal = [same_head & (cc <= rc), same_head & (cc >= rc)]
    rt = lax.broadcasted_iota(jnp.int32, (TL, TL), 0)
    ct = lax.broadcasted_iota(jnp.int32, (TL, TL), 1)
    same = (rt // CHUNK) == (ct // CHUNK)
    tri = [jnp.where(same & (ct <= rt), 1.0, 0.0).astype(BF16),
           jnp.where(same & (ct >= rt), 1.0, 0.0).astype(BF16)]
    lane = lax.broadcasted_iota(jnp.int32, (CHUNK, LANES), 1)
    pair_masks = [lane < GLA_DK, lane >= GLA_DK]
    lbv = lb_ref[...]
    n_pairs = GLA_HEADS // 2
    st_g = [[stg[d, p] for p in range(n_pairs)] for d in range(2)]
    st_h = [[sth[d, p] for p in range(HG_HEADS // 2)] for d in range(2)]
    gates = [_even_gates(d, dir_refs[d][3], dir_refs[d][5], dir_refs[d][2], tri[d], w2_ref, ba_ref, lbv)
             for d in range(2)]

    n_chunks = TL // CHUNK
    for ci in range(n_chunks):
        for d in range(2):
            qk, va, _, _, ib, _, o_ref = dir_refs[d]
            b_all, key_h, qh = gates[d]
            fwd = d == 0
            cpos = ci if fwd else n_chunks - 1 - ci
            rows = slice(cpos * CHUNK, (cpos + 1) * CHUNK)
            for p in range(n_pairs):
                ln = slice(p * LANES, (p + 1) * LANES)
                kl = slice(GLA_QK + p * LANES, GLA_QK + (p + 1) * LANES)
                c0, c1 = 2 * p * GLA_DV, (2 * p + 1) * GLA_DV
                o, st_g[d][p] = _gla_pair_chunk(
                    qk[rows, ln].astype(F32) * (GLA_DK ** -0.5), qk[rows, kl].astype(F32), b_all[rows, ln],
                    va[rows, c0:c0 + GLA_DV], va[rows, c1:c1 + GLA_DV], st_g[d][p],
                    pair_masks, bd_causal[d], fwd)
                o_ref[rows, c0:c0 + GLA_DV] = o[0:CHUNK].astype(BF16)
                o_ref[rows, c1:c1 + GLA_DV] = o[CHUNK:2 * CHUNK].astype(BF16)
            for p in range(HG_HEADS // 2):
                l0 = slice(2 * p * LANES, (2 * p + 1) * LANES)
                l1 = slice((2 * p + 1) * LANES, (2 * p + 2) * LANES)
                bl0 = slice(GLA_QK + l0.start, GLA_QK + l0.stop)
                bl1 = slice(GLA_QK + l1.start, GLA_QK + l1.stop)
                o, st_h[d][p] = _hgrn_pair_chunk(
                    qh[rows, l0], key_h[rows, l0], b_all[rows, bl0], ib[rows, l0],
                    qh[rows, l1], key_h[rows, l1], b_all[rows, bl1], ib[rows, l1],
                    st_h[d][p], bd_causal[d], fwd)
                o_ref[rows, GLA_V + l0.start:GLA_V + l0.stop] = o[0:CHUNK].astype(BF16)
                o_ref[rows, GLA_V + l1.start:GLA_V + l1.stop] = o[CHUNK:2 * CHUNK].astype(BF16)
    for d in range(2):
        for p in range(n_pairs):
            stg[d, p] = st_g[d][p]
        for p in range(HG_HEADS // 2):
            sth[d, p] = st_h[d][p]


def _even_scan_kernel(tf_ref, tb_ref, first_ref, last_ref, seq_ref,
                      qk_f, va_f, qb_f, fg_f, ib_f, al_f,
                      qk_b, va_b, qb_b, fg_b, ib_b, al_b,
                      w2_ref, ba_ref, lb_ref, sg_ref, sh_ref, *rest, n_prev):
    pg_ref, ph_ref = rest[:2] if n_prev else (None, None)
    of_ref, ob_ref, stg_ref, sth_ref, stg, sth = rest[-6:]
    i = pl.program_id(0)
    is_first = first_ref[i] == 1
    is_last = last_ref[i] == 1
    is_ctx = seq_ref[i] < CTX_B

    @pl.when(jnp.logical_and(is_first, is_ctx))
    def _():
        stg[...] = jnp.zeros_like(stg)
        sth[...] = jnp.zeros_like(sth)

    @pl.when(jnp.logical_and(is_first, jnp.logical_not(is_ctx)))
    def _():
        for d in range(2):
            for p in range(GLA_HEADS // 2):
                stg[d, p] = sg_ref[d, p].T
            for h in range(HG_HEADS):
                sth[d, h // 2, :, (h % 2) * HG_DK:(h % 2 + 1) * HG_DK] = sh_ref[d, h].T

    _even_tile([(qk_f, va_f, qb_f, fg_f, ib_f, al_f, of_ref), (qk_b, va_b, qb_b, fg_b, ib_b, al_b, ob_ref)],
               w2_ref, ba_ref, lb_ref, stg, sth)

    @pl.when(jnp.logical_and(is_last, is_ctx))
    def _():
        for k in range(n_prev):
            stg_ref[k] = pg_ref[k]
            sth_ref[k] = ph_ref[k]
        for d in range(2):
            for p in range(GLA_HEADS // 2):
                stg_ref[n_prev, d, p] = stg[d, p].T
            for h in range(HG_HEADS):
                sth_ref[n_prev, d, h] = sth[d, h // 2, :, (h % 2) * HG_DK:(h % 2 + 1) * HG_DK].T


def _even_scan(zm, zg, w2_pad, b_a, lb, s_gla, s_hgrn, prev):
    n_prev = 0 if prev is None else prev[0].shape[1]
    work = _scan_work()

    def zspec(width, col_block, which):
        return pl.BlockSpec((TL, width), lambda i, tf, tb, fi, la, sq: ((tf, tb)[which][i], col_block))

    def dir_specs(which):
        return [zspec(512, EV_QA // 512, which), zspec(512, EV_VA // 512, which),
                zspec(512, EV_QB // 512, which), zspec(512, (EV_FF, EV_FB)[which] // 512, which),
                zspec(512, EV_IB // 512, which), zspec(LANES, EV_AL // LANES, which)]

    def dir_args():
        return [zm, zm, zm, zg, zm, zg]

    def const(shape):
        return pl.BlockSpec(shape, lambda i, *_: (0,) * len(shape))

    def s_in(shape):
        return pl.BlockSpec((None,) + shape,
                            lambda i, tf, tb, fi, la, sq: (jnp.clip(sq[i] - CTX_B, 0, SMP_B - 1),) + (0,) * len(shape))

    def ctx_states(n_layers, shape):
        return pl.BlockSpec((None, n_layers) + shape,
                            lambda i, tf, tb, fi, la, sq: (jnp.minimum(sq[i], CTX_B - 1), 0) + (0,) * len(shape))

    gshape = (2, GLA_HEADS // 2, LANES, LANES)
    hshape = (2, HG_HEADS, HG_DK, HG_DV)
    grid_spec = pltpu.PrefetchScalarGridSpec(
        num_scalar_prefetch=5, grid=(N_WORK,),
        in_specs=dir_specs(0) + dir_specs(1) + [
            const((2, LANES, GLA_QK)), const((2, 1, GLA_QK)), const((1, HG_K)),
            s_in(gshape), s_in(hshape)]
        + ([ctx_states(n_prev, gshape), ctx_states(n_prev, hshape)] if n_prev else []),
        out_specs=[pl.BlockSpec((TL, D), lambda i, tf, tb, fi, la, sq: (tf[i], 0)),
                   pl.BlockSpec((TL, D), lambda i, tf, tb, fi, la, sq: (tb[i], 0)),
                   ctx_states(n_prev + 1, gshape), ctx_states(n_prev + 1, hshape)],
        scratch_shapes=[pltpu.VMEM(gshape, F32),
                        pltpu.VMEM((2, HG_HEADS // 2, HG_DV, 2 * HG_DK), F32)])
    return pl.pallas_call(
        functools.partial(_even_scan_kernel, n_prev=n_prev), grid_spec=grid_spec,
        out_shape=[jax.ShapeDtypeStruct((T_ALL, D), BF16), jax.ShapeDtypeStruct((T_ALL, D), BF16),
                   jax.ShapeDtypeStruct((CTX_B, n_prev + 1) + gshape, F32),
                   jax.ShapeDtypeStruct((CTX_B, n_prev + 1) + hshape, F32)],
        compiler_params=_params(1), name="even_scan",
    )(*work, *dir_args(), *dir_args(), w2_pad, b_a, lb, s_gla, s_hgrn, *(prev if n_prev else []))


RET_HP = 2


def _ret_tables():
    lg = np.log1p(-np.exp2(-5.0 - np.arange(RET_HEADS, dtype=np.float64)))
    pos = np.arange(TL, dtype=np.float64)
    dm = np.zeros((2, RET_HEADS, TL, TL))
    rq = np.zeros((2, RET_HEADS, TL, 1))
    rk = np.zeros((2, RET_HEADS, TL, 1))
    gc = np.zeros((2, RET_HEADS, 1, RET_DV))
    diff = pos[:, None] - pos[None, :]
    kscale = RET_DK ** -0.5
    for h in range(RET_HEADS):
        dm[0, h] = np.where(diff >= 0, np.exp(lg[h] * np.maximum(diff, 0)), 0.0) * kscale
        rq[0, h, :, 0] = np.exp(lg[h] * (pos + 1))
        rk[0, h, :, 0] = np.exp(lg[h] * (TL - 1 - pos)) * kscale
        gc[0, h] = np.exp(lg[h] * TL)
        lb = lg[RET_HEADS - 1 - h]
        dm[1, h] = np.where(diff <= 0, np.exp(lb * np.maximum(-diff, 0)), 0.0) * kscale
        rq[1, h, :, 0] = np.exp(lb * (TL - pos))
        rk[1, h, :, 0] = np.exp(lb * pos) * kscale
        gc[1, h] = np.exp(lb * TL)
    return tuple(jnp.asarray(a, F32) for a in (dm, rq, rk, gc))


def _odd_dir(d, q_ref, k_ref, v_ref, dm_ref, rq_ref, rk_ref, gc_ref, st, o_ref):
    for hh in range(RET_HP):
        k = k_ref[:, hh * RET_DK:(hh + 1) * RET_DK]
        q = q_ref[:, hh * RET_DK:(hh + 1) * RET_DK]
        v = v_ref[:, hh * RET_DV:(hh + 1) * RET_DV]
        a = _dot_nt(q, k) * dm_ref[d, hh]
        s = st[d, hh]
        o = _dot(a.astype(BF16), v) + rq_ref[d, hh] * _dot(q, s.astype(BF16))
        kk = (k.astype(F32) * rk_ref[d, hh]).astype(BF16)
        st[d, hh] = gc_ref[d, hh] * s + _dot_tn(kk, v)
        o_ref[:, hh * RET_DV:(hh + 1) * RET_DV] = o.astype(BF16)


def _odd_scan_kernel(tf_ref, tb_ref, first_ref, last_ref, seq_ref,
                     q_f, k_f, v_f, q_b, k_b, v_b,
                     dm_ref, rq_ref, rk_ref, gc_ref, s0_ref, *rest, n_prev):
    prev_ref = rest[0] if n_prev else None
    of_ref, ob_ref, sout_ref, st = rest[-4:]
    i = pl.program_id(1)
    is_first = first_ref[i] == 1
    is_last = last_ref[i] == 1
    is_ctx = seq_ref[i] < CTX_B

    @pl.when(jnp.logical_and(is_first, is_ctx))
    def _():
        st[...] = jnp.zeros_like(st)

    @pl.when(jnp.logical_and(is_first, jnp.logical_not(is_ctx)))
    def _():
        st[...] = s0_ref[...]

    _odd_dir(0, q_f, k_f, v_f, dm_ref, rq_ref, rk_ref, gc_ref, st, of_ref)
    _odd_dir(1, q_b, k_b, v_b, dm_ref, rq_ref, rk_ref, gc_ref, st, ob_ref)

    @pl.when(jnp.logical_and(is_last, is_ctx))
    def _():
        for k in range(n_prev):
            sout_ref[k] = prev_ref[k]
        sout_ref[n_prev] = st[...]


def _odd_scan(z, tables, s_ret, prev):
    n_prev = 0 if prev is None else prev.shape[1]
    work = _scan_work()
    dm, rq, rk, gc = tables
    qw, vw = RET_HP * RET_DK, RET_HP * RET_DV

    def zspec(width, col0, which):
        return pl.BlockSpec((TL, width),
                            lambda hp, i, tf, tb, fi, la, sq: ((tf, tb)[which][i], col0 // width + hp))

    def dir_specs(which):
        return [zspec(qw, OD_Q, which), zspec(qw, OD_K, which), zspec(vw, OD_V, which)]

    def table(shape):
        return pl.BlockSpec((2, RET_HP) + shape, lambda hp, i, *_: (0, hp) + (0,) * len(shape))

    sshape = (2, RET_HP, RET_DK, RET_DV)

    def ctx_states(n_layers):
        return pl.BlockSpec((None, n_layers) + sshape,
                            lambda hp, i, tf, tb, fi, la, sq: (jnp.minimum(sq[i], CTX_B - 1), 0, 0, hp, 0, 0))

    grid_spec = pltpu.PrefetchScalarGridSpec(
        num_scalar_prefetch=5, grid=(RET_HEADS // RET_HP, N_WORK),
        in_specs=dir_specs(0) + dir_specs(1) + [
            table((TL, TL)), table((TL, 1)), table((TL, 1)), table((1, RET_DV)),
            pl.BlockSpec((None,) + sshape,
                         lambda hp, i, tf, tb, fi, la, sq:
                         (jnp.clip(sq[i] - CTX_B, 0, SMP_B - 1), 0, hp, 0, 0))]
        + ([ctx_states(n_prev)] if n_prev else []),
        out_specs=[pl.BlockSpec((TL, vw), lambda hp, i, tf, tb, fi, la, sq: (tf[i], hp)),
                   pl.BlockSpec((TL, vw), lambda hp, i, tf, tb, fi, la, sq: (tb[i], hp)),
                   ctx_states(n_prev + 1)],
        scratch_shapes=[pltpu.VMEM(sshape, F32)])
    return pl.pallas_call(
        functools.partial(_odd_scan_kernel, n_prev=n_prev), grid_spec=grid_spec,
        out_shape=[jax.ShapeDtypeStruct((T_ALL, RET_V), BF16), jax.ShapeDtypeStruct((T_ALL, RET_V), BF16),
                   jax.ShapeDtypeStruct((CTX_B, n_prev + 1, 2, RET_HEADS, RET_DK, RET_DV), F32)],
        compiler_params=_params(2), name="odd_scan",
    )(*work, z, z, z, z, z, z, dm, rq, rk, gc, s_ret, *([prev] if n_prev else []))


def _head_rms(o, width):
    parts = []
    for j in range(0, o.shape[1], width):
        blk = o[:, j:j + width]
        parts.append(blk * lax.rsqrt(jnp.mean(blk * blk, axis=-1, keepdims=True) + NORM_EPS))
    return parts


def _route(h2, wr_hi_ref, wr_lo_ref, br_ref):
    h_hi, h_lo = _split_bf16(h2)
    logits = (_dot(h_hi, wr_hi_ref[...]) + _dot(h_hi, wr_lo_ref[...]) + _dot(h_lo, wr_hi_ref[...])
              + br_ref[...])
    lane_i = lax.broadcasted_iota(jnp.int32, logits.shape, 1)
    lane = lane_i.astype(F32)
    neg = -jnp.inf
    big = 1e9
    is_grp = (lane_i >= N_EXPERTS) & (lane_i < N_EXPERTS + N_GROUPS)
    lg = jnp.where(is_grp, logits, neg)
    mg = jnp.max(lg, axis=-1, keepdims=True)
    gsel = jnp.min(jnp.where(lg == mg, lane - N_EXPERTS, big), axis=-1, keepdims=True)
    pg = 1.0 / jnp.sum(jnp.where(is_grp, jnp.exp(lg - mg), 0.0), axis=-1, keepdims=True)
    in_grp = (lane_i < N_EXPERTS) & ((lane_i // EXP_PER_GROUP).astype(F32) == gsel)
    le = jnp.where(in_grp, logits, neg)
    m1 = jnp.max(le, axis=-1, keepdims=True)
    i1 = jnp.min(jnp.where(le == m1, lane, big), axis=-1, keepdims=True)
    le2 = jnp.where(lane == i1, neg, le)
    m2 = jnp.max(le2, axis=-1, keepdims=True)
    i2 = jnp.min(jnp.where(le2 == m2, lane, big), axis=-1, keepdims=True)
    e2 = jnp.exp(m2 - m1)
    w1 = pg / (1.0 + e2)
    w2 = pg * e2 / (1.0 + e2)
    first_lo = i1 < i2
    ia = jnp.minimum(i1, i2) - gsel * EXP_PER_GROUP
    ib = jnp.maximum(i1, i2) - gsel * EXP_PER_GROUP
    pair = gsel * PAIRS_PER_GROUP + ia * (2 * EXP_PER_GROUP - 1 - ia) * 0.5 + (ib - ia - 1.0)
    return pair, jnp.where(first_lo, w1, w2), jnp.where(first_lo, w2, w1)


def _post_tail(i, x, out, m, nw2_ref, wr_hi_ref, wr_lo_ref, br_ref,
               x1_ref, hrow_ref, meta_ref, cnt_ref, carry):
    x1 = x + m[2:3] * out
    x1_ref[...] = x1
    h2 = _norm_mod(x1, nw2_ref[...], m[3:4], m[4:5])
    pair, w_a, w_b = _route(h2, wr_hi_ref, wr_lo_ref, br_ref)

    @pl.when(i == 0)
    def _():
        carry[...] = jnp.zeros_like(carry)

    lane = lax.broadcasted_iota(jnp.int32, (TM, LANES), 1)
    onehot = jnp.where(lane.astype(F32) == pair, 1.0, 0.0)
    r = lax.broadcasted_iota(jnp.int32, (TM, TM), 0)
    c = lax.broadcasted_iota(jnp.int32, (TM, TM), 1)
    earlier = jnp.where(c < r, 1.0, 0.0).astype(BF16)
    before = _dot(earlier, onehot.astype(BF16)) + carry[...]
    rank = jnp.sum(onehot * before, axis=-1, keepdims=True)
    carry[...] += jnp.sum(onehot, axis=0, keepdims=True)
    cnt_ref[...] = carry[...]
    meta_ref[...] = jnp.where(lane == 0, pair, jnp.where(lane == 1, rank, 0.0))
    hrow_ref[:, 0:D] = h2
    hrow_ref[:, D:ROW_W] = jnp.where(lane == 0, w_a, jnp.where(lane == 1, w_b, 0.0))


def _post_even_kernel(*refs, n_x):
    x_refs = refs[:n_x]
    (of_ref, ob_ref, ra_ref, gb_ref, mod_ref, gn_ref, wo_ref, nw2_ref, wr_hi_ref, wr_lo_ref, br_ref,
     x1_ref, hrow_ref, meta_ref, cnt_ref, carry) = refs[n_x:]
    o = of_ref[...].astype(F32) + ob_ref[...].astype(F32)
    normed = jnp.concatenate(_head_rms(o, GLA_DV), axis=1)
    gate = jnp.concatenate([_silu(ra_ref[...].astype(F32)), _silu(gb_ref[...].astype(F32))], axis=1)
    mixed = (normed * gn_ref[...] * gate).astype(BF16)
    out = _dot(mixed, wo_ref[...])
    _post_tail(pl.program_id(0), _load_x(x_refs), out, mod_ref[...], nw2_ref, wr_hi_ref, wr_lo_ref, br_ref,
               x1_ref, hrow_ref, meta_ref, cnt_ref, carry)


def _post_odd_kernel(of_ref, ob_ref, g_ref, x_ref, mod_ref, wo_ref,
                     nw2_ref, wr_hi_ref, wr_lo_ref, br_ref,
                     x1_ref, hrow_ref, meta_ref, cnt_ref, carry):
    o = of_ref[...].astype(F32) + ob_ref[...].astype(F32)
    normed = jnp.concatenate(_head_rms(o, RET_DV), axis=1)
    mixed = (normed * _silu(g_ref[...].astype(F32))).astype(BF16)
    out = _dot(mixed, wo_ref[...])
    _post_tail(pl.program_id(0), x_ref[...], out, mod_ref[...], nw2_ref, wr_hi_ref, wr_lo_ref, br_ref,
               x1_ref, hrow_ref, meta_ref, cnt_ref, carry)


def _post_common_specs():
    tile = lambda w: pl.BlockSpec((TM, w), lambda i: (i, 0))
    const = lambda s: pl.BlockSpec(s, lambda i: (0,) * len(s))
    mod = pl.BlockSpec((None, 6, D), lambda i: (_cond_of_tile(i, TM), 0, 0))
    out_specs = [tile(D), tile(ROW_W), tile(LANES), const((1, LANES))]
    out_shape = [jax.ShapeDtypeStruct((T_ALL, D), F32),
                 jax.ShapeDtypeStruct((T_ALL, ROW_W), F32),
                 jax.ShapeDtypeStruct((T_ALL, LANES), F32),
                 jax.ShapeDtypeStruct((1, LANES), F32)]
    scratch = [pltpu.VMEM((1, LANES), F32)]
    return tile, const, mod, out_specs, out_shape, scratch


def _post_even(o_f, o_b, z, x_parts, mods_l, gn, wo_bf, nw2, wr_hi, wr_lo, br):
    tile, const, mod, out_specs, out_shape, scratch = _post_common_specs()
    zcol = lambda c0: pl.BlockSpec((TM, 512), lambda i: (i, c0 // 512))
    return pl.pallas_call(
        functools.partial(_post_even_kernel, n_x=len(x_parts)), grid=(T_ALL // TM,),
        in_specs=_x_specs(x_parts) + [
                  tile(D), tile(D), zcol(EV_RA), zcol(EV_GB), mod, const((1, D)),
                  const((D, D)), const((1, D)), const((D, ROUTER_N)), const((D, ROUTER_N)),
                  const((1, ROUTER_N))],
        out_specs=out_specs, out_shape=out_shape, scratch_shapes=scratch,
        compiler_params=_params(1), name="post_even",
    )(*x_parts, o_f, o_b, z, z, mods_l, gn, wo_bf, nw2, wr_hi, wr_lo, br)


def _post_odd(o_f, o_b, z, x, mods_l, wo_bf, nw2, wr_hi, wr_lo, br):
    tile, const, mod, out_specs, out_shape, scratch = _post_common_specs()
    return pl.pallas_call(
        _post_odd_kernel, grid=(T_ALL // TM,),
        in_specs=[tile(RET_V), tile(RET_V),
                  pl.BlockSpec((TM, RET_V), lambda i: (i, OD_G // RET_V)), tile(D), mod,
                  const((RET_V, D)), const((1, D)), const((D, ROUTER_N)), const((D, ROUTER_N)),
                  const((1, ROUTER_N))],
        out_specs=out_specs, out_shape=out_shape, scratch_shapes=scratch,
        compiler_params=_params(1), name="post_odd",
    )(o_f, o_b, z, x, mods_l, wo_bf, nw2, wr_hi, wr_lo, br)


def _pair_tables():
    ea, eb = [], []
    for g in range(N_GROUPS):
        for a in range(EXP_PER_GROUP):
            for b in range(a + 1, EXP_PER_GROUP):
                ea.append(g * EXP_PER_GROUP + a)
                eb.append(g * EXP_PER_GROUP + b)
    pad = LANES - len(ea)
    return (jnp.asarray(ea + [0] * pad, jnp.int32), jnp.asarray(eb + [0] * pad, jnp.int32))


def _dispatch_plan(meta, counts):
    pair = meta[:, 0].astype(jnp.int32)
    rank = meta[:, 1].astype(jnp.int32)
    cnt = counts[0].astype(jnp.int32)
    tiles_g = (cnt + TMG - 1) // TMG
    tile_end = jnp.cumsum(tiles_g)
    tile_start = tile_end - tiles_g
    slot0 = tile_start * TMG
    ids = jnp.arange(LANES, dtype=jnp.int32)
    dest = rank + jnp.sum(jnp.where(pair[:, None] == ids[None, :], slot0[None, :], 0), axis=1)
    n_real = tile_end[-1]
    j = jnp.arange(N_TILES_MOE, dtype=jnp.int32)
    jj = jnp.minimum(j, n_real - 1)
    grp = jnp.sum((jj[:, None] >= tile_end[None, :]).astype(jnp.int32), axis=1)
    ea_t, eb_t = _pair_tables()
    onehot_g = grp[:, None] == ids[None, :]
    pick = lambda v: jnp.sum(jnp.where(onehot_g, v[None, :], 0), axis=1)
    nv = jnp.clip(pick(cnt) - (jj - pick(tile_start)) * TMG, 0, TMG)
    nv = jnp.where(j < n_real, nv, 0)
    return dest, pick(ea_t), pick(eb_t), nv, n_real.reshape(1)


DISPATCH_STEPS = T_ALL // DISPATCH_ROWS


def _dispatch_kernel(dest_ref, nv_ref, h_ref, hs_hbm, zeros, sem, pad_sem):
    j = pl.program_id(0)
    base = j * DISPATCH_ROWS

    def pad_tile(tile):
        return pltpu.make_async_copy(zeros, hs_hbm.at[pl.ds(tile * TMG, TMG)], pad_sem.at[0])

    @pl.when(j == 0)
    def _():
        zeros[...] = jnp.zeros_like(zeros)

        def fill(tile, carry):
            @pl.when(nv_ref[tile] < TMG)
            def _():
                pad_tile(tile).start()
            return carry

        def drain(tile, carry):
            @pl.when(nv_ref[tile] < TMG)
            def _():
                pad_tile(tile).wait()
            return carry

        lax.fori_loop(0, N_TILES_MOE, fill, 0)
        lax.fori_loop(0, N_TILES_MOE, drain, 0)

    def issue(g, carry):
        for k in range(ROW_DMA_UNROLL):
            r = g * ROW_DMA_UNROLL + k
            pltpu.make_async_copy(h_ref.at[pl.ds(r, 1)], hs_hbm.at[pl.ds(dest_ref[base + r], 1)],
                                  sem.at[0]).start(priority=k % 2)
        return carry

    lax.fori_loop(0, DISPATCH_ROWS // ROW_DMA_UNROLL, issue, 0)
    pltpu.make_async_copy(h_ref, hs_hbm.at[pl.ds(0, DISPATCH_ROWS)], sem.at[0]).wait()


def _dispatch(dest, nv, hrow):
    grid_spec = pltpu.PrefetchScalarGridSpec(
        num_scalar_prefetch=2, grid=(DISPATCH_STEPS,),
        in_specs=[pl.BlockSpec((DISPATCH_ROWS, ROW_W), lambda j, *_: (j, 0))],
        out_specs=pl.BlockSpec(memory_space=pl.ANY),
        scratch_shapes=[pltpu.VMEM((TMG, ROW_W), F32),
                        pltpu.SemaphoreType.DMA((1,)), pltpu.SemaphoreType.DMA((1,))])
    return pl.pallas_call(
        _dispatch_kernel, grid_spec=grid_spec,
        out_shape=jax.ShapeDtypeStruct((N_TILES_MOE * TMG, ROW_W), F32),
        compiler_params=_params(1), name="moe_dispatch",
    )(dest, nv, hrow)


def _moe_kernel(ea_ref, eb_ref, nreal_ref, hs_ref,
                wga_ref, wua_ref, wda_ref, wgb_ref, wub_ref, wdb_ref, ys_ref):
    j = pl.program_id(0)

    @pl.when(j < nreal_ref[0])
    def _():
        h = hs_ref[:, 0:D].astype(BF16)
        wrow = hs_ref[:, D:ROW_W]
        w_a, w_b = wrow[:, 0:1], wrow[:, 1:2]
        act_a = _silu(_dot(h, wga_ref[0])) * _dot(h, wua_ref[0]) * w_a
        act_b = _silu(_dot(h, wgb_ref[0])) * _dot(h, wub_ref[0]) * w_b
        ys_ref[...] = _dot(act_a.astype(BF16), wda_ref[0]) + _dot(act_b.astype(BF16), wdb_ref[0])

    @pl.when(j >= nreal_ref[0])
    def _():
        ys_ref[...] = jnp.zeros_like(ys_ref)


def _moe(ea, eb, n_real, hs, wg_bf, wu_bf, wd_bf):
    def wspec(shape, which):
        return pl.BlockSpec((1,) + shape, lambda j, ea, eb, nr: ((ea, eb)[which][j], 0, 0))

    up, down = (D, D_EXPERT), (D_EXPERT, D)
    grid_spec = pltpu.PrefetchScalarGridSpec(
        num_scalar_prefetch=3, grid=(N_TILES_MOE,),
        in_specs=[pl.BlockSpec((TMG, ROW_W), lambda j, ea, eb, nr: (jnp.minimum(j, nr[0] - 1), 0)),
                  wspec(up, 0), wspec(up, 0), wspec(down, 0),
                  wspec(up, 1), wspec(up, 1), wspec(down, 1)],
        out_specs=pl.BlockSpec((TMG, D), lambda j, ea, eb, nr: (j, 0)))
    return pl.pallas_call(
        _moe_kernel, grid_spec=grid_spec,
        out_shape=jax.ShapeDtypeStruct((N_TILES_MOE * TMG, D), F32),
        compiler_params=_params(1), name="moe",
    )(ea, eb, n_real, hs, wg_bf, wu_bf, wd_bf, wg_bf, wu_bf, wd_bf)


def _combine_kernel(dest_ref, ys_hbm, x1_ref, mod_ref, nwf_ref, o_ref, buf, sem, *, tile0, final):
    j = pl.program_id(0)
    n = pl.num_programs(0)
    slot = j % 2

    def gather(tile, s):
        def issue(g, carry):
            for k in range(ROW_DMA_UNROLL):
                r = g * ROW_DMA_UNROLL + k
                pltpu.make_async_copy(ys_hbm.at[pl.ds(dest_ref[(tile0 + tile) * TM + r], 1)],
                                      buf.at[s, pl.ds(r, 1)], sem.at[s]).start(priority=k % 2)
            return carry
        lax.fori_loop(0, TM // ROW_DMA_UNROLL, issue, 0)

    @pl.when(j == 0)
    def _():
        gather(0, 0)

    @pl.when(j + 1 < n)
    def _():
        gather(j + 1, 1 - slot)

    pltpu.make_async_copy(ys_hbm.at[pl.ds(0, TM)], buf.at[slot], sem.at[slot]).wait()
    x = x1_ref[...] + mod_ref[5:6] * buf[slot]
    if final:
        var = jnp.mean(x * x, axis=-1, keepdims=True)
        x = x * lax.rsqrt(var + NORM_EPS) * nwf_ref[...]
    o_ref[...] = x


def _combine(dest, ys, x1, mods_l, nwf, tile0=0, n_tok=T_ALL, final=False):
    grid_spec = pltpu.PrefetchScalarGridSpec(
        num_scalar_prefetch=1, grid=(n_tok // TM,),
        in_specs=[pl.BlockSpec(memory_space=pl.ANY),
                  pl.BlockSpec((TM, D), lambda i, dest: (tile0 + i, 0)),
                  pl.BlockSpec((None, 6, D), lambda i, dest: (_cond_of_tile(tile0 + i, TM), 0, 0)),
                  pl.BlockSpec((1, D), lambda i, dest: (0, 0))],
        out_specs=pl.BlockSpec((TM, D), lambda i, dest: (i, 0)),
        scratch_shapes=[pltpu.VMEM((2, TM, D), F32), pltpu.SemaphoreType.DMA((2,))])
    return pl.pallas_call(
        functools.partial(_combine_kernel, tile0=tile0, final=final), grid_spec=grid_spec,
        out_shape=jax.ShapeDtypeStruct((n_tok, D), F32),
        compiler_params=_params(1), name="moe_combine",
    )(dest, ys, x1, mods_l, nwf)


def _rope_tables():
    freqs = ROPE_BASE ** (-jnp.arange(ROPE_PAIRS, dtype=F32) / ROPE_PAIRS)
    t = jnp.arange(SMP_L)
    halves_c, halves_s = [], []
    for p in (t // GRID_W, t % GRID_W):
        ang = p.astype(F32)[:, None] * freqs
        cs, sn = jnp.cos(ang), jnp.sin(ang)
        halves_c += [cs, cs]
        halves_s += [-sn, sn]
    cos_t = jnp.concatenate(halves_c, axis=1)
    sin_t = jnp.concatenate(halves_s, axis=1)
    cos_t = jnp.concatenate([jnp.ones((TM, RET_DK), F32), cos_t], axis=0)
    sin_t = jnp.concatenate([jnp.zeros((TM, RET_DK), F32), sin_t], axis=0)
    return cos_t, sin_t


def kernel(x_prompt, x_sample, state_gla, state_hgrn, state_ret, c, c_ctx, norm1_w, norm2_w, normf_w,
           w_mod, b_mod, w_in_even, gla_w_alpha, gla_b_alpha, hgrn_lb_logits, gla_norm_w, hgrn_norm_w,
           w_out_even, w_in_odd, w_out_odd, router_g_w, router_g_b, router_e_w, router_e_b,
           moe_w_gate, moe_w_up, moe_w_down):
    x_parts = [x_prompt.reshape(T_CTX, D), x_sample.reshape(T_SMP, D)]
    cond8 = jnp.concatenate([c_ctx[None, :], c, jnp.zeros((COND_ROWS - N_COND, D), F32)], axis=0)
    mods = _mods(cond8, w_mod, b_mod).reshape(DEPTH, COND_ROWS, 6, D)
    lb_all = jnp.cumsum(jax.nn.softmax(hgrn_lb_logits.astype(F32), axis=0), axis=0)[:N_EVEN]
    cos_t, sin_t = _rope_tables()
    ret_tables = _ret_tables()

    nwf = normf_w.reshape(1, D)
    st_even, st_ret = None, None
    for l in range(DEPTH):
        mods_l = mods[l]
        nw1 = norm1_w[l].reshape(1, D)
        nw2 = norm2_w[l].reshape(1, D)
        wr = jnp.concatenate([router_e_w[l], router_g_w[l],
                              jnp.zeros((D, ROUTER_N - N_EXPERTS - N_GROUPS), F32)], axis=1)
        wr_hi = wr.astype(BF16)
        wr_lo = (wr - wr_hi.astype(F32)).astype(BF16)
        br = jnp.concatenate([router_e_b[l], router_g_b[l],
                              jnp.zeros((ROUTER_N - N_EXPERTS - N_GROUPS,), F32)]).reshape(1, ROUTER_N)
        if l % 2 == 0:
            e = l // 2
            w = w_in_even[e]
            a0 = 2 * GLA_QK + 2 * GLA_V
            a1 = a0 + 2 * GLA_RANK
            f0, f1 = a1 + HG_K, a1 + 3 * HG_K
            wm_bf = jnp.concatenate([w[:, :a0], w[:, a1:f0], w[:, f1:]], axis=1).astype(BF16)
            wg_bf = jnp.concatenate([w[:, f0:f1], w[:, a0:a1],
                                     jnp.zeros((D, LANES - 2 * GLA_RANK), F32)], axis=1).astype(BF16)
            zm, zg = _inproj_even(x_parts, mods_l, nw1, wm_bf, wg_bf)
            w2_pad = jnp.zeros((2, LANES, GLA_QK), F32)
            for d in range(2):
                w2_pad = w2_pad.at[d, d * GLA_RANK:(d + 1) * GLA_RANK].set(gla_w_alpha[e, d])
            s_g = state_gla[:, e].reshape(SMP_B, 2, GLA_HEADS // 2, LANES, LANES)
            o_f, o_b, ng, nh = _even_scan(zm, zg, w2_pad.astype(BF16), gla_b_alpha[e].reshape(2, 1, GLA_QK),
                                          lb_all[e].reshape(1, HG_K), s_g, state_hgrn[:, e], st_even)
            st_even = (ng, nh)
            gn = jnp.concatenate([gla_norm_w[e], hgrn_norm_w[e]]).reshape(1, D)
            x1, hrow, meta, counts = _post_even(o_f, o_b, zm, x_parts, mods_l, gn, w_out_even[e].astype(BF16),
                                                nw2, wr_hi, wr_lo, br)
        else:
            j = l // 2
            z = _inproj_odd(x_parts[0], mods_l, nw1, w_in_odd[j].astype(BF16), cos_t, sin_t)
            o_f, o_b, st_ret = _odd_scan(z, ret_tables, state_ret[:, j], st_ret)
            x1, hrow, meta, counts = _post_odd(o_f, o_b, z, x_parts[0], mods_l, w_out_odd[j].astype(BF16),
                                               nw2, wr_hi, wr_lo, br)
        dest, ea, eb, nv, n_real = _dispatch_plan(meta, counts)
        hs = _dispatch(dest, nv, hrow)
        ys = _moe(ea, eb, n_real, hs, moe_w_gate[l].astype(BF16), moe_w_up[l].astype(BF16),
                  moe_w_down[l].astype(BF16))
        if l < DEPTH - 1:
            x_parts = [_combine(dest, ys, x1, mods_l, nwf)]
    y_prompt = _combine(dest, ys, x1, mods_l, nwf, 0, T_CTX, final=True).reshape(CTX_B, CTX_L, D)
    y_sample = _combine(dest, ys, x1, mods_l, nwf, T_CTX // TM, T_SMP, final=True).reshape(SMP_B, SMP_L, D)
    st_gla = st_even[0].reshape(CTX_B, N_EVEN, 2, GLA_HEADS, GLA_DK, GLA_DV)
    return (y_prompt, y_sample, st_gla, st_even[1], st_ret)
```

```python
import functools

import numpy as np
import jax
import jax.numpy as jnp
from jax import lax
from jax.experimental import pallas as pl
from jax.experimental.pallas import tpu as pltpu

F32 = jnp.float32
BF16 = jnp.bfloat16

D = 1024
CTX_B, CTX_L = 32, 256
SMP_B, SMP_L = 4, 4096
DEPTH = 4
N_EVEN, N_ODD = 2, 2
GRID_W = 64
NORM_EPS = 1e-6
T_CTX = CTX_B * CTX_L
T_SMP = SMP_B * SMP_L
T_ALL = T_CTX + T_SMP
N_COND = 1 + SMP_B
COND_ROWS = 8

GLA_HEADS, GLA_DK, GLA_DV, GLA_RANK, GLA_TAU = 4, 64, 128, 16, 16.0
GLA_QK, GLA_V = GLA_HEADS * GLA_DK, GLA_HEADS * GLA_DV
HG_HEADS, HG_DK, HG_DV = 4, 128, 128
HG_K, HG_V = HG_HEADS * HG_DK, HG_HEADS * HG_DV
RET_HEADS, RET_DK, RET_DV = 4, 256, 512
RET_QK, RET_V = RET_HEADS * RET_DK, RET_HEADS * RET_DV
ROPE_BASE, ROPE_PAIRS = 10000.0, 64
N_GROUPS, EXP_PER_GROUP, N_EXPERTS, D_EXPERT = 4, 8, 32, 256

LANES = 128
CHUNK = 64
TL = 256
TM = 512
PAIRS_PER_GROUP = EXP_PER_GROUP * (EXP_PER_GROUP - 1) // 2
N_PAIRS = N_GROUPS * PAIRS_PER_GROUP
TMG = 256
N_TILES_MOE = T_ALL // TMG + N_PAIRS
ROW_W = D + LANES
DISPATCH_ROWS = 1024
SUBLANES = 8
VMEM_LIMIT = 56 * 1024 * 1024

EV_QA, EV_KA, EV_VA, EV_RA, EV_QB, EV_IB, EV_GB = 0, 256, 512, 1024, 1536, 2048, 2560
EV_MAIN = 3072
EV_FF, EV_FB, EV_AL = 0, 512, 1024
EV_GATE = 1024 + LANES
OD_Q, OD_K, OD_V, OD_G = 0, 1024, 2048, 4096
OD_N = 6144
ROUTER_N = LANES


def _dot(a, b):
    return jnp.dot(a, b, preferred_element_type=F32)


def _dot_nt(a, b):
    return lax.dot_general(a, b, (((1,), (1,)), ((), ())), preferred_element_type=F32)


def _dot_tn(a, b):
    return lax.dot_general(a, b, (((0,), (0,)), ((), ())), preferred_element_type=F32)


def _split_bf16(x):
    hi = x.astype(BF16)
    lo = (x - hi.astype(F32)).astype(BF16)
    return hi, lo


def _sigmoid(x):
    return 1.0 / (1.0 + jnp.exp(-x))


def _silu(x):
    return x * _sigmoid(x)


def _params(n_axes):
    return pltpu.CompilerParams(dimension_semantics=("arbitrary",) * n_axes,
                                vmem_limit_bytes=VMEM_LIMIT)


def _cond_of_tile(i, tm):
    n_ctx = T_CTX // tm
    per_seq = SMP_L // tm
    return jnp.where(i < n_ctx, 0, 1 + (i - n_ctx) // per_seq)


def _norm_mod(x, nw, shift, scale):
    var = jnp.mean(x * x, axis=-1, keepdims=True)
    return x * lax.rsqrt(var + NORM_EPS) * nw * (1.0 + scale) + shift


def _mods_kernel(cond_ref, w_ref, b_ref, o_ref):
    c = cond_ref[...]
    a_hi, a_lo = _split_bf16(_silu(c))
    w_hi, w_lo = _split_bf16(w_ref[0])
    o_ref[0] = _dot(a_hi, w_hi) + _dot(a_hi, w_lo) + _dot(a_lo, w_hi) + b_ref[0]


def _mods(cond8, w_mod, b_mod):
    return pl.pallas_call(
        _mods_kernel,
        grid=(DEPTH, 6),
        in_specs=[pl.BlockSpec((COND_ROWS, D), lambda l, j: (0, 0)),
                  pl.BlockSpec((1, D, D), lambda l, j: (l, 0, j)),
                  pl.BlockSpec((1, 1, D), lambda l, j: (l, 0, j))],
        out_specs=pl.BlockSpec((1, COND_ROWS, D), lambda l, j: (l, 0, j)),
        out_shape=jax.ShapeDtypeStruct((DEPTH, COND_ROWS, 6 * D), F32),
        compiler_params=_params(2), name="mods",
    )(cond8, w_mod, b_mod.reshape(DEPTH, 1, 6 * D))


def _x_specs(x_parts):
    if len(x_parts) == 1:
        return [pl.BlockSpec((TM, D), lambda i: (i, 0))]
    n_ctx = T_CTX // TM
    return [pl.BlockSpec((TM, D), lambda i: (jnp.minimum(i, n_ctx - 1), 0)),
            pl.BlockSpec((TM, D), lambda i: (jnp.maximum(i - n_ctx, 0), 0))]


def _load_x(x_refs):
    if len(x_refs) == 1:
        return x_refs[0][...]
    return jnp.where(pl.program_id(0) < T_CTX // TM, x_refs[0][...], x_refs[1][...])


def _inproj_even_kernel(*refs, n_x):
    x_refs = refs[:n_x]
    mod_ref, nw_ref, wm_ref, wg_ref, zm_ref, zg_ref = refs[n_x:]
    m = mod_ref[...]
    h = _norm_mod(_load_x(x_refs), nw_ref[...], m[0:1], m[1:2]).astype(BF16)
    for j in range(0, EV_MAIN, 1024):
        zm_ref[:, j:j + 1024] = _dot(h, wm_ref[:, j:j + 1024]).astype(BF16)
    for j, width in ((0, 1024), (EV_AL, LANES)):
        zg_ref[:, j:j + width] = _dot(h, wg_ref[:, j:j + width])


def _inproj_odd_kernel(x_ref, mod_ref, nw_ref, w_ref, cos_ref, sin_ref, z_ref):
    m = mod_ref[...]
    h = _norm_mod(x_ref[...], nw_ref[...], m[0:1], m[1:2]).astype(BF16)
    for j in range(0, OD_V, 2 * LANES):
        zz = _dot(h, w_ref[:, j:j + 2 * LANES])
        for s in range(2):
            blk = zz[:, s * LANES:(s + 1) * LANES]
            cs = cos_ref[:, s * LANES:(s + 1) * LANES]
            sn = sin_ref[:, s * LANES:(s + 1) * LANES]
            z_ref[:, j + s * LANES:j + (s + 1) * LANES] = (
                blk * cs + pltpu.roll(blk, ROPE_PAIRS, 1) * sn).astype(BF16)
    step = 1024
    for j in range(OD_V, OD_N, step):
        z_ref[:, j:j + step] = _dot(h, w_ref[:, j:j + step]).astype(BF16)


def _inproj_even(x_parts, mods_l, nw, wm_bf, wg_bf):
    return pl.pallas_call(
        functools.partial(_inproj_even_kernel, n_x=len(x_parts)),
        grid=(T_ALL // TM,),
        in_specs=_x_specs(x_parts) + [
                  pl.BlockSpec((None, 6, D), lambda i: (_cond_of_tile(i, TM), 0, 0)),
                  pl.BlockSpec((1, D), lambda i: (0, 0)),
                  pl.BlockSpec((D, EV_MAIN), lambda i: (0, 0)),
                  pl.BlockSpec((D, EV_GATE), lambda i: (0, 0))],
        out_specs=[pl.BlockSpec((TM, EV_MAIN), lambda i: (i, 0)),
                   pl.BlockSpec((TM, EV_GATE), lambda i: (i, 0))],
        out_shape=[jax.ShapeDtypeStruct((T_ALL, EV_MAIN), BF16),
                   jax.ShapeDtypeStruct((T_ALL, EV_GATE), F32)],
        compiler_params=_params(1), name="inproj_even",
    )(*x_parts, mods_l, nw, wm_bf, wg_bf)


def _rope_block(i):
    n_ctx = T_CTX // TM
    per_seq = SMP_L // TM
    return jnp.where(i < n_ctx, 0, 1 + (i - n_ctx) % per_seq)


def _inproj_odd(x, mods_l, nw, w_bf, cos_t, sin_t):
    return pl.pallas_call(
        _inproj_odd_kernel,
        grid=(T_ALL // TM,),
        in_specs=[pl.BlockSpec((TM, D), lambda i: (i, 0)),
                  pl.BlockSpec((None, 6, D), lambda i: (_cond_of_tile(i, TM), 0, 0)),
                  pl.BlockSpec((1, D), lambda i: (0, 0)),
                  pl.BlockSpec((D, OD_N), lambda i: (0, 0)),
                  pl.BlockSpec((TM, RET_DK), lambda i: (_rope_block(i), 0)),
                  pl.BlockSpec((TM, RET_DK), lambda i: (_rope_block(i), 0))],
        out_specs=pl.BlockSpec((TM, OD_N), lambda i: (i, 0)),
        out_shape=jax.ShapeDtypeStruct((T_ALL, OD_N), BF16),
        compiler_params=_params(1), name="inproj_odd",
    )(x, mods_l, nw, w_bf, cos_t, sin_t)


def _scan_work():
    tf, tb, first, last, seq = [], [], [], [], []
    base = 0
    for s, length in enumerate([CTX_L] * CTX_B + [SMP_L] * SMP_B):
        n = length // TL
        for t in range(n):
            tf.append(base + t)
            tb.append(base + n - 1 - t)
            first.append(int(t == 0))
            last.append(int(t == n - 1))
            seq.append(s)
        base += n
    return tuple(np.asarray(a, np.int32) for a in (tf, tb, first, last, seq))


N_WORK = T_ALL // TL


def _decay_factors(q, k, b, fwd):
    b_last = b[CHUNK - 1:CHUNK] if fwd else b[0:1]
    b_mid = b[CHUNK // 2:CHUNK // 2 + 1]
    qi = q * jnp.exp(b - b_mid)
    ki = k * jnp.exp(b_mid - b)
    q_in = qi * jnp.exp(b_mid)
    kk = ki * jnp.exp(b_last - b_mid)
    return qi, ki, q_in, kk, jnp.exp(b_last)


def _stack(a, b):
    return jnp.concatenate([a, b], axis=0)


def _gla_pair_chunk(q, k, b, v0, v1, st, masks, bd_causal, fwd):
    qi, ki, q_in, kk, g = _decay_factors(q, k, b, fwd)
    m0, m1 = masks
    split = lambda t: _stack(jnp.where(m0, t, 0.0), jnp.where(m1, t, 0.0)).astype(BF16)
    ki_bf = ki.astype(BF16)
    a = jnp.where(bd_causal, _dot_nt(split(qi), _stack(ki_bf, ki_bf)), 0.0)
    vs = _stack(v0, v1)
    o = _dot(a.astype(BF16), vs) + _dot_nt(split(q_in), st.astype(BF16))
    return o, st * g + _dot_tn(vs, split(kk))


def _hgrn_pair_chunk(q0, k0, b0, v0, q1, k1, b1, v1, st, bd_causal, fwd):
    qi0, ki0, qin0, kk0, g0 = _decay_factors(q0, k0, b0, fwd)
    qi1, ki1, qin1, kk1, g1 = _decay_factors(q1, k1, b1, fwd)
    zero = jnp.zeros((CHUNK, LANES), F32)
    wide = lambda t0, t1: _stack(jnp.concatenate([t0, zero], axis=1),
                                 jnp.concatenate([zero, t1], axis=1)).astype(BF16)
    a = jnp.where(bd_causal, _dot_nt(_stack(qi0, qi1).astype(BF16), _stack(ki0, ki1).astype(BF16)), 0.0)
    vs = _stack(v0, v1)
    o = _dot(a.astype(BF16), vs) + _dot_nt(wide(qin0, qin1), st.astype(BF16))
    return o, st * jnp.concatenate([g0, g1], axis=1) + _dot_tn(vs, wide(kk0, kk1))


def _even_gates(d, fg, al, qb, tri, w2_ref, ba_ref, lbv):
    y = _dot(al[...].astype(BF16), w2_ref[d]) + ba_ref[d]
    la_a = (jnp.minimum(y, 0.0) - jnp.log(1.0 + jnp.exp(-jnp.abs(y)))) * (1.0 / GLA_TAU)
    f = fg[...]
    t = jnp.exp(-jnp.abs(f))
    rcp = 1.0 / (1.0 + t)
    sig_f = jnp.where(f >= 0.0, rcp, t * rcp)
    sig_nf = jnp.where(f >= 0.0, t * rcp, rcp)
    la_h = jnp.log(lbv + (1.0 - lbv) * sig_f)
    key_h = (1.0 - lbv) * sig_nf
    la_hi, la_lo = _split_bf16(jnp.concatenate([la_a, la_h], axis=1))
    b_all = _dot(tri, la_hi) + _dot(tri, la_lo)
    return b_all, key_h, _silu(qb[...].astype(F32))


def _even_tile(dir_refs, w2_ref, ba_ref, lb_ref, stg, sth):
    rc = lax.broadcasted_iota(jnp.int32, (2 * CHUNK, 2 * CHUNK), 0)
    cc = lax.broadcasted_iota(jnp.int32, (2 * CHUNK, 2 * CHUNK), 1)
    same_head = (rc // CHUNK) == (cc // CHUNK)
    bd_causal = [same_head & (cc <= rc), same_head & (cc >= rc)]
    rt = lax.broadcasted_iota(jnp.int32, (TL, TL), 0)
    ct = lax.broadcasted_iota(jnp.int32, (TL, TL), 1)
    same = (rt // CHUNK) == (ct // CHUNK)
    tri = [jnp.where(same & (ct <= rt), 1.0, 0.0).astype(BF16),
           jnp.where(same & (ct >= rt), 1.0, 0.0).astype(BF16)]
    lane = lax.broadcasted_iota(jnp.int32, (CHUNK, LANES), 1)
    pair_masks = [lane < GLA_DK, lane >= GLA_DK]
    lbv = lb_ref[...]
    n_pairs = GLA_HEADS // 2
    st_g = [[stg[d, p] for p in range(n_pairs)] for d in range(2)]
    st_h = [[sth[d, p] for p in range(HG_HEADS // 2)] for d in range(2)]
    gates = [_even_gates(d, dir_refs[d][3], dir_refs[d][5], dir_refs[d][2], tri[d], w2_ref, ba_ref, lbv)
             for d in range(2)]

    n_chunks = TL // CHUNK
    for ci in range(n_chunks):
        for d in range(2):
            qk, va, _, _, ib, _, o_ref = dir_refs[d]
            b_all, key_h, qh = gates[d]
            fwd = d == 0
            cpos = ci if fwd else n_chunks - 1 - ci
            rows = slice(cpos * CHUNK, (cpos + 1) * CHUNK)
            for p in range(n_pairs):
                ln = slice(p * LANES, (p + 1) * LANES)
                kl = slice(GLA_QK + p * LANES, GLA_QK + (p + 1) * LANES)
                c0, c1 = 2 * p * GLA_DV, (2 * p + 1) * GLA_DV
                o, st_g[d][p] = _gla_pair_chunk(
                    qk[rows, ln].astype(F32) * (GLA_DK ** -0.5), qk[rows, kl].astype(F32), b_all[rows, ln],
                    va[rows, c0:c0 + GLA_DV], va[rows, c1:c1 + GLA_DV], st_g[d][p],
                    pair_masks, bd_causal[d], fwd)
                o_ref[rows, c0:c0 + GLA_DV] = o[0:CHUNK].astype(BF16)
                o_ref[rows, c1:c1 + GLA_DV] = o[CHUNK:2 * CHUNK].astype(BF16)
            for p in range(HG_HEADS // 2):
                l0 = slice(2 * p * LANES, (2 * p + 1) * LANES)
                l1 = slice((2 * p + 1) * LANES, (2 * p + 2) * LANES)
                bl0 = slice(GLA_QK + l0.start, GLA_QK + l0.stop)
                bl1 = slice(GLA_QK + l1.start, GLA_QK + l1.stop)
                o, st_h[d][p] = _hgrn_pair_chunk(
                    qh[rows, l0], key_h[rows, l0], b_all[rows, bl0], ib[rows, l0],
                    qh[rows, l1], key_h[rows, l1], b_all[rows, bl1], ib[rows, l1],
                    st_h[d][p], bd_causal[d], fwd)
                o_ref[rows, GLA_V + l0.start:GLA_V + l0.stop] = o[0:CHUNK].astype(BF16)
                o_ref[rows, GLA_V + l1.start:GLA_V + l1.stop] = o[CHUNK:2 * CHUNK].astype(BF16)
    for d in range(2):
        for p in range(n_pairs):
            stg[d, p] = st_g[d][p]
        for p in range(HG_HEADS // 2):
            sth[d, p] = st_h[d][p]


def _even_scan_kernel(tf_ref, tb_ref, first_ref, last_ref, seq_ref,
                      qk_f, va_f, qb_f, fg_f, ib_f, al_f,
                      qk_b, va_b, qb_b, fg_b, ib_b, al_b,
                      w2_ref, ba_ref, lb_ref, sg_ref, sh_ref, *rest, n_prev):
    pg_ref, ph_ref = rest[:2] if n_prev else (None, None)
    of_ref, ob_ref, stg_ref, sth_ref, stg, sth = rest[-6:]
    i = pl.program_id(0)
    is_first = first_ref[i] == 1
    is_last = last_ref[i] == 1
    is_ctx = seq_ref[i] < CTX_B

    @pl.when(jnp.logical_and(is_first, is_ctx))
    def _():
        stg[...] = jnp.zeros_like(stg)
        sth[...] = jnp.zeros_like(sth)

    @pl.when(jnp.logical_and(is_first, jnp.logical_not(is_ctx)))
    def _():
        for d in range(2):
            for p in range(GLA_HEADS // 2):
                stg[d, p] = sg_ref[d, p].T
            for h in range(HG_HEADS):
                sth[d, h // 2, :, (h % 2) * HG_DK:(h % 2 + 1) * HG_DK] = sh_ref[d, h].T

    _even_tile([(qk_f, va_f, qb_f, fg_f, ib_f, al_f, of_ref), (qk_b, va_b, qb_b, fg_b, ib_b, al_b, ob_ref)],
               w2_ref, ba_ref, lb_ref, stg, sth)

    @pl.when(jnp.logical_and(is_last, is_ctx))
    def _():
        for k in range(n_prev):
            stg_ref[k] = pg_ref[k]
            sth_ref[k] = ph_ref[k]
        for d in range(2):
            for p in range(GLA_HEADS // 2):
                stg_ref[n_prev, d, p] = stg[d, p].T
            for h in range(HG_HEADS):
                sth_ref[n_prev, d, h] = sth[d, h // 2, :, (h % 2) * HG_DK:(h % 2 + 1) * HG_DK].T


def _even_scan(zm, zg, w2_pad, b_a, lb, s_gla, s_hgrn, prev):
    n_prev = 0 if prev is None else prev[0].shape[1]
    work = _scan_work()

    def zspec(width, col_block, which):
        return pl.BlockSpec((TL, width), lambda i, tf, tb, fi, la, sq: ((tf, tb)[which][i], col_block))

    def dir_specs(which):
        return [zspec(512, EV_QA // 512, which), zspec(512, EV_VA // 512, which),
                zspec(512, EV_QB // 512, which), zspec(512, (EV_FF, EV_FB)[which] // 512, which),
                zspec(512, EV_IB // 512, which), zspec(LANES, EV_AL // LANES, which)]

    def dir_args():
        return [zm, zm, zm, zg, zm, zg]

    def const(shape):
        return pl.BlockSpec(shape, lambda i, *_: (0,) * len(shape))

    def s_in(shape):
        return pl.BlockSpec((None, None) + shape,
                            lambda i, tf, tb, fi, la, sq:
                            (jnp.clip(sq[i] - CTX_B, 0, SMP_B - 1), n_prev) + (0,) * len(shape))

    def ctx_states(n_layers, shape):
        return pl.BlockSpec((None, n_layers) + shape,
                            lambda i, tf, tb, fi, la, sq: (jnp.minimum(sq[i], CTX_B - 1), 0) + (0,) * len(shape))

    gshape = (2, GLA_HEADS // 2, LANES, LANES)
    hshape = (2, HG_HEADS, HG_DK, HG_DV)
    grid_spec = pltpu.PrefetchScalarGridSpec(
        num_scalar_prefetch=5, grid=(N_WORK,),
        in_specs=dir_specs(0) + dir_specs(1) + [
            const((2, LANES, GLA_QK)), const((2, 1, GLA_QK)), const((1, HG_K)),
            s_in(gshape), s_in(hshape)]
        + ([ctx_states(n_prev, gshape), ctx_states(n_prev, hshape)] if n_prev else []),
        out_specs=[pl.BlockSpec((TL, D), lambda i, tf, tb, fi, la, sq: (tf[i], 0)),
                   pl.BlockSpec((TL, D), lambda i, tf, tb, fi, la, sq: (tb[i], 0)),
                   ctx_states(n_prev + 1, gshape), ctx_states(n_prev + 1, hshape)],
        scratch_shapes=[pltpu.VMEM(gshape, F32),
                        pltpu.VMEM((2, HG_HEADS // 2, HG_DV, 2 * HG_DK), F32)])
    return pl.pallas_call(
        functools.partial(_even_scan_kernel, n_prev=n_prev), grid_spec=grid_spec,
        out_shape=[jax.ShapeDtypeStruct((T_ALL, D), BF16), jax.ShapeDtypeStruct((T_ALL, D), BF16),
                   jax.ShapeDtypeStruct((CTX_B, n_prev + 1) + gshape, F32),
                   jax.ShapeDtypeStruct((CTX_B, n_prev + 1) + hshape, F32)],
        compiler_params=_params(1), name="even_scan",
    )(*work, *dir_args(), *dir_args(), w2_pad, b_a, lb, s_gla, s_hgrn, *(prev if n_prev else []))


RET_HP = 2


def _ret_tables():
    lg = np.log1p(-np.exp2(-5.0 - np.arange(RET_HEADS, dtype=np.float64)))
    pos = np.arange(TL, dtype=np.float64)
    dm = np.zeros((2, RET_HEADS, TL, TL))
    rq = np.zeros((2, RET_HEADS, TL, 1))
    rk = np.zeros((2, RET_HEADS, TL, 1))
    gc = np.zeros((2, RET_HEADS, 1, RET_DV))
    diff = pos[:, None] - pos[None, :]
    kscale = RET_DK ** -0.5
    for h in range(RET_HEADS):
        dm[0, h] = np.where(diff >= 0, np.exp(lg[h] * np.maximum(diff, 0)), 0.0) * kscale
        rq[0, h, :, 0] = np.exp(lg[h] * (pos + 1))
        rk[0, h, :, 0] = np.exp(lg[h] * (TL - 1 - pos)) * kscale
        gc[0, h] = np.exp(lg[h] * TL)
        lb = lg[RET_HEADS - 1 - h]
        dm[1, h] = np.where(diff <= 0, np.exp(lb * np.maximum(-diff, 0)), 0.0) * kscale
        rq[1, h, :, 0] = np.exp(lb * (TL - pos))
        rk[1, h, :, 0] = np.exp(lb * pos) * kscale
        gc[1, h] = np.exp(lb * TL)
    return tuple(jnp.asarray(a, F32) for a in (dm, rq, rk, gc))


def _odd_dir(d, q_ref, k_ref, v_ref, dm_ref, rq_ref, rk_ref, gc_ref, st, o_ref):
    for hh in range(RET_HP):
        k = k_ref[:, hh * RET_DK:(hh + 1) * RET_DK]
        q = q_ref[:, hh * RET_DK:(hh + 1) * RET_DK]
        v = v_ref[:, hh * RET_DV:(hh + 1) * RET_DV]
        a = _dot_nt(q, k) * dm_ref[d, hh]
        s = st[d, hh]
        o = _dot(a.astype(BF16), v) + rq_ref[d, hh] * _dot(q, s.astype(BF16))
        kk = (k.astype(F32) * rk_ref[d, hh]).astype(BF16)
        st[d, hh] = gc_ref[d, hh] * s + _dot_tn(kk, v)
        o_ref[:, hh * RET_DV:(hh + 1) * RET_DV] = o.astype(BF16)


def _odd_scan_kernel(tf_ref, tb_ref, first_ref, last_ref, seq_ref,
                     q_f, k_f, v_f, q_b, k_b, v_b,
                     dm_ref, rq_ref, rk_ref, gc_ref, s0_ref, *rest, n_prev):
    prev_ref = rest[0] if n_prev else None
    of_ref, ob_ref, sout_ref, st = rest[-4:]
    i = pl.program_id(1)
    is_first = first_ref[i] == 1
    is_last = last_ref[i] == 1
    is_ctx = seq_ref[i] < CTX_B

    @pl.when(jnp.logical_and(is_first, is_ctx))
    def _():
        st[...] = jnp.zeros_like(st)

    @pl.when(jnp.logical_and(is_first, jnp.logical_not(is_ctx)))
    def _():
        st[...] = s0_ref[...]

    _odd_dir(0, q_f, k_f, v_f, dm_ref, rq_ref, rk_ref, gc_ref, st, of_ref)
    _odd_dir(1, q_b, k_b, v_b, dm_ref, rq_ref, rk_ref, gc_ref, st, ob_ref)

    @pl.when(jnp.logical_and(is_last, is_ctx))
    def _():
        for k in range(n_prev):
            sout_ref[k] = prev_ref[k]
        sout_ref[n_prev] = st[...]


def _odd_scan(z, tables, s_ret, prev):
    n_prev = 0 if prev is None else prev.shape[1]
    work = _scan_work()
    dm, rq, rk, gc = tables
    qw, vw = RET_HP * RET_DK, RET_HP * RET_DV

    def zspec(width, col0, which):
        return pl.BlockSpec((TL, width),
                            lambda hp, i, tf, tb, fi, la, sq: ((tf, tb)[which][i], col0 // width + hp))

    def dir_specs(which):
        return [zspec(qw, OD_Q, which), zspec(qw, OD_K, which), zspec(vw, OD_V, which)]

    def table(shape):
        return pl.BlockSpec((2, RET_HP) + shape, lambda hp, i, *_: (0, hp) + (0,) * len(shape))

    sshape = (2, RET_HP, RET_DK, RET_DV)

    def ctx_states(n_layers):
        return pl.BlockSpec((None, n_layers) + sshape,
                            lambda hp, i, tf, tb, fi, la, sq: (jnp.minimum(sq[i], CTX_B - 1), 0, 0, hp, 0, 0))

    grid_spec = pltpu.PrefetchScalarGridSpec(
        num_scalar_prefetch=5, grid=(RET_HEADS // RET_HP, N_WORK),
        in_specs=dir_specs(0) + dir_specs(1) + [
            table((TL, TL)), table((TL, 1)), table((TL, 1)), table((1, RET_DV)),
            pl.BlockSpec((None, None) + sshape,
                         lambda hp, i, tf, tb, fi, la, sq:
                         (jnp.clip(sq[i] - CTX_B, 0, SMP_B - 1), n_prev, 0, hp, 0, 0))]
        + ([ctx_states(n_prev)] if n_prev else []),
        out_specs=[pl.BlockSpec((TL, vw), lambda hp, i, tf, tb, fi, la, sq: (tf[i], hp)),
                   pl.BlockSpec((TL, vw), lambda hp, i, tf, tb, fi, la, sq: (tb[i], hp)),
                   ctx_states(n_prev + 1)],
        scratch_shapes=[pltpu.VMEM(sshape, F32)])
    return pl.pallas_call(
        functools.partial(_odd_scan_kernel, n_prev=n_prev), grid_spec=grid_spec,
        out_shape=[jax.ShapeDtypeStruct((T_ALL, RET_V), BF16), jax.ShapeDtypeStruct((T_ALL, RET_V), BF16),
                   jax.ShapeDtypeStruct((CTX_B, n_prev + 1, 2, RET_HEADS, RET_DK, RET_DV), F32)],
        compiler_params=_params(2), name="odd_scan",
    )(*work, z, z, z, z, z, z, dm, rq, rk, gc, s_ret, *([prev] if n_prev else []))


def _head_rms(o, width):
    parts = []
    for j in range(0, o.shape[1], width):
        blk = o[:, j:j + width]
        parts.append(blk * lax.rsqrt(jnp.mean(blk * blk, axis=-1, keepdims=True) + NORM_EPS))
    return parts


def _route(h2, wr_hi_ref, wr_lo_ref, br_ref):
    h_hi, h_lo = _split_bf16(h2)
    logits = (_dot(h_hi, wr_hi_ref[...]) + _dot(h_hi, wr_lo_ref[...]) + _dot(h_lo, wr_hi_ref[...])
              + br_ref[...])
    lane_i = lax.broadcasted_iota(jnp.int32, logits.shape, 1)
    lane = lane_i.astype(F32)
    neg = -jnp.inf
    big = 1e9
    is_grp = (lane_i >= N_EXPERTS) & (lane_i < N_EXPERTS + N_GROUPS)
    lg = jnp.where(is_grp, logits, neg)
    mg = jnp.max(lg, axis=-1, keepdims=True)
    gsel = jnp.min(jnp.where(lg == mg, lane - N_EXPERTS, big), axis=-1, keepdims=True)
    pg = 1.0 / jnp.sum(jnp.where(is_grp, jnp.exp(lg - mg), 0.0), axis=-1, keepdims=True)
    in_grp = (lane_i < N_EXPERTS) & ((lane_i // EXP_PER_GROUP).astype(F32) == gsel)
    le = jnp.where(in_grp, logits, neg)
    m1 = jnp.max(le, axis=-1, keepdims=True)
    i1 = jnp.min(jnp.where(le == m1, lane, big), axis=-1, keepdims=True)
    le2 = jnp.where(lane == i1, neg, le)
    m2 = jnp.max(le2, axis=-1, keepdims=True)
    i2 = jnp.min(jnp.where(le2 == m2, lane, big), axis=-1, keepdims=True)
    e2 = jnp.exp(m2 - m1)
    w1 = pg / (1.0 + e2)
    w2 = pg * e2 / (1.0 + e2)
    first_lo = i1 < i2
    ia = jnp.minimum(i1, i2) - gsel * EXP_PER_GROUP
    ib = jnp.maximum(i1, i2) - gsel * EXP_PER_GROUP
    pair = gsel * PAIRS_PER_GROUP + ia * (2 * EXP_PER_GROUP - 1 - ia) * 0.5 + (ib - ia - 1.0)
    return pair, jnp.where(first_lo, w1, w2), jnp.where(first_lo, w2, w1)


def _post_tail(i, x, out, m, nw2_ref, wr_hi_ref, wr_lo_ref, br_ref,
               x1_ref, hrow_ref, meta_ref, cnt_ref, carry):
    x1 = x + m[2:3] * out
    x1_ref[...] = x1
    h2 = _norm_mod(x1, nw2_ref[...], m[3:4], m[4:5])
    pair, w_a, w_b = _route(h2, wr_hi_ref, wr_lo_ref, br_ref)

    @pl.when(i == 0)
    def _():
        carry[...] = jnp.zeros_like(carry)

    lane = lax.broadcasted_iota(jnp.int32, (TM, LANES), 1)
    onehot = jnp.where(lane.astype(F32) == pair, 1.0, 0.0)
    r = lax.broadcasted_iota(jnp.int32, (TM, TM), 0)
    c = lax.broadcasted_iota(jnp.int32, (TM, TM), 1)
    earlier = jnp.where(c < r, 1.0, 0.0).astype(BF16)
    before = _dot(earlier, onehot.astype(BF16)) + carry[...]
    rank = jnp.sum(onehot * before, axis=-1, keepdims=True)
    carry[...] += jnp.sum(onehot, axis=0, keepdims=True)
    cnt_ref[...] = carry[...]
    meta_ref[...] = jnp.where(lane == 0, pair, jnp.where(lane == 1, rank, 0.0))
    hrow_ref[:, 0:D] = h2
    hrow_ref[:, D:ROW_W] = jnp.where(lane == 0, w_a, jnp.where(lane == 1, w_b, 0.0))


def _post_even_kernel(*refs, n_x):
    x_refs = refs[:n_x]
    (of_ref, ob_ref, ra_ref, gb_ref, mod_ref, gn_ref, wo_ref, nw2_ref, wr_hi_ref, wr_lo_ref, br_ref,
     x1_ref, hrow_ref, meta_ref, cnt_ref, carry) = refs[n_x:]
    o = of_ref[...].astype(F32) + ob_ref[...].astype(F32)
    normed = jnp.concatenate(_head_rms(o, GLA_DV), axis=1)
    gate = jnp.concatenate([_silu(ra_ref[...].astype(F32)), _silu(gb_ref[...].astype(F32))], axis=1)
    mixed = (normed * gn_ref[...] * gate).astype(BF16)
    out = _dot(mixed, wo_ref[...])
    _post_tail(pl.program_id(0), _load_x(x_refs), out, mod_ref[...], nw2_ref, wr_hi_ref, wr_lo_ref, br_ref,
               x1_ref, hrow_ref, meta_ref, cnt_ref, carry)


def _post_odd_kernel(of_ref, ob_ref, g_ref, x_ref, mod_ref, wo_ref,
                     nw2_ref, wr_hi_ref, wr_lo_ref, br_ref,
                     x1_ref, hrow_ref, meta_ref, cnt_ref, carry):
    o = of_ref[...].astype(F32) + ob_ref[...].astype(F32)
    normed = jnp.concatenate(_head_rms(o, RET_DV), axis=1)
    mixed = (normed * _silu(g_ref[...].astype(F32))).astype(BF16)
    out = _dot(mixed, wo_ref[...])
    _post_tail(pl.program_id(0), x_ref[...], out, mod_ref[...], nw2_ref, wr_hi_ref, wr_lo_ref, br_ref,
               x1_ref, hrow_ref, meta_ref, cnt_ref, carry)


def _post_common_specs():
    tile = lambda w: pl.BlockSpec((TM, w), lambda i: (i, 0))
    const = lambda s: pl.BlockSpec(s, lambda i: (0,) * len(s))
    mod = pl.BlockSpec((None, 6, D), lambda i: (_cond_of_tile(i, TM), 0, 0))
    out_specs = [tile(D), tile(ROW_W), tile(LANES), const((1, LANES))]
    out_shape = [jax.ShapeDtypeStruct((T_ALL, D), F32),
                 jax.ShapeDtypeStruct((T_ALL, ROW_W), F32),
                 jax.ShapeDtypeStruct((T_ALL, LANES), F32),
                 jax.ShapeDtypeStruct((1, LANES), F32)]
    scratch = [pltpu.VMEM((1, LANES), F32)]
    return tile, const, mod, out_specs, out_shape, scratch


def _post_even(o_f, o_b, z, x_parts, mods_l, gn, wo_bf, nw2, wr_hi, wr_lo, br):
    tile, const, mod, out_specs, out_shape, scratch = _post_common_specs()
    zcol = lambda c0: pl.BlockSpec((TM, 512), lambda i: (i, c0 // 512))
    return pl.pallas_call(
        functools.partial(_post_even_kernel, n_x=len(x_parts)), grid=(T_ALL // TM,),
        in_specs=_x_specs(x_parts) + [
                  tile(D), tile(D), zcol(EV_RA), zcol(EV_GB), mod, const((1, D)),
                  const((D, D)), const((1, D)), const((D, ROUTER_N)), const((D, ROUTER_N)),
                  const((1, ROUTER_N))],
        out_specs=out_specs, out_shape=out_shape, scratch_shapes=scratch,
        compiler_params=_params(1), name="post_even",
    )(*x_parts, o_f, o_b, z, z, mods_l, gn, wo_bf, nw2, wr_hi, wr_lo, br)


def _post_odd(o_f, o_b, z, x, mods_l, wo_bf, nw2, wr_hi, wr_lo, br):
    tile, const, mod, out_specs, out_shape, scratch = _post_common_specs()
    return pl.pallas_call(
        _post_odd_kernel, grid=(T_ALL // TM,),
        in_specs=[tile(RET_V), tile(RET_V),
                  pl.BlockSpec((TM, RET_V), lambda i: (i, OD_G // RET_V)), tile(D), mod,
                  const((RET_V, D)), const((1, D)), const((D, ROUTER_N)), const((D, ROUTER_N)),
                  const((1, ROUTER_N))],
        out_specs=out_specs, out_shape=out_shape, scratch_shapes=scratch,
        compiler_params=_params(1), name="post_odd",
    )(o_f, o_b, z, x, mods_l, wo_bf, nw2, wr_hi, wr_lo, br)


def _pair_tables():
    ea, eb = [], []
    for g in range(N_GROUPS):
        for a in range(EXP_PER_GROUP):
            for b in range(a + 1, EXP_PER_GROUP):
                ea.append(g * EXP_PER_GROUP + a)
                eb.append(g * EXP_PER_GROUP + b)
    pad = LANES - len(ea)
    return (jnp.asarray(ea + [0] * pad, jnp.int32), jnp.asarray(eb + [0] * pad, jnp.int32))


def _dispatch_plan(meta, counts):
    pair = meta[:, 0].astype(jnp.int32)
    rank = meta[:, 1].astype(jnp.int32)
    cnt = counts[0].astype(jnp.int32)
    tiles_g = (cnt + TMG - 1) // TMG
    tile_end = jnp.cumsum(tiles_g)
    tile_start = tile_end - tiles_g
    slot0 = tile_start * TMG
    ids = jnp.arange(LANES, dtype=jnp.int32)
    dest = rank + jnp.sum(jnp.where(pair[:, None] == ids[None, :], slot0[None, :], 0), axis=1)
    n_real = tile_end[-1]
    j = jnp.arange(N_TILES_MOE, dtype=jnp.int32)
    jj = jnp.minimum(j, n_real - 1)
    grp = jnp.sum((jj[:, None] >= tile_end[None, :]).astype(jnp.int32), axis=1)
    ea_t, eb_t = _pair_tables()
    onehot_g = grp[:, None] == ids[None, :]
    pick = lambda v: jnp.sum(jnp.where(onehot_g, v[None, :], 0), axis=1)
    nv = jnp.clip(pick(cnt) - (jj - pick(tile_start)) * TMG, 0, TMG)
    nv = jnp.where(j < n_real, nv, 0)
    return dest // SUBLANES, dest % SUBLANES, pick(ea_t), pick(eb_t), nv, n_real.reshape(1)


DISPATCH_STEPS = T_ALL // DISPATCH_ROWS


def _dispatch_kernel(dhi_ref, dlo_ref, nv_ref, h_ref, hs_hbm, zeros, sem, pad_sem):
    j = pl.program_id(0)
    base = j * DISPATCH_ROWS
    tile_sub = TMG // SUBLANES

    def pad_tile(tile):
        return pltpu.make_async_copy(zeros, hs_hbm.at[pl.ds(tile * tile_sub, tile_sub)], pad_sem.at[0])

    @pl.when(j == 0)
    def _():
        zeros[...] = jnp.zeros_like(zeros)

        def fill(tile, carry):
            @pl.when(nv_ref[tile] < TMG)
            def _():
                pad_tile(tile).start()
            return carry

        def drain(tile, carry):
            @pl.when(nv_ref[tile] < TMG)
            def _():
                pad_tile(tile).wait()
            return carry

        lax.fori_loop(0, N_TILES_MOE, fill, 0)
        lax.fori_loop(0, N_TILES_MOE, drain, 0)

    def issue(g, carry):
        for k in range(SUBLANES):
            t = base + g * SUBLANES + k
            pltpu.make_async_copy(h_ref.at[g, pl.ds(k, 1)], hs_hbm.at[dhi_ref[t], pl.ds(dlo_ref[t], 1)],
                                  sem.at[0]).start(priority=k % 2)
        return carry

    lax.fori_loop(0, DISPATCH_ROWS // SUBLANES, issue, 0)
    pltpu.make_async_copy(h_ref, hs_hbm.at[pl.ds(0, DISPATCH_ROWS // SUBLANES)], sem.at[0]).wait()


def _dispatch(dest_hi, dest_lo, nv, hrow):
    grid_spec = pltpu.PrefetchScalarGridSpec(
        num_scalar_prefetch=3, grid=(DISPATCH_STEPS,),
        in_specs=[pl.BlockSpec((DISPATCH_ROWS // SUBLANES, SUBLANES, ROW_W), lambda j, *_: (j, 0, 0))],
        out_specs=pl.BlockSpec(memory_space=pl.ANY),
        scratch_shapes=[pltpu.VMEM((TMG // SUBLANES, SUBLANES, ROW_W), F32),
                        pltpu.SemaphoreType.DMA((1,)), pltpu.SemaphoreType.DMA((1,))])
    hs = pl.pallas_call(
        _dispatch_kernel, grid_spec=grid_spec,
        out_shape=jax.ShapeDtypeStruct((N_TILES_MOE * TMG // SUBLANES, SUBLANES, ROW_W), F32),
        compiler_params=_params(1), name="moe_dispatch",
    )(dest_hi, dest_lo, nv, hrow.reshape(T_ALL // SUBLANES, SUBLANES, ROW_W))
    return hs.reshape(N_TILES_MOE * TMG, ROW_W)


def _moe_kernel(ea_ref, eb_ref, nreal_ref, hs_ref,
                wga_ref, wua_ref, wda_ref, wgb_ref, wub_ref, wdb_ref, ys_ref):
    j = pl.program_id(0)

    @pl.when(j < nreal_ref[0])
    def _():
        h = hs_ref[:, 0:D].astype(BF16)
        wrow = hs_ref[:, D:ROW_W]
        w_a, w_b = wrow[:, 0:1], wrow[:, 1:2]
        act_a = _silu(_dot(h, wga_ref[0])) * _dot(h, wua_ref[0]) * w_a
        act_b = _silu(_dot(h, wgb_ref[0])) * _dot(h, wub_ref[0]) * w_b
        ys_ref[...] = _dot(act_a.astype(BF16), wda_ref[0]) + _dot(act_b.astype(BF16), wdb_ref[0])

    @pl.when(j >= nreal_ref[0])
    def _():
        ys_ref[...] = jnp.zeros_like(ys_ref)


def _moe(ea, eb, n_real, hs, wg_bf, wu_bf, wd_bf, layer):
    def wspec(shape, which):
        return pl.BlockSpec((None, 1) + shape, lambda j, ea, eb, nr: (layer, (ea, eb)[which][j], 0, 0))

    up, down = (D, D_EXPERT), (D_EXPERT, D)
    grid_spec = pltpu.PrefetchScalarGridSpec(
        num_scalar_prefetch=3, grid=(N_TILES_MOE,),
        in_specs=[pl.BlockSpec((TMG, ROW_W), lambda j, ea, eb, nr: (jnp.minimum(j, nr[0] - 1), 0)),
                  wspec(up, 0), wspec(up, 0), wspec(down, 0),
                  wspec(up, 1), wspec(up, 1), wspec(down, 1)],
        out_specs=pl.BlockSpec((TMG, D), lambda j, ea, eb, nr: (j, 0)))
    return pl.pallas_call(
        _moe_kernel, grid_spec=grid_spec,
        out_shape=jax.ShapeDtypeStruct((N_TILES_MOE * TMG, D), F32),
        compiler_params=_params(1), name="moe",
    )(ea, eb, n_real, hs, wg_bf, wu_bf, wd_bf, wg_bf, wu_bf, wd_bf)


def _combine_kernel(dhi_ref, dlo_ref, ys_hbm, x1_ref, mod_ref, nwf_ref, o_ref, buf, sem, *, tile0, final):
    j = pl.program_id(0)
    n = pl.num_programs(0)

    def gather(tile, s):
        def issue(g, carry):
            for k in range(SUBLANES):
                t = (tile0 + tile) * TM + g * SUBLANES + k
                pltpu.make_async_copy(ys_hbm.at[dhi_ref[t], pl.ds(dlo_ref[t], 1)],
                                      buf.at[s, g, pl.ds(k, 1)], sem.at[s]).start(priority=k % 2)
            return carry
        lax.fori_loop(0, TM // SUBLANES, issue, 0)

    @pl.when(j == 0)
    def _():
        gather(0, 0)

    for s in range(2):
        @pl.when(jnp.logical_and(j + 1 < n, (j + 1) % 2 == s))
        def _():
            gather(j + 1, s)

    for s in range(2):
        @pl.when(j % 2 == s)
        def _():
            pltpu.make_async_copy(ys_hbm.at[pl.ds(0, TM // SUBLANES)], buf.at[s], sem.at[s]).wait()
            x = x1_ref[...] + mod_ref[5:6] * buf[s].reshape(TM, D)
            if final:
                var = jnp.mean(x * x, axis=-1, keepdims=True)
                x = x * lax.rsqrt(var + NORM_EPS) * nwf_ref[...]
            o_ref[...] = x


def _combine(dest_hi, dest_lo, ys, x1, mods_l, nwf, tile0=0, n_tok=T_ALL, final=False):
    grid_spec = pltpu.PrefetchScalarGridSpec(
        num_scalar_prefetch=2, grid=(n_tok // TM,),
        in_specs=[pl.BlockSpec(memory_space=pl.ANY),
                  pl.BlockSpec((TM, D), lambda i, *_: (tile0 + i, 0)),
                  pl.BlockSpec((None, 6, D), lambda i, *_: (_cond_of_tile(tile0 + i, TM), 0, 0)),
                  pl.BlockSpec((1, D), lambda i, *_: (0, 0))],
        out_specs=pl.BlockSpec((TM, D), lambda i, *_: (i, 0)),
        scratch_shapes=[pltpu.VMEM((2, TM // SUBLANES, SUBLANES, D), F32), pltpu.SemaphoreType.DMA((2,))])
    return pl.pallas_call(
        functools.partial(_combine_kernel, tile0=tile0, final=final), grid_spec=grid_spec,
        out_shape=jax.ShapeDtypeStruct((n_tok, D), F32),
        compiler_params=_params(1), name="moe_combine",
    )(dest_hi, dest_lo, ys.reshape(N_TILES_MOE * TMG // SUBLANES, SUBLANES, D), x1, mods_l, nwf)


def _rope_tables():
    freqs = ROPE_BASE ** (-jnp.arange(ROPE_PAIRS, dtype=F32) / ROPE_PAIRS)
    t = jnp.arange(SMP_L)
    halves_c, halves_s = [], []
    for p in (t // GRID_W, t % GRID_W):
        ang = p.astype(F32)[:, None] * freqs
        cs, sn = jnp.cos(ang), jnp.sin(ang)
        halves_c += [cs, cs]
        halves_s += [-sn, sn]
    cos_t = jnp.concatenate(halves_c, axis=1)
    sin_t = jnp.concatenate(halves_s, axis=1)
    cos_t = jnp.concatenate([jnp.ones((TM, RET_DK), F32), cos_t], axis=0)
    sin_t = jnp.concatenate([jnp.zeros((TM, RET_DK), F32), sin_t], axis=0)
    return cos_t, sin_t


def kernel(x_prompt, x_sample, state_gla, state_hgrn, state_ret, c, c_ctx, norm1_w, norm2_w, normf_w,
           w_mod, b_mod, w_in_even, gla_w_alpha, gla_b_alpha, hgrn_lb_logits, gla_norm_w, hgrn_norm_w,
           w_out_even, w_in_odd, w_out_odd, router_g_w, router_g_b, router_e_w, router_e_b,
           moe_w_gate, moe_w_up, moe_w_down):
    x_parts = [x_prompt.reshape(T_CTX, D), x_sample.reshape(T_SMP, D)]
    cond8 = jnp.concatenate([c_ctx[None, :], c, jnp.zeros((COND_ROWS - N_COND, D), F32)], axis=0)
    mods = _mods(cond8, w_mod, b_mod).reshape(DEPTH, COND_ROWS, 6, D)
    lb_all = jnp.cumsum(jax.nn.softmax(hgrn_lb_logits.astype(F32), axis=0), axis=0)[:N_EVEN]
    cos_t, sin_t = _rope_tables()
    ret_tables = _ret_tables()

    nwf = normf_w.reshape(1, D)
    moe_g, moe_u, moe_d = moe_w_gate.astype(BF16), moe_w_up.astype(BF16), moe_w_down.astype(BF16)
    st_even, st_ret = None, None
    for l in range(DEPTH):
        mods_l = mods[l]
        nw1 = norm1_w[l].reshape(1, D)
        nw2 = norm2_w[l].reshape(1, D)
        wr = jnp.concatenate([router_e_w[l], router_g_w[l],
                              jnp.zeros((D, ROUTER_N - N_EXPERTS - N_GROUPS), F32)], axis=1)
        wr_hi = wr.astype(BF16)
        wr_lo = (wr - wr_hi.astype(F32)).astype(BF16)
        br = jnp.concatenate([router_e_b[l], router_g_b[l],
                              jnp.zeros((ROUTER_N - N_EXPERTS - N_GROUPS,), F32)]).reshape(1, ROUTER_N)
        if l % 2 == 0:
            e = l // 2
            w = w_in_even[e]
            a0 = 2 * GLA_QK + 2 * GLA_V
            a1 = a0 + 2 * GLA_RANK
            f0, f1 = a1 + HG_K, a1 + 3 * HG_K
            wm_bf = jnp.concatenate([w[:, :a0], w[:, a1:f0], w[:, f1:]], axis=1).astype(BF16)
            wg_bf = jnp.concatenate([w[:, f0:f1], w[:, a0:a1],
                                     jnp.zeros((D, LANES - 2 * GLA_RANK), F32)], axis=1).astype(BF16)
            zm, zg = _inproj_even(x_parts, mods_l, nw1, wm_bf, wg_bf)
            w2_pad = jnp.zeros((2, LANES, GLA_QK), F32)
            for d in range(2):
                w2_pad = w2_pad.at[d, d * GLA_RANK:(d + 1) * GLA_RANK].set(gla_w_alpha[e, d])
            s_g = state_gla.reshape(SMP_B, N_EVEN, 2, GLA_HEADS // 2, LANES, LANES)
            o_f, o_b, ng, nh = _even_scan(zm, zg, w2_pad.astype(BF16), gla_b_alpha[e].reshape(2, 1, GLA_QK),
                                          lb_all[e].reshape(1, HG_K), s_g, state_hgrn, st_even)
            st_even = (ng, nh)
            gn = jnp.concatenate([gla_norm_w[e], hgrn_norm_w[e]]).reshape(1, D)
            x1, hrow, meta, counts = _post_even(o_f, o_b, zm, x_parts, mods_l, gn, w_out_even[e].astype(BF16),
                                                nw2, wr_hi, wr_lo, br)
        else:
            j = l // 2
            z = _inproj_odd(x_parts[0], mods_l, nw1, w_in_odd[j].astype(BF16), cos_t, sin_t)
            o_f, o_b, st_ret = _odd_scan(z, ret_tables, state_ret, st_ret)
            x1, hrow, meta, counts = _post_odd(o_f, o_b, z, x_parts[0], mods_l, w_out_odd[j].astype(BF16),
                                               nw2, wr_hi, wr_lo, br)
        dhi, dlo, ea, eb, nv, n_real = _dispatch_plan(meta, counts)
        hs = _dispatch(dhi, dlo, nv, hrow)
        ys = _moe(ea, eb, n_real, hs, moe_g, moe_u, moe_d, l)
        if l < DEPTH - 1:
            x_parts = [_combine(dhi, dlo, ys, x1, mods_l, nwf)]
    y_prompt = _combine(dhi, dlo, ys, x1, mods_l, nwf, 0, T_CTX, final=True).reshape(CTX_B, CTX_L, D)
    y_sample = _combine(dhi, dlo, ys, x1, mods_l, nwf, T_CTX // TM, T_SMP,
                        final=True).reshape(SMP_B, SMP_L, D)
    st_gla = st_even[0].reshape(CTX_B, N_EVEN, 2, GLA_HEADS, GLA_DK, GLA_DV)
    return (y_prompt, y_sample, st_gla, st_even[1], st_ret)
```

```python
import functools

import numpy as np
import jax
import jax.numpy as jnp
from jax import lax
from jax.experimental import pallas as pl
from jax.experimental.pallas import tpu as pltpu

F32 = jnp.float32
BF16 = jnp.bfloat16

D = 1024
CTX_B, CTX_L = 32, 256
SMP_B, SMP_L = 4, 4096
DEPTH = 4
N_EVEN, N_ODD = 2, 2
GRID_W = 64
NORM_EPS = 1e-6
T_CTX = CTX_B * CTX_L
T_SMP = SMP_B * SMP_L
T_ALL = T_CTX + T_SMP
N_COND = 1 + SMP_B
COND_ROWS = 8

GLA_HEADS, GLA_DK, GLA_DV, GLA_RANK, GLA_TAU = 4, 64, 128, 16, 16.0
GLA_QK, GLA_V = GLA_HEADS * GLA_DK, GLA_HEADS * GLA_DV
HG_HEADS, HG_DK, HG_DV = 4, 128, 128
HG_K, HG_V = HG_HEADS * HG_DK, HG_HEADS * HG_DV
RET_HEADS, RET_DK, RET_DV = 4, 256, 512
RET_QK, RET_V = RET_HEADS * RET_DK, RET_HEADS * RET_DV
ROPE_BASE, ROPE_PAIRS = 10000.0, 64
N_GROUPS, EXP_PER_GROUP, N_EXPERTS, D_EXPERT = 4, 8, 32, 256

LANES = 128
CHUNK = 64
TL = 256
TM = 512
PAIRS_PER_GROUP = EXP_PER_GROUP * (EXP_PER_GROUP - 1) // 2
N_PAIRS = N_GROUPS * PAIRS_PER_GROUP
TMG = 256
N_TILES_MOE = T_ALL // TMG + N_PAIRS
ROW_W = D + LANES
DISPATCH_ROWS = 2048
SUBLANES = 8
VMEM_LIMIT = 56 * 1024 * 1024

EV_QA, EV_KA, EV_VA, EV_RA, EV_QB, EV_IB, EV_GB = 0, 256, 512, 1024, 1536, 2048, 2560
EV_MAIN = 3072
EV_FF, EV_FB, EV_AL = 0, 512, 1024
EV_GATE = 1024 + LANES
OD_Q, OD_K, OD_V, OD_G = 0, 1024, 2048, 4096
OD_N = 6144
ROUTER_N = LANES


def _dot(a, b):
    return jnp.dot(a, b, preferred_element_type=F32)


def _dot_nt(a, b):
    return lax.dot_general(a, b, (((1,), (1,)), ((), ())), preferred_element_type=F32)


def _dot_tn(a, b):
    return lax.dot_general(a, b, (((0,), (0,)), ((), ())), preferred_element_type=F32)


def _split_bf16(x):
    hi = x.astype(BF16)
    lo = (x - hi.astype(F32)).astype(BF16)
    return hi, lo


def _sigmoid(x):
    return 1.0 / (1.0 + jnp.exp(-x))


def _silu(x):
    return x * _sigmoid(x)


def _params(n_axes):
    return pltpu.CompilerParams(dimension_semantics=("arbitrary",) * n_axes,
                                vmem_limit_bytes=VMEM_LIMIT)


def _cond_of_tile(i, tm):
    n_ctx = T_CTX // tm
    per_seq = SMP_L // tm
    return jnp.where(i < n_ctx, 0, 1 + (i - n_ctx) // per_seq)


def _norm_mod(x, nw, shift, scale):
    var = jnp.mean(x * x, axis=-1, keepdims=True)
    return x * lax.rsqrt(var + NORM_EPS) * nw * (1.0 + scale) + shift


def _mods_kernel(cond_ref, w_ref, b_ref, o_ref):
    c = cond_ref[...]
    a_hi, a_lo = _split_bf16(_silu(c))
    w_hi, w_lo = _split_bf16(w_ref[0])
    o_ref[0] = _dot(a_hi, w_hi) + _dot(a_hi, w_lo) + _dot(a_lo, w_hi) + b_ref[0]


def _mods(cond8, w_mod, b_mod):
    return pl.pallas_call(
        _mods_kernel,
        grid=(DEPTH, 6),
        in_specs=[pl.BlockSpec((COND_ROWS, D), lambda l, j: (0, 0)),
                  pl.BlockSpec((1, D, D), lambda l, j: (l, 0, j)),
                  pl.BlockSpec((1, 1, D), lambda l, j: (l, 0, j))],
        out_specs=pl.BlockSpec((1, COND_ROWS, D), lambda l, j: (l, 0, j)),
        out_shape=jax.ShapeDtypeStruct((DEPTH, COND_ROWS, 6 * D), F32),
        compiler_params=_params(2), name="mods",
    )(cond8, w_mod, b_mod.reshape(DEPTH, 1, 6 * D))


def _x_specs(x_parts):
    if len(x_parts) == 1:
        return [pl.BlockSpec((TM, D), lambda i: (i, 0))]
    n_ctx = T_CTX // TM
    return [pl.BlockSpec((TM, D), lambda i: (jnp.minimum(i, n_ctx - 1), 0)),
            pl.BlockSpec((TM, D), lambda i: (jnp.maximum(i - n_ctx, 0), 0))]


def _load_x(x_refs):
    if len(x_refs) == 1:
        return x_refs[0][...]
    return jnp.where(pl.program_id(0) < T_CTX // TM, x_refs[0][...], x_refs[1][...])


def _inproj_even_kernel(*refs, n_x):
    x_refs = refs[:n_x]
    mod_ref, nw_ref, wm_ref, wg_ref, zm_ref, zg_ref = refs[n_x:]
    m = mod_ref[...]
    h = _norm_mod(_load_x(x_refs), nw_ref[...], m[0:1], m[1:2]).astype(BF16)
    for j in range(0, EV_MAIN, 1024):
        zm_ref[:, j:j + 1024] = _dot(h, wm_ref[:, j:j + 1024]).astype(BF16)
    for j, width in ((0, 1024), (EV_AL, LANES)):
        zg_ref[:, j:j + width] = _dot(h, wg_ref[:, j:j + width])


def _inproj_odd_kernel(x_ref, mod_ref, nw_ref, w_ref, cos_ref, sin_ref, z_ref):
    m = mod_ref[...]
    h = _norm_mod(x_ref[...], nw_ref[...], m[0:1], m[1:2]).astype(BF16)
    for j in range(0, OD_V, 2 * LANES):
        zz = _dot(h, w_ref[:, j:j + 2 * LANES])
        for s in range(2):
            blk = zz[:, s * LANES:(s + 1) * LANES]
            cs = cos_ref[:, s * LANES:(s + 1) * LANES]
            sn = sin_ref[:, s * LANES:(s + 1) * LANES]
            z_ref[:, j + s * LANES:j + (s + 1) * LANES] = (
                blk * cs + pltpu.roll(blk, ROPE_PAIRS, 1) * sn).astype(BF16)
    step = 1024
    for j in range(OD_V, OD_N, step):
        z_ref[:, j:j + step] = _dot(h, w_ref[:, j:j + step]).astype(BF16)


def _inproj_even(x_parts, mods_l, nw, wm_bf, wg_bf):
    return pl.pallas_call(
        functools.partial(_inproj_even_kernel, n_x=len(x_parts)),
        grid=(T_ALL // TM,),
        in_specs=_x_specs(x_parts) + [
                  pl.BlockSpec((None, 6, D), lambda i: (_cond_of_tile(i, TM), 0, 0)),
                  pl.BlockSpec((1, D), lambda i: (0, 0)),
                  pl.BlockSpec((D, EV_MAIN), lambda i: (0, 0)),
                  pl.BlockSpec((D, EV_GATE), lambda i: (0, 0))],
        out_specs=[pl.BlockSpec((TM, EV_MAIN), lambda i: (i, 0)),
                   pl.BlockSpec((TM, EV_GATE), lambda i: (i, 0))],
        out_shape=[jax.ShapeDtypeStruct((T_ALL, EV_MAIN), BF16),
                   jax.ShapeDtypeStruct((T_ALL, EV_GATE), F32)],
        compiler_params=_params(1), name="inproj_even",
    )(*x_parts, mods_l, nw, wm_bf, wg_bf)


def _rope_block(i):
    n_ctx = T_CTX // TM
    per_seq = SMP_L // TM
    return jnp.where(i < n_ctx, 0, 1 + (i - n_ctx) % per_seq)


def _inproj_odd(x, mods_l, nw, w_bf, cos_t, sin_t):
    return pl.pallas_call(
        _inproj_odd_kernel,
        grid=(T_ALL // TM,),
        in_specs=[pl.BlockSpec((TM, D), lambda i: (i, 0)),
                  pl.BlockSpec((None, 6, D), lambda i: (_cond_of_tile(i, TM), 0, 0)),
                  pl.BlockSpec((1, D), lambda i: (0, 0)),
                  pl.BlockSpec((D, OD_N), lambda i: (0, 0)),
                  pl.BlockSpec((TM, RET_DK), lambda i: (_rope_block(i), 0)),
                  pl.BlockSpec((TM, RET_DK), lambda i: (_rope_block(i), 0))],
        out_specs=pl.BlockSpec((TM, OD_N), lambda i: (i, 0)),
        out_shape=jax.ShapeDtypeStruct((T_ALL, OD_N), BF16),
        compiler_params=_params(1), name="inproj_odd",
    )(x, mods_l, nw, w_bf, cos_t, sin_t)


def _scan_work():
    tf, tb, first, last, seq = [], [], [], [], []
    base = 0
    for s, length in enumerate([CTX_L] * CTX_B + [SMP_L] * SMP_B):
        n = length // TL
        for t in range(n):
            tf.append(base + t)
            tb.append(base + n - 1 - t)
            first.append(int(t == 0))
            last.append(int(t == n - 1))
            seq.append(s)
        base += n
    return tuple(np.asarray(a, np.int32) for a in (tf, tb, first, last, seq))


N_WORK = T_ALL // TL


def _decay_factors(q, k, b, fwd):
    b_last = b[CHUNK - 1:CHUNK] if fwd else b[0:1]
    b_mid = b[CHUNK // 2:CHUNK // 2 + 1]
    qi = q * jnp.exp(b - b_mid)
    ki = k * jnp.exp(b_mid - b)
    q_in = qi * jnp.exp(b_mid)
    kk = ki * jnp.exp(b_last - b_mid)
    return qi, ki, q_in, kk, jnp.exp(b_last)


def _stack(a, b):
    return jnp.concatenate([a, b], axis=0)


def _gla_pair_chunk(q, k, b, v0, v1, st, masks, bd_causal, fwd):
    qi, ki, q_in, kk, g = _decay_factors(q, k, b, fwd)
    m0, m1 = masks
    split = lambda t: _stack(jnp.where(m0, t, 0.0), jnp.where(m1, t, 0.0)).astype(BF16)
    ki_bf = ki.astype(BF16)
    a = jnp.where(bd_causal, _dot_nt(split(qi), _stack(ki_bf, ki_bf)), 0.0)
    vs = _stack(v0, v1)
    o = _dot(a.astype(BF16), vs) + _dot_nt(split(q_in), st.astype(BF16))
    return o, st * g + _dot_tn(vs, split(kk))


def _hgrn_pair_chunk(q0, k0, b0, v0, q1, k1, b1, v1, st, bd_causal, fwd):
    qi0, ki0, qin0, kk0, g0 = _decay_factors(q0, k0, b0, fwd)
    qi1, ki1, qin1, kk1, g1 = _decay_factors(q1, k1, b1, fwd)
    zero = jnp.zeros((CHUNK, LANES), F32)
    wide = lambda t0, t1: _stack(jnp.concatenate([t0, zero], axis=1),
                                 jnp.concatenate([zero, t1], axis=1)).astype(BF16)
    a = jnp.where(bd_causal, _dot_nt(_stack(qi0, qi1).astype(BF16), _stack(ki0, ki1).astype(BF16)), 0.0)
    vs = _stack(v0, v1)
    o = _dot(a.astype(BF16), vs) + _dot_nt(wide(qin0, qin1), st.astype(BF16))
    return o, st * jnp.concatenate([g0, g1], axis=1) + _dot_tn(vs, wide(kk0, kk1))


def _even_gates(d, fg, al, qb, tri, w2_ref, ba_ref, lbv):
    y = _dot(al[...].astype(BF16), w2_ref[d]) + ba_ref[d]
    la_a = (jnp.minimum(y, 0.0) - jnp.log(1.0 + jnp.exp(-jnp.abs(y)))) * (1.0 / GLA_TAU)
    f = fg[...]
    t = jnp.exp(-jnp.abs(f))
    rcp = 1.0 / (1.0 + t)
    sig_f = jnp.where(f >= 0.0, rcp, t * rcp)
    sig_nf = jnp.where(f >= 0.0, t * rcp, rcp)
    la_h = jnp.log(lbv + (1.0 - lbv) * sig_f)
    key_h = (1.0 - lbv) * sig_nf
    la_hi, la_lo = _split_bf16(jnp.concatenate([la_a, la_h], axis=1))
    b_all = _dot(tri, la_hi) + _dot(tri, la_lo)
    return b_all, key_h, _silu(qb[...].astype(F32))


def _even_tile(dir_refs, w2_ref, ba_ref, lb_ref, stg, sth):
    rc = lax.broadcasted_iota(jnp.int32, (2 * CHUNK, 2 * CHUNK), 0)
    cc = lax.broadcasted_iota(jnp.int32, (2 * CHUNK, 2 * CHUNK), 1)
    same_head = (rc // CHUNK) == (cc // CHUNK)
    bd_causal = [same_head & (cc <= rc), same_head & (cc >= rc)]
    rt = lax.broadcasted_iota(jnp.int32, (TL, TL), 0)
    ct = lax.broadcasted_iota(jnp.int32, (TL, TL), 1)
    same = (rt // CHUNK) == (ct // CHUNK)
    tri = [jnp.where(same & (ct <= rt), 1.0, 0.0).astype(BF16),
           jnp.where(same & (ct >= rt), 1.0, 0.0).astype(BF16)]
    lane = lax.broadcasted_iota(jnp.int32, (CHUNK, LANES), 1)
    pair_masks = [lane < GLA_DK, lane >= GLA_DK]
    lbv = lb_ref[...]
    n_pairs = GLA_HEADS // 2
    st_g = [[stg[d, p] for p in range(n_pairs)] for d in range(2)]
    st_h = [[sth[d, p] for p in range(HG_HEADS // 2)] for d in range(2)]
    gates = [_even_gates(d, dir_refs[d][3], dir_refs[d][5], dir_refs[d][2], tri[d], w2_ref, ba_ref, lbv)
             for d in range(2)]

    n_chunks = TL // CHUNK
    for ci in range(n_chunks):
        for d in range(2):
            qk, va, _, _, ib, _, o_ref = dir_refs[d]
            b_all, key_h, qh = gates[d]
            fwd = d == 0
            cpos = ci if fwd else n_chunks - 1 - ci
            rows = slice(cpos * CHUNK, (cpos + 1) * CHUNK)
            for p in range(n_pairs):
                ln = slice(p * LANES, (p + 1) * LANES)
                kl = slice(GLA_QK + p * LANES, GLA_QK + (p + 1) * LANES)
                c0, c1 = 2 * p * GLA_DV, (2 * p + 1) * GLA_DV
                o, st_g[d][p] = _gla_pair_chunk(
                    qk[rows, ln].astype(F32) * (GLA_DK ** -0.5), qk[rows, kl].astype(F32), b_all[rows, ln],
                    va[rows, c0:c0 + GLA_DV], va[rows, c1:c1 + GLA_DV], st_g[d][p],
                    pair_masks, bd_causal[d], fwd)
                o_ref[rows, c0:c0 + GLA_DV] = o[0:CHUNK].astype(BF16)
                o_ref[rows, c1:c1 + GLA_DV] = o[CHUNK:2 * CHUNK].astype(BF16)
            for p in range(HG_HEADS // 2):
                l0 = slice(2 * p * LANES, (2 * p + 1) * LANES)
                l1 = slice((2 * p + 1) * LANES, (2 * p + 2) * LANES)
                bl0 = slice(GLA_QK + l0.start, GLA_QK + l0.stop)
                bl1 = slice(GLA_QK + l1.start, GLA_QK + l1.stop)
                o, st_h[d][p] = _hgrn_pair_chunk(
                    qh[rows, l0], key_h[rows, l0], b_all[rows, bl0], ib[rows, l0],
                    qh[rows, l1], key_h[rows, l1], b_all[rows, bl1], ib[rows, l1],
                    st_h[d][p], bd_causal[d], fwd)
                o_ref[rows, GLA_V + l0.start:GLA_V + l0.stop] = o[0:CHUNK].astype(BF16)
                o_ref[rows, GLA_V + l1.start:GLA_V + l1.stop] = o[CHUNK:2 * CHUNK].astype(BF16)
    for d in range(2):
        for p in range(n_pairs):
            stg[d, p] = st_g[d][p]
        for p in range(HG_HEADS // 2):
            sth[d, p] = st_h[d][p]


def _even_scan_kernel(tf_ref, tb_ref, first_ref, last_ref, seq_ref,
                      qk_f, va_f, qb_f, fg_f, ib_f, al_f,
                      qk_b, va_b, qb_b, fg_b, ib_b, al_b,
                      w2_ref, ba_ref, lb_ref, sg_ref, sh_ref, *rest, n_prev):
    pg_ref, ph_ref = rest[:2] if n_prev else (None, None)
    of_ref, ob_ref, stg_ref, sth_ref, stg, sth = rest[-6:]
    i = pl.program_id(0)
    is_first = first_ref[i] == 1
    is_last = last_ref[i] == 1
    is_ctx = seq_ref[i] < CTX_B

    @pl.when(jnp.logical_and(is_first, is_ctx))
    def _():
        stg[...] = jnp.zeros_like(stg)
        sth[...] = jnp.zeros_like(sth)

    @pl.when(jnp.logical_and(is_first, jnp.logical_not(is_ctx)))
    def _():
        for d in range(2):
            for p in range(GLA_HEADS // 2):
                stg[d, p] = sg_ref[d, p].T
            for h in range(HG_HEADS):
                sth[d, h // 2, :, (h % 2) * HG_DK:(h % 2 + 1) * HG_DK] = sh_ref[d, h].T

    _even_tile([(qk_f, va_f, qb_f, fg_f, ib_f, al_f, of_ref), (qk_b, va_b, qb_b, fg_b, ib_b, al_b, ob_ref)],
               w2_ref, ba_ref, lb_ref, stg, sth)

    @pl.when(jnp.logical_and(is_last, is_ctx))
    def _():
        for k in range(n_prev):
            stg_ref[k] = pg_ref[k]
            sth_ref[k] = ph_ref[k]
        for d in range(2):
            for p in range(GLA_HEADS // 2):
                stg_ref[n_prev, d, p] = stg[d, p].T
            for h in range(HG_HEADS):
                sth_ref[n_prev, d, h] = sth[d, h // 2, :, (h % 2) * HG_DK:(h % 2 + 1) * HG_DK].T


def _even_scan(zm, zg, w2_pad, b_a, lb, s_gla, s_hgrn, prev):
    n_prev = 0 if prev is None else prev[0].shape[1]
    work = _scan_work()

    def zspec(width, col_block, which):
        return pl.BlockSpec((TL, width), lambda i, tf, tb, fi, la, sq: ((tf, tb)[which][i], col_block))

    def dir_specs(which):
        return [zspec(512, EV_QA // 512, which), zspec(512, EV_VA // 512, which),
                zspec(512, EV_QB // 512, which), zspec(512, (EV_FF, EV_FB)[which] // 512, which),
                zspec(512, EV_IB // 512, which), zspec(LANES, EV_AL // LANES, which)]

    def dir_args():
        return [zm, zm, zm, zg, zm, zg]

    def const(shape):
        return pl.BlockSpec(shape, lambda i, *_: (0,) * len(shape))

    def s_in(shape):
        return pl.BlockSpec((None, None) + shape,
                            lambda i, tf, tb, fi, la, sq:
                            (jnp.clip(sq[i] - CTX_B, 0, SMP_B - 1), n_prev) + (0,) * len(shape))

    def ctx_states(n_layers, shape):
        return pl.BlockSpec((None, n_layers) + shape,
                            lambda i, tf, tb, fi, la, sq: (jnp.minimum(sq[i], CTX_B - 1), 0) + (0,) * len(shape))

    gshape = (2, GLA_HEADS // 2, LANES, LANES)
    hshape = (2, HG_HEADS, HG_DK, HG_DV)
    grid_spec = pltpu.PrefetchScalarGridSpec(
        num_scalar_prefetch=5, grid=(N_WORK,),
        in_specs=dir_specs(0) + dir_specs(1) + [
            const((2, LANES, GLA_QK)), const((2, 1, GLA_QK)), const((1, HG_K)),
            s_in(gshape), s_in(hshape)]
        + ([ctx_states(n_prev, gshape), ctx_states(n_prev, hshape)] if n_prev else []),
        out_specs=[pl.BlockSpec((TL, D), lambda i, tf, tb, fi, la, sq: (tf[i], 0)),
                   pl.BlockSpec((TL, D), lambda i, tf, tb, fi, la, sq: (tb[i], 0)),
                   ctx_states(n_prev + 1, gshape), ctx_states(n_prev + 1, hshape)],
        scratch_shapes=[pltpu.VMEM(gshape, F32),
                        pltpu.VMEM((2, HG_HEADS // 2, HG_DV, 2 * HG_DK), F32)])
    return pl.pallas_call(
        functools.partial(_even_scan_kernel, n_prev=n_prev), grid_spec=grid_spec,
        out_shape=[jax.ShapeDtypeStruct((T_ALL, D), BF16), jax.ShapeDtypeStruct((T_ALL, D), BF16),
                   jax.ShapeDtypeStruct((CTX_B, n_prev + 1) + gshape, F32),
                   jax.ShapeDtypeStruct((CTX_B, n_prev + 1) + hshape, F32)],
        compiler_params=_params(1), name="even_scan",
    )(*work, *dir_args(), *dir_args(), w2_pad, b_a, lb, s_gla, s_hgrn, *(prev if n_prev else []))


RET_HP = 2


def _ret_tables():
    lg = np.log1p(-np.exp2(-5.0 - np.arange(RET_HEADS, dtype=np.float64)))
    pos = np.arange(TL, dtype=np.float64)
    dm = np.zeros((2, RET_HEADS, TL, TL))
    rq = np.zeros((2, RET_HEADS, TL, 1))
    rk = np.zeros((2, RET_HEADS, TL, 1))
    gc = np.zeros((2, RET_HEADS, 1, RET_DV))
    diff = pos[:, None] - pos[None, :]
    kscale = RET_DK ** -0.5
    for h in range(RET_HEADS):
        dm[0, h] = np.where(diff >= 0, np.exp(lg[h] * np.maximum(diff, 0)), 0.0) * kscale
        rq[0, h, :, 0] = np.exp(lg[h] * (pos + 1))
        rk[0, h, :, 0] = np.exp(lg[h] * (TL - 1 - pos)) * kscale
        gc[0, h] = np.exp(lg[h] * TL)
        lb = lg[RET_HEADS - 1 - h]
        dm[1, h] = np.where(diff <= 0, np.exp(lb * np.maximum(-diff, 0)), 0.0) * kscale
        rq[1, h, :, 0] = np.exp(lb * (TL - pos))
        rk[1, h, :, 0] = np.exp(lb * pos) * kscale
        gc[1, h] = np.exp(lb * TL)
    return tuple(jnp.asarray(a, F32) for a in (dm, rq, rk, gc))


def _odd_dir(d, q_ref, k_ref, v_ref, dm_ref, rq_ref, rk_ref, gc_ref, st, o_ref, zero_state):
    for hh in range(RET_HP):
        k = k_ref[:, hh * RET_DK:(hh + 1) * RET_DK]
        q = q_ref[:, hh * RET_DK:(hh + 1) * RET_DK]
        v = v_ref[:, hh * RET_DV:(hh + 1) * RET_DV]
        a = _dot_nt(q, k) * dm_ref[d, hh]
        o = _dot(a.astype(BF16), v)
        kk = (k.astype(F32) * rk_ref[d, hh]).astype(BF16)
        s_new = _dot_tn(kk, v)
        if not zero_state:
            s = st[d, hh]
            o = o + rq_ref[d, hh] * _dot(q, s.astype(BF16))
            s_new = gc_ref[d, hh] * s + s_new
        st[d, hh] = s_new
        o_ref[:, hh * RET_DV:(hh + 1) * RET_DV] = o.astype(BF16)


def _odd_scan_kernel(tf_ref, tb_ref, first_ref, last_ref, seq_ref,
                     q_f, k_f, v_f, q_b, k_b, v_b,
                     dm_ref, rq_ref, rk_ref, gc_ref, s0_ref, *rest, n_prev):
    prev_ref = rest[0] if n_prev else None
    of_ref, ob_ref, sout_ref, st = rest[-4:]
    i = pl.program_id(1)
    is_first = first_ref[i] == 1
    is_last = last_ref[i] == 1
    is_ctx = seq_ref[i] < CTX_B

    from_zero = jnp.logical_and(is_first, is_ctx)

    @pl.when(jnp.logical_and(is_first, jnp.logical_not(is_ctx)))
    def _():
        st[...] = s0_ref[...]

    for zero_state in (True, False):
        @pl.when(from_zero if zero_state else jnp.logical_not(from_zero))
        def _():
            _odd_dir(0, q_f, k_f, v_f, dm_ref, rq_ref, rk_ref, gc_ref, st, of_ref, zero_state)
            _odd_dir(1, q_b, k_b, v_b, dm_ref, rq_ref, rk_ref, gc_ref, st, ob_ref, zero_state)

    @pl.when(jnp.logical_and(is_last, is_ctx))
    def _():
        for k in range(n_prev):
            sout_ref[k] = prev_ref[k]
        sout_ref[n_prev] = st[...]


def _odd_scan(z, tables, s_ret, prev):
    n_prev = 0 if prev is None else prev.shape[1]
    work = _scan_work()
    dm, rq, rk, gc = tables
    qw, vw = RET_HP * RET_DK, RET_HP * RET_DV

    def zspec(width, col0, which):
        return pl.BlockSpec((TL, width),
                            lambda hp, i, tf, tb, fi, la, sq: ((tf, tb)[which][i], col0 // width + hp))

    def dir_specs(which):
        return [zspec(qw, OD_Q, which), zspec(qw, OD_K, which), zspec(vw, OD_V, which)]

    def table(shape):
        return pl.BlockSpec((2, RET_HP) + shape, lambda hp, i, *_: (0, hp) + (0,) * len(shape))

    sshape = (2, RET_HP, RET_DK, RET_DV)

    def ctx_states(n_layers):
        return pl.BlockSpec((None, n_layers) + sshape,
                            lambda hp, i, tf, tb, fi, la, sq: (jnp.minimum(sq[i], CTX_B - 1), 0, 0, hp, 0, 0))

    grid_spec = pltpu.PrefetchScalarGridSpec(
        num_scalar_prefetch=5, grid=(RET_HEADS // RET_HP, N_WORK),
        in_specs=dir_specs(0) + dir_specs(1) + [
            table((TL, TL)), table((TL, 1)), table((TL, 1)), table((1, RET_DV)),
            pl.BlockSpec((None, None) + sshape,
                         lambda hp, i, tf, tb, fi, la, sq:
                         (jnp.clip(sq[i] - CTX_B, 0, SMP_B - 1), n_prev, 0, hp, 0, 0))]
        + ([ctx_states(n_prev)] if n_prev else []),
        out_specs=[pl.BlockSpec((TL, vw), lambda hp, i, tf, tb, fi, la, sq: (tf[i], hp)),
                   pl.BlockSpec((TL, vw), lambda hp, i, tf, tb, fi, la, sq: (tb[i], hp)),
                   ctx_states(n_prev + 1)],
        scratch_shapes=[pltpu.VMEM(sshape, F32)])
    return pl.pallas_call(
        functools.partial(_odd_scan_kernel, n_prev=n_prev), grid_spec=grid_spec,
        out_shape=[jax.ShapeDtypeStruct((T_ALL, RET_V), BF16), jax.ShapeDtypeStruct((T_ALL, RET_V), BF16),
                   jax.ShapeDtypeStruct((CTX_B, n_prev + 1, 2, RET_HEADS, RET_DK, RET_DV), F32)],
        compiler_params=_params(2), name="odd_scan",
    )(*work, z, z, z, z, z, z, dm, rq, rk, gc, s_ret, *([prev] if n_prev else []))


def _head_rms(o, width):
    parts = []
    for j in range(0, o.shape[1], width):
        blk = o[:, j:j + width]
        parts.append(blk * lax.rsqrt(jnp.mean(blk * blk, axis=-1, keepdims=True) + NORM_EPS))
    return parts


def _route(h2, wr_hi_ref, wr_lo_ref, br_ref):
    h_hi, h_lo = _split_bf16(h2)
    logits = (_dot(h_hi, wr_hi_ref[...]) + _dot(h_hi, wr_lo_ref[...]) + _dot(h_lo, wr_hi_ref[...])
              + br_ref[...])
    lane_i = lax.broadcasted_iota(jnp.int32, logits.shape, 1)
    lane = lane_i.astype(F32)
    neg = -jnp.inf
    big = 1e9
    is_grp = (lane_i >= N_EXPERTS) & (lane_i < N_EXPERTS + N_GROUPS)
    lg = jnp.where(is_grp, logits, neg)
    mg = jnp.max(lg, axis=-1, keepdims=True)
    gsel = jnp.min(jnp.where(lg == mg, lane - N_EXPERTS, big), axis=-1, keepdims=True)
    pg = 1.0 / jnp.sum(jnp.where(is_grp, jnp.exp(lg - mg), 0.0), axis=-1, keepdims=True)
    in_grp = (lane_i < N_EXPERTS) & ((lane_i // EXP_PER_GROUP).astype(F32) == gsel)
    le = jnp.where(in_grp, logits, neg)
    m1 = jnp.max(le, axis=-1, keepdims=True)
    i1 = jnp.min(jnp.where(le == m1, lane, big), axis=-1, keepdims=True)
    le2 = jnp.where(lane == i1, neg, le)
    m2 = jnp.max(le2, axis=-1, keepdims=True)
    i2 = jnp.min(jnp.where(le2 == m2, lane, big), axis=-1, keepdims=True)
    e2 = jnp.exp(m2 - m1)
    w1 = pg / (1.0 + e2)
    w2 = pg * e2 / (1.0 + e2)
    first_lo = i1 < i2
    ia = jnp.minimum(i1, i2) - gsel * EXP_PER_GROUP
    ib = jnp.maximum(i1, i2) - gsel * EXP_PER_GROUP
    pair = gsel * PAIRS_PER_GROUP + ia * (2 * EXP_PER_GROUP - 1 - ia) * 0.5 + (ib - ia - 1.0)
    return pair, jnp.where(first_lo, w1, w2), jnp.where(first_lo, w2, w1)


def _post_tail(i, x, out, m, nw2_ref, wr_hi_ref, wr_lo_ref, br_ref,
               x1_ref, hrow_ref, meta_ref, cnt_ref, carry):
    x1 = x + m[2:3] * out
    x1_ref[...] = x1
    h2 = _norm_mod(x1, nw2_ref[...], m[3:4], m[4:5])
    pair, w_a, w_b = _route(h2, wr_hi_ref, wr_lo_ref, br_ref)

    @pl.when(i == 0)
    def _():
        carry[...] = jnp.zeros_like(carry)

    lane = lax.broadcasted_iota(jnp.int32, (TM, LANES), 1)
    onehot = jnp.where(lane.astype(F32) == pair, 1.0, 0.0)
    r = lax.broadcasted_iota(jnp.int32, (TM, TM), 0)
    c = lax.broadcasted_iota(jnp.int32, (TM, TM), 1)
    earlier = jnp.where(c < r, 1.0, 0.0).astype(BF16)
    before = _dot(earlier, onehot.astype(BF16)) + carry[...]
    rank = jnp.sum(onehot * before, axis=-1, keepdims=True)
    carry[...] += jnp.sum(onehot, axis=0, keepdims=True)
    cnt_ref[...] = carry[...]
    meta_ref[...] = jnp.where(lane == 0, pair, jnp.where(lane == 1, rank, 0.0))
    hrow_ref[:, 0:D] = h2
    hrow_ref[:, D:ROW_W] = jnp.where(lane == 0, w_a, jnp.where(lane == 1, w_b, 0.0))


def _post_even_kernel(*refs, n_x):
    x_refs = refs[:n_x]
    (of_ref, ob_ref, ra_ref, gb_ref, mod_ref, gn_ref, wo_ref, nw2_ref, wr_hi_ref, wr_lo_ref, br_ref,
     x1_ref, hrow_ref, meta_ref, cnt_ref, carry) = refs[n_x:]
    o = of_ref[...].astype(F32) + ob_ref[...].astype(F32)
    normed = jnp.concatenate(_head_rms(o, GLA_DV), axis=1)
    gate = jnp.concatenate([_silu(ra_ref[...].astype(F32)), _silu(gb_ref[...].astype(F32))], axis=1)
    mixed = (normed * gn_ref[...] * gate).astype(BF16)
    out = _dot(mixed, wo_ref[...])
    _post_tail(pl.program_id(0), _load_x(x_refs), out, mod_ref[...], nw2_ref, wr_hi_ref, wr_lo_ref, br_ref,
               x1_ref, hrow_ref, meta_ref, cnt_ref, carry)


def _post_odd_kernel(of_ref, ob_ref, g_ref, x_ref, mod_ref, wo_ref,
                     nw2_ref, wr_hi_ref, wr_lo_ref, br_ref,
                     x1_ref, hrow_ref, meta_ref, cnt_ref, carry):
    o = of_ref[...].astype(F32) + ob_ref[...].astype(F32)
    normed = jnp.concatenate(_head_rms(o, RET_DV), axis=1)
    mixed = (normed * _silu(g_ref[...].astype(F32))).astype(BF16)
    out = _dot(mixed, wo_ref[...])
    _post_tail(pl.program_id(0), x_ref[...], out, mod_ref[...], nw2_ref, wr_hi_ref, wr_lo_ref, br_ref,
               x1_ref, hrow_ref, meta_ref, cnt_ref, carry)


def _post_common_specs():
    tile = lambda w: pl.BlockSpec((TM, w), lambda i: (i, 0))
    const = lambda s: pl.BlockSpec(s, lambda i: (0,) * len(s))
    mod = pl.BlockSpec((None, 6, D), lambda i: (_cond_of_tile(i, TM), 0, 0))
    out_specs = [tile(D), tile(ROW_W), tile(LANES), const((1, LANES))]
    out_shape = [jax.ShapeDtypeStruct((T_ALL, D), F32),
                 jax.ShapeDtypeStruct((T_ALL, ROW_W), F32),
                 jax.ShapeDtypeStruct((T_ALL, LANES), F32),
                 jax.ShapeDtypeStruct((1, LANES), F32)]
    scratch = [pltpu.VMEM((1, LANES), F32)]
    return tile, const, mod, out_specs, out_shape, scratch


def _post_even(o_f, o_b, z, x_parts, mods_l, gn, wo_bf, nw2, wr_hi, wr_lo, br):
    tile, const, mod, out_specs, out_shape, scratch = _post_common_specs()
    zcol = lambda c0: pl.BlockSpec((TM, 512), lambda i: (i, c0 // 512))
    return pl.pallas_call(
        functools.partial(_post_even_kernel, n_x=len(x_parts)), grid=(T_ALL // TM,),
        in_specs=_x_specs(x_parts) + [
                  tile(D), tile(D), zcol(EV_RA), zcol(EV_GB), mod, const((1, D)),
                  const((D, D)), const((1, D)), const((D, ROUTER_N)), const((D, ROUTER_N)),
                  const((1, ROUTER_N))],
        out_specs=out_specs, out_shape=out_shape, scratch_shapes=scratch,
        compiler_params=_params(1), name="post_even",
    )(*x_parts, o_f, o_b, z, z, mods_l, gn, wo_bf, nw2, wr_hi, wr_lo, br)


def _post_odd(o_f, o_b, z, x, mods_l, wo_bf, nw2, wr_hi, wr_lo, br):
    tile, const, mod, out_specs, out_shape, scratch = _post_common_specs()
    return pl.pallas_call(
        _post_odd_kernel, grid=(T_ALL // TM,),
        in_specs=[tile(RET_V), tile(RET_V),
                  pl.BlockSpec((TM, RET_V), lambda i: (i, OD_G // RET_V)), tile(D), mod,
                  const((RET_V, D)), const((1, D)), const((D, ROUTER_N)), const((D, ROUTER_N)),
                  const((1, ROUTER_N))],
        out_specs=out_specs, out_shape=out_shape, scratch_shapes=scratch,
        compiler_params=_params(1), name="post_odd",
    )(o_f, o_b, z, x, mods_l, wo_bf, nw2, wr_hi, wr_lo, br)


def _pair_tables():
    ea, eb = [], []
    for g in range(N_GROUPS):
        for a in range(EXP_PER_GROUP):
            for b in range(a + 1, EXP_PER_GROUP):
                ea.append(g * EXP_PER_GROUP + a)
                eb.append(g * EXP_PER_GROUP + b)
    pad = LANES - len(ea)
    return (jnp.asarray(ea + [0] * pad, jnp.int32), jnp.asarray(eb + [0] * pad, jnp.int32))


def _dispatch_plan(meta, counts):
    pair = meta[:, 0].astype(jnp.int32)
    rank = meta[:, 1].astype(jnp.int32)
    cnt = counts[0].astype(jnp.int32)
    tiles_g = (cnt + TMG - 1) // TMG
    tile_end = jnp.cumsum(tiles_g)
    tile_start = tile_end - tiles_g
    slot0 = tile_start * TMG
    ids = jnp.arange(LANES, dtype=jnp.int32)
    dest = rank + jnp.sum(jnp.where(pair[:, None] == ids[None, :], slot0[None, :], 0), axis=1)
    n_real = tile_end[-1]
    j = jnp.arange(N_TILES_MOE, dtype=jnp.int32)
    jj = jnp.minimum(j, n_real - 1)
    grp = jnp.sum((jj[:, None] >= tile_end[None, :]).astype(jnp.int32), axis=1)
    ea_t, eb_t = _pair_tables()
    onehot_g = grp[:, None] == ids[None, :]
    pick = lambda v: jnp.sum(jnp.where(onehot_g, v[None, :], 0), axis=1)
    nv = jnp.clip(pick(cnt) - (jj - pick(tile_start)) * TMG, 0, TMG)
    nv = jnp.where(j < n_real, nv, 0)
    return dest // SUBLANES, dest % SUBLANES, pick(ea_t), pick(eb_t), nv, n_real.reshape(1)


DISPATCH_STEPS = T_ALL // DISPATCH_ROWS


def _dispatch_kernel(dhi_ref, dlo_ref, nv_ref, h_ref, hs_hbm, zeros, sem, pad_sem):
    j = pl.program_id(0)
    base = j * DISPATCH_ROWS
    tile_sub = TMG // SUBLANES

    def pad_tile(tile):
        return pltpu.make_async_copy(zeros, hs_hbm.at[pl.ds(tile * tile_sub, tile_sub)], pad_sem.at[0])

    @pl.when(j == 0)
    def _():
        zeros[...] = jnp.zeros_like(zeros)

        def fill(tile, carry):
            @pl.when(nv_ref[tile] < TMG)
            def _():
                pad_tile(tile).start()
            return carry

        def drain(tile, carry):
            @pl.when(nv_ref[tile] < TMG)
            def _():
                pad_tile(tile).wait()
            return carry

        lax.fori_loop(0, N_TILES_MOE, fill, 0)
        lax.fori_loop(0, N_TILES_MOE, drain, 0)

    def issue(g, carry):
        for k in range(SUBLANES):
            t = base + g * SUBLANES + k
            pltpu.make_async_copy(h_ref.at[g, pl.ds(k, 1)], hs_hbm.at[dhi_ref[t], pl.ds(dlo_ref[t], 1)],
                                  sem.at[0]).start(priority=k % 2)
        return carry

    lax.fori_loop(0, DISPATCH_ROWS // SUBLANES, issue, 0)
    pltpu.make_async_copy(h_ref, hs_hbm.at[pl.ds(0, DISPATCH_ROWS // SUBLANES)], sem.at[0]).wait()


def _dispatch(dest_hi, dest_lo, nv, hrow):
    grid_spec = pltpu.PrefetchScalarGridSpec(
        num_scalar_prefetch=3, grid=(DISPATCH_STEPS,),
        in_specs=[pl.BlockSpec((DISPATCH_ROWS // SUBLANES, SUBLANES, ROW_W), lambda j, *_: (j, 0, 0))],
        out_specs=pl.BlockSpec(memory_space=pl.ANY),
        scratch_shapes=[pltpu.VMEM((TMG // SUBLANES, SUBLANES, ROW_W), F32),
                        pltpu.SemaphoreType.DMA((1,)), pltpu.SemaphoreType.DMA((1,))])
    hs = pl.pallas_call(
        _dispatch_kernel, grid_spec=grid_spec,
        out_shape=jax.ShapeDtypeStruct((N_TILES_MOE * TMG // SUBLANES, SUBLANES, ROW_W), F32),
        compiler_params=_params(1), name="moe_dispatch",
    )(dest_hi, dest_lo, nv, hrow.reshape(T_ALL // SUBLANES, SUBLANES, ROW_W))
    return hs.reshape(N_TILES_MOE * TMG, ROW_W)


def _moe_kernel(ea_ref, eb_ref, nreal_ref, hs_ref,
                wga_ref, wua_ref, wda_ref, wgb_ref, wub_ref, wdb_ref, ys_ref):
    j = pl.program_id(0)

    @pl.when(j < nreal_ref[0])
    def _():
        h = hs_ref[:, 0:D].astype(BF16)
        wrow = hs_ref[:, D:ROW_W]
        w_a, w_b = wrow[:, 0:1], wrow[:, 1:2]
        act_a = _silu(_dot(h, wga_ref[0])) * _dot(h, wua_ref[0]) * w_a
        act_b = _silu(_dot(h, wgb_ref[0])) * _dot(h, wub_ref[0]) * w_b
        ys_ref[...] = _dot(act_a.astype(BF16), wda_ref[0]) + _dot(act_b.astype(BF16), wdb_ref[0])

    @pl.when(j >= nreal_ref[0])
    def _():
        ys_ref[...] = jnp.zeros_like(ys_ref)


def _moe(ea, eb, n_real, hs, wg_bf, wu_bf, wd_bf, layer):
    def wspec(shape, which):
        return pl.BlockSpec((None, 1) + shape, lambda j, ea, eb, nr: (layer, (ea, eb)[which][j], 0, 0))

    up, down = (D, D_EXPERT), (D_EXPERT, D)
    grid_spec = pltpu.PrefetchScalarGridSpec(
        num_scalar_prefetch=3, grid=(N_TILES_MOE,),
        in_specs=[pl.BlockSpec((TMG, ROW_W), lambda j, ea, eb, nr: (jnp.minimum(j, nr[0] - 1), 0)),
                  wspec(up, 0), wspec(up, 0), wspec(down, 0),
                  wspec(up, 1), wspec(up, 1), wspec(down, 1)],
        out_specs=pl.BlockSpec((TMG, D), lambda j, ea, eb, nr: (j, 0)))
    return pl.pallas_call(
        _moe_kernel, grid_spec=grid_spec,
        out_shape=jax.ShapeDtypeStruct((N_TILES_MOE * TMG, D), F32),
        compiler_params=_params(1), name="moe",
    )(ea, eb, n_real, hs, wg_bf, wu_bf, wd_bf, wg_bf, wu_bf, wd_bf)


def _combine_kernel(dhi_ref, dlo_ref, ys_hbm, x1_ref, mod_ref, nwf_ref, o_ref, buf, sem, *, tile0, final):
    j = pl.program_id(0)
    n = pl.num_programs(0)

    def gather(tile, s):
        def issue(g, carry):
            for k in range(SUBLANES):
                t = (tile0 + tile) * TM + g * SUBLANES + k
                pltpu.make_async_copy(ys_hbm.at[dhi_ref[t], pl.ds(dlo_ref[t], 1)],
                                      buf.at[s, g, pl.ds(k, 1)], sem.at[s]).start(priority=k % 2)
            return carry
        lax.fori_loop(0, TM // SUBLANES, issue, 0)

    @pl.when(j == 0)
    def _():
        gather(0, 0)

    for s in range(2):
        @pl.when(jnp.logical_and(j + 1 < n, (j + 1) % 2 == s))
        def _():
            gather(j + 1, s)

    for s in range(2):
        @pl.when(j % 2 == s)
        def _():
            pltpu.make_async_copy(ys_hbm.at[pl.ds(0, TM // SUBLANES)], buf.at[s], sem.at[s]).wait()
            x = x1_ref[...] + mod_ref[5:6] * buf[s].reshape(TM, D)
            if final:
                var = jnp.mean(x * x, axis=-1, keepdims=True)
                x = x * lax.rsqrt(var + NORM_EPS) * nwf_ref[...]
            o_ref[...] = x


def _combine(dest_hi, dest_lo, ys, x1, mods_l, nwf, tile0=0, n_tok=T_ALL, final=False):
    grid_spec = pltpu.PrefetchScalarGridSpec(
        num_scalar_prefetch=2, grid=(n_tok // TM,),
        in_specs=[pl.BlockSpec(memory_space=pl.ANY),
                  pl.BlockSpec((TM, D), lambda i, *_: (tile0 + i, 0)),
                  pl.BlockSpec((None, 6, D), lambda i, *_: (_cond_of_tile(tile0 + i, TM), 0, 0)),
                  pl.BlockSpec((1, D), lambda i, *_: (0, 0))],
        out_specs=pl.BlockSpec((TM, D), lambda i, *_: (i, 0)),
        scratch_shapes=[pltpu.VMEM((2, TM // SUBLANES, SUBLANES, D), F32), pltpu.SemaphoreType.DMA((2,))])
    return pl.pallas_call(
        functools.partial(_combine_kernel, tile0=tile0, final=final), grid_spec=grid_spec,
        out_shape=jax.ShapeDtypeStruct((n_tok, D), F32),
        compiler_params=_params(1), name="moe_combine",
    )(dest_hi, dest_lo, ys.reshape(N_TILES_MOE * TMG // SUBLANES, SUBLANES, D), x1, mods_l, nwf)


def _rope_tables():
    freqs = ROPE_BASE ** (-jnp.arange(ROPE_PAIRS, dtype=F32) / ROPE_PAIRS)
    t = jnp.arange(SMP_L)
    halves_c, halves_s = [], []
    for p in (t // GRID_W, t % GRID_W):
        ang = p.astype(F32)[:, None] * freqs
        cs, sn = jnp.cos(ang), jnp.sin(ang)
        halves_c += [cs, cs]
        halves_s += [-sn, sn]
    cos_t = jnp.concatenate(halves_c, axis=1)
    sin_t = jnp.concatenate(halves_s, axis=1)
    cos_t = jnp.concatenate([jnp.ones((TM, RET_DK), F32), cos_t], axis=0)
    sin_t = jnp.concatenate([jnp.zeros((TM, RET_DK), F32), sin_t], axis=0)
    return cos_t, sin_t


def kernel(x_prompt, x_sample, state_gla, state_hgrn, state_ret, c, c_ctx, norm1_w, norm2_w, normf_w,
           w_mod, b_mod, w_in_even, gla_w_alpha, gla_b_alpha, hgrn_lb_logits, gla_norm_w, hgrn_norm_w,
           w_out_even, w_in_odd, w_out_odd, router_g_w, router_g_b, router_e_w, router_e_b,
           moe_w_gate, moe_w_up, moe_w_down):
    x_parts = [x_prompt.reshape(T_CTX, D), x_sample.reshape(T_SMP, D)]
    cond8 = jnp.concatenate([c_ctx[None, :], c, jnp.zeros((COND_ROWS - N_COND, D), F32)], axis=0)
    mods = _mods(cond8, w_mod, b_mod).reshape(DEPTH, COND_ROWS, 6, D)
    lb_all = jnp.cumsum(jax.nn.softmax(hgrn_lb_logits.astype(F32), axis=0), axis=0)[:N_EVEN]
    cos_t, sin_t = _rope_tables()
    ret_tables = _ret_tables()

    nwf = normf_w.reshape(1, D)
    moe_g, moe_u, moe_d = moe_w_gate.astype(BF16), moe_w_up.astype(BF16), moe_w_down.astype(BF16)
    st_even, st_ret = None, None
    for l in range(DEPTH):
        mods_l = mods[l]
        nw1 = norm1_w[l].reshape(1, D)
        nw2 = norm2_w[l].reshape(1, D)
        wr = jnp.concatenate([router_e_w[l], router_g_w[l],
                              jnp.zeros((D, ROUTER_N - N_EXPERTS - N_GROUPS), F32)], axis=1)
        wr_hi = wr.astype(BF16)
        wr_lo = (wr - wr_hi.astype(F32)).astype(BF16)
        br = jnp.concatenate([router_e_b[l], router_g_b[l],
                              jnp.zeros((ROUTER_N - N_EXPERTS - N_GROUPS,), F32)]).reshape(1, ROUTER_N)
        if l % 2 == 0:
            e = l // 2
            w = w_in_even[e]
            a0 = 2 * GLA_QK + 2 * GLA_V
            a1 = a0 + 2 * GLA_RANK
            f0, f1 = a1 + HG_K, a1 + 3 * HG_K
            wm_bf = jnp.concatenate([w[:, :a0], w[:, a1:f0], w[:, f1:]], axis=1).astype(BF16)
            wg_bf = jnp.concatenate([w[:, f0:f1], w[:, a0:a1],
                                     jnp.zeros((D, LANES - 2 * GLA_RANK), F32)], axis=1).astype(BF16)
            zm, zg = _inproj_even(x_parts, mods_l, nw1, wm_bf, wg_bf)
            w2_pad = jnp.zeros((2, LANES, GLA_QK), F32)
            for d in range(2):
                w2_pad = w2_pad.at[d, d * GLA_RANK:(d + 1) * GLA_RANK].set(gla_w_alpha[e, d])
            s_g = state_gla.reshape(SMP_B, N_EVEN, 2, GLA_HEADS // 2, LANES, LANES)
            o_f, o_b, ng, nh = _even_scan(zm, zg, w2_pad.astype(BF16), gla_b_alpha[e].reshape(2, 1, GLA_QK),
                                          lb_all[e].reshape(1, HG_K), s_g, state_hgrn, st_even)
            st_even = (ng, nh)
            gn = jnp.concatenate([gla_norm_w[e], hgrn_norm_w[e]]).reshape(1, D)
            x1, hrow, meta, counts = _post_even(o_f, o_b, zm, x_parts, mods_l, gn, w_out_even[e].astype(BF16),
                                                nw2, wr_hi, wr_lo, br)
        else:
            j = l // 2
            z = _inproj_odd(x_parts[0], mods_l, nw1, w_in_odd[j].astype(BF16), cos_t, sin_t)
            o_f, o_b, st_ret = _odd_scan(z, ret_tables, state_ret, st_ret)
            x1, hrow, meta, counts = _post_odd(o_f, o_b, z, x_parts[0], mods_l, w_out_odd[j].astype(BF16),
                                               nw2, wr_hi, wr_lo, br)
        dhi, dlo, ea, eb, nv, n_real = _dispatch_plan(meta, counts)
        hs = _dispatch(dhi, dlo, nv, hrow)
        ys = _moe(ea, eb, n_real, hs, moe_g, moe_u, moe_d, l)
        if l < DEPTH - 1:
            x_parts = [_combine(dhi, dlo, ys, x1, mods_l, nwf)]
    y_prompt = _combine(dhi, dlo, ys, x1, mods_l, nwf, 0, T_CTX, final=True).reshape(CTX_B, CTX_L, D)
    y_sample = _combine(dhi, dlo, ys, x1, mods_l, nwf, T_CTX // TM, T_SMP,
                        final=True).reshape(SMP_B, SMP_L, D)
    st_gla = st_even[0].reshape(CTX_B, N_EVEN, 2, GLA_HEADS, GLA_DK, GLA_DV)
    return (y_prompt, y_sample, st_gla, st_even[1], st_ret)
```

```python
import functools

import numpy as np
import jax
import jax.numpy as jnp
from jax import lax
from jax.experimental import pallas as pl
from jax.experimental.pallas import tpu as pltpu

F32 = jnp.float32
BF16 = jnp.bfloat16

D = 1024
CTX_B, CTX_L = 32, 256
SMP_B, SMP_L = 4, 4096
DEPTH = 4
N_EVEN, N_ODD = 2, 2
GRID_W = 64
NORM_EPS = 1e-6
T_CTX = CTX_B * CTX_L
T_SMP = SMP_B * SMP_L
T_ALL = T_CTX + T_SMP
N_COND = 1 + SMP_B
COND_ROWS = 8

GLA_HEADS, GLA_DK, GLA_DV, GLA_RANK, GLA_TAU = 4, 64, 128, 16, 16.0
GLA_QK, GLA_V = GLA_HEADS * GLA_DK, GLA_HEADS * GLA_DV
HG_HEADS, HG_DK, HG_DV = 4, 128, 128
HG_K, HG_V = HG_HEADS * HG_DK, HG_HEADS * HG_DV
RET_HEADS, RET_DK, RET_DV = 4, 256, 512
RET_QK, RET_V = RET_HEADS * RET_DK, RET_HEADS * RET_DV
ROPE_BASE, ROPE_PAIRS = 10000.0, 64
N_GROUPS, EXP_PER_GROUP, N_EXPERTS, D_EXPERT = 4, 8, 32, 256

LANES = 128
CHUNK = 64
TL = 256
TM = 512
PAIRS_PER_GROUP = EXP_PER_GROUP * (EXP_PER_GROUP - 1) // 2
N_PAIRS = N_GROUPS * PAIRS_PER_GROUP
TMG = 256
N_TILES_MOE = T_ALL // TMG + N_PAIRS
ROW_W = D + LANES
DISPATCH_ROWS = 2048
SUBLANES = 8
VMEM_LIMIT = 56 * 1024 * 1024

EV_QA, EV_KA, EV_VA, EV_RA, EV_QB, EV_IB, EV_GB = 0, 256, 512, 1024, 1536, 2048, 2560
EV_MAIN = 3072
EV_FF, EV_FB, EV_AL = 0, 512, 1024
EV_GATE = 1024 + LANES
OD_Q, OD_K, OD_V, OD_G = 0, 1024, 2048, 4096
OD_N = 6144
ROUTER_N = LANES


def _dot(a, b):
    return jnp.dot(a, b, preferred_element_type=F32)


def _dot_nt(a, b):
    return lax.dot_general(a, b, (((1,), (1,)), ((), ())), preferred_element_type=F32)


def _dot_tn(a, b):
    return lax.dot_general(a, b, (((0,), (0,)), ((), ())), preferred_element_type=F32)


def _split_bf16(x):
    hi = x.astype(BF16)
    lo = (x - hi.astype(F32)).astype(BF16)
    return hi, lo


def _sigmoid(x):
    return 1.0 / (1.0 + jnp.exp(-x))


def _silu(x):
    return x * _sigmoid(x)


def _params(n_axes):
    return pltpu.CompilerParams(dimension_semantics=("arbitrary",) * n_axes,
                                vmem_limit_bytes=VMEM_LIMIT)


def _cond_of_tile(i, tm):
    n_ctx = T_CTX // tm
    per_seq = SMP_L // tm
    return jnp.where(i < n_ctx, 0, 1 + (i - n_ctx) // per_seq)


def _norm_mod(x, nw, shift, scale):
    var = jnp.mean(x * x, axis=-1, keepdims=True)
    return x * lax.rsqrt(var + NORM_EPS) * nw * (1.0 + scale) + shift


def _mods_kernel(cond_ref, w_ref, b_ref, o_ref):
    c = cond_ref[...]
    a_hi, a_lo = _split_bf16(_silu(c))
    w_hi, w_lo = _split_bf16(w_ref[0])
    o_ref[0] = _dot(a_hi, w_hi) + _dot(a_hi, w_lo) + _dot(a_lo, w_hi) + b_ref[0]


def _mods(cond8, w_mod, b_mod):
    return pl.pallas_call(
        _mods_kernel,
        grid=(DEPTH, 6),
        in_specs=[pl.BlockSpec((COND_ROWS, D), lambda l, j: (0, 0)),
                  pl.BlockSpec((1, D, D), lambda l, j: (l, 0, j)),
                  pl.BlockSpec((1, 1, D), lambda l, j: (l, 0, j))],
        out_specs=pl.BlockSpec((1, COND_ROWS, D), lambda l, j: (l, 0, j)),
        out_shape=jax.ShapeDtypeStruct((DEPTH, COND_ROWS, 6 * D), F32),
        compiler_params=_params(2), name="mods",
    )(cond8, w_mod, b_mod.reshape(DEPTH, 1, 6 * D))


def _x_specs(x_parts):
    if len(x_parts) == 1:
        return [pl.BlockSpec((TM, D), lambda i: (i, 0))]
    n_ctx = T_CTX // TM
    return [pl.BlockSpec((TM, D), lambda i: (jnp.minimum(i, n_ctx - 1), 0)),
            pl.BlockSpec((TM, D), lambda i: (jnp.maximum(i - n_ctx, 0), 0))]


def _load_x(x_refs):
    if len(x_refs) == 1:
        return x_refs[0][...]
    return jnp.where(pl.program_id(0) < T_CTX // TM, x_refs[0][...], x_refs[1][...])


def _expert_cast_specs(layer):
    assert T_ALL // TM >= N_EXPERTS
    shapes = ((D, D_EXPERT), (D, D_EXPERT), (D_EXPERT, D))
    expert = lambda i: jnp.minimum(i, N_EXPERTS - 1)
    in_specs = [pl.BlockSpec((None, 1) + s, lambda i: (layer, expert(i), 0, 0)) for s in shapes]
    out_specs = [pl.BlockSpec((1,) + s, lambda i: (expert(i), 0, 0)) for s in shapes]
    out_shape = [jax.ShapeDtypeStruct((N_EXPERTS,) + s, BF16) for s in shapes]
    return in_specs, out_specs, out_shape


def _cast_expert(src_refs, dst_refs):
    @pl.when(pl.program_id(0) < N_EXPERTS)
    def _():
        for src, dst in zip(src_refs, dst_refs):
            dst[...] = src[...].astype(BF16)


def _inproj_even_kernel(*refs, n_x):
    x_refs = refs[:n_x]
    mod_ref, nw_ref, wm_ref, wg_ref = refs[n_x:n_x + 4]
    zm_ref, zg_ref = refs[n_x + 7:n_x + 9]
    _cast_expert(refs[n_x + 4:n_x + 7], refs[n_x + 9:])
    m = mod_ref[...]
    h = _norm_mod(_load_x(x_refs), nw_ref[...], m[0:1], m[1:2]).astype(BF16)
    for j in range(0, EV_MAIN, 1024):
        zm_ref[:, j:j + 1024] = _dot(h, wm_ref[:, j:j + 1024]).astype(BF16)
    for j, width in ((0, 1024), (EV_AL, LANES)):
        zg_ref[:, j:j + width] = _dot(h, wg_ref[:, j:j + width])


def _inproj_odd_kernel(x_ref, mod_ref, nw_ref, w_ref, cos_ref, sin_ref, eg_ref, eu_ref, ed_ref,
                       z_ref, og_ref, ou_ref, od_ref):
    _cast_expert((eg_ref, eu_ref, ed_ref), (og_ref, ou_ref, od_ref))
    m = mod_ref[...]
    h = _norm_mod(x_ref[...], nw_ref[...], m[0:1], m[1:2]).astype(BF16)
    for j in range(0, OD_V, 2 * LANES):
        zz = _dot(h, w_ref[:, j:j + 2 * LANES])
        for s in range(2):
            blk = zz[:, s * LANES:(s + 1) * LANES]
            cs = cos_ref[:, s * LANES:(s + 1) * LANES]
            sn = sin_ref[:, s * LANES:(s + 1) * LANES]
            z_ref[:, j + s * LANES:j + (s + 1) * LANES] = (
                blk * cs + pltpu.roll(blk, ROPE_PAIRS, 1) * sn).astype(BF16)
    step = 1024
    for j in range(OD_V, OD_N, step):
        z_ref[:, j:j + step] = _dot(h, w_ref[:, j:j + step]).astype(BF16)


def _inproj_even(x_parts, mods_l, nw, wm_bf, wg_bf, moe_w, layer):
    c_in, c_out, c_shape = _expert_cast_specs(layer)
    return pl.pallas_call(
        functools.partial(_inproj_even_kernel, n_x=len(x_parts)),
        grid=(T_ALL // TM,),
        in_specs=_x_specs(x_parts) + [
                  pl.BlockSpec((None, 6, D), lambda i: (_cond_of_tile(i, TM), 0, 0)),
                  pl.BlockSpec((1, D), lambda i: (0, 0)),
                  pl.BlockSpec((D, EV_MAIN), lambda i: (0, 0)),
                  pl.BlockSpec((D, EV_GATE), lambda i: (0, 0))] + c_in,
        out_specs=[pl.BlockSpec((TM, EV_MAIN), lambda i: (i, 0)),
                   pl.BlockSpec((TM, EV_GATE), lambda i: (i, 0))] + c_out,
        out_shape=[jax.ShapeDtypeStruct((T_ALL, EV_MAIN), BF16),
                   jax.ShapeDtypeStruct((T_ALL, EV_GATE), F32)] + c_shape,
        compiler_params=_params(1), name="inproj_even",
    )(*x_parts, mods_l, nw, wm_bf, wg_bf, *moe_w)


def _rope_block(i):
    n_ctx = T_CTX // TM
    per_seq = SMP_L // TM
    return jnp.where(i < n_ctx, 0, 1 + (i - n_ctx) % per_seq)


def _inproj_odd(x, mods_l, nw, w_bf, cos_t, sin_t, moe_w, layer):
    c_in, c_out, c_shape = _expert_cast_specs(layer)
    return pl.pallas_call(
        _inproj_odd_kernel,
        grid=(T_ALL // TM,),
        in_specs=[pl.BlockSpec((TM, D), lambda i: (i, 0)),
                  pl.BlockSpec((None, 6, D), lambda i: (_cond_of_tile(i, TM), 0, 0)),
                  pl.BlockSpec((1, D), lambda i: (0, 0)),
                  pl.BlockSpec((D, OD_N), lambda i: (0, 0), pipeline_mode=pl.Buffered(1)),
                  pl.BlockSpec((TM, RET_DK), lambda i: (_rope_block(i), 0)),
                  pl.BlockSpec((TM, RET_DK), lambda i: (_rope_block(i), 0))] + c_in,
        out_specs=[pl.BlockSpec((TM, OD_N), lambda i: (i, 0))] + c_out,
        out_shape=[jax.ShapeDtypeStruct((T_ALL, OD_N), BF16)] + c_shape,
        compiler_params=_params(1), name="inproj_odd",
    )(x, mods_l, nw, w_bf, cos_t, sin_t, *moe_w)


def _scan_work():
    tf, tb, first, last, seq = [], [], [], [], []
    base = 0
    for s, length in enumerate([CTX_L] * CTX_B + [SMP_L] * SMP_B):
        n = length // TL
        for t in range(n):
            tf.append(base + t)
            tb.append(base + n - 1 - t)
            first.append(int(t == 0))
            last.append(int(t == n - 1))
            seq.append(s)
        base += n
    return tuple(np.asarray(a, np.int32) for a in (tf, tb, first, last, seq))


N_WORK = T_ALL // TL


def _decay_factors(q, k, b, fwd):
    b_last = b[CHUNK - 1:CHUNK] if fwd else b[0:1]
    b_mid = b[CHUNK // 2:CHUNK // 2 + 1]
    qi = q * jnp.exp(b - b_mid)
    ki = k * jnp.exp(b_mid - b)
    q_in = qi * jnp.exp(b_mid)
    kk = ki * jnp.exp(b_last - b_mid)
    return qi, ki, q_in, kk, jnp.exp(b_last)


def _stack(a, b):
    return jnp.concatenate([a, b], axis=0)


def _gla_pair_chunk(q, k, b, v0, v1, st, masks, bd_causal, fwd):
    qi, ki, q_in, kk, g = _decay_factors(q, k, b, fwd)
    m0, m1 = masks
    split = lambda t: _stack(jnp.where(m0, t, 0.0), jnp.where(m1, t, 0.0)).astype(BF16)
    ki_bf = ki.astype(BF16)
    a = jnp.where(bd_causal, _dot_nt(split(qi), _stack(ki_bf, ki_bf)), 0.0)
    vs = _stack(v0, v1)
    o = _dot(a.astype(BF16), vs) + _dot_nt(split(q_in), st.astype(BF16))
    return o, st * g + _dot_tn(vs, split(kk))


def _hgrn_pair_chunk(q0, k0, b0, v0, q1, k1, b1, v1, st, bd_causal, fwd):
    qi0, ki0, qin0, kk0, g0 = _decay_factors(q0, k0, b0, fwd)
    qi1, ki1, qin1, kk1, g1 = _decay_factors(q1, k1, b1, fwd)
    zero = jnp.zeros((CHUNK, LANES), F32)
    wide = lambda t0, t1: _stack(jnp.concatenate([t0, zero], axis=1),
                                 jnp.concatenate([zero, t1], axis=1)).astype(BF16)
    a = jnp.where(bd_causal, _dot_nt(_stack(qi0, qi1).astype(BF16), _stack(ki0, ki1).astype(BF16)), 0.0)
    vs = _stack(v0, v1)
    o = _dot(a.astype(BF16), vs) + _dot_nt(wide(qin0, qin1), st.astype(BF16))
    return o, st * jnp.concatenate([g0, g1], axis=1) + _dot_tn(vs, wide(kk0, kk1))


def _even_gates(d, fg, al, qb, tri, w2_ref, ba_ref, lbv):
    y = _dot(al[...].astype(BF16), w2_ref[d]) + ba_ref[d]
    la_a = (jnp.minimum(y, 0.0) - jnp.log(1.0 + jnp.exp(-jnp.abs(y)))) * (1.0 / GLA_TAU)
    f = fg[...]
    t = jnp.exp(-jnp.abs(f))
    rcp = 1.0 / (1.0 + t)
    sig_f = jnp.where(f >= 0.0, rcp, t * rcp)
    sig_nf = jnp.where(f >= 0.0, t * rcp, rcp)
    la_h = jnp.log(lbv + (1.0 - lbv) * sig_f)
    key_h = (1.0 - lbv) * sig_nf
    la_hi, la_lo = _split_bf16(jnp.concatenate([la_a, la_h], axis=1))
    b_all = _dot(tri, la_hi) + _dot(tri, la_lo)
    return b_all, key_h, _silu(qb[...].astype(F32))


def _even_tile(dir_refs, w2_ref, ba_ref, lb_ref, stg, sth):
    rc = lax.broadcasted_iota(jnp.int32, (2 * CHUNK, 2 * CHUNK), 0)
    cc = lax.broadcasted_iota(jnp.int32, (2 * CHUNK, 2 * CHUNK), 1)
    same_head = (rc // CHUNK) == (cc // CHUNK)
    bd_causal = [same_head & (cc <= rc), same_head & (cc >= rc)]
    rt = lax.broadcasted_iota(jnp.int32, (TL, TL), 0)
    ct = lax.broadcasted_iota(jnp.int32, (TL, TL), 1)
    same = (rt // CHUNK) == (ct // CHUNK)
    tri = [jnp.where(same & (ct <= rt), 1.0, 0.0).astype(BF16),
           jnp.where(same & (ct >= rt), 1.0, 0.0).astype(BF16)]
    lane = lax.broadcasted_iota(jnp.int32, (CHUNK, LANES), 1)
    pair_masks = [lane < GLA_DK, lane >= GLA_DK]
    lbv = lb_ref[...]
    n_pairs = GLA_HEADS // 2
    st_g = [[stg[d, p] for p in range(n_pairs)] for d in range(2)]
    st_h = [[sth[d, p] for p in range(HG_HEADS // 2)] for d in range(2)]
    gates = [_even_gates(d, dir_refs[d][3], dir_refs[d][5], dir_refs[d][2], tri[d], w2_ref, ba_ref, lbv)
             for d in range(2)]

    n_chunks = TL // CHUNK
    for ci in range(n_chunks):
        for d in range(2):
            qk, va, _, _, ib, _, o_ref = dir_refs[d]
            b_all, key_h, qh = gates[d]
            fwd = d == 0
            cpos = ci if fwd else n_chunks - 1 - ci
            rows = slice(cpos * CHUNK, (cpos + 1) * CHUNK)
            for p in range(n_pairs):
                ln = slice(p * LANES, (p + 1) * LANES)
                kl = slice(GLA_QK + p * LANES, GLA_QK + (p + 1) * LANES)
                c0, c1 = 2 * p * GLA_DV, (2 * p + 1) * GLA_DV
                o, st_g[d][p] = _gla_pair_chunk(
                    qk[rows, ln].astype(F32) * (GLA_DK ** -0.5), qk[rows, kl].astype(F32), b_all[rows, ln],
                    va[rows, c0:c0 + GLA_DV], va[rows, c1:c1 + GLA_DV], st_g[d][p],
                    pair_masks, bd_causal[d], fwd)
                o_ref[rows, c0:c0 + GLA_DV] = o[0:CHUNK].astype(BF16)
                o_ref[rows, c1:c1 + GLA_DV] = o[CHUNK:2 * CHUNK].astype(BF16)
            for p in range(HG_HEADS // 2):
                l0 = slice(2 * p * LANES, (2 * p + 1) * LANES)
                l1 = slice((2 * p + 1) * LANES, (2 * p + 2) * LANES)
                bl0 = slice(GLA_QK + l0.start, GLA_QK + l0.stop)
                bl1 = slice(GLA_QK + l1.start, GLA_QK + l1.stop)
                o, st_h[d][p] = _hgrn_pair_chunk(
                    qh[rows, l0], key_h[rows, l0], b_all[rows, bl0], ib[rows, l0],
                    qh[rows, l1], key_h[rows, l1], b_all[rows, bl1], ib[rows, l1],
                    st_h[d][p], bd_causal[d], fwd)
                o_ref[rows, GLA_V + l0.start:GLA_V + l0.stop] = o[0:CHUNK].astype(BF16)
                o_ref[rows, GLA_V + l1.start:GLA_V + l1.stop] = o[CHUNK:2 * CHUNK].astype(BF16)
    for d in range(2):
        for p in range(n_pairs):
            stg[d, p] = st_g[d][p]
        for p in range(HG_HEADS // 2):
            sth[d, p] = st_h[d][p]


def _even_scan_kernel(tf_ref, tb_ref, first_ref, last_ref, seq_ref,
                      qk_f, va_f, qb_f, fg_f, ib_f, al_f,
                      qk_b, va_b, qb_b, fg_b, ib_b, al_b,
                      w2_ref, ba_ref, lb_ref, sg_ref, sh_ref, *rest, n_prev):
    pg_ref, ph_ref = rest[:2] if n_prev else (None, None)
    of_ref, ob_ref, stg_ref, sth_ref, stg, sth = rest[-6:]
    i = pl.program_id(0)
    is_first = first_ref[i] == 1
    is_last = last_ref[i] == 1
    is_ctx = seq_ref[i] < CTX_B

    @pl.when(jnp.logical_and(is_first, is_ctx))
    def _():
        stg[...] = jnp.zeros_like(stg)
        sth[...] = jnp.zeros_like(sth)

    @pl.when(jnp.logical_and(is_first, jnp.logical_not(is_ctx)))
    def _():
        for d in range(2):
            for p in range(GLA_HEADS // 2):
                stg[d, p] = sg_ref[d, p].T
            for h in range(HG_HEADS):
                sth[d, h // 2, :, (h % 2) * HG_DK:(h % 2 + 1) * HG_DK] = sh_ref[d, h].T

    _even_tile([(qk_f, va_f, qb_f, fg_f, ib_f, al_f, of_ref), (qk_b, va_b, qb_b, fg_b, ib_b, al_b, ob_ref)],
               w2_ref, ba_ref, lb_ref, stg, sth)

    @pl.when(jnp.logical_and(is_last, is_ctx))
    def _():
        for k in range(n_prev):
            stg_ref[k] = pg_ref[k]
            sth_ref[k] = ph_ref[k]
        for d in range(2):
            for p in range(GLA_HEADS // 2):
                stg_ref[n_prev, d, p] = stg[d, p].T
            for h in range(HG_HEADS):
                sth_ref[n_prev, d, h] = sth[d, h // 2, :, (h % 2) * HG_DK:(h % 2 + 1) * HG_DK].T


def _even_scan(zm, zg, w2_pad, b_a, lb, s_gla, s_hgrn, prev):
    n_prev = 0 if prev is None else prev[0].shape[1]
    work = _scan_work()

    def zspec(width, col_block, which):
        return pl.BlockSpec((TL, width), lambda i, tf, tb, fi, la, sq: ((tf, tb)[which][i], col_block))

    def dir_specs(which):
        return [zspec(512, EV_QA // 512, which), zspec(512, EV_VA // 512, which),
                zspec(512, EV_QB // 512, which), zspec(512, (EV_FF, EV_FB)[which] // 512, which),
                zspec(512, EV_IB // 512, which), zspec(LANES, EV_AL // LANES, which)]

    def dir_args():
        return [zm, zm, zm, zg, zm, zg]

    def const(shape):
        return pl.BlockSpec(shape, lambda i, *_: (0,) * len(shape))

    def s_in(shape):
        return pl.BlockSpec((None, None) + shape,
                            lambda i, tf, tb, fi, la, sq:
                            (jnp.clip(sq[i] - CTX_B, 0, SMP_B - 1), n_prev) + (0,) * len(shape))

    def ctx_states(n_layers, shape):
        return pl.BlockSpec((None, n_layers) + shape,
                            lambda i, tf, tb, fi, la, sq: (jnp.minimum(sq[i], CTX_B - 1), 0) + (0,) * len(shape))

    gshape = (2, GLA_HEADS // 2, LANES, LANES)
    hshape = (2, HG_HEADS, HG_DK, HG_DV)
    grid_spec = pltpu.PrefetchScalarGridSpec(
        num_scalar_prefetch=5, grid=(N_WORK,),
        in_specs=dir_specs(0) + dir_specs(1) + [
            const((2, LANES, GLA_QK)), const((2, 1, GLA_QK)), const((1, HG_K)),
            s_in(gshape), s_in(hshape)]
        + ([ctx_states(n_prev, gshape), ctx_states(n_prev, hshape)] if n_prev else []),
        out_specs=[pl.BlockSpec((TL, D), lambda i, tf, tb, fi, la, sq: (tf[i], 0)),
                   pl.BlockSpec((TL, D), lambda i, tf, tb, fi, la, sq: (tb[i], 0)),
                   ctx_states(n_prev + 1, gshape), ctx_states(n_prev + 1, hshape)],
        scratch_shapes=[pltpu.VMEM(gshape, F32),
                        pltpu.VMEM((2, HG_HEADS // 2, HG_DV, 2 * HG_DK), F32)])
    return pl.pallas_call(
        functools.partial(_even_scan_kernel, n_prev=n_prev), grid_spec=grid_spec,
        out_shape=[jax.ShapeDtypeStruct((T_ALL, D), BF16), jax.ShapeDtypeStruct((T_ALL, D), BF16),
                   jax.ShapeDtypeStruct((CTX_B, n_prev + 1) + gshape, F32),
                   jax.ShapeDtypeStruct((CTX_B, n_prev + 1) + hshape, F32)],
        compiler_params=_params(1), name="even_scan",
    )(*work, *dir_args(), *dir_args(), w2_pad, b_a, lb, s_gla, s_hgrn, *(prev if n_prev else []))


RET_HP = 2


def _ret_tables():
    lg = np.log1p(-np.exp2(-5.0 - np.arange(RET_HEADS, dtype=np.float64)))
    pos = np.arange(TL, dtype=np.float64)
    dm = np.zeros((2, RET_HEADS, TL, TL))
    rq = np.zeros((2, RET_HEADS, TL, 1))
    rk = np.zeros((2, RET_HEADS, TL, 1))
    gc = np.zeros((2, RET_HEADS, 1, RET_DV))
    diff = pos[:, None] - pos[None, :]
    kscale = RET_DK ** -0.5
    for h in range(RET_HEADS):
        dm[0, h] = np.where(diff >= 0, np.exp(lg[h] * np.maximum(diff, 0)), 0.0) * kscale
        rq[0, h, :, 0] = np.exp(lg[h] * (pos + 1))
        rk[0, h, :, 0] = np.exp(lg[h] * (TL - 1 - pos)) * kscale
        gc[0, h] = np.exp(lg[h] * TL)
        lb = lg[RET_HEADS - 1 - h]
        dm[1, h] = np.where(diff <= 0, np.exp(lb * np.maximum(-diff, 0)), 0.0) * kscale
        rq[1, h, :, 0] = np.exp(lb * (TL - pos))
        rk[1, h, :, 0] = np.exp(lb * pos) * kscale
        gc[1, h] = np.exp(lb * TL)
    return tuple(jnp.asarray(a, F32) for a in (dm, rq, rk, gc))


def _odd_dir(d, q_ref, k_ref, v_ref, dm_ref, rq_ref, rk_ref, gc_ref, st, o_ref, zero_state):
    for hh in range(RET_HP):
        k = k_ref[:, hh * RET_DK:(hh + 1) * RET_DK]
        q = q_ref[:, hh * RET_DK:(hh + 1) * RET_DK]
        v = v_ref[:, hh * RET_DV:(hh + 1) * RET_DV]
        a = _dot_nt(q, k) * dm_ref[d, hh]
        o = _dot(a.astype(BF16), v)
        kk = (k.astype(F32) * rk_ref[d, hh]).astype(BF16)
        s_new = _dot_tn(kk, v)
        if not zero_state:
            s = st[d, hh]
            o = o + rq_ref[d, hh] * _dot(q, s.astype(BF16))
            s_new = gc_ref[d, hh] * s + s_new
        st[d, hh] = s_new
        o_ref[:, hh * RET_DV:(hh + 1) * RET_DV] = o.astype(BF16)


def _odd_scan_kernel(tf_ref, tb_ref, first_ref, last_ref, seq_ref,
                     q_f, k_f, v_f, q_b, k_b, v_b,
                     dm_ref, rq_ref, rk_ref, gc_ref, s0_ref, *rest, n_prev):
    prev_ref = rest[0] if n_prev else None
    of_ref, ob_ref, sout_ref, st = rest[-4:]
    i = pl.program_id(1)
    is_first = first_ref[i] == 1
    is_last = last_ref[i] == 1
    is_ctx = seq_ref[i] < CTX_B

    from_zero = jnp.logical_and(is_first, is_ctx)

    @pl.when(jnp.logical_and(is_first, jnp.logical_not(is_ctx)))
    def _():
        st[...] = s0_ref[...]

    for zero_state in (True, False):
        @pl.when(from_zero if zero_state else jnp.logical_not(from_zero))
        def _():
            _odd_dir(0, q_f, k_f, v_f, dm_ref, rq_ref, rk_ref, gc_ref, st, of_ref, zero_state)
            _odd_dir(1, q_b, k_b, v_b, dm_ref, rq_ref, rk_ref, gc_ref, st, ob_ref, zero_state)

    @pl.when(jnp.logical_and(is_last, is_ctx))
    def _():
        for k in range(n_prev):
            sout_ref[k] = prev_ref[k]
        sout_ref[n_prev] = st[...]


def _odd_scan(z, tables, s_ret, prev):
    n_prev = 0 if prev is None else prev.shape[1]
    work = _scan_work()
    dm, rq, rk, gc = tables
    qw, vw = RET_HP * RET_DK, RET_HP * RET_DV

    def zspec(width, col0, which):
        return pl.BlockSpec((TL, width),
                            lambda hp, i, tf, tb, fi, la, sq: ((tf, tb)[which][i], col0 // width + hp))

    def dir_specs(which):
        return [zspec(qw, OD_Q, which), zspec(qw, OD_K, which), zspec(vw, OD_V, which)]

    def table(shape):
        return pl.BlockSpec((2, RET_HP) + shape, lambda hp, i, *_: (0, hp) + (0,) * len(shape))

    sshape = (2, RET_HP, RET_DK, RET_DV)

    def ctx_states(n_layers):
        return pl.BlockSpec((None, n_layers) + sshape,
                            lambda hp, i, tf, tb, fi, la, sq: (jnp.minimum(sq[i], CTX_B - 1), 0, 0, hp, 0, 0))

    grid_spec = pltpu.PrefetchScalarGridSpec(
        num_scalar_prefetch=5, grid=(RET_HEADS // RET_HP, N_WORK),
        in_specs=dir_specs(0) + dir_specs(1) + [
            table((TL, TL)), table((TL, 1)), table((TL, 1)), table((1, RET_DV)),
            pl.BlockSpec((None, None) + sshape,
                         lambda hp, i, tf, tb, fi, la, sq:
                         (jnp.clip(sq[i] - CTX_B, 0, SMP_B - 1), n_prev, 0, hp, 0, 0))]
        + ([ctx_states(n_prev)] if n_prev else []),
        out_specs=[pl.BlockSpec((TL, vw), lambda hp, i, tf, tb, fi, la, sq: (tf[i], hp)),
                   pl.BlockSpec((TL, vw), lambda hp, i, tf, tb, fi, la, sq: (tb[i], hp)),
                   ctx_states(n_prev + 1)],
        scratch_shapes=[pltpu.VMEM(sshape, F32)])
    return pl.pallas_call(
        functools.partial(_odd_scan_kernel, n_prev=n_prev), grid_spec=grid_spec,
        out_shape=[jax.ShapeDtypeStruct((T_ALL, RET_V), BF16), jax.ShapeDtypeStruct((T_ALL, RET_V), BF16),
                   jax.ShapeDtypeStruct((CTX_B, n_prev + 1, 2, RET_HEADS, RET_DK, RET_DV), F32)],
        compiler_params=_params(2), name="odd_scan",
    )(*work, z, z, z, z, z, z, dm, rq, rk, gc, s_ret, *([prev] if n_prev else []))


def _head_rms(o, width):
    parts = []
    for j in range(0, o.shape[1], width):
        blk = o[:, j:j + width]
        parts.append(blk * lax.rsqrt(jnp.mean(blk * blk, axis=-1, keepdims=True) + NORM_EPS))
    return parts


def _route(h2, wr_hi_ref, wr_lo_ref, br_ref):
    h_hi, h_lo = _split_bf16(h2)
    logits = (_dot(h_hi, wr_hi_ref[...]) + _dot(h_hi, wr_lo_ref[...]) + _dot(h_lo, wr_hi_ref[...])
              + br_ref[...])
    lane_i = lax.broadcasted_iota(jnp.int32, logits.shape, 1)
    lane = lane_i.astype(F32)
    neg = -jnp.inf
    big = 1e9
    is_grp = (lane_i >= N_EXPERTS) & (lane_i < N_EXPERTS + N_GROUPS)
    lg = jnp.where(is_grp, logits, neg)
    mg = jnp.max(lg, axis=-1, keepdims=True)
    gsel = jnp.min(jnp.where(lg == mg, lane - N_EXPERTS, big), axis=-1, keepdims=True)
    pg = 1.0 / jnp.sum(jnp.where(is_grp, jnp.exp(lg - mg), 0.0), axis=-1, keepdims=True)
    in_grp = (lane_i < N_EXPERTS) & ((lane_i // EXP_PER_GROUP).astype(F32) == gsel)
    le = jnp.where(in_grp, logits, neg)
    m1 = jnp.max(le, axis=-1, keepdims=True)
    i1 = jnp.min(jnp.where(le == m1, lane, big), axis=-1, keepdims=True)
    le2 = jnp.where(lane == i1, neg, le)
    m2 = jnp.max(le2, axis=-1, keepdims=True)
    i2 = jnp.min(jnp.where(le2 == m2, lane, big), axis=-1, keepdims=True)
    e2 = jnp.exp(m2 - m1)
    w1 = pg / (1.0 + e2)
    w2 = pg * e2 / (1.0 + e2)
    first_lo = i1 < i2
    ia = jnp.minimum(i1, i2) - gsel * EXP_PER_GROUP
    ib = jnp.maximum(i1, i2) - gsel * EXP_PER_GROUP
    pair = gsel * PAIRS_PER_GROUP + ia * (2 * EXP_PER_GROUP - 1 - ia) * 0.5 + (ib - ia - 1.0)
    return pair, jnp.where(first_lo, w1, w2), jnp.where(first_lo, w2, w1)


def _post_tail(i, x, out, m, nw2_ref, wr_hi_ref, wr_lo_ref, br_ref,
               x1_ref, hrow_ref, meta_ref, cnt_ref, carry):
    x1 = x + m[2:3] * out
    x1_ref[...] = x1
    h2 = _norm_mod(x1, nw2_ref[...], m[3:4], m[4:5])
    pair, w_a, w_b = _route(h2, wr_hi_ref, wr_lo_ref, br_ref)

    @pl.when(i == 0)
    def _():
        carry[...] = jnp.zeros_like(carry)

    lane = lax.broadcasted_iota(jnp.int32, (TM, LANES), 1)
    onehot = jnp.where(lane.astype(F32) == pair, 1.0, 0.0)
    r = lax.broadcasted_iota(jnp.int32, (TM, TM), 0)
    c = lax.broadcasted_iota(jnp.int32, (TM, TM), 1)
    earlier = jnp.where(c < r, 1.0, 0.0).astype(BF16)
    before = _dot(earlier, onehot.astype(BF16)) + carry[...]
    rank = jnp.sum(onehot * before, axis=-1, keepdims=True)
    carry[...] += jnp.sum(onehot, axis=0, keepdims=True)
    cnt_ref[...] = carry[...]
    meta_ref[...] = jnp.where(lane == 0, pair, jnp.where(lane == 1, rank, 0.0))
    hrow_ref[:, 0:D] = h2
    hrow_ref[:, D:ROW_W] = jnp.where(lane == 0, w_a, jnp.where(lane == 1, w_b, 0.0))


def _post_even_kernel(*refs, n_x):
    x_refs = refs[:n_x]
    (of_ref, ob_ref, ra_ref, gb_ref, mod_ref, gn_ref, wo_ref, nw2_ref, wr_hi_ref, wr_lo_ref, br_ref,
     x1_ref, hrow_ref, meta_ref, cnt_ref, carry) = refs[n_x:]
    o = of_ref[...].astype(F32) + ob_ref[...].astype(F32)
    normed = jnp.concatenate(_head_rms(o, GLA_DV), axis=1)
    gate = jnp.concatenate([_silu(ra_ref[...].astype(F32)), _silu(gb_ref[...].astype(F32))], axis=1)
    mixed = (normed * gn_ref[...] * gate).astype(BF16)
    out = _dot(mixed, wo_ref[...])
    _post_tail(pl.program_id(0), _load_x(x_refs), out, mod_ref[...], nw2_ref, wr_hi_ref, wr_lo_ref, br_ref,
               x1_ref, hrow_ref, meta_ref, cnt_ref, carry)


def _post_odd_kernel(of_ref, ob_ref, g_ref, x_ref, mod_ref, wo_ref,
                     nw2_ref, wr_hi_ref, wr_lo_ref, br_ref,
                     x1_ref, hrow_ref, meta_ref, cnt_ref, carry):
    o = of_ref[...].astype(F32) + ob_ref[...].astype(F32)
    normed = jnp.concatenate(_head_rms(o, RET_DV), axis=1)
    mixed = (normed * _silu(g_ref[...].astype(F32))).astype(BF16)
    out = _dot(mixed, wo_ref[...])
    _post_tail(pl.program_id(0), x_ref[...], out, mod_ref[...], nw2_ref, wr_hi_ref, wr_lo_ref, br_ref,
               x1_ref, hrow_ref, meta_ref, cnt_ref, carry)


def _post_common_specs():
    tile = lambda w: pl.BlockSpec((TM, w), lambda i: (i, 0))
    const = lambda s: pl.BlockSpec(s, lambda i: (0,) * len(s))
    mod = pl.BlockSpec((None, 6, D), lambda i: (_cond_of_tile(i, TM), 0, 0))
    out_specs = [tile(D), tile(ROW_W), tile(LANES), const((1, LANES))]
    out_shape = [jax.ShapeDtypeStruct((T_ALL, D), F32),
                 jax.ShapeDtypeStruct((T_ALL, ROW_W), F32),
                 jax.ShapeDtypeStruct((T_ALL, LANES), F32),
                 jax.ShapeDtypeStruct((1, LANES), F32)]
    scratch = [pltpu.VMEM((1, LANES), F32)]
    return tile, const, mod, out_specs, out_shape, scratch


def _post_even(o_f, o_b, z, x_parts, mods_l, gn, wo_bf, nw2, wr_hi, wr_lo, br):
    tile, const, mod, out_specs, out_shape, scratch = _post_common_specs()
    zcol = lambda c0: pl.BlockSpec((TM, 512), lambda i: (i, c0 // 512))
    return pl.pallas_call(
        functools.partial(_post_even_kernel, n_x=len(x_parts)), grid=(T_ALL // TM,),
        in_specs=_x_specs(x_parts) + [
                  tile(D), tile(D), zcol(EV_RA), zcol(EV_GB), mod, const((1, D)),
                  const((D, D)), const((1, D)), const((D, ROUTER_N)), const((D, ROUTER_N)),
                  const((1, ROUTER_N))],
        out_specs=out_specs, out_shape=out_shape, scratch_shapes=scratch,
        compiler_params=_params(1), name="post_even",
    )(*x_parts, o_f, o_b, z, z, mods_l, gn, wo_bf, nw2, wr_hi, wr_lo, br)


def _post_odd(o_f, o_b, z, x, mods_l, wo_bf, nw2, wr_hi, wr_lo, br):
    tile, const, mod, out_specs, out_shape, scratch = _post_common_specs()
    return pl.pallas_call(
        _post_odd_kernel, grid=(T_ALL // TM,),
        in_specs=[tile(RET_V), tile(RET_V),
                  pl.BlockSpec((TM, RET_V), lambda i: (i, OD_G // RET_V)), tile(D), mod,
                  const((RET_V, D)), const((1, D)), const((D, ROUTER_N)), const((D, ROUTER_N)),
                  const((1, ROUTER_N))],
        out_specs=out_specs, out_shape=out_shape, scratch_shapes=scratch,
        compiler_params=_params(1), name="post_odd",
    )(o_f, o_b, z, x, mods_l, wo_bf, nw2, wr_hi, wr_lo, br)


def _pair_tables():
    ea, eb = [], []
    for g in range(N_GROUPS):
        for a in range(EXP_PER_GROUP):
            for b in range(a + 1, EXP_PER_GROUP):
                ea.append(g * EXP_PER_GROUP + a)
                eb.append(g * EXP_PER_GROUP + b)
    pad = LANES - len(ea)
    return (jnp.asarray(ea + [0] * pad, jnp.int32), jnp.asarray(eb + [0] * pad, jnp.int32))


def _dispatch_plan(meta, counts):
    pair = meta[:, 0].astype(jnp.int32)
    rank = meta[:, 1].astype(jnp.int32)
    cnt = counts[0].astype(jnp.int32)
    tiles_g = (cnt + TMG - 1) // TMG
    tile_end = jnp.cumsum(tiles_g)
    tile_start = tile_end - tiles_g
    slot0 = tile_start * TMG
    ids = jnp.arange(LANES, dtype=jnp.int32)
    dest = rank + jnp.sum(jnp.where(pair[:, None] == ids[None, :], slot0[None, :], 0), axis=1)
    n_real = tile_end[-1]
    j = jnp.arange(N_TILES_MOE, dtype=jnp.int32)
    jj = jnp.minimum(j, n_real - 1)
    grp = jnp.sum((jj[:, None] >= tile_end[None, :]).astype(jnp.int32), axis=1)
    ea_t, eb_t = _pair_tables()
    onehot_g = grp[:, None] == ids[None, :]
    pick = lambda v: jnp.sum(jnp.where(onehot_g, v[None, :], 0), axis=1)
    nv = jnp.clip(pick(cnt) - (jj - pick(tile_start)) * TMG, 0, TMG)
    nv = jnp.where(j < n_real, nv, 0)
    return dest // SUBLANES, dest % SUBLANES, pick(ea_t), pick(eb_t), nv, n_real.reshape(1)


DISPATCH_STEPS = T_ALL // DISPATCH_ROWS


def _dispatch_kernel(dhi_ref, dlo_ref, nv_ref, h_ref, hs_hbm, zeros, sem, pad_sem):
    j = pl.program_id(0)
    base = j * DISPATCH_ROWS
    tile_sub = TMG // SUBLANES

    def pad_tile(tile):
        return pltpu.make_async_copy(zeros, hs_hbm.at[pl.ds(tile * tile_sub, tile_sub)], pad_sem.at[0])

    @pl.when(j == 0)
    def _():
        zeros[...] = jnp.zeros_like(zeros)

        def fill(tile, carry):
            @pl.when(nv_ref[tile] < TMG)
            def _():
                pad_tile(tile).start()
            return carry

        def drain(tile, carry):
            @pl.when(nv_ref[tile] < TMG)
            def _():
                pad_tile(tile).wait()
            return carry

        lax.fori_loop(0, N_TILES_MOE, fill, 0)
        lax.fori_loop(0, N_TILES_MOE, drain, 0)

    def issue(g, carry):
        for k in range(SUBLANES):
            t = base + g * SUBLANES + k
            pltpu.make_async_copy(h_ref.at[g, pl.ds(k, 1)], hs_hbm.at[dhi_ref[t], pl.ds(dlo_ref[t], 1)],
                                  sem.at[0]).start(priority=k % 2)
        return carry

    lax.fori_loop(0, DISPATCH_ROWS // SUBLANES, issue, 0)
    pltpu.make_async_copy(h_ref, hs_hbm.at[pl.ds(0, DISPATCH_ROWS // SUBLANES)], sem.at[0]).wait()


def _dispatch(dest_hi, dest_lo, nv, hrow):
    grid_spec = pltpu.PrefetchScalarGridSpec(
        num_scalar_prefetch=3, grid=(DISPATCH_STEPS,),
        in_specs=[pl.BlockSpec((DISPATCH_ROWS // SUBLANES, SUBLANES, ROW_W), lambda j, *_: (j, 0, 0))],
        out_specs=pl.BlockSpec(memory_space=pl.ANY),
        scratch_shapes=[pltpu.VMEM((TMG // SUBLANES, SUBLANES, ROW_W), F32),
                        pltpu.SemaphoreType.DMA((1,)), pltpu.SemaphoreType.DMA((1,))])
    hs = pl.pallas_call(
        _dispatch_kernel, grid_spec=grid_spec,
        out_shape=jax.ShapeDtypeStruct((N_TILES_MOE * TMG // SUBLANES, SUBLANES, ROW_W), F32),
        compiler_params=_params(1), name="moe_dispatch",
    )(dest_hi, dest_lo, nv, hrow.reshape(T_ALL // SUBLANES, SUBLANES, ROW_W))
    return hs.reshape(N_TILES_MOE * TMG, ROW_W)


def _moe_kernel(ea_ref, eb_ref, nreal_ref, hs_ref,
                wga_ref, wua_ref, wda_ref, wgb_ref, wub_ref, wdb_ref, ys_ref):
    j = pl.program_id(0)

    @pl.when(j < nreal_ref[0])
    def _():
        h = hs_ref[:, 0:D].astype(BF16)
        wrow = hs_ref[:, D:ROW_W]
        w_a, w_b = wrow[:, 0:1], wrow[:, 1:2]
        act_a = _silu(_dot(h, wga_ref[0])) * _dot(h, wua_ref[0]) * w_a
        act_b = _silu(_dot(h, wgb_ref[0])) * _dot(h, wub_ref[0]) * w_b
        ys_ref[...] = _dot(act_a.astype(BF16), wda_ref[0]) + _dot(act_b.astype(BF16), wdb_ref[0])

    @pl.when(j >= nreal_ref[0])
    def _():
        ys_ref[...] = jnp.zeros_like(ys_ref)


def _moe(ea, eb, n_real, hs, wg_bf, wu_bf, wd_bf):
    def wspec(shape, which):
        return pl.BlockSpec((1,) + shape, lambda j, ea, eb, nr: ((ea, eb)[which][j], 0, 0))

    up, down = (D, D_EXPERT), (D_EXPERT, D)
    grid_spec = pltpu.PrefetchScalarGridSpec(
        num_scalar_prefetch=3, grid=(N_TILES_MOE,),
        in_specs=[pl.BlockSpec((TMG, ROW_W), lambda j, ea, eb, nr: (jnp.minimum(j, nr[0] - 1), 0)),
                  wspec(up, 0), wspec(up, 0), wspec(down, 0),
                  wspec(up, 1), wspec(up, 1), wspec(down, 1)],
        out_specs=pl.BlockSpec((TMG, D), lambda j, ea, eb, nr: (j, 0)))
    return pl.pallas_call(
        _moe_kernel, grid_spec=grid_spec,
        out_shape=jax.ShapeDtypeStruct((N_TILES_MOE * TMG, D), F32),
        compiler_params=_params(1), name="moe",
    )(ea, eb, n_real, hs, wg_bf, wu_bf, wd_bf, wg_bf, wu_bf, wd_bf)


def _combine_kernel(dhi_ref, dlo_ref, ys_hbm, x1_ref, mod_ref, nwf_ref, o_ref, buf, sem, *, tile0, final):
    j = pl.program_id(0)
    n = pl.num_programs(0)

    def gather(tile, s):
        def issue(g, carry):
            for k in range(SUBLANES):
                t = (tile0 + tile) * TM + g * SUBLANES + k
                pltpu.make_async_copy(ys_hbm.at[dhi_ref[t], pl.ds(dlo_ref[t], 1)],
                                      buf.at[s, g, pl.ds(k, 1)], sem.at[s]).start(priority=k % 2)
            return carry
        lax.fori_loop(0, TM // SUBLANES, issue, 0)

    @pl.when(j == 0)
    def _():
        gather(0, 0)

    for s in range(2):
        @pl.when(jnp.logical_and(j + 1 < n, (j + 1) % 2 == s))
        def _():
            gather(j + 1, s)

    for s in range(2):
        @pl.when(j % 2 == s)
        def _():
            pltpu.make_async_copy(ys_hbm.at[pl.ds(0, TM // SUBLANES)], buf.at[s], sem.at[s]).wait()
            x = x1_ref[...] + mod_ref[5:6] * buf[s].reshape(TM, D)
            if final:
                var = jnp.mean(x * x, axis=-1, keepdims=True)
                x = x * lax.rsqrt(var + NORM_EPS) * nwf_ref[...]
            o_ref[...] = x


def _combine(dest_hi, dest_lo, ys, x1, mods_l, nwf, tile0=0, n_tok=T_ALL, final=False):
    grid_spec = pltpu.PrefetchScalarGridSpec(
        num_scalar_prefetch=2, grid=(n_tok // TM,),
        in_specs=[pl.BlockSpec(memory_space=pl.ANY),
                  pl.BlockSpec((TM, D), lambda i, *_: (tile0 + i, 0)),
                  pl.BlockSpec((None, 6, D), lambda i, *_: (_cond_of_tile(tile0 + i, TM), 0, 0)),
                  pl.BlockSpec((1, D), lambda i, *_: (0, 0))],
        out_specs=pl.BlockSpec((TM, D), lambda i, *_: (i, 0)),
        scratch_shapes=[pltpu.VMEM((2, TM // SUBLANES, SUBLANES, D), F32), pltpu.SemaphoreType.DMA((2,))])
    return pl.pallas_call(
        functools.partial(_combine_kernel, tile0=tile0, final=final), grid_spec=grid_spec,
        out_shape=jax.ShapeDtypeStruct((n_tok, D), F32),
        compiler_params=_params(1), name="moe_combine",
    )(dest_hi, dest_lo, ys.reshape(N_TILES_MOE * TMG // SUBLANES, SUBLANES, D), x1, mods_l, nwf)


def _rope_tables():
    freqs = ROPE_BASE ** (-jnp.arange(ROPE_PAIRS, dtype=F32) / ROPE_PAIRS)
    t = jnp.arange(SMP_L)
    halves_c, halves_s = [], []
    for p in (t // GRID_W, t % GRID_W):
        ang = p.astype(F32)[:, None] * freqs
        cs, sn = jnp.cos(ang), jnp.sin(ang)
        halves_c += [cs, cs]
        halves_s += [-sn, sn]
    cos_t = jnp.concatenate(halves_c, axis=1)
    sin_t = jnp.concatenate(halves_s, axis=1)
    cos_t = jnp.concatenate([jnp.ones((TM, RET_DK), F32), cos_t], axis=0)
    sin_t = jnp.concatenate([jnp.zeros((TM, RET_DK), F32), sin_t], axis=0)
    return cos_t, sin_t


def kernel(x_prompt, x_sample, state_gla, state_hgrn, state_ret, c, c_ctx, norm1_w, norm2_w, normf_w,
           w_mod, b_mod, w_in_even, gla_w_alpha, gla_b_alpha, hgrn_lb_logits, gla_norm_w, hgrn_norm_w,
           w_out_even, w_in_odd, w_out_odd, router_g_w, router_g_b, router_e_w, router_e_b,
           moe_w_gate, moe_w_up, moe_w_down):
    x_parts = [x_prompt.reshape(T_CTX, D), x_sample.reshape(T_SMP, D)]
    cond8 = jnp.concatenate([c_ctx[None, :], c, jnp.zeros((COND_ROWS - N_COND, D), F32)], axis=0)
    mods = _mods(cond8, w_mod, b_mod).reshape(DEPTH, COND_ROWS, 6, D)
    lb_all = jnp.cumsum(jax.nn.softmax(hgrn_lb_logits.astype(F32), axis=0), axis=0)[:N_EVEN]
    cos_t, sin_t = _rope_tables()
    ret_tables = _ret_tables()

    nwf = normf_w.reshape(1, D)
    moe_w = (moe_w_gate, moe_w_up, moe_w_down)
    st_even, st_ret = None, None
    for l in range(DEPTH):
        mods_l = mods[l]
        nw1 = norm1_w[l].reshape(1, D)
        nw2 = norm2_w[l].reshape(1, D)
        wr = jnp.concatenate([router_e_w[l], router_g_w[l],
                              jnp.zeros((D, ROUTER_N - N_EXPERTS - N_GROUPS), F32)], axis=1)
        wr_hi = wr.astype(BF16)
        wr_lo = (wr - wr_hi.astype(F32)).astype(BF16)
        br = jnp.concatenate([router_e_b[l], router_g_b[l],
                              jnp.zeros((ROUTER_N - N_EXPERTS - N_GROUPS,), F32)]).reshape(1, ROUTER_N)
        if l % 2 == 0:
            e = l // 2
            w = w_in_even[e]
            a0 = 2 * GLA_QK + 2 * GLA_V
            a1 = a0 + 2 * GLA_RANK
            f0, f1 = a1 + HG_K, a1 + 3 * HG_K
            wm_bf = jnp.concatenate([w[:, :a0], w[:, a1:f0], w[:, f1:]], axis=1).astype(BF16)
            wg_bf = jnp.concatenate([w[:, f0:f1], w[:, a0:a1],
                                     jnp.zeros((D, LANES - 2 * GLA_RANK), F32)], axis=1).astype(BF16)
            zm, zg, *moe_bf = _inproj_even(x_parts, mods_l, nw1, wm_bf, wg_bf, moe_w, l)
            w2_pad = jnp.zeros((2, LANES, GLA_QK), F32)
            for d in range(2):
                w2_pad = w2_pad.at[d, d * GLA_RANK:(d + 1) * GLA_RANK].set(gla_w_alpha[e, d])
            s_g = state_gla.reshape(SMP_B, N_EVEN, 2, GLA_HEADS // 2, LANES, LANES)
            o_f, o_b, ng, nh = _even_scan(zm, zg, w2_pad.astype(BF16), gla_b_alpha[e].reshape(2, 1, GLA_QK),
                                          lb_all[e].reshape(1, HG_K), s_g, state_hgrn, st_even)
            st_even = (ng, nh)
            gn = jnp.concatenate([gla_norm_w[e], hgrn_norm_w[e]]).reshape(1, D)
            x1, hrow, meta, counts = _post_even(o_f, o_b, zm, x_parts, mods_l, gn, w_out_even[e].astype(BF16),
                                                nw2, wr_hi, wr_lo, br)
        else:
            j = l // 2
            z, *moe_bf = _inproj_odd(x_parts[0], mods_l, nw1, w_in_odd[j].astype(BF16), cos_t, sin_t,
                                     moe_w, l)
            o_f, o_b, st_ret = _odd_scan(z, ret_tables, state_ret, st_ret)
            x1, hrow, meta, counts = _post_odd(o_f, o_b, z, x_parts[0], mods_l, w_out_odd[j].astype(BF16),
                                               nw2, wr_hi, wr_lo, br)
        dhi, dlo, ea, eb, nv, n_real = _dispatch_plan(meta, counts)
        hs = _dispatch(dhi, dlo, nv, hrow)
        ys = _moe(ea, eb, n_real, hs, *moe_bf)
        if l < DEPTH - 1:
            x_parts = [_combine(dhi, dlo, ys, x1, mods_l, nwf)]
    y_prompt = _combine(dhi, dlo, ys, x1, mods_l, nwf, 0, T_CTX, final=True).reshape(CTX_B, CTX_L, D)
    y_sample = _combine(dhi, dlo, ys, x1, mods_l, nwf, T_CTX // TM, T_SMP,
                        final=True).reshape(SMP_B, SMP_L, D)
    st_gla = st_even[0].reshape(CTX_B, N_EVEN, 2, GLA_HEADS, GLA_DK, GLA_DV)
    return (y_prompt, y_sample, st_gla, st_even[1], st_ret)
```

```python
import functools

import numpy as np
import jax
import jax.numpy as jnp
from jax import lax
from jax.experimental import pallas as pl
from jax.experimental.pallas import tpu as pltpu

F32 = jnp.float32
BF16 = jnp.bfloat16

D = 1024
CTX_B, CTX_L = 32, 256
SMP_B, SMP_L = 4, 4096
DEPTH = 4
N_EVEN, N_ODD = 2, 2
GRID_W = 64
NORM_EPS = 1e-6
T_CTX = CTX_B * CTX_L
T_SMP = SMP_B * SMP_L
T_ALL = T_CTX + T_SMP
N_COND = 1 + SMP_B
COND_ROWS = 8

GLA_HEADS, GLA_DK, GLA_DV, GLA_RANK, GLA_TAU = 4, 64, 128, 16, 16.0
GLA_QK, GLA_V = GLA_HEADS * GLA_DK, GLA_HEADS * GLA_DV
HG_HEADS, HG_DK, HG_DV = 4, 128, 128
HG_K, HG_V = HG_HEADS * HG_DK, HG_HEADS * HG_DV
RET_HEADS, RET_DK, RET_DV = 4, 256, 512
RET_QK, RET_V = RET_HEADS * RET_DK, RET_HEADS * RET_DV
ROPE_BASE, ROPE_PAIRS = 10000.0, 64
N_GROUPS, EXP_PER_GROUP, N_EXPERTS, D_EXPERT = 4, 8, 32, 256

LANES = 128
CHUNK = 64
TL = 256
TM = 512
PAIRS_PER_GROUP = EXP_PER_GROUP * (EXP_PER_GROUP - 1) // 2
N_PAIRS = N_GROUPS * PAIRS_PER_GROUP
TMG = 256
N_TILES_MOE = T_ALL // TMG + N_PAIRS
ROW_W = D + LANES
DISPATCH_ROWS = 2048
SUBLANES = 8
VMEM_LIMIT = 56 * 1024 * 1024

EV_QA, EV_KA, EV_VA, EV_RA, EV_QB, EV_IB, EV_GB = 0, 256, 512, 1024, 1536, 2048, 2560
EV_MAIN = 3072
EV_FF, EV_FB, EV_AL = 0, 512, 1024
EV_GATE = 1024 + LANES
OD_Q, OD_K, OD_V, OD_G = 0, 1024, 2048, 4096
OD_N = 6144
ROUTER_N = LANES


def _dot(a, b):
    return jnp.dot(a, b, preferred_element_type=F32)


def _dot_nt(a, b):
    return lax.dot_general(a, b, (((1,), (1,)), ((), ())), preferred_element_type=F32)


def _dot_tn(a, b):
    return lax.dot_general(a, b, (((0,), (0,)), ((), ())), preferred_element_type=F32)


def _split_bf16(x):
    hi = x.astype(BF16)
    lo = (x - hi.astype(F32)).astype(BF16)
    return hi, lo


def _sigmoid(x):
    return 1.0 / (1.0 + jnp.exp(-x))


def _silu(x):
    return x * _sigmoid(x)


def _params(n_axes):
    return pltpu.CompilerParams(dimension_semantics=("arbitrary",) * n_axes,
                                vmem_limit_bytes=VMEM_LIMIT)


def _cond_of_tile(i, tm):
    n_ctx = T_CTX // tm
    per_seq = SMP_L // tm
    return jnp.where(i < n_ctx, 0, 1 + (i - n_ctx) // per_seq)


def _norm_mod(x, nw, shift, scale):
    var = jnp.mean(x * x, axis=-1, keepdims=True)
    return x * lax.rsqrt(var + NORM_EPS) * nw * (1.0 + scale) + shift


def _mods_kernel(cond_ref, w_ref, b_ref, o_ref):
    c = cond_ref[...]
    a_hi, a_lo = _split_bf16(_silu(c))
    w_hi, w_lo = _split_bf16(w_ref[0])
    o_ref[0] = _dot(a_hi, w_hi) + _dot(a_hi, w_lo) + _dot(a_lo, w_hi) + b_ref[0]


def _mods(cond8, w_mod, b_mod):
    return pl.pallas_call(
        _mods_kernel,
        grid=(DEPTH, 6),
        in_specs=[pl.BlockSpec((COND_ROWS, D), lambda l, j: (0, 0)),
                  pl.BlockSpec((1, D, D), lambda l, j: (l, 0, j)),
                  pl.BlockSpec((1, 1, D), lambda l, j: (l, 0, j))],
        out_specs=pl.BlockSpec((1, COND_ROWS, D), lambda l, j: (l, 0, j)),
        out_shape=jax.ShapeDtypeStruct((DEPTH, COND_ROWS, 6 * D), F32),
        compiler_params=_params(2), name="mods",
    )(cond8, w_mod, b_mod.reshape(DEPTH, 1, 6 * D))


def _x_specs(x_parts):
    if len(x_parts) == 1:
        return [pl.BlockSpec((TM, D), lambda i: (i, 0))]
    n_ctx = T_CTX // TM
    return [pl.BlockSpec((TM, D), lambda i: (jnp.minimum(i, n_ctx - 1), 0)),
            pl.BlockSpec((TM, D), lambda i: (jnp.maximum(i - n_ctx, 0), 0))]


def _load_x(x_refs):
    if len(x_refs) == 1:
        return x_refs[0][...]
    return jnp.where(pl.program_id(0) < T_CTX // TM, x_refs[0][...], x_refs[1][...])


def _expert_cast_specs(layer):
    assert T_ALL // TM >= N_EXPERTS
    shapes = ((D, D_EXPERT), (D, D_EXPERT), (D_EXPERT, D))
    expert = lambda i: jnp.minimum(i, N_EXPERTS - 1)
    in_specs = [pl.BlockSpec((None, 1) + s, lambda i: (layer, expert(i), 0, 0)) for s in shapes]
    out_specs = [pl.BlockSpec((1,) + s, lambda i: (expert(i), 0, 0)) for s in shapes]
    out_shape = [jax.ShapeDtypeStruct((N_EXPERTS,) + s, BF16) for s in shapes]
    return in_specs, out_specs, out_shape


def _cast_expert(src_refs, dst_refs):
    @pl.when(pl.program_id(0) < N_EXPERTS)
    def _():
        for src, dst in zip(src_refs, dst_refs):
            dst[...] = src[...].astype(BF16)


def _inproj_even_kernel(*refs, n_x):
    x_refs = refs[:n_x]
    mod_ref, nw_ref, wm_ref, wg_ref = refs[n_x:n_x + 4]
    zm_ref, zg_ref = refs[n_x + 7:n_x + 9]
    _cast_expert(refs[n_x + 4:n_x + 7], refs[n_x + 9:])
    m = mod_ref[...]
    h = _norm_mod(_load_x(x_refs), nw_ref[...], m[0:1], m[1:2]).astype(BF16)
    for j in range(0, EV_MAIN, 1024):
        zm_ref[:, j:j + 1024] = _dot(h, wm_ref[:, j:j + 1024]).astype(BF16)
    for j, width in ((0, 1024), (EV_AL, LANES)):
        zg_ref[:, j:j + width] = _dot(h, wg_ref[:, j:j + width])


def _inproj_odd_kernel(x_ref, mod_ref, nw_ref, w_ref, cos_ref, sin_ref, eg_ref, eu_ref, ed_ref,
                       z_ref, og_ref, ou_ref, od_ref):
    _cast_expert((eg_ref, eu_ref, ed_ref), (og_ref, ou_ref, od_ref))
    m = mod_ref[...]
    h = _norm_mod(x_ref[...], nw_ref[...], m[0:1], m[1:2]).astype(BF16)
    for j in range(0, OD_V, 2 * LANES):
        zz = _dot(h, w_ref[:, j:j + 2 * LANES])
        for s in range(2):
            blk = zz[:, s * LANES:(s + 1) * LANES]
            cs = cos_ref[:, s * LANES:(s + 1) * LANES]
            sn = sin_ref[:, s * LANES:(s + 1) * LANES]
            z_ref[:, j + s * LANES:j + (s + 1) * LANES] = (
                blk * cs + pltpu.roll(blk, ROPE_PAIRS, 1) * sn).astype(BF16)
    step = 1024
    for j in range(OD_V, OD_N, step):
        z_ref[:, j:j + step] = _dot(h, w_ref[:, j:j + step]).astype(BF16)


def _inproj_even(x_parts, mods_l, nw, wm_bf, wg_bf, moe_w, layer):
    c_in, c_out, c_shape = _expert_cast_specs(layer)
    return pl.pallas_call(
        functools.partial(_inproj_even_kernel, n_x=len(x_parts)),
        grid=(T_ALL // TM,),
        in_specs=_x_specs(x_parts) + [
                  pl.BlockSpec((None, 6, D), lambda i: (_cond_of_tile(i, TM), 0, 0)),
                  pl.BlockSpec((1, D), lambda i: (0, 0)),
                  pl.BlockSpec((D, EV_MAIN), lambda i: (0, 0)),
                  pl.BlockSpec((D, EV_GATE), lambda i: (0, 0))] + c_in,
        out_specs=[pl.BlockSpec((TM, EV_MAIN), lambda i: (i, 0)),
                   pl.BlockSpec((TM, EV_GATE), lambda i: (i, 0))] + c_out,
        out_shape=[jax.ShapeDtypeStruct((T_ALL, EV_MAIN), BF16),
                   jax.ShapeDtypeStruct((T_ALL, EV_GATE), F32)] + c_shape,
        compiler_params=_params(1), name="inproj_even",
    )(*x_parts, mods_l, nw, wm_bf, wg_bf, *moe_w)


def _rope_block(i):
    n_ctx = T_CTX // TM
    per_seq = SMP_L // TM
    return jnp.where(i < n_ctx, 0, 1 + (i - n_ctx) % per_seq)


def _inproj_odd(x, mods_l, nw, w_bf, cos_t, sin_t, moe_w, layer):
    c_in, c_out, c_shape = _expert_cast_specs(layer)
    return pl.pallas_call(
        _inproj_odd_kernel,
        grid=(T_ALL // TM,),
        in_specs=[pl.BlockSpec((TM, D), lambda i: (i, 0)),
                  pl.BlockSpec((None, 6, D), lambda i: (_cond_of_tile(i, TM), 0, 0)),
                  pl.BlockSpec((1, D), lambda i: (0, 0)),
                  pl.BlockSpec((D, OD_N), lambda i: (0, 0), pipeline_mode=pl.Buffered(1)),
                  pl.BlockSpec((TM, RET_DK), lambda i: (_rope_block(i), 0)),
                  pl.BlockSpec((TM, RET_DK), lambda i: (_rope_block(i), 0))] + c_in,
        out_specs=[pl.BlockSpec((TM, OD_N), lambda i: (i, 0))] + c_out,
        out_shape=[jax.ShapeDtypeStruct((T_ALL, OD_N), BF16)] + c_shape,
        compiler_params=_params(1), name="inproj_odd",
    )(x, mods_l, nw, w_bf, cos_t, sin_t, *moe_w)


def _scan_work():
    tf, tb, first, last, seq = [], [], [], [], []
    base = 0
    for s, length in enumerate([CTX_L] * CTX_B + [SMP_L] * SMP_B):
        n = length // TL
        for t in range(n):
            tf.append(base + t)
            tb.append(base + n - 1 - t)
            first.append(int(t == 0))
            last.append(int(t == n - 1))
            seq.append(s)
        base += n
    return tuple(np.asarray(a, np.int32) for a in (tf, tb, first, last, seq))


N_WORK = T_ALL // TL


def _decay_factors(q, k, b, fwd):
    b_last = b[CHUNK - 1:CHUNK] if fwd else b[0:1]
    b_mid = b[CHUNK // 2:CHUNK // 2 + 1]
    qi = q * jnp.exp(b - b_mid)
    ki = k * jnp.exp(b_mid - b)
    q_in = qi * jnp.exp(b_mid)
    kk = ki * jnp.exp(b_last - b_mid)
    return qi, ki, q_in, kk, jnp.exp(b_last)


def _stack(a, b):
    return jnp.concatenate([a, b], axis=0)


def _gla_pair_chunk(q, k, b, v0, v1, st, masks, bd_causal, fwd):
    qi, ki, q_in, kk, g = _decay_factors(q, k, b, fwd)
    m0, m1 = masks
    split = lambda t: _stack(jnp.where(m0, t, 0.0), jnp.where(m1, t, 0.0)).astype(BF16)
    ki_bf = ki.astype(BF16)
    a = jnp.where(bd_causal, _dot_nt(split(qi), _stack(ki_bf, ki_bf)), 0.0)
    vs = _stack(v0, v1)
    o = _dot(a.astype(BF16), vs) + _dot_nt(split(q_in), st.astype(BF16))
    return o, st * g + _dot_tn(vs, split(kk))


def _hgrn_pair_chunk(q0, k0, b0, v0, q1, k1, b1, v1, st, bd_causal, fwd):
    qi0, ki0, qin0, kk0, g0 = _decay_factors(q0, k0, b0, fwd)
    qi1, ki1, qin1, kk1, g1 = _decay_factors(q1, k1, b1, fwd)
    zero = jnp.zeros((CHUNK, LANES), F32)
    wide = lambda t0, t1: _stack(jnp.concatenate([t0, zero], axis=1),
                                 jnp.concatenate([zero, t1], axis=1)).astype(BF16)
    a = jnp.where(bd_causal, _dot_nt(_stack(qi0, qi1).astype(BF16), _stack(ki0, ki1).astype(BF16)), 0.0)
    vs = _stack(v0, v1)
    o = _dot(a.astype(BF16), vs) + _dot_nt(wide(qin0, qin1), st.astype(BF16))
    return o, st * jnp.concatenate([g0, g1], axis=1) + _dot_tn(vs, wide(kk0, kk1))


def _even_gates(d, fg, al, qb, tri, w2_ref, ba_ref, lbv):
    y = _dot(al[...].astype(BF16), w2_ref[d]) + ba_ref[d]
    la_a = (jnp.minimum(y, 0.0) - jnp.log(1.0 + jnp.exp(-jnp.abs(y)))) * (1.0 / GLA_TAU)
    f = fg[...]
    t = jnp.exp(-jnp.abs(f))
    rcp = 1.0 / (1.0 + t)
    sig_f = jnp.where(f >= 0.0, rcp, t * rcp)
    sig_nf = jnp.where(f >= 0.0, t * rcp, rcp)
    la_h = jnp.log(lbv + (1.0 - lbv) * sig_f)
    key_h = (1.0 - lbv) * sig_nf
    la_hi, la_lo = _split_bf16(jnp.concatenate([la_a, la_h], axis=1))
    b_all = _dot(tri, la_hi) + _dot(tri, la_lo)
    return b_all, key_h, _silu(qb[...].astype(F32))


def _even_tile(dir_refs, w2_ref, ba_ref, lb_ref, stg, sth):
    rc = lax.broadcasted_iota(jnp.int32, (2 * CHUNK, 2 * CHUNK), 0)
    cc = lax.broadcasted_iota(jnp.int32, (2 * CHUNK, 2 * CHUNK), 1)
    same_head = (rc // CHUNK) == (cc // CHUNK)
    bd_causal = [same_head & (cc <= rc), same_head & (cc >= rc)]
    rt = lax.broadcasted_iota(jnp.int32, (TL, TL), 0)
    ct = lax.broadcasted_iota(jnp.int32, (TL, TL), 1)
    same = (rt // CHUNK) == (ct // CHUNK)
    tri = [jnp.where(same & (ct <= rt), 1.0, 0.0).astype(BF16),
           jnp.where(same & (ct >= rt), 1.0, 0.0).astype(BF16)]
    lane = lax.broadcasted_iota(jnp.int32, (CHUNK, LANES), 1)
    pair_masks = [lane < GLA_DK, lane >= GLA_DK]
    lbv = lb_ref[...]
    n_pairs = GLA_HEADS // 2
    st_g = [[stg[d, p] for p in range(n_pairs)] for d in range(2)]
    st_h = [[sth[d, p] for p in range(HG_HEADS // 2)] for d in range(2)]
    gates = [_even_gates(d, dir_refs[d][3], dir_refs[d][5], dir_refs[d][2], tri[d], w2_ref, ba_ref, lbv)
             for d in range(2)]

    n_chunks = TL // CHUNK
    for ci in range(n_chunks):
        for d in range(2):
            qk, va, _, _, ib, _, o_ref = dir_refs[d]
            b_all, key_h, qh = gates[d]
            fwd = d == 0
            cpos = ci if fwd else n_chunks - 1 - ci
            rows = slice(cpos * CHUNK, (cpos + 1) * CHUNK)
            for p in range(n_pairs):
                ln = slice(p * LANES, (p + 1) * LANES)
                kl = slice(GLA_QK + p * LANES, GLA_QK + (p + 1) * LANES)
                c0, c1 = 2 * p * GLA_DV, (2 * p + 1) * GLA_DV
                o, st_g[d][p] = _gla_pair_chunk(
                    qk[rows, ln].astype(F32) * (GLA_DK ** -0.5), qk[rows, kl].astype(F32), b_all[rows, ln],
                    va[rows, c0:c0 + GLA_DV], va[rows, c1:c1 + GLA_DV], st_g[d][p],
                    pair_masks, bd_causal[d], fwd)
                o_ref[rows, c0:c0 + GLA_DV] = o[0:CHUNK].astype(BF16)
                o_ref[rows, c1:c1 + GLA_DV] = o[CHUNK:2 * CHUNK].astype(BF16)
            for p in range(HG_HEADS // 2):
                l0 = slice(2 * p * LANES, (2 * p + 1) * LANES)
                l1 = slice((2 * p + 1) * LANES, (2 * p + 2) * LANES)
                bl0 = slice(GLA_QK + l0.start, GLA_QK + l0.stop)
                bl1 = slice(GLA_QK + l1.start, GLA_QK + l1.stop)
                o, st_h[d][p] = _hgrn_pair_chunk(
                    qh[rows, l0], key_h[rows, l0], b_all[rows, bl0], ib[rows, l0],
                    qh[rows, l1], key_h[rows, l1], b_all[rows, bl1], ib[rows, l1],
                    st_h[d][p], bd_causal[d], fwd)
                o_ref[rows, GLA_V + l0.start:GLA_V + l0.stop] = o[0:CHUNK].astype(BF16)
                o_ref[rows, GLA_V + l1.start:GLA_V + l1.stop] = o[CHUNK:2 * CHUNK].astype(BF16)
    for d in range(2):
        for p in range(n_pairs):
            stg[d, p] = st_g[d][p]
        for p in range(HG_HEADS // 2):
            sth[d, p] = st_h[d][p]


def _even_scan_kernel(tf_ref, tb_ref, first_ref, last_ref, seq_ref,
                      qk_f, va_f, qb_f, fg_f, ib_f, al_f,
                      qk_b, va_b, qb_b, fg_b, ib_b, al_b,
                      w2_ref, ba_ref, lb_ref, sg_ref, sh_ref, *rest, n_prev):
    pg_ref, ph_ref = rest[:2] if n_prev else (None, None)
    of_ref, ob_ref, stg_ref, sth_ref, stg, sth = rest[-6:]
    i = pl.program_id(0)
    is_first = first_ref[i] == 1
    is_last = last_ref[i] == 1
    is_ctx = seq_ref[i] < CTX_B

    @pl.when(jnp.logical_and(is_first, is_ctx))
    def _():
        stg[...] = jnp.zeros_like(stg)
        sth[...] = jnp.zeros_like(sth)

    @pl.when(jnp.logical_and(is_first, jnp.logical_not(is_ctx)))
    def _():
        for d in range(2):
            for p in range(GLA_HEADS // 2):
                stg[d, p] = sg_ref[d, p].T
            for h in range(HG_HEADS):
                sth[d, h // 2, :, (h % 2) * HG_DK:(h % 2 + 1) * HG_DK] = sh_ref[d, h].T

    _even_tile([(qk_f, va_f, qb_f, fg_f, ib_f, al_f, of_ref), (qk_b, va_b, qb_b, fg_b, ib_b, al_b, ob_ref)],
               w2_ref, ba_ref, lb_ref, stg, sth)

    @pl.when(jnp.logical_and(is_last, is_ctx))
    def _():
        for k in range(n_prev):
            stg_ref[k] = pg_ref[k]
            sth_ref[k] = ph_ref[k]
        for d in range(2):
            for p in range(GLA_HEADS // 2):
                stg_ref[n_prev, d, p] = stg[d, p].T
            for h in range(HG_HEADS):
                sth_ref[n_prev, d, h] = sth[d, h // 2, :, (h % 2) * HG_DK:(h % 2 + 1) * HG_DK].T


def _even_scan(zm, zg, w2_pad, b_a, lb, s_gla, s_hgrn, prev):
    n_prev = 0 if prev is None else prev[0].shape[1]
    work = _scan_work()

    def zspec(width, col_block, which):
        return pl.BlockSpec((TL, width), lambda i, tf, tb, fi, la, sq: ((tf, tb)[which][i], col_block))

    def dir_specs(which):
        return [zspec(512, EV_QA // 512, which), zspec(512, EV_VA // 512, which),
                zspec(512, EV_QB // 512, which), zspec(512, (EV_FF, EV_FB)[which] // 512, which),
                zspec(512, EV_IB // 512, which), zspec(LANES, EV_AL // LANES, which)]

    def dir_args():
        return [zm, zm, zm, zg, zm, zg]

    def const(shape):
        return pl.BlockSpec(shape, lambda i, *_: (0,) * len(shape))

    def s_in(shape):
        return pl.BlockSpec((None, None) + shape,
                            lambda i, tf, tb, fi, la, sq:
                            (jnp.clip(sq[i] - CTX_B, 0, SMP_B - 1), n_prev) + (0,) * len(shape))

    def ctx_states(n_layers, shape):
        return pl.BlockSpec((None, n_layers) + shape,
                            lambda i, tf, tb, fi, la, sq: (jnp.minimum(sq[i], CTX_B - 1), 0) + (0,) * len(shape))

    gshape = (2, GLA_HEADS // 2, LANES, LANES)
    hshape = (2, HG_HEADS, HG_DK, HG_DV)
    grid_spec = pltpu.PrefetchScalarGridSpec(
        num_scalar_prefetch=5, grid=(N_WORK,),
        in_specs=dir_specs(0) + dir_specs(1) + [
            const((2, LANES, GLA_QK)), const((2, 1, GLA_QK)), const((1, HG_K)),
            s_in(gshape), s_in(hshape)]
        + ([ctx_states(n_prev, gshape), ctx_states(n_prev, hshape)] if n_prev else []),
        out_specs=[pl.BlockSpec((TL, D), lambda i, tf, tb, fi, la, sq: (tf[i], 0)),
                   pl.BlockSpec((TL, D), lambda i, tf, tb, fi, la, sq: (tb[i], 0)),
                   ctx_states(n_prev + 1, gshape), ctx_states(n_prev + 1, hshape)],
        scratch_shapes=[pltpu.VMEM(gshape, F32),
                        pltpu.VMEM((2, HG_HEADS // 2, HG_DV, 2 * HG_DK), F32)])
    return pl.pallas_call(
        functools.partial(_even_scan_kernel, n_prev=n_prev), grid_spec=grid_spec,
        out_shape=[jax.ShapeDtypeStruct((T_ALL, D), BF16), jax.ShapeDtypeStruct((T_ALL, D), BF16),
                   jax.ShapeDtypeStruct((CTX_B, n_prev + 1) + gshape, F32),
                   jax.ShapeDtypeStruct((CTX_B, n_prev + 1) + hshape, F32)],
        compiler_params=_params(1), name="even_scan",
    )(*work, *dir_args(), *dir_args(), w2_pad, b_a, lb, s_gla, s_hgrn, *(prev if n_prev else []))


RET_HP = 2


def _ret_tables():
    lg = np.log1p(-np.exp2(-5.0 - np.arange(RET_HEADS, dtype=np.float64)))
    pos = np.arange(TL, dtype=np.float64)
    dm = np.zeros((2, RET_HEADS, TL, TL))
    rq = np.zeros((2, RET_HEADS, TL, 1))
    rk = np.zeros((2, RET_HEADS, TL, 1))
    gc = np.zeros((2, RET_HEADS, 1, RET_DV))
    diff = pos[:, None] - pos[None, :]
    kscale = RET_DK ** -0.5
    for h in range(RET_HEADS):
        dm[0, h] = np.where(diff >= 0, np.exp(lg[h] * np.maximum(diff, 0)), 0.0) * kscale
        rq[0, h, :, 0] = np.exp(lg[h] * (pos + 1))
        rk[0, h, :, 0] = np.exp(lg[h] * (TL - 1 - pos)) * kscale
        gc[0, h] = np.exp(lg[h] * TL)
        lb = lg[RET_HEADS - 1 - h]
        dm[1, h] = np.where(diff <= 0, np.exp(lb * np.maximum(-diff, 0)), 0.0) * kscale
        rq[1, h, :, 0] = np.exp(lb * (TL - pos))
        rk[1, h, :, 0] = np.exp(lb * pos) * kscale
        gc[1, h] = np.exp(lb * TL)
    return tuple(jnp.asarray(a, F32) for a in (dm, rq, rk, gc))


def _odd_dir(d, q_ref, k_ref, v_ref, dm_ref, rq_ref, rk_ref, gc_ref, st, o_ref, zero_state):
    for hh in range(RET_HP):
        k = k_ref[:, hh * RET_DK:(hh + 1) * RET_DK]
        q = q_ref[:, hh * RET_DK:(hh + 1) * RET_DK]
        v = v_ref[:, hh * RET_DV:(hh + 1) * RET_DV]
        a = _dot_nt(q, k) * dm_ref[d, hh]
        o = _dot(a.astype(BF16), v)
        kk = (k.astype(F32) * rk_ref[d, hh]).astype(BF16)
        s_new = _dot_tn(kk, v)
        if not zero_state:
            s = st[d, hh]
            o = o + rq_ref[d, hh] * _dot(q, s.astype(BF16))
            s_new = gc_ref[d, hh] * s + s_new
        st[d, hh] = s_new
        o_ref[:, hh * RET_DV:(hh + 1) * RET_DV] = o.astype(BF16)


def _odd_scan_kernel(tf_ref, tb_ref, first_ref, last_ref, seq_ref,
                     q_f, k_f, v_f, q_b, k_b, v_b,
                     dm_ref, rq_ref, rk_ref, gc_ref, s0_ref, *rest, n_prev):
    prev_ref = rest[0] if n_prev else None
    of_ref, ob_ref, sout_ref, st = rest[-4:]
    i = pl.program_id(1)
    is_first = first_ref[i] == 1
    is_last = last_ref[i] == 1
    is_ctx = seq_ref[i] < CTX_B

    from_zero = jnp.logical_and(is_first, is_ctx)

    @pl.when(jnp.logical_and(is_first, jnp.logical_not(is_ctx)))
    def _():
        st[...] = s0_ref[...]

    for zero_state in (True, False):
        @pl.when(from_zero if zero_state else jnp.logical_not(from_zero))
        def _():
            _odd_dir(0, q_f, k_f, v_f, dm_ref, rq_ref, rk_ref, gc_ref, st, of_ref, zero_state)
            _odd_dir(1, q_b, k_b, v_b, dm_ref, rq_ref, rk_ref, gc_ref, st, ob_ref, zero_state)

    @pl.when(jnp.logical_and(is_last, is_ctx))
    def _():
        for k in range(n_prev):
            sout_ref[k] = prev_ref[k]
        sout_ref[n_prev] = st[...]


def _odd_scan(z, tables, s_ret, prev):
    n_prev = 0 if prev is None else prev.shape[1]
    work = _scan_work()
    dm, rq, rk, gc = tables
    qw, vw = RET_HP * RET_DK, RET_HP * RET_DV

    def zspec(width, col0, which):
        return pl.BlockSpec((TL, width),
                            lambda hp, i, tf, tb, fi, la, sq: ((tf, tb)[which][i], col0 // width + hp))

    def dir_specs(which):
        return [zspec(qw, OD_Q, which), zspec(qw, OD_K, which), zspec(vw, OD_V, which)]

    def table(shape):
        return pl.BlockSpec((2, RET_HP) + shape, lambda hp, i, *_: (0, hp) + (0,) * len(shape))

    sshape = (2, RET_HP, RET_DK, RET_DV)

    def ctx_states(n_layers):
        return pl.BlockSpec((None, n_layers) + sshape,
                            lambda hp, i, tf, tb, fi, la, sq: (jnp.minimum(sq[i], CTX_B - 1), 0, 0, hp, 0, 0))

    grid_spec = pltpu.PrefetchScalarGridSpec(
        num_scalar_prefetch=5, grid=(RET_HEADS // RET_HP, N_WORK),
        in_specs=dir_specs(0) + dir_specs(1) + [
            table((TL, TL)), table((TL, 1)), table((TL, 1)), table((1, RET_DV)),
            pl.BlockSpec((None, None) + sshape,
                         lambda hp, i, tf, tb, fi, la, sq:
                         (jnp.clip(sq[i] - CTX_B, 0, SMP_B - 1), n_prev, 0, hp, 0, 0))]
        + ([ctx_states(n_prev)] if n_prev else []),
        out_specs=[pl.BlockSpec((TL, vw), lambda hp, i, tf, tb, fi, la, sq: (tf[i], hp)),
                   pl.BlockSpec((TL, vw), lambda hp, i, tf, tb, fi, la, sq: (tb[i], hp)),
                   ctx_states(n_prev + 1)],
        scratch_shapes=[pltpu.VMEM(sshape, F32)])
    return pl.pallas_call(
        functools.partial(_odd_scan_kernel, n_prev=n_prev), grid_spec=grid_spec,
        out_shape=[jax.ShapeDtypeStruct((T_ALL, RET_V), BF16), jax.ShapeDtypeStruct((T_ALL, RET_V), BF16),
                   jax.ShapeDtypeStruct((CTX_B, n_prev + 1, 2, RET_HEADS, RET_DK, RET_DV), F32)],
        compiler_params=_params(2), name="odd_scan",
    )(*work, z, z, z, z, z, z, dm, rq, rk, gc, s_ret, *([prev] if n_prev else []))


def _head_rms(o, width):
    parts = []
    for j in range(0, o.shape[1], width):
        blk = o[:, j:j + width]
        parts.append(blk * lax.rsqrt(jnp.mean(blk * blk, axis=-1, keepdims=True) + NORM_EPS))
    return parts


def _route(h2, wr_hi_ref, wr_lo_ref, br_ref):
    h_hi, h_lo = _split_bf16(h2)
    logits = (_dot(h_hi, wr_hi_ref[...]) + _dot(h_hi, wr_lo_ref[...]) + _dot(h_lo, wr_hi_ref[...])
              + br_ref[...])
    lane_i = lax.broadcasted_iota(jnp.int32, logits.shape, 1)
    lane = lane_i.astype(F32)
    neg = -jnp.inf
    big = 1e9
    is_grp = (lane_i >= N_EXPERTS) & (lane_i < N_EXPERTS + N_GROUPS)
    lg = jnp.where(is_grp, logits, neg)
    mg = jnp.max(lg, axis=-1, keepdims=True)
    gsel = jnp.min(jnp.where(lg == mg, lane - N_EXPERTS, big), axis=-1, keepdims=True)
    pg = 1.0 / jnp.sum(jnp.where(is_grp, jnp.exp(lg - mg), 0.0), axis=-1, keepdims=True)
    in_grp = (lane_i < N_EXPERTS) & ((lane_i // EXP_PER_GROUP).astype(F32) == gsel)
    le = jnp.where(in_grp, logits, neg)
    m1 = jnp.max(le, axis=-1, keepdims=True)
    i1 = jnp.min(jnp.where(le == m1, lane, big), axis=-1, keepdims=True)
    le2 = jnp.where(lane == i1, neg, le)
    m2 = jnp.max(le2, axis=-1, keepdims=True)
    i2 = jnp.min(jnp.where(le2 == m2, lane, big), axis=-1, keepdims=True)
    e2 = jnp.exp(m2 - m1)
    w1 = pg / (1.0 + e2)
    w2 = pg * e2 / (1.0 + e2)
    first_lo = i1 < i2
    ia = jnp.minimum(i1, i2) - gsel * EXP_PER_GROUP
    ib = jnp.maximum(i1, i2) - gsel * EXP_PER_GROUP
    pair = gsel * PAIRS_PER_GROUP + ia * (2 * EXP_PER_GROUP - 1 - ia) * 0.5 + (ib - ia - 1.0)
    return pair, jnp.where(first_lo, w1, w2), jnp.where(first_lo, w2, w1)


def _post_tail(i, x, out, m, nw2_ref, wr_hi_ref, wr_lo_ref, br_ref,
               x1_ref, hrow_ref, meta_ref, cnt_ref, carry):
    x1 = x + m[2:3] * out
    x1_ref[...] = x1
    h2 = _norm_mod(x1, nw2_ref[...], m[3:4], m[4:5])
    pair, w_a, w_b = _route(h2, wr_hi_ref, wr_lo_ref, br_ref)

    @pl.when(i == 0)
    def _():
        carry[...] = jnp.zeros_like(carry)

    lane = lax.broadcasted_iota(jnp.int32, (TM, LANES), 1)
    onehot = jnp.where(lane.astype(F32) == pair, 1.0, 0.0)
    r = lax.broadcasted_iota(jnp.int32, (TM, TM), 0)
    c = lax.broadcasted_iota(jnp.int32, (TM, TM), 1)
    earlier = jnp.where(c < r, 1.0, 0.0).astype(BF16)
    before = _dot(earlier, onehot.astype(BF16)) + carry[...]
    rank = jnp.sum(onehot * before, axis=-1, keepdims=True)
    carry[...] += jnp.sum(onehot, axis=0, keepdims=True)
    cnt_ref[...] = carry[...]
    meta_ref[...] = jnp.where(lane == 0, pair, jnp.where(lane == 1, rank, 0.0))
    hrow_ref[:, 0:D] = h2
    hrow_ref[:, D:ROW_W] = jnp.where(lane == 0, w_a, jnp.where(lane == 1, w_b, 0.0))


def _post_even_kernel(*refs, n_x):
    x_refs = refs[:n_x]
    (of_ref, ob_ref, ra_ref, gb_ref, mod_ref, gn_ref, wo_ref, nw2_ref, wr_hi_ref, wr_lo_ref, br_ref,
     x1_ref, hrow_ref, meta_ref, cnt_ref, carry) = refs[n_x:]
    o = of_ref[...].astype(F32) + ob_ref[...].astype(F32)
    normed = jnp.concatenate(_head_rms(o, GLA_DV), axis=1)
    gate = jnp.concatenate([_silu(ra_ref[...].astype(F32)), _silu(gb_ref[...].astype(F32))], axis=1)
    mixed = (normed * gn_ref[...] * gate).astype(BF16)
    out = _dot(mixed, wo_ref[...])
    _post_tail(pl.program_id(0), _load_x(x_refs), out, mod_ref[...], nw2_ref, wr_hi_ref, wr_lo_ref, br_ref,
               x1_ref, hrow_ref, meta_ref, cnt_ref, carry)


def _post_odd_kernel(of_ref, ob_ref, g_ref, x_ref, mod_ref, wo_ref,
                     nw2_ref, wr_hi_ref, wr_lo_ref, br_ref,
                     x1_ref, hrow_ref, meta_ref, cnt_ref, carry):
    o = of_ref[...].astype(F32) + ob_ref[...].astype(F32)
    normed = jnp.concatenate(_head_rms(o, RET_DV), axis=1)
    mixed = (normed * _silu(g_ref[...].astype(F32))).astype(BF16)
    out = _dot(mixed, wo_ref[...])
    _post_tail(pl.program_id(0), x_ref[...], out, mod_ref[...], nw2_ref, wr_hi_ref, wr_lo_ref, br_ref,
               x1_ref, hrow_ref, meta_ref, cnt_ref, carry)


def _post_common_specs():
    tile = lambda w: pl.BlockSpec((TM, w), lambda i: (i, 0))
    const = lambda s: pl.BlockSpec(s, lambda i: (0,) * len(s))
    mod = pl.BlockSpec((None, 6, D), lambda i: (_cond_of_tile(i, TM), 0, 0))
    out_specs = [tile(D), tile(ROW_W), tile(LANES), const((1, LANES))]
    out_shape = [jax.ShapeDtypeStruct((T_ALL, D), F32),
                 jax.ShapeDtypeStruct((T_ALL, ROW_W), F32),
                 jax.ShapeDtypeStruct((T_ALL, LANES), F32),
                 jax.ShapeDtypeStruct((1, LANES), F32)]
    scratch = [pltpu.VMEM((1, LANES), F32)]
    return tile, const, mod, out_specs, out_shape, scratch


def _post_even(o_f, o_b, z, x_parts, mods_l, gn, wo_bf, nw2, wr_hi, wr_lo, br):
    tile, const, mod, out_specs, out_shape, scratch = _post_common_specs()
    zcol = lambda c0: pl.BlockSpec((TM, 512), lambda i: (i, c0 // 512))
    return pl.pallas_call(
        functools.partial(_post_even_kernel, n_x=len(x_parts)), grid=(T_ALL // TM,),
        in_specs=_x_specs(x_parts) + [
                  tile(D), tile(D), zcol(EV_RA), zcol(EV_GB), mod, const((1, D)),
                  const((D, D)), const((1, D)), const((D, ROUTER_N)), const((D, ROUTER_N)),
                  const((1, ROUTER_N))],
        out_specs=out_specs, out_shape=out_shape, scratch_shapes=scratch,
        compiler_params=_params(1), name="post_even",
    )(*x_parts, o_f, o_b, z, z, mods_l, gn, wo_bf, nw2, wr_hi, wr_lo, br)


def _post_odd(o_f, o_b, z, x, mods_l, wo_bf, nw2, wr_hi, wr_lo, br):
    tile, const, mod, out_specs, out_shape, scratch = _post_common_specs()
    return pl.pallas_call(
        _post_odd_kernel, grid=(T_ALL // TM,),
        in_specs=[tile(RET_V), tile(RET_V),
                  pl.BlockSpec((TM, RET_V), lambda i: (i, OD_G // RET_V)), tile(D), mod,
                  const((RET_V, D)), const((1, D)), const((D, ROUTER_N)), const((D, ROUTER_N)),
                  const((1, ROUTER_N))],
        out_specs=out_specs, out_shape=out_shape, scratch_shapes=scratch,
        compiler_params=_params(1), name="post_odd",
    )(o_f, o_b, z, x, mods_l, wo_bf, nw2, wr_hi, wr_lo, br)


def _pair_tables():
    ea, eb = [], []
    for g in range(N_GROUPS):
        for a in range(EXP_PER_GROUP):
            for b in range(a + 1, EXP_PER_GROUP):
                ea.append(g * EXP_PER_GROUP + a)
                eb.append(g * EXP_PER_GROUP + b)
    pad = LANES - len(ea)
    return (jnp.asarray(ea + [0] * pad, jnp.int32), jnp.asarray(eb + [0] * pad, jnp.int32))


def _dispatch_plan(meta, counts):
    pair = meta[:, 0].astype(jnp.int32)
    rank = meta[:, 1].astype(jnp.int32)
    cnt = counts[0].astype(jnp.int32)
    tiles_g = (cnt + TMG - 1) // TMG
    tile_end = jnp.cumsum(tiles_g)
    tile_start = tile_end - tiles_g
    slot0 = tile_start * TMG
    ids = jnp.arange(LANES, dtype=jnp.int32)
    dest = rank + jnp.sum(jnp.where(pair[:, None] == ids[None, :], slot0[None, :], 0), axis=1)
    n_real = tile_end[-1]
    j = jnp.arange(N_TILES_MOE, dtype=jnp.int32)
    jj = jnp.minimum(j, n_real - 1)
    grp = jnp.sum((jj[:, None] >= tile_end[None, :]).astype(jnp.int32), axis=1)
    ea_t, eb_t = _pair_tables()
    onehot_g = grp[:, None] == ids[None, :]
    pick = lambda v: jnp.sum(jnp.where(onehot_g, v[None, :], 0), axis=1)
    nv = jnp.clip(pick(cnt) - (jj - pick(tile_start)) * TMG, 0, TMG)
    nv = jnp.where(j < n_real, nv, 0)
    return dest // SUBLANES, dest % SUBLANES, pick(ea_t), pick(eb_t), nv, n_real.reshape(1)


DISPATCH_STEPS = T_ALL // DISPATCH_ROWS


def _dispatch_kernel(dhi_ref, dlo_ref, nv_ref, h_ref, hs_hbm, zeros, sem, pad_sem):
    j = pl.program_id(0)
    base = j * DISPATCH_ROWS
    tile_sub = TMG // SUBLANES

    def pad_tile(tile):
        return pltpu.make_async_copy(zeros, hs_hbm.at[pl.ds(tile * tile_sub, tile_sub)], pad_sem.at[0])

    @pl.when(j == 0)
    def _():
        zeros[...] = jnp.zeros_like(zeros)

        def fill(tile, carry):
            @pl.when(nv_ref[tile] < TMG)
            def _():
                pad_tile(tile).start()
            return carry

        def drain(tile, carry):
            @pl.when(nv_ref[tile] < TMG)
            def _():
                pad_tile(tile).wait()
            return carry

        lax.fori_loop(0, N_TILES_MOE, fill, 0)
        lax.fori_loop(0, N_TILES_MOE, drain, 0)

    def issue(g, carry):
        for k in range(SUBLANES):
            t = base + g * SUBLANES + k
            pltpu.make_async_copy(h_ref.at[g, pl.ds(k, 1)], hs_hbm.at[dhi_ref[t], pl.ds(dlo_ref[t], 1)],
                                  sem.at[0]).start()
        return carry

    lax.fori_loop(0, DISPATCH_ROWS // SUBLANES, issue, 0)
    pltpu.make_async_copy(h_ref, hs_hbm.at[pl.ds(0, DISPATCH_ROWS // SUBLANES)], sem.at[0]).wait()


def _dispatch(dest_hi, dest_lo, nv, hrow):
    grid_spec = pltpu.PrefetchScalarGridSpec(
        num_scalar_prefetch=3, grid=(DISPATCH_STEPS,),
        in_specs=[pl.BlockSpec((DISPATCH_ROWS // SUBLANES, SUBLANES, ROW_W), lambda j, *_: (j, 0, 0))],
        out_specs=pl.BlockSpec(memory_space=pl.ANY),
        scratch_shapes=[pltpu.VMEM((TMG // SUBLANES, SUBLANES, ROW_W), F32),
                        pltpu.SemaphoreType.DMA((1,)), pltpu.SemaphoreType.DMA((1,))])
    hs = pl.pallas_call(
        _dispatch_kernel, grid_spec=grid_spec,
        out_shape=jax.ShapeDtypeStruct((N_TILES_MOE * TMG // SUBLANES, SUBLANES, ROW_W), F32),
        compiler_params=_params(1), name="moe_dispatch",
    )(dest_hi, dest_lo, nv, hrow.reshape(T_ALL // SUBLANES, SUBLANES, ROW_W))
    return hs.reshape(N_TILES_MOE * TMG, ROW_W)


def _moe_kernel(ea_ref, eb_ref, nreal_ref, hs_ref,
                wga_ref, wua_ref, wda_ref, wgb_ref, wub_ref, wdb_ref, ys_ref):
    j = pl.program_id(0)

    @pl.when(j < nreal_ref[0])
    def _():
        h = hs_ref[:, 0:D].astype(BF16)
        wrow = hs_ref[:, D:ROW_W]
        w_a, w_b = wrow[:, 0:1], wrow[:, 1:2]
        act_a = _silu(_dot(h, wga_ref[0])) * _dot(h, wua_ref[0]) * w_a
        act_b = _silu(_dot(h, wgb_ref[0])) * _dot(h, wub_ref[0]) * w_b
        ys_ref[...] = _dot(act_a.astype(BF16), wda_ref[0]) + _dot(act_b.astype(BF16), wdb_ref[0])

    @pl.when(j >= nreal_ref[0])
    def _():
        ys_ref[...] = jnp.zeros_like(ys_ref)


def _moe(ea, eb, n_real, hs, wg_bf, wu_bf, wd_bf):
    def wspec(shape, which):
        return pl.BlockSpec((1,) + shape, lambda j, ea, eb, nr: ((ea, eb)[which][j], 0, 0))

    up, down = (D, D_EXPERT), (D_EXPERT, D)
    grid_spec = pltpu.PrefetchScalarGridSpec(
        num_scalar_prefetch=3, grid=(N_TILES_MOE,),
        in_specs=[pl.BlockSpec((TMG, ROW_W), lambda j, ea, eb, nr: (jnp.minimum(j, nr[0] - 1), 0)),
                  wspec(up, 0), wspec(up, 0), wspec(down, 0),
                  wspec(up, 1), wspec(up, 1), wspec(down, 1)],
        out_specs=pl.BlockSpec((TMG, D), lambda j, ea, eb, nr: (j, 0)))
    return pl.pallas_call(
        _moe_kernel, grid_spec=grid_spec,
        out_shape=jax.ShapeDtypeStruct((N_TILES_MOE * TMG, D), F32),
        compiler_params=_params(1), name="moe",
    )(ea, eb, n_real, hs, wg_bf, wu_bf, wd_bf, wg_bf, wu_bf, wd_bf)


def _combine_kernel(dhi_ref, dlo_ref, ys_hbm, x1_ref, mod_ref, nwf_ref, o_ref, buf, sem, *, tile0, final):
    j = pl.program_id(0)
    n = pl.num_programs(0)

    def gather(tile, s):
        def issue(g, carry):
            for k in range(SUBLANES):
                t = (tile0 + tile) * TM + g * SUBLANES + k
                pltpu.make_async_copy(ys_hbm.at[dhi_ref[t], pl.ds(dlo_ref[t], 1)],
                                      buf.at[s, g, pl.ds(k, 1)], sem.at[s]).start()
            return carry
        lax.fori_loop(0, TM // SUBLANES, issue, 0)

    @pl.when(j == 0)
    def _():
        gather(0, 0)

    for s in range(2):
        @pl.when(jnp.logical_and(j + 1 < n, (j + 1) % 2 == s))
        def _():
            gather(j + 1, s)

    for s in range(2):
        @pl.when(j % 2 == s)
        def _():
            pltpu.make_async_copy(ys_hbm.at[pl.ds(0, TM // SUBLANES)], buf.at[s], sem.at[s]).wait()
            x = x1_ref[...] + mod_ref[5:6] * buf[s].reshape(TM, D)
            if final:
                var = jnp.mean(x * x, axis=-1, keepdims=True)
                x = x * lax.rsqrt(var + NORM_EPS) * nwf_ref[...]
            o_ref[...] = x


def _combine(dest_hi, dest_lo, ys, x1, mods_l, nwf, tile0=0, n_tok=T_ALL, final=False):
    grid_spec = pltpu.PrefetchScalarGridSpec(
        num_scalar_prefetch=2, grid=(n_tok // TM,),
        in_specs=[pl.BlockSpec(memory_space=pl.ANY),
                  pl.BlockSpec((TM, D), lambda i, *_: (tile0 + i, 0)),
                  pl.BlockSpec((None, 6, D), lambda i, *_: (_cond_of_tile(tile0 + i, TM), 0, 0)),
                  pl.BlockSpec((1, D), lambda i, *_: (0, 0))],
        out_specs=pl.BlockSpec((TM, D), lambda i, *_: (i, 0)),
        scratch_shapes=[pltpu.VMEM((2, TM // SUBLANES, SUBLANES, D), F32), pltpu.SemaphoreType.DMA((2,))])
    return pl.pallas_call(
        functools.partial(_combine_kernel, tile0=tile0, final=final), grid_spec=grid_spec,
        out_shape=jax.ShapeDtypeStruct((n_tok, D), F32),
        compiler_params=_params(1), name="moe_combine",
    )(dest_hi, dest_lo, ys.reshape(N_TILES_MOE * TMG // SUBLANES, SUBLANES, D), x1, mods_l, nwf)


def _rope_tables():
    freqs = ROPE_BASE ** (-jnp.arange(ROPE_PAIRS, dtype=F32) / ROPE_PAIRS)
    t = jnp.arange(SMP_L)
    halves_c, halves_s = [], []
    for p in (t // GRID_W, t % GRID_W):
        ang = p.astype(F32)[:, None] * freqs
        cs, sn = jnp.cos(ang), jnp.sin(ang)
        halves_c += [cs, cs]
        halves_s += [-sn, sn]
    cos_t = jnp.concatenate(halves_c, axis=1)
    sin_t = jnp.concatenate(halves_s, axis=1)
    cos_t = jnp.concatenate([jnp.ones((TM, RET_DK), F32), cos_t], axis=0)
    sin_t = jnp.concatenate([jnp.zeros((TM, RET_DK), F32), sin_t], axis=0)
    return cos_t, sin_t


def kernel(x_prompt, x_sample, state_gla, state_hgrn, state_ret, c, c_ctx, norm1_w, norm2_w, normf_w,
           w_mod, b_mod, w_in_even, gla_w_alpha, gla_b_alpha, hgrn_lb_logits, gla_norm_w, hgrn_norm_w,
           w_out_even, w_in_odd, w_out_odd, router_g_w, router_g_b, router_e_w, router_e_b,
           moe_w_gate, moe_w_up, moe_w_down):
    x_parts = [x_prompt.reshape(T_CTX, D), x_sample.reshape(T_SMP, D)]
    cond8 = jnp.concatenate([c_ctx[None, :], c, jnp.zeros((COND_ROWS - N_COND, D), F32)], axis=0)
    mods = _mods(cond8, w_mod, b_mod).reshape(DEPTH, COND_ROWS, 6, D)
    lb_all = jnp.cumsum(jax.nn.softmax(hgrn_lb_logits.astype(F32), axis=0), axis=0)[:N_EVEN]
    cos_t, sin_t = _rope_tables()
    ret_tables = _ret_tables()

    nwf = normf_w.reshape(1, D)
    moe_w = (moe_w_gate, moe_w_up, moe_w_down)
    st_even, st_ret = None, None
    for l in range(DEPTH):
        mods_l = mods[l]
        nw1 = norm1_w[l].reshape(1, D)
        nw2 = norm2_w[l].reshape(1, D)
        wr = jnp.concatenate([router_e_w[l], router_g_w[l],
                              jnp.zeros((D, ROUTER_N - N_EXPERTS - N_GROUPS), F32)], axis=1)
        wr_hi = wr.astype(BF16)
        wr_lo = (wr - wr_hi.astype(F32)).astype(BF16)
        br = jnp.concatenate([router_e_b[l], router_g_b[l],
                              jnp.zeros((ROUTER_N - N_EXPERTS - N_GROUPS,), F32)]).reshape(1, ROUTER_N)
        if l % 2 == 0:
            e = l // 2
            w = w_in_even[e]
            a0 = 2 * GLA_QK + 2 * GLA_V
            a1 = a0 + 2 * GLA_RANK
            f0, f1 = a1 + HG_K, a1 + 3 * HG_K
            wm_bf = jnp.concatenate([w[:, :a0], w[:, a1:f0], w[:, f1:]], axis=1).astype(BF16)
            wg_bf = jnp.concatenate([w[:, f0:f1], w[:, a0:a1],
                                     jnp.zeros((D, LANES - 2 * GLA_RANK), F32)], axis=1).astype(BF16)
            zm, zg, *moe_bf = _inproj_even(x_parts, mods_l, nw1, wm_bf, wg_bf, moe_w, l)
            w2_pad = jnp.zeros((2, LANES, GLA_QK), F32)
            for d in range(2):
                w2_pad = w2_pad.at[d, d * GLA_RANK:(d + 1) * GLA_RANK].set(gla_w_alpha[e, d])
            s_g = state_gla.reshape(SMP_B, N_EVEN, 2, GLA_HEADS // 2, LANES, LANES)
            o_f, o_b, ng, nh = _even_scan(zm, zg, w2_pad.astype(BF16), gla_b_alpha[e].reshape(2, 1, GLA_QK),
                                          lb_all[e].reshape(1, HG_K), s_g, state_hgrn, st_even)
            st_even = (ng, nh)
            gn = jnp.concatenate([gla_norm_w[e], hgrn_norm_w[e]]).reshape(1, D)
            x1, hrow, meta, counts = _post_even(o_f, o_b, zm, x_parts, mods_l, gn, w_out_even[e].astype(BF16),
                                                nw2, wr_hi, wr_lo, br)
        else:
            j = l // 2
            z, *moe_bf = _inproj_odd(x_parts[0], mods_l, nw1, w_in_odd[j].astype(BF16), cos_t, sin_t,
                                     moe_w, l)
            o_f, o_b, st_ret = _odd_scan(z, ret_tables, state_ret, st_ret)
            x1, hrow, meta, counts = _post_odd(o_f, o_b, z, x_parts[0], mods_l, w_out_odd[j].astype(BF16),
                                               nw2, wr_hi, wr_lo, br)
        dhi, dlo, ea, eb, nv, n_real = _dispatch_plan(meta, counts)
        hs = _dispatch(dhi, dlo, nv, hrow)
        ys = _moe(ea, eb, n_real, hs, *moe_bf)
        if l < DEPTH - 1:
            x_parts = [_combine(dhi, dlo, ys, x1, mods_l, nwf)]
    y_prompt = _combine(dhi, dlo, ys, x1, mods_l, nwf, 0, T_CTX, final=True).reshape(CTX_B, CTX_L, D)
    y_sample = _combine(dhi, dlo, ys, x1, mods_l, nwf, T_CTX // TM, T_SMP,
                        final=True).reshape(SMP_B, SMP_L, D)
    st_gla = st_even[0].reshape(CTX_B, N_EVEN, 2, GLA_HEADS, GLA_DK, GLA_DV)
    return (y_prompt, y_sample, st_gla, st_even[1], st_ret)
```

```python
import functools

import numpy as np
import jax
import jax.numpy as jnp
from jax import lax
from jax.experimental import pallas as pl
from jax.experimental.pallas import tpu as pltpu

F32 = jnp.float32
BF16 = jnp.bfloat16

D = 1024
CTX_B, CTX_L = 32, 256
SMP_B, SMP_L = 4, 4096
DEPTH = 4
N_EVEN, N_ODD = 2, 2
GRID_W = 64
NORM_EPS = 1e-6
T_CTX = CTX_B * CTX_L
T_SMP = SMP_B * SMP_L
T_ALL = T_CTX + T_SMP
N_COND = 1 + SMP_B
COND_ROWS = 8

GLA_HEADS, GLA_DK, GLA_DV, GLA_RANK, GLA_TAU = 4, 64, 128, 16, 16.0
GLA_QK, GLA_V = GLA_HEADS * GLA_DK, GLA_HEADS * GLA_DV
HG_HEADS, HG_DK, HG_DV = 4, 128, 128
HG_K, HG_V = HG_HEADS * HG_DK, HG_HEADS * HG_DV
RET_HEADS, RET_DK, RET_DV = 4, 256, 512
RET_QK, RET_V = RET_HEADS * RET_DK, RET_HEADS * RET_DV
ROPE_BASE, ROPE_PAIRS = 10000.0, 64
N_GROUPS, EXP_PER_GROUP, N_EXPERTS, D_EXPERT = 4, 8, 32, 256

LANES = 128
CHUNK = 64
TL = 256
TM = 512
PAIRS_PER_GROUP = EXP_PER_GROUP * (EXP_PER_GROUP - 1) // 2
N_PAIRS = N_GROUPS * PAIRS_PER_GROUP
TMG = 256
N_TILES_MOE = T_ALL // TMG + N_PAIRS
ROW_W = D + LANES
DISPATCH_ROWS = 2048
SUBLANES = 8
VMEM_LIMIT = 56 * 1024 * 1024

EV_QA, EV_KA, EV_VA, EV_RA, EV_QB, EV_IB, EV_GB = 0, 256, 512, 1024, 1536, 2048, 2560
EV_MAIN = 3072
EV_FF, EV_FB, EV_AL = 0, 512, 1024
EV_GATE = 1024 + LANES
OD_Q, OD_K, OD_V, OD_G = 0, 1024, 2048, 4096
OD_N = 6144
ROUTER_N = LANES


def _dot(a, b):
    return jnp.dot(a, b, preferred_element_type=F32)


def _dot_nt(a, b):
    return lax.dot_general(a, b, (((1,), (1,)), ((), ())), preferred_element_type=F32)


def _dot_tn(a, b):
    return lax.dot_general(a, b, (((0,), (0,)), ((), ())), preferred_element_type=F32)


def _split_bf16(x):
    hi = x.astype(BF16)
    lo = (x - hi.astype(F32)).astype(BF16)
    return hi, lo


def _sigmoid(x):
    return 1.0 / (1.0 + jnp.exp(-x))


def _silu(x):
    return x * _sigmoid(x)


def _params(n_axes):
    return pltpu.CompilerParams(dimension_semantics=("arbitrary",) * n_axes,
                                vmem_limit_bytes=VMEM_LIMIT)


def _cond_of_tile(i, tm):
    n_ctx = T_CTX // tm
    per_seq = SMP_L // tm
    return jnp.where(i < n_ctx, 0, 1 + (i - n_ctx) // per_seq)


def _norm_mod(x, nw, shift, scale):
    var = jnp.mean(x * x, axis=-1, keepdims=True)
    return x * lax.rsqrt(var + NORM_EPS) * nw * (1.0 + scale) + shift


def _mods_kernel(cond_ref, w_ref, b_ref, o_ref):
    c = cond_ref[...]
    a_hi, a_lo = _split_bf16(_silu(c))
    w_hi, w_lo = _split_bf16(w_ref[0])
    o_ref[0] = _dot(a_hi, w_hi) + _dot(a_hi, w_lo) + _dot(a_lo, w_hi) + b_ref[0]


def _mods(cond8, w_mod, b_mod):
    return pl.pallas_call(
        _mods_kernel,
        grid=(DEPTH, 6),
        in_specs=[pl.BlockSpec((COND_ROWS, D), lambda l, j: (0, 0)),
                  pl.BlockSpec((1, D, D), lambda l, j: (l, 0, j)),
                  pl.BlockSpec((1, 1, D), lambda l, j: (l, 0, j))],
        out_specs=pl.BlockSpec((1, COND_ROWS, D), lambda l, j: (l, 0, j)),
        out_shape=jax.ShapeDtypeStruct((DEPTH, COND_ROWS, 6 * D), F32),
        compiler_params=_params(2), name="mods",
    )(cond8, w_mod, b_mod.reshape(DEPTH, 1, 6 * D))


def _x_specs(x_parts):
    if len(x_parts) == 1:
        return [pl.BlockSpec((TM, D), lambda i: (i, 0))]
    n_ctx = T_CTX // TM
    return [pl.BlockSpec((TM, D), lambda i: (jnp.minimum(i, n_ctx - 1), 0)),
            pl.BlockSpec((TM, D), lambda i: (jnp.maximum(i - n_ctx, 0), 0))]


def _load_x(x_refs):
    if len(x_refs) == 1:
        return x_refs[0][...]
    return jnp.where(pl.program_id(0) < T_CTX // TM, x_refs[0][...], x_refs[1][...])


def _expert_cast_specs(layer):
    assert T_ALL // TM >= N_EXPERTS
    shapes = ((D, D_EXPERT), (D, D_EXPERT), (D_EXPERT, D))
    expert = lambda i: jnp.minimum(i, N_EXPERTS - 1)
    in_specs = [pl.BlockSpec((None, 1) + s, lambda i: (layer, expert(i), 0, 0)) for s in shapes]
    out_specs = [pl.BlockSpec((1,) + s, lambda i: (expert(i), 0, 0)) for s in shapes]
    out_shape = [jax.ShapeDtypeStruct((N_EXPERTS,) + s, BF16) for s in shapes]
    return in_specs, out_specs, out_shape


def _cast_expert(src_refs, dst_refs):
    @pl.when(pl.program_id(0) < N_EXPERTS)
    def _():
        for src, dst in zip(src_refs, dst_refs):
            dst[...] = src[...].astype(BF16)


def _inproj_even_kernel(*refs, n_x):
    x_refs = refs[:n_x]
    mod_ref, nw_ref, wm_ref, wg_ref = refs[n_x:n_x + 4]
    zm_ref, zg_ref = refs[n_x + 7:n_x + 9]
    _cast_expert(refs[n_x + 4:n_x + 7], refs[n_x + 9:])
    m = mod_ref[...]
    h = _norm_mod(_load_x(x_refs), nw_ref[...], m[0:1], m[1:2]).astype(BF16)
    for j in range(0, EV_MAIN, 1024):
        zm_ref[:, j:j + 1024] = _dot(h, wm_ref[:, j:j + 1024]).astype(BF16)
    for j, width in ((0, 1024), (EV_AL, LANES)):
        zg_ref[:, j:j + width] = _dot(h, wg_ref[:, j:j + width])


def _inproj_odd_kernel(x_ref, mod_ref, nw_ref, w_ref, cos_ref, sin_ref, eg_ref, eu_ref, ed_ref,
                       z_ref, og_ref, ou_ref, od_ref):
    _cast_expert((eg_ref, eu_ref, ed_ref), (og_ref, ou_ref, od_ref))
    m = mod_ref[...]
    h = _norm_mod(x_ref[...], nw_ref[...], m[0:1], m[1:2]).astype(BF16)
    for j in range(0, OD_V, 2 * LANES):
        zz = _dot(h, w_ref[:, j:j + 2 * LANES])
        for s in range(2):
            blk = zz[:, s * LANES:(s + 1) * LANES]
            cs = cos_ref[:, s * LANES:(s + 1) * LANES]
            sn = sin_ref[:, s * LANES:(s + 1) * LANES]
            z_ref[:, j + s * LANES:j + (s + 1) * LANES] = (
                blk * cs + pltpu.roll(blk, ROPE_PAIRS, 1) * sn).astype(BF16)
    step = 1024
    for j in range(OD_V, OD_N, step):
        z_ref[:, j:j + step] = _dot(h, w_ref[:, j:j + step]).astype(BF16)


def _inproj_even(x_parts, mods_l, nw, wm_bf, wg_bf, moe_w, layer):
    c_in, c_out, c_shape = _expert_cast_specs(layer)
    return pl.pallas_call(
        functools.partial(_inproj_even_kernel, n_x=len(x_parts)),
        grid=(T_ALL // TM,),
        in_specs=_x_specs(x_parts) + [
                  pl.BlockSpec((None, 6, D), lambda i: (_cond_of_tile(i, TM), 0, 0)),
                  pl.BlockSpec((1, D), lambda i: (0, 0)),
                  pl.BlockSpec((D, EV_MAIN), lambda i: (0, 0)),
                  pl.BlockSpec((D, EV_GATE), lambda i: (0, 0))] + c_in,
        out_specs=[pl.BlockSpec((TM, EV_MAIN), lambda i: (i, 0)),
                   pl.BlockSpec((TM, EV_GATE), lambda i: (i, 0))] + c_out,
        out_shape=[jax.ShapeDtypeStruct((T_ALL, EV_MAIN), BF16),
                   jax.ShapeDtypeStruct((T_ALL, EV_GATE), F32)] + c_shape,
        compiler_params=_params(1), name="inproj_even",
    )(*x_parts, mods_l, nw, wm_bf, wg_bf, *moe_w)


def _rope_block(i):
    n_ctx = T_CTX // TM
    per_seq = SMP_L // TM
    return jnp.where(i < n_ctx, 0, 1 + (i - n_ctx) % per_seq)


def _inproj_odd(x, mods_l, nw, w_bf, cos_t, sin_t, moe_w, layer):
    c_in, c_out, c_shape = _expert_cast_specs(layer)
    return pl.pallas_call(
        _inproj_odd_kernel,
        grid=(T_ALL // TM,),
        in_specs=[pl.BlockSpec((TM, D), lambda i: (i, 0)),
                  pl.BlockSpec((None, 6, D), lambda i: (_cond_of_tile(i, TM), 0, 0)),
                  pl.BlockSpec((1, D), lambda i: (0, 0)),
                  pl.BlockSpec((D, OD_N), lambda i: (0, 0), pipeline_mode=pl.Buffered(1)),
                  pl.BlockSpec((TM, RET_DK), lambda i: (_rope_block(i), 0)),
                  pl.BlockSpec((TM, RET_DK), lambda i: (_rope_block(i), 0))] + c_in,
        out_specs=[pl.BlockSpec((TM, OD_N), lambda i: (i, 0))] + c_out,
        out_shape=[jax.ShapeDtypeStruct((T_ALL, OD_N), BF16)] + c_shape,
        compiler_params=_params(1), name="inproj_odd",
    )(x, mods_l, nw, w_bf, cos_t, sin_t, *moe_w)


def _scan_work():
    tf, tb, first, last, seq = [], [], [], [], []
    base = 0
    for s, length in enumerate([CTX_L] * CTX_B + [SMP_L] * SMP_B):
        n = length // TL
        for t in range(n):
            tf.append(base + t)
            tb.append(base + n - 1 - t)
            first.append(int(t == 0))
            last.append(int(t == n - 1))
            seq.append(s)
        base += n
    return tuple(np.asarray(a, np.int32) for a in (tf, tb, first, last, seq))


N_WORK = T_ALL // TL


def _decay_factors(q, k, b, fwd):
    b_last = b[CHUNK - 1:CHUNK] if fwd else b[0:1]
    b_mid = b[CHUNK // 2:CHUNK // 2 + 1]
    qi = q * jnp.exp(b - b_mid)
    ki = k * jnp.exp(b_mid - b)
    q_in = qi * jnp.exp(b_mid)
    kk = ki * jnp.exp(b_last - b_mid)
    return qi, ki, q_in, kk, jnp.exp(b_last)


def _stack(a, b):
    return jnp.concatenate([a, b], axis=0)


def _gla_pair_chunk(q, k, b, v0, v1, st, masks, bd_causal, fwd):
    qi, ki, q_in, kk, g = _decay_factors(q, k, b, fwd)
    m0, m1 = masks
    split = lambda t: _stack(jnp.where(m0, t, 0.0), jnp.where(m1, t, 0.0)).astype(BF16)
    ki_bf = ki.astype(BF16)
    a = jnp.where(bd_causal, _dot_nt(split(qi), _stack(ki_bf, ki_bf)), 0.0)
    vs = _stack(v0, v1)
    o = _dot(a.astype(BF16), vs) + _dot_nt(split(q_in), st.astype(BF16))
    return o, st * g + _dot_tn(vs, split(kk))


def _hgrn_pair_chunk(q0, k0, b0, v0, q1, k1, b1, v1, st, bd_causal, fwd):
    qi0, ki0, qin0, kk0, g0 = _decay_factors(q0, k0, b0, fwd)
    qi1, ki1, qin1, kk1, g1 = _decay_factors(q1, k1, b1, fwd)
    zero = jnp.zeros((CHUNK, LANES), F32)
    wide = lambda t0, t1: _stack(jnp.concatenate([t0, zero], axis=1),
                                 jnp.concatenate([zero, t1], axis=1)).astype(BF16)
    a = jnp.where(bd_causal, _dot_nt(_stack(qi0, qi1).astype(BF16), _stack(ki0, ki1).astype(BF16)), 0.0)
    vs = _stack(v0, v1)
    o = _dot(a.astype(BF16), vs) + _dot_nt(wide(qin0, qin1), st.astype(BF16))
    return o, st * jnp.concatenate([g0, g1], axis=1) + _dot_tn(vs, wide(kk0, kk1))


def _even_gates(d, fg, al, qb, tri, w2_ref, ba_ref, lbv):
    y = _dot(al[...].astype(BF16), w2_ref[d]) + ba_ref[d]
    la_a = (jnp.minimum(y, 0.0) - jnp.log(1.0 + jnp.exp(-jnp.abs(y)))) * (1.0 / GLA_TAU)
    f = fg[...]
    t = jnp.exp(-jnp.abs(f))
    rcp = 1.0 / (1.0 + t)
    sig_f = jnp.where(f >= 0.0, rcp, t * rcp)
    sig_nf = jnp.where(f >= 0.0, t * rcp, rcp)
    la_h = jnp.log(lbv + (1.0 - lbv) * sig_f)
    key_h = (1.0 - lbv) * sig_nf
    la_hi, la_lo = _split_bf16(jnp.concatenate([la_a, la_h], axis=1))
    b_all = _dot(tri, la_hi) + _dot(tri, la_lo)
    return b_all, key_h, _silu(qb[...].astype(F32))


def _even_tile(dir_refs, w2_ref, ba_ref, lb_ref, stg, sth):
    rc = lax.broadcasted_iota(jnp.int32, (2 * CHUNK, 2 * CHUNK), 0)
    cc = lax.broadcasted_iota(jnp.int32, (2 * CHUNK, 2 * CHUNK), 1)
    same_head = (rc // CHUNK) == (cc // CHUNK)
    bd_causal = [same_head & (cc <= rc), same_head & (cc >= rc)]
    rt = lax.broadcasted_iota(jnp.int32, (TL, TL), 0)
    ct = lax.broadcasted_iota(jnp.int32, (TL, TL), 1)
    same = (rt // CHUNK) == (ct // CHUNK)
    tri = [jnp.where(same & (ct <= rt), 1.0, 0.0).astype(BF16),
           jnp.where(same & (ct >= rt), 1.0, 0.0).astype(BF16)]
    lane = lax.broadcasted_iota(jnp.int32, (CHUNK, LANES), 1)
    pair_masks = [lane < GLA_DK, lane >= GLA_DK]
    lbv = lb_ref[...]
    n_pairs = GLA_HEADS // 2
    st_g = [[stg[d, p] for p in range(n_pairs)] for d in range(2)]
    st_h = [[sth[d, p] for p in range(HG_HEADS // 2)] for d in range(2)]
    gates = [_even_gates(d, dir_refs[d][3], dir_refs[d][5], dir_refs[d][2], tri[d], w2_ref, ba_ref, lbv)
             for d in range(2)]

    n_chunks = TL // CHUNK
    for ci in range(n_chunks):
        for d in range(2):
            qk, va, _, _, ib, _, o_ref = dir_refs[d]
            b_all, key_h, qh = gates[d]
            fwd = d == 0
            cpos = ci if fwd else n_chunks - 1 - ci
            rows = slice(cpos * CHUNK, (cpos + 1) * CHUNK)
            for p in range(n_pairs):
                ln = slice(p * LANES, (p + 1) * LANES)
                kl = slice(GLA_QK + p * LANES, GLA_QK + (p + 1) * LANES)
                c0, c1 = 2 * p * GLA_DV, (2 * p + 1) * GLA_DV
                o, st_g[d][p] = _gla_pair_chunk(
                    qk[rows, ln].astype(F32) * (GLA_DK ** -0.5), qk[rows, kl].astype(F32), b_all[rows, ln],
                    va[rows, c0:c0 + GLA_DV], va[rows, c1:c1 + GLA_DV], st_g[d][p],
                    pair_masks, bd_causal[d], fwd)
                o_ref[rows, c0:c0 + GLA_DV] = o[0:CHUNK].astype(BF16)
                o_ref[rows, c1:c1 + GLA_DV] = o[CHUNK:2 * CHUNK].astype(BF16)
            for p in range(HG_HEADS // 2):
                l0 = slice(2 * p * LANES, (2 * p + 1) * LANES)
                l1 = slice((2 * p + 1) * LANES, (2 * p + 2) * LANES)
                bl0 = slice(GLA_QK + l0.start, GLA_QK + l0.stop)
                bl1 = slice(GLA_QK + l1.start, GLA_QK + l1.stop)
                o, st_h[d][p] = _hgrn_pair_chunk(
                    qh[rows, l0], key_h[rows, l0], b_all[rows, bl0], ib[rows, l0],
                    qh[rows, l1], key_h[rows, l1], b_all[rows, bl1], ib[rows, l1],
                    st_h[d][p], bd_causal[d], fwd)
                o_ref[rows, GLA_V + l0.start:GLA_V + l0.stop] = o[0:CHUNK].astype(BF16)
                o_ref[rows, GLA_V + l1.start:GLA_V + l1.stop] = o[CHUNK:2 * CHUNK].astype(BF16)
    for d in range(2):
        for p in range(n_pairs):
            stg[d, p] = st_g[d][p]
        for p in range(HG_HEADS // 2):
            sth[d, p] = st_h[d][p]


def _even_scan_kernel(tf_ref, tb_ref, first_ref, last_ref, seq_ref,
                      qk_f, va_f, qb_f, fg_f, ib_f, al_f,
                      qk_b, va_b, qb_b, fg_b, ib_b, al_b,
                      w2_ref, ba_ref, lb_ref, sg_ref, sh_ref, *rest, n_prev):
    pg_ref, ph_ref = rest[:2] if n_prev else (None, None)
    of_ref, ob_ref, stg_ref, sth_ref, stg, sth = rest[-6:]
    i = pl.program_id(0)
    is_first = first_ref[i] == 1
    is_last = last_ref[i] == 1
    is_ctx = seq_ref[i] < CTX_B

    @pl.when(jnp.logical_and(is_first, is_ctx))
    def _():
        stg[...] = jnp.zeros_like(stg)
        sth[...] = jnp.zeros_like(sth)

    @pl.when(jnp.logical_and(is_first, jnp.logical_not(is_ctx)))
    def _():
        for d in range(2):
            for p in range(GLA_HEADS // 2):
                stg[d, p] = sg_ref[d, p].T
            for h in range(HG_HEADS):
                sth[d, h // 2, :, (h % 2) * HG_DK:(h % 2 + 1) * HG_DK] = sh_ref[d, h].T

    _even_tile([(qk_f, va_f, qb_f, fg_f, ib_f, al_f, of_ref), (qk_b, va_b, qb_b, fg_b, ib_b, al_b, ob_ref)],
               w2_ref, ba_ref, lb_ref, stg, sth)

    @pl.when(jnp.logical_and(is_last, is_ctx))
    def _():
        for k in range(n_prev):
            stg_ref[k] = pg_ref[k]
            sth_ref[k] = ph_ref[k]
        for d in range(2):
            for p in range(GLA_HEADS // 2):
                stg_ref[n_prev, d, p] = stg[d, p].T
            for h in range(HG_HEADS):
                sth_ref[n_prev, d, h] = sth[d, h // 2, :, (h % 2) * HG_DK:(h % 2 + 1) * HG_DK].T


def _even_scan(zm, zg, w2_pad, b_a, lb, s_gla, s_hgrn, prev):
    n_prev = 0 if prev is None else prev[0].shape[1]
    work = _scan_work()

    def zspec(width, col_block, which):
        return pl.BlockSpec((TL, width), lambda i, tf, tb, fi, la, sq: ((tf, tb)[which][i], col_block))

    def dir_specs(which):
        return [zspec(512, EV_QA // 512, which), zspec(512, EV_VA // 512, which),
                zspec(512, EV_QB // 512, which), zspec(512, (EV_FF, EV_FB)[which] // 512, which),
                zspec(512, EV_IB // 512, which), zspec(LANES, EV_AL // LANES, which)]

    def dir_args():
        return [zm, zm, zm, zg, zm, zg]

    def const(shape):
        return pl.BlockSpec(shape, lambda i, *_: (0,) * len(shape))

    def s_in(shape):
        return pl.BlockSpec((None, None) + shape,
                            lambda i, tf, tb, fi, la, sq:
                            (jnp.clip(sq[i] - CTX_B, 0, SMP_B - 1), n_prev) + (0,) * len(shape))

    def ctx_states(n_layers, shape):
        return pl.BlockSpec((None, n_layers) + shape,
                            lambda i, tf, tb, fi, la, sq: (jnp.minimum(sq[i], CTX_B - 1), 0) + (0,) * len(shape))

    gshape = (2, GLA_HEADS // 2, LANES, LANES)
    hshape = (2, HG_HEADS, HG_DK, HG_DV)
    grid_spec = pltpu.PrefetchScalarGridSpec(
        num_scalar_prefetch=5, grid=(N_WORK,),
        in_specs=dir_specs(0) + dir_specs(1) + [
            const((2, LANES, GLA_QK)), const((2, 1, GLA_QK)), const((1, HG_K)),
            s_in(gshape), s_in(hshape)]
        + ([ctx_states(n_prev, gshape), ctx_states(n_prev, hshape)] if n_prev else []),
        out_specs=[pl.BlockSpec((TL, D), lambda i, tf, tb, fi, la, sq: (tf[i], 0)),
                   pl.BlockSpec((TL, D), lambda i, tf, tb, fi, la, sq: (tb[i], 0)),
                   ctx_states(n_prev + 1, gshape), ctx_states(n_prev + 1, hshape)],
        scratch_shapes=[pltpu.VMEM(gshape, F32),
                        pltpu.VMEM((2, HG_HEADS // 2, HG_DV, 2 * HG_DK), F32)])
    return pl.pallas_call(
        functools.partial(_even_scan_kernel, n_prev=n_prev), grid_spec=grid_spec,
        out_shape=[jax.ShapeDtypeStruct((T_ALL, D), BF16), jax.ShapeDtypeStruct((T_ALL, D), BF16),
                   jax.ShapeDtypeStruct((CTX_B, n_prev + 1) + gshape, F32),
                   jax.ShapeDtypeStruct((CTX_B, n_prev + 1) + hshape, F32)],
        compiler_params=_params(1), name="even_scan",
    )(*work, *dir_args(), *dir_args(), w2_pad, b_a, lb, s_gla, s_hgrn, *(prev if n_prev else []))


RET_HP = 2


def _ret_tables():
    lg = np.log1p(-np.exp2(-5.0 - np.arange(RET_HEADS, dtype=np.float64)))
    pos = np.arange(TL, dtype=np.float64)
    dm = np.zeros((2, RET_HEADS, TL, TL))
    rq = np.zeros((2, RET_HEADS, TL, 1))
    rk = np.zeros((2, RET_HEADS, TL, 1))
    gc = np.zeros((2, RET_HEADS, 1, RET_DV))
    diff = pos[:, None] - pos[None, :]
    kscale = RET_DK ** -0.5
    for h in range(RET_HEADS):
        dm[0, h] = np.where(diff >= 0, np.exp(lg[h] * np.maximum(diff, 0)), 0.0) * kscale
        rq[0, h, :, 0] = np.exp(lg[h] * (pos + 1))
        rk[0, h, :, 0] = np.exp(lg[h] * (TL - 1 - pos)) * kscale
        gc[0, h] = np.exp(lg[h] * TL)
        lb = lg[RET_HEADS - 1 - h]
        dm[1, h] = np.where(diff <= 0, np.exp(lb * np.maximum(-diff, 0)), 0.0) * kscale
        rq[1, h, :, 0] = np.exp(lb * (TL - pos))
        rk[1, h, :, 0] = np.exp(lb * pos) * kscale
        gc[1, h] = np.exp(lb * TL)
    return tuple(jnp.asarray(a, F32) for a in (dm, rq, rk, gc))


def _odd_dir(d, q_ref, k_ref, v_ref, dm_ref, rq_ref, rk_ref, gc_ref, st, o_ref, zero_state):
    for hh in range(RET_HP):
        k = k_ref[:, hh * RET_DK:(hh + 1) * RET_DK]
        q = q_ref[:, hh * RET_DK:(hh + 1) * RET_DK]
        v = v_ref[:, hh * RET_DV:(hh + 1) * RET_DV]
        a = _dot_nt(q, k) * dm_ref[d, hh]
        o = _dot(a.astype(BF16), v)
        kk = (k.astype(F32) * rk_ref[d, hh]).astype(BF16)
        s_new = _dot_tn(kk, v)
        if not zero_state:
            s = st[d, hh]
            o = o + rq_ref[d, hh] * _dot(q, s.astype(BF16))
            s_new = gc_ref[d, hh] * s + s_new
        st[d, hh] = s_new
        o_ref[:, hh * RET_DV:(hh + 1) * RET_DV] = o.astype(BF16)


def _odd_scan_kernel(tf_ref, tb_ref, first_ref, last_ref, seq_ref,
                     q_f, k_f, v_f, q_b, k_b, v_b,
                     dm_ref, rq_ref, rk_ref, gc_ref, s0_ref, *rest, n_prev):
    prev_ref = rest[0] if n_prev else None
    of_ref, ob_ref, sout_ref, st = rest[-4:]
    i = pl.program_id(1)
    is_first = first_ref[i] == 1
    is_last = last_ref[i] == 1
    is_ctx = seq_ref[i] < CTX_B

    from_zero = jnp.logical_and(is_first, is_ctx)

    @pl.when(jnp.logical_and(is_first, jnp.logical_not(is_ctx)))
    def _():
        st[...] = s0_ref[...]

    for zero_state in (True, False):
        @pl.when(from_zero if zero_state else jnp.logical_not(from_zero))
        def _():
            _odd_dir(0, q_f, k_f, v_f, dm_ref, rq_ref, rk_ref, gc_ref, st, of_ref, zero_state)
            _odd_dir(1, q_b, k_b, v_b, dm_ref, rq_ref, rk_ref, gc_ref, st, ob_ref, zero_state)

    @pl.when(jnp.logical_and(is_last, is_ctx))
    def _():
        for k in range(n_prev):
            sout_ref[k] = prev_ref[k]
        sout_ref[n_prev] = st[...]


def _odd_scan(z, tables, s_ret, prev):
    n_prev = 0 if prev is None else prev.shape[1]
    work = _scan_work()
    dm, rq, rk, gc = tables
    qw, vw = RET_HP * RET_DK, RET_HP * RET_DV

    def zspec(width, col0, which):
        return pl.BlockSpec((TL, width),
                            lambda hp, i, tf, tb, fi, la, sq: ((tf, tb)[which][i], col0 // width + hp))

    def dir_specs(which):
        return [zspec(qw, OD_Q, which), zspec(qw, OD_K, which), zspec(vw, OD_V, which)]

    def table(shape):
        return pl.BlockSpec((2, RET_HP) + shape, lambda hp, i, *_: (0, hp) + (0,) * len(shape))

    sshape = (2, RET_HP, RET_DK, RET_DV)

    def ctx_states(n_layers):
        return pl.BlockSpec((None, n_layers) + sshape,
                            lambda hp, i, tf, tb, fi, la, sq: (jnp.minimum(sq[i], CTX_B - 1), 0, 0, hp, 0, 0))

    grid_spec = pltpu.PrefetchScalarGridSpec(
        num_scalar_prefetch=5, grid=(RET_HEADS // RET_HP, N_WORK),
        in_specs=dir_specs(0) + dir_specs(1) + [
            table((TL, TL)), table((TL, 1)), table((TL, 1)), table((1, RET_DV)),
            pl.BlockSpec((None, None) + sshape,
                         lambda hp, i, tf, tb, fi, la, sq:
                         (jnp.clip(sq[i] - CTX_B, 0, SMP_B - 1), n_prev, 0, hp, 0, 0))]
        + ([ctx_states(n_prev)] if n_prev else []),
        out_specs=[pl.BlockSpec((TL, vw), lambda hp, i, tf, tb, fi, la, sq: (tf[i], hp)),
                   pl.BlockSpec((TL, vw), lambda hp, i, tf, tb, fi, la, sq: (tb[i], hp)),
                   ctx_states(n_prev + 1)],
        scratch_shapes=[pltpu.VMEM(sshape, F32)])
    return pl.pallas_call(
        functools.partial(_odd_scan_kernel, n_prev=n_prev), grid_spec=grid_spec,
        out_shape=[jax.ShapeDtypeStruct((T_ALL, RET_V), BF16), jax.ShapeDtypeStruct((T_ALL, RET_V), BF16),
                   jax.ShapeDtypeStruct((CTX_B, n_prev + 1, 2, RET_HEADS, RET_DK, RET_DV), F32)],
        compiler_params=_params(2), name="odd_scan",
    )(*work, z, z, z, z, z, z, dm, rq, rk, gc, s_ret, *([prev] if n_prev else []))


def _head_rms(o, width):
    parts = []
    for j in range(0, o.shape[1], width):
        blk = o[:, j:j + width]
        parts.append(blk * lax.rsqrt(jnp.mean(blk * blk, axis=-1, keepdims=True) + NORM_EPS))
    return parts


def _route(h2, wr_hi_ref, wr_lo_ref, br_ref):
    h_hi, h_lo = _split_bf16(h2)
    logits = (_dot(h_hi, wr_hi_ref[...]) + _dot(h_hi, wr_lo_ref[...]) + _dot(h_lo, wr_hi_ref[...])
              + br_ref[...])
    lane_i = lax.broadcasted_iota(jnp.int32, logits.shape, 1)
    lane = lane_i.astype(F32)
    neg = -jnp.inf
    big = 1e9
    is_grp = (lane_i >= N_EXPERTS) & (lane_i < N_EXPERTS + N_GROUPS)
    lg = jnp.where(is_grp, logits, neg)
    mg = jnp.max(lg, axis=-1, keepdims=True)
    gsel = jnp.min(jnp.where(lg == mg, lane - N_EXPERTS, big), axis=-1, keepdims=True)
    pg = 1.0 / jnp.sum(jnp.where(is_grp, jnp.exp(lg - mg), 0.0), axis=-1, keepdims=True)
    in_grp = (lane_i < N_EXPERTS) & ((lane_i // EXP_PER_GROUP).astype(F32) == gsel)
    le = jnp.where(in_grp, logits, neg)
    m1 = jnp.max(le, axis=-1, keepdims=True)
    i1 = jnp.min(jnp.where(le == m1, lane, big), axis=-1, keepdims=True)
    le2 = jnp.where(lane == i1, neg, le)
    m2 = jnp.max(le2, axis=-1, keepdims=True)
    i2 = jnp.min(jnp.where(le2 == m2, lane, big), axis=-1, keepdims=True)
    e2 = jnp.exp(m2 - m1)
    w1 = pg / (1.0 + e2)
    w2 = pg * e2 / (1.0 + e2)
    first_lo = i1 < i2
    ia = jnp.minimum(i1, i2) - gsel * EXP_PER_GROUP
    ib = jnp.maximum(i1, i2) - gsel * EXP_PER_GROUP
    pair = gsel * PAIRS_PER_GROUP + ia * (2 * EXP_PER_GROUP - 1 - ia) * 0.5 + (ib - ia - 1.0)
    return pair, jnp.where(first_lo, w1, w2), jnp.where(first_lo, w2, w1)


def _post_tail(i, x, out, m, nw2_ref, wr_hi_ref, wr_lo_ref, br_ref,
               x1_ref, hrow_ref, meta_ref, cnt_ref, carry):
    x1 = x + m[2:3] * out
    x1_ref[...] = x1
    h2 = _norm_mod(x1, nw2_ref[...], m[3:4], m[4:5])
    pair, w_a, w_b = _route(h2, wr_hi_ref, wr_lo_ref, br_ref)

    @pl.when(i == 0)
    def _():
        carry[...] = jnp.zeros_like(carry)

    lane = lax.broadcasted_iota(jnp.int32, (TM, LANES), 1)
    onehot = jnp.where(lane.astype(F32) == pair, 1.0, 0.0)
    r = lax.broadcasted_iota(jnp.int32, (TM, TM), 0)
    c = lax.broadcasted_iota(jnp.int32, (TM, TM), 1)
    earlier = jnp.where(c < r, 1.0, 0.0).astype(BF16)
    before = _dot(earlier, onehot.astype(BF16)) + carry[...]
    rank = jnp.sum(onehot * before, axis=-1, keepdims=True)
    carry[...] += jnp.sum(onehot, axis=0, keepdims=True)
    cnt_ref[...] = carry[...]
    meta_ref[...] = jnp.where(lane == 0, pair, jnp.where(lane == 1, rank, 0.0))
    hrow_ref[:, 0:D] = h2
    hrow_ref[:, D:ROW_W] = jnp.where(lane == 0, w_a, jnp.where(lane == 1, w_b, 0.0))


def _post_even_kernel(*refs, n_x):
    x_refs = refs[:n_x]
    (of_ref, ob_ref, ra_ref, gb_ref, mod_ref, gn_ref, wo_ref, nw2_ref, wr_hi_ref, wr_lo_ref, br_ref,
     x1_ref, hrow_ref, meta_ref, cnt_ref, carry) = refs[n_x:]
    o = of_ref[...].astype(F32) + ob_ref[...].astype(F32)
    normed = jnp.concatenate(_head_rms(o, GLA_DV), axis=1)
    gate = jnp.concatenate([_silu(ra_ref[...].astype(F32)), _silu(gb_ref[...].astype(F32))], axis=1)
    mixed = (normed * gn_ref[...] * gate).astype(BF16)
    out = _dot(mixed, wo_ref[...])
    _post_tail(pl.program_id(0), _load_x(x_refs), out, mod_ref[...], nw2_ref, wr_hi_ref, wr_lo_ref, br_ref,
               x1_ref, hrow_ref, meta_ref, cnt_ref, carry)


def _post_odd_kernel(of_ref, ob_ref, g_ref, x_ref, mod_ref, wo_ref,
                     nw2_ref, wr_hi_ref, wr_lo_ref, br_ref,
                     x1_ref, hrow_ref, meta_ref, cnt_ref, carry):
    o = of_ref[...].astype(F32) + ob_ref[...].astype(F32)
    normed = jnp.concatenate(_head_rms(o, RET_DV), axis=1)
    mixed = (normed * _silu(g_ref[...].astype(F32))).astype(BF16)
    out = _dot(mixed, wo_ref[...])
    _post_tail(pl.program_id(0), x_ref[...], out, mod_ref[...], nw2_ref, wr_hi_ref, wr_lo_ref, br_ref,
               x1_ref, hrow_ref, meta_ref, cnt_ref, carry)


def _post_common_specs():
    tile = lambda w: pl.BlockSpec((TM, w), lambda i: (i, 0))
    const = lambda s: pl.BlockSpec(s, lambda i: (0,) * len(s))
    mod = pl.BlockSpec((None, 6, D), lambda i: (_cond_of_tile(i, TM), 0, 0))
    out_specs = [tile(D), tile(ROW_W), tile(LANES), const((1, LANES))]
    out_shape = [jax.ShapeDtypeStruct((T_ALL, D), F32),
                 jax.ShapeDtypeStruct((T_ALL, ROW_W), F32),
                 jax.ShapeDtypeStruct((T_ALL, LANES), F32),
                 jax.ShapeDtypeStruct((1, LANES), F32)]
    scratch = [pltpu.VMEM((1, LANES), F32)]
    return tile, const, mod, out_specs, out_shape, scratch


def _post_even(o_f, o_b, z, x_parts, mods_l, gn, wo_bf, nw2, wr_hi, wr_lo, br):
    tile, const, mod, out_specs, out_shape, scratch = _post_common_specs()
    zcol = lambda c0: pl.BlockSpec((TM, 512), lambda i: (i, c0 // 512))
    return pl.pallas_call(
        functools.partial(_post_even_kernel, n_x=len(x_parts)), grid=(T_ALL // TM,),
        in_specs=_x_specs(x_parts) + [
                  tile(D), tile(D), zcol(EV_RA), zcol(EV_GB), mod, const((1, D)),
                  const((D, D)), const((1, D)), const((D, ROUTER_N)), const((D, ROUTER_N)),
                  const((1, ROUTER_N))],
        out_specs=out_specs, out_shape=out_shape, scratch_shapes=scratch,
        compiler_params=_params(1), name="post_even",
    )(*x_parts, o_f, o_b, z, z, mods_l, gn, wo_bf, nw2, wr_hi, wr_lo, br)


def _post_odd(o_f, o_b, z, x, mods_l, wo_bf, nw2, wr_hi, wr_lo, br):
    tile, const, mod, out_specs, out_shape, scratch = _post_common_specs()
    return pl.pallas_call(
        _post_odd_kernel, grid=(T_ALL // TM,),
        in_specs=[tile(RET_V), tile(RET_V),
                  pl.BlockSpec((TM, RET_V), lambda i: (i, OD_G // RET_V)), tile(D), mod,
                  const((RET_V, D)), const((1, D)), const((D, ROUTER_N)), const((D, ROUTER_N)),
                  const((1, ROUTER_N))],
        out_specs=out_specs, out_shape=out_shape, scratch_shapes=scratch,
        compiler_params=_params(1), name="post_odd",
    )(o_f, o_b, z, x, mods_l, wo_bf, nw2, wr_hi, wr_lo, br)


def _pair_tables():
    ea, eb = [], []
    for g in range(N_GROUPS):
        for a in range(EXP_PER_GROUP):
            for b in range(a + 1, EXP_PER_GROUP):
                ea.append(g * EXP_PER_GROUP + a)
                eb.append(g * EXP_PER_GROUP + b)
    pad = LANES - len(ea)
    return (jnp.asarray(ea + [0] * pad, jnp.int32), jnp.asarray(eb + [0] * pad, jnp.int32))


def _dispatch_plan(meta, counts):
    pair = meta[:, 0].astype(jnp.int32)
    rank = meta[:, 1].astype(jnp.int32)
    cnt = counts[0].astype(jnp.int32)
    tiles_g = (cnt + TMG - 1) // TMG
    tile_end = jnp.cumsum(tiles_g)
    tile_start = tile_end - tiles_g
    slot0 = tile_start * TMG
    ids = jnp.arange(LANES, dtype=jnp.int32)
    dest = rank + jnp.sum(jnp.where(pair[:, None] == ids[None, :], slot0[None, :], 0), axis=1)
    n_real = tile_end[-1]
    j = jnp.arange(N_TILES_MOE, dtype=jnp.int32)
    jj = jnp.minimum(j, n_real - 1)
    grp = jnp.sum((jj[:, None] >= tile_end[None, :]).astype(jnp.int32), axis=1)
    ea_t, eb_t = _pair_tables()
    onehot_g = grp[:, None] == ids[None, :]
    pick = lambda v: jnp.sum(jnp.where(onehot_g, v[None, :], 0), axis=1)
    nv = jnp.clip(pick(cnt) - (jj - pick(tile_start)) * TMG, 0, TMG)
    nv = jnp.where(j < n_real, nv, 0)
    return dest // SUBLANES, dest % SUBLANES, pick(ea_t), pick(eb_t), nv, n_real.reshape(1)


DISPATCH_STEPS = T_ALL // DISPATCH_ROWS


def _dispatch_kernel(dhi_ref, dlo_ref, nv_ref, h_ref, hs_hbm, zeros, sem, pad_sem):
    j = pl.program_id(0)
    base = j * DISPATCH_ROWS
    tile_sub = TMG // SUBLANES

    def pad_copies(tile, start):
        nv = nv_ref[tile]
        group0 = tile * tile_sub
        rem = nv % SUBLANES

        def row(r, carry):
            cp = pltpu.make_async_copy(zeros.at[0, pl.ds(0, 1)],
                                       hs_hbm.at[group0 + nv // SUBLANES, pl.ds(r, 1)], pad_sem.at[0])
            cp.start() if start else cp.wait()
            return carry

        lax.fori_loop(jnp.where(rem == 0, SUBLANES, rem), SUBLANES, row, 0)
        pos = group0 + (nv + SUBLANES - 1) // SUBLANES
        left = group0 + tile_sub - pos
        size = tile_sub
        while size >= 1:
            take = (left & size) != 0

            @pl.when(take)
            def _(pos=pos, size=size):
                cp = pltpu.make_async_copy(zeros.at[pl.ds(0, size)], hs_hbm.at[pl.ds(pos, size)],
                                           pad_sem.at[0])
                cp.start() if start else cp.wait()

            pos = pos + jnp.where(take, size, 0)
            size //= 2

    @pl.when(j == 0)
    def _():
        zeros[...] = jnp.zeros_like(zeros)
        lax.fori_loop(0, N_TILES_MOE, lambda tile, c: (pad_copies(tile, True), c)[1], 0)

    @pl.when(j == pl.num_programs(0) - 1)
    def _():
        lax.fori_loop(0, N_TILES_MOE, lambda tile, c: (pad_copies(tile, False), c)[1], 0)

    def issue(g, carry):
        for k in range(SUBLANES):
            t = base + g * SUBLANES + k
            pltpu.make_async_copy(h_ref.at[g, pl.ds(k, 1)], hs_hbm.at[dhi_ref[t], pl.ds(dlo_ref[t], 1)],
                                  sem.at[0]).start()
        return carry

    lax.fori_loop(0, DISPATCH_ROWS // SUBLANES, issue, 0)
    pltpu.make_async_copy(h_ref, hs_hbm.at[pl.ds(0, DISPATCH_ROWS // SUBLANES)], sem.at[0]).wait()


def _dispatch(dest_hi, dest_lo, nv, hrow):
    grid_spec = pltpu.PrefetchScalarGridSpec(
        num_scalar_prefetch=3, grid=(DISPATCH_STEPS,),
        in_specs=[pl.BlockSpec((DISPATCH_ROWS // SUBLANES, SUBLANES, ROW_W), lambda j, *_: (j, 0, 0))],
        out_specs=pl.BlockSpec(memory_space=pl.ANY),
        scratch_shapes=[pltpu.VMEM((TMG // SUBLANES, SUBLANES, ROW_W), F32),
                        pltpu.SemaphoreType.DMA((1,)), pltpu.SemaphoreType.DMA((1,))])
    hs = pl.pallas_call(
        _dispatch_kernel, grid_spec=grid_spec,
        out_shape=jax.ShapeDtypeStruct((N_TILES_MOE * TMG // SUBLANES, SUBLANES, ROW_W), F32),
        compiler_params=_params(1), name="moe_dispatch",
    )(dest_hi, dest_lo, nv, hrow.reshape(T_ALL // SUBLANES, SUBLANES, ROW_W))
    return hs.reshape(N_TILES_MOE * TMG, ROW_W)


def _moe_kernel(ea_ref, eb_ref, nreal_ref, hs_ref,
                wga_ref, wua_ref, wda_ref, wgb_ref, wub_ref, wdb_ref, ys_ref):
    j = pl.program_id(0)

    @pl.when(j < nreal_ref[0])
    def _():
        h = hs_ref[:, 0:D].astype(BF16)
        wrow = hs_ref[:, D:ROW_W]
        w_a, w_b = wrow[:, 0:1], wrow[:, 1:2]
        act_a = _silu(_dot(h, wga_ref[0])) * _dot(h, wua_ref[0]) * w_a
        act_b = _silu(_dot(h, wgb_ref[0])) * _dot(h, wub_ref[0]) * w_b
        ys_ref[...] = _dot(act_a.astype(BF16), wda_ref[0]) + _dot(act_b.astype(BF16), wdb_ref[0])

    @pl.when(j >= nreal_ref[0])
    def _():
        ys_ref[...] = jnp.zeros_like(ys_ref)


def _moe(ea, eb, n_real, hs, wg_bf, wu_bf, wd_bf):
    def wspec(shape, which):
        return pl.BlockSpec((1,) + shape, lambda j, ea, eb, nr: ((ea, eb)[which][j], 0, 0))

    up, down = (D, D_EXPERT), (D_EXPERT, D)
    grid_spec = pltpu.PrefetchScalarGridSpec(
        num_scalar_prefetch=3, grid=(N_TILES_MOE,),
        in_specs=[pl.BlockSpec((TMG, ROW_W), lambda j, ea, eb, nr: (jnp.minimum(j, nr[0] - 1), 0)),
                  wspec(up, 0), wspec(up, 0), wspec(down, 0),
                  wspec(up, 1), wspec(up, 1), wspec(down, 1)],
        out_specs=pl.BlockSpec((TMG, D), lambda j, ea, eb, nr: (j, 0)))
    return pl.pallas_call(
        _moe_kernel, grid_spec=grid_spec,
        out_shape=jax.ShapeDtypeStruct((N_TILES_MOE * TMG, D), F32),
        compiler_params=_params(1), name="moe",
    )(ea, eb, n_real, hs, wg_bf, wu_bf, wd_bf, wg_bf, wu_bf, wd_bf)


def _combine_kernel(dhi_ref, dlo_ref, ys_hbm, x1_ref, mod_ref, nwf_ref, o_ref, buf, sem, *, tile0, final):
    j = pl.program_id(0)
    n = pl.num_programs(0)

    def gather(tile, s):
        def issue(g, carry):
            for k in range(SUBLANES):
                t = (tile0 + tile) * TM + g * SUBLANES + k
                pltpu.make_async_copy(ys_hbm.at[dhi_ref[t], pl.ds(dlo_ref[t], 1)],
                                      buf.at[s, g, pl.ds(k, 1)], sem.at[s]).start()
            return carry
        lax.fori_loop(0, TM // SUBLANES, issue, 0)

    @pl.when(j == 0)
    def _():
        gather(0, 0)

    for s in range(2):
        @pl.when(jnp.logical_and(j + 1 < n, (j + 1) % 2 == s))
        def _():
            gather(j + 1, s)

    for s in range(2):
        @pl.when(j % 2 == s)
        def _():
            pltpu.make_async_copy(ys_hbm.at[pl.ds(0, TM // SUBLANES)], buf.at[s], sem.at[s]).wait()
            x = x1_ref[...] + mod_ref[5:6] * buf[s].reshape(TM, D)
            if final:
                var = jnp.mean(x * x, axis=-1, keepdims=True)
                x = x * lax.rsqrt(var + NORM_EPS) * nwf_ref[...]
            o_ref[...] = x


def _combine(dest_hi, dest_lo, ys, x1, mods_l, nwf, tile0=0, n_tok=T_ALL, final=False):
    grid_spec = pltpu.PrefetchScalarGridSpec(
        num_scalar_prefetch=2, grid=(n_tok // TM,),
        in_specs=[pl.BlockSpec(memory_space=pl.ANY),
                  pl.BlockSpec((TM, D), lambda i, *_: (tile0 + i, 0)),
                  pl.BlockSpec((None, 6, D), lambda i, *_: (_cond_of_tile(tile0 + i, TM), 0, 0)),
                  pl.BlockSpec((1, D), lambda i, *_: (0, 0))],
        out_specs=pl.BlockSpec((TM, D), lambda i, *_: (i, 0)),
        scratch_shapes=[pltpu.VMEM((2, TM // SUBLANES, SUBLANES, D), F32), pltpu.SemaphoreType.DMA((2,))])
    return pl.pallas_call(
        functools.partial(_combine_kernel, tile0=tile0, final=final), grid_spec=grid_spec,
        out_shape=jax.ShapeDtypeStruct((n_tok, D), F32),
        compiler_params=_params(1), name="moe_combine",
    )(dest_hi, dest_lo, ys.reshape(N_TILES_MOE * TMG // SUBLANES, SUBLANES, D), x1, mods_l, nwf)


def _rope_tables():
    freqs = ROPE_BASE ** (-jnp.arange(ROPE_PAIRS, dtype=F32) / ROPE_PAIRS)
    t = jnp.arange(SMP_L)
    halves_c, halves_s = [], []
    for p in (t // GRID_W, t % GRID_W):
        ang = p.astype(F32)[:, None] * freqs
        cs, sn = jnp.cos(ang), jnp.sin(ang)
        halves_c += [cs, cs]
        halves_s += [-sn, sn]
    cos_t = jnp.concatenate(halves_c, axis=1)
    sin_t = jnp.concatenate(halves_s, axis=1)
    cos_t = jnp.concatenate([jnp.ones((TM, RET_DK), F32), cos_t], axis=0)
    sin_t = jnp.concatenate([jnp.zeros((TM, RET_DK), F32), sin_t], axis=0)
    return cos_t, sin_t


def kernel(x_prompt, x_sample, state_gla, state_hgrn, state_ret, c, c_ctx, norm1_w, norm2_w, normf_w,
           w_mod, b_mod, w_in_even, gla_w_alpha, gla_b_alpha, hgrn_lb_logits, gla_norm_w, hgrn_norm_w,
           w_out_even, w_in_odd, w_out_odd, router_g_w, router_g_b, router_e_w, router_e_b,
           moe_w_gate, moe_w_up, moe_w_down):
    x_parts = [x_prompt.reshape(T_CTX, D), x_sample.reshape(T_SMP, D)]
    cond8 = jnp.concatenate([c_ctx[None, :], c, jnp.zeros((COND_ROWS - N_COND, D), F32)], axis=0)
    mods = _mods(cond8, w_mod, b_mod).reshape(DEPTH, COND_ROWS, 6, D)
    lb_all = jnp.cumsum(jax.nn.softmax(hgrn_lb_logits.astype(F32), axis=0), axis=0)[:N_EVEN]
    cos_t, sin_t = _rope_tables()
    ret_tables = _ret_tables()

    nwf = normf_w.reshape(1, D)
    moe_w = (moe_w_gate, moe_w_up, moe_w_down)
    st_even, st_ret = None, None
    for l in range(DEPTH):
        mods_l = mods[l]
        nw1 = norm1_w[l].reshape(1, D)
        nw2 = norm2_w[l].reshape(1, D)
        wr = jnp.concatenate([router_e_w[l], router_g_w[l],
                              jnp.zeros((D, ROUTER_N - N_EXPERTS - N_GROUPS), F32)], axis=1)
        wr_hi = wr.astype(BF16)
        wr_lo = (wr - wr_hi.astype(F32)).astype(BF16)
        br = jnp.concatenate([router_e_b[l], router_g_b[l],
                              jnp.zeros((ROUTER_N - N_EXPERTS - N_GROUPS,), F32)]).reshape(1, ROUTER_N)
        if l % 2 == 0:
            e = l // 2
            w = w_in_even[e]
            a0 = 2 * GLA_QK + 2 * GLA_V
            a1 = a0 + 2 * GLA_RANK
            f0, f1 = a1 + HG_K, a1 + 3 * HG_K
            wm_bf = jnp.concatenate([w[:, :a0], w[:, a1:f0], w[:, f1:]], axis=1).astype(BF16)
            wg_bf = jnp.concatenate([w[:, f0:f1], w[:, a0:a1],
                                     jnp.zeros((D, LANES - 2 * GLA_RANK), F32)], axis=1).astype(BF16)
            zm, zg, *moe_bf = _inproj_even(x_parts, mods_l, nw1, wm_bf, wg_bf, moe_w, l)
            w2_pad = jnp.zeros((2, LANES, GLA_QK), F32)
            for d in range(2):
                w2_pad = w2_pad.at[d, d * GLA_RANK:(d + 1) * GLA_RANK].set(gla_w_alpha[e, d])
            s_g = state_gla.reshape(SMP_B, N_EVEN, 2, GLA_HEADS // 2, LANES, LANES)
            o_f, o_b, ng, nh = _even_scan(zm, zg, w2_pad.astype(BF16), gla_b_alpha[e].reshape(2, 1, GLA_QK),
                                          lb_all[e].reshape(1, HG_K), s_g, state_hgrn, st_even)
            st_even = (ng, nh)
            gn = jnp.concatenate([gla_norm_w[e], hgrn_norm_w[e]]).reshape(1, D)
            x1, hrow, meta, counts = _post_even(o_f, o_b, zm, x_parts, mods_l, gn, w_out_even[e].astype(BF16),
                                                nw2, wr_hi, wr_lo, br)
        else:
            j = l // 2
            z, *moe_bf = _inproj_odd(x_parts[0], mods_l, nw1, w_in_odd[j].astype(BF16), cos_t, sin_t,
                                     moe_w, l)
            o_f, o_b, st_ret = _odd_scan(z, ret_tables, state_ret, st_ret)
            x1, hrow, meta, counts = _post_odd(o_f, o_b, z, x_parts[0], mods_l, w_out_odd[j].astype(BF16),
                                               nw2, wr_hi, wr_lo, br)
        dhi, dlo, ea, eb, nv, n_real = _dispatch_plan(meta, counts)
        hs = _dispatch(dhi, dlo, nv, hrow)
        ys = _moe(ea, eb, n_real, hs, *moe_bf)
        if l < DEPTH - 1:
            x_parts = [_combine(dhi, dlo, ys, x1, mods_l, nwf)]
    y_prompt = _combine(dhi, dlo, ys, x1, mods_l, nwf, 0, T_CTX, final=True).reshape(CTX_B, CTX_L, D)
    y_sample = _combine(dhi, dlo, ys, x1, mods_l, nwf, T_CTX // TM, T_SMP,
                        final=True).reshape(SMP_B, SMP_L, D)
    st_gla = st_even[0].reshape(CTX_B, N_EVEN, 2, GLA_HEADS, GLA_DK, GLA_DV)
    return (y_prompt, y_sample, st_gla, st_even[1], st_ret)
```
